```python
import math
import jax
import jax.numpy as jnp
from jax import lax

D_MODEL = 1024
BATCH = 16
SEQ = 4096
DEPTH = 4

CTX_LEN = 256
GRID_W = 64
HEAD_DIM = 128
N_HEADS_MIX = 4
MIX_W = N_HEADS_MIX * HEAD_DIM
N_BRANCH = 3
CHUNK = 64
CONV_K = 3
FFN_DIM = 3584
N_EXPERTS = 8
TOP_K = 2
LN_EPS = 1e-5
RMS_EPS = 1e-6
L2_EPS = 1e-6
MASK_NEG = -1e30
DEEPNORM_ALPHA = (2 * DEPTH) ** 0.25
DEEPNORM_BETA = (8 * DEPTH) ** -0.25
IN_SIZES = (MIX_W,) * 5 + (MIX_W,) * 4 + (4 * N_HEADS_MIX, 3 * MIX_W, MIX_W, 4 * N_HEADS_MIX, N_BRANCH * D_MODEL)
N_IN = sum(IN_SIZES)
N_DENSE = (DEPTH + 1) // 2
N_MOE = DEPTH // 2

kernel_name = 'hybrid_hgrn2_mlstm_gdn_moe_dit_trunk'


def _ln_plain(x):
    xf = x.astype(jnp.float32)
    mu = jnp.mean(xf, -1, keepdims=True)
    var = jnp.mean(jnp.square(xf - mu), -1, keepdims=True)
    return ((xf - mu) * lax.rsqrt(var + LN_EPS)).astype(x.dtype)


def _post_ln(x, g, b):
    xf = x.astype(jnp.float32)
    mu = jnp.mean(xf, -1, keepdims=True)
    var = jnp.mean(jnp.square(xf - mu), -1, keepdims=True)
    y = (xf - mu) * lax.rsqrt(var + LN_EPS) * g.astype(jnp.float32) + b.astype(jnp.float32)
    return y.astype(x.dtype)


def _modulate(x, shift, scale):
    return _ln_plain(x) * (1.0 + scale) + shift


def _split_cols(p):
    out, start = [], 0
    for s in IN_SIZES:
        out.append(p[..., start:start + s])
        start += s
    return out


def _heads(t, h):
    b, l, _ = t.shape
    return t.reshape(b, l, h, -1).transpose(0, 2, 1, 3)


def _unheads(t):
    b, h, l, d = t.shape
    return t.transpose(0, 2, 1, 3).reshape(b, l, h * d)


def _head_scalars(t, h):
    b, l, _ = t.shape
    return t.reshape(b, l, 4, h).transpose(2, 0, 3, 1)


def _l2n(t):
    return t * lax.rsqrt(jnp.sum(jnp.square(t), -1, keepdims=True) + L2_EPS)


def _centred_conv(x, w):
    k = w.shape[0]
    p = k // 2
    l = x.shape[1]
    xp = jnp.pad(x, ((0, 0), (p, p), (0, 0)))
    return sum(xp[:, j:j + l] * w[j] for j in range(k))


def _to_chunks(t):
    b, h, l = t.shape[:3]
    t = t.reshape((b, h, l // CHUNK, CHUNK) + t.shape[3:])
    return jnp.moveaxis(t, 2, 0)


def _from_chunks(t):
    t = jnp.moveaxis(t, 0, 2)
    return t.reshape(t.shape[:2] + (-1,) + t.shape[4:])


def _raster_to_colmajor(t):
    b, h, l = t.shape[:3]
    rows = l // GRID_W
    rest = t.shape[3:]
    t = t.reshape((b, h, rows, GRID_W) + rest)
    return jnp.swapaxes(t, 2, 3).reshape((b, h, l) + rest)


def _colmajor_to_raster(t):
    b, h, l = t.shape[:3]
    rows = l // GRID_W
    rest = t.shape[3:]
    t = t.reshape((b, h, GRID_W, rows) + rest)
    return jnp.swapaxes(t, 2, 3).reshape((b, h, l) + rest)


def _map_dirs(fn, dirs):
    return tuple(tuple(fn(t) for t in d) for d in dirs)


def _masked_exp(tri, diff):
    return jnp.where(tri, jnp.exp(jnp.where(tri, diff, 0.0)), 0.0)


def _gla_scan(q, k, v, log_f, state=None):
    b, h, _, dk = q.shape
    dv = v.shape[-1]
    if state is None:
        state = jnp.zeros((b, h, dk, dv), jnp.float32)
    tri = jnp.tril(jnp.ones((CHUNK, CHUNK), bool))[:, :, None]

    def step(s, inp):
        qc, kc, vc, gc = inp
        g = jnp.cumsum(gc, axis=-2)
        dec = _masked_exp(tri, g[..., :, None, :] - g[..., None, :, :])
        a = jnp.einsum('bhtk,bhsk,bhtsk->bhts', qc, kc, dec)
        o = jnp.einsum('bhts,bhsv->bhtv', a, vc) + jnp.einsum('bhtk,bhkv->bhtv', qc * jnp.exp(g), s)
        g_last = g[..., -1:, :]
        s = jnp.exp(g_last[..., 0, :])[..., None] * s + jnp.einsum('bhsk,bhsv->bhkv', kc * jnp.exp(g_last - g), vc)
        return s, o

    state, o = lax.scan(step, state, tuple(_to_chunks(t) for t in (q, k, v, log_f)))
    return _from_chunks(o), state


def _mlstm_scan(q, k, v, ig, lf, state=None):
    b, h, _, dk = q.shape
    dv = v.shape[-1]
    if state is None:
        state = (jnp.zeros((b, h, dk, dv), jnp.float32), jnp.zeros((b, h, dk), jnp.float32), jnp.zeros((b, h), jnp.float32))
    tri = jnp.tril(jnp.ones((CHUNK, CHUNK), bool))

    def step(carry, inp):
        cm, n, m = carry
        qc, kc, vc, ic, fc = inp
        bb = jnp.cumsum(fc, axis=-1)
        log_d = jnp.where(tri, bb[..., :, None] - bb[..., None, :] + ic[..., None, :], MASK_NEG)
        log_inter = bb + m[..., None]
        m_t = jnp.maximum(jnp.max(log_d, -1), log_inter)
        d = jnp.where(tri, jnp.exp(log_d - m_t[..., None]), 0.0)
        inter = jnp.exp(log_inter - m_t)
        s = jnp.einsum('bhtd,bhsd->bhts', qc, kc) * d
        num = jnp.einsum('bhts,bhsv->bhtv', s, vc) + inter[..., None] * jnp.einsum('bhtd,bhdv->bhtv', qc, cm)
        den = jnp.sum(s, -1) + inter * jnp.einsum('bhtd,bhd->bht', qc, n)
        hh = num / jnp.maximum(jnp.abs(den), jnp.exp(-m_t))[..., None]
        m_new = m_t[..., -1]
        w = jnp.exp(bb[..., -1:] - bb + ic - m_new[..., None])
        decay = jnp.exp(bb[..., -1] + m - m_new)
        cm = decay[..., None, None] * cm + jnp.einsum('bhs,bhsk,bhsv->bhkv', w, kc, vc)
        n = decay[..., None] * n + jnp.einsum('bhs,bhsk->bhk', w, kc)
        return (cm, n, m_new), hh

    state, hh = lax.scan(step, state, tuple(_to_chunks(t) for t in (q, k, v, ig, lf)))
    return _from_chunks(hh), state


def _gdn_scan(q, k, v, g, beta, state=None):
    b, h, _, dk = q.shape
    dv = v.shape[-1]
    if state is None:
        state = jnp.zeros((b, h, dk, dv), jnp.float32)
    qc, kc, vc, gc, bc = (_to_chunks(t) for t in (q, k, v, g, beta))
    tri = jnp.tril(jnp.ones((CHUNK, CHUNK), bool))
    strict = jnp.tril(jnp.ones((CHUNK, CHUNK), bool), -1)
    gam = jnp.cumsum(gc, axis=-1)
    ldec = _masked_exp(tri, gam[..., :, None] - gam[..., None, :])
    kb = kc * bc[..., None]
    a = jnp.where(strict, jnp.einsum('nbhtd,nbhsd->nbhts', kb, kc) * ldec, 0.0)
    rhs = jnp.concatenate([vc * bc[..., None], kb * jnp.exp(gam)[..., None]], axis=-1)
    sol = lax.linalg.triangular_solve(a + jnp.eye(CHUNK, dtype=a.dtype), rhs, left_side=True, lower=True, unit_diagonal=True)
    u, w = sol[..., :dv], sol[..., dv:]
    qk = jnp.einsum('nbhtd,nbhsd->nbhts', qc, kc) * ldec
    qg = qc * jnp.exp(gam)[..., None]
    kg = kc * jnp.exp(gam[..., -1:] - gam)[..., None]
    g_last = jnp.exp(gam[..., -1])

    def step(s, inp):
        uc, wc, qkc, qgc, kgc, glc = inp
        v_new = uc - jnp.einsum('bhtk,bhkv->bhtv', wc, s)
        o = jnp.einsum('bhtk,bhkv->bhtv', qgc, s) + jnp.einsum('bhts,bhsv->bhtv', qkc, v_new)
        s = glc[..., None, None] * s + jnp.einsum('bhsk,bhsv->bhkv', kgc, v_new)
        return s, o

    state, o = lax.scan(step, state, (u, w, qk, qg, kg, g_last))
    return _from_chunks(o), state


def _bidir_scan(scan, ctx_dirs, lat_dirs):
    o_ctx, o_lat = [], []
    for d in range(2):
        ca, la = ctx_dirs[d], lat_dirs[d]
        if d == 1:
            ca = tuple(jnp.flip(t, 2) for t in ca)
            la = tuple(jnp.flip(t, 2) for t in la)
        oc, st = scan(*ca)
        ol, _ = scan(*la, st)
        if d == 1:
            oc, ol = jnp.flip(oc, 2), jnp.flip(ol, 2)
        o_ctx.append(oc)
        o_lat.append(ol)
    return o_ctx[0] + o_ctx[1], o_lat[0] + o_lat[1]


def _mixer_inputs(h, w_in, conv_w, lb, m_bias, a_log, dt_bias):
    f32 = jnp.float32
    hh = N_HEADS_MIX
    (a_q, a_ff, a_fb, a_i, a_g, b_q, b_k, b_v, b_o, b_gt, c_qkv, c_z, c_gt, merge) = _split_cols(h @ w_in)

    def hd(t):
        return _heads(t, hh).astype(f32)

    lbh = lb.astype(f32).reshape(hh, 1, HEAD_DIM)

    def hgrn_gates(f_pre):
        f_pre = hd(f_pre)
        log_f = jnp.log(lbh + (1.0 - lbh) * jax.nn.sigmoid(f_pre))
        return (1.0 - lbh) * jax.nn.sigmoid(-f_pre), log_f

    aq, ai = hd(a_q), hd(a_i)
    k_f, lf_f = hgrn_gates(a_ff)
    k_b, lf_b = hgrn_gates(a_fb)
    a_dirs = ((aq, k_f, ai, lf_f), (aq, k_b, ai, lf_b))
    bq, bk, bv = hd(b_q), hd(b_k) * HEAD_DIM ** -0.5, hd(b_v)
    bg = _head_scalars(b_gt, hh).astype(f32) + m_bias.astype(f32)[:, None, :, None]
    b_dirs = ((bq, bk, bv, bg[0], jax.nn.log_sigmoid(bg[2])), (bq, bk, bv, bg[1], jax.nn.log_sigmoid(bg[3])))
    qkv = jax.nn.silu(_centred_conv(c_qkv, conv_w))
    cq, ck, cv = (hd(t) for t in jnp.split(qkv, 3, axis=-1))
    cq = _l2n(cq) * HEAD_DIM ** -0.5
    ck = _l2n(ck)
    cg = _head_scalars(c_gt, hh).astype(f32)
    decay = -jnp.exp(a_log.astype(f32))[:, None, :, None] * jax.nn.softplus(cg[:2] + dt_bias.astype(f32)[:, None, :, None])
    beta = jax.nn.sigmoid(cg[2:])
    c_dirs = ((cq, ck, cv, decay[0], beta[0]), (cq, ck, cv, decay[1], beta[1]))
    return a_dirs, b_dirs, c_dirs, (a_g, b_o, c_z, merge)


def _gated_head_norm(o, gain, gate, act):
    h, d = o.shape[1], o.shape[3]
    y = o * lax.rsqrt(jnp.mean(jnp.square(o), -1, keepdims=True) + RMS_EPS) * gain.astype(jnp.float32).reshape(h, 1, d)
    return _unheads(y * act(_heads(gate, h).astype(jnp.float32)))


def _merge_branches(outs, gates, mix_norm, w_branch, w_out):
    a_g, b_o, c_z, merge = gates
    dt = merge.dtype
    ys = (_gated_head_norm(outs[0], mix_norm[0], a_g, jax.nn.silu),
          _gated_head_norm(outs[1], mix_norm[1], b_o, jax.nn.sigmoid),
          _gated_head_norm(outs[2], mix_norm[2], c_z, jax.nn.silu))
    b, l, _ = merge.shape
    g = jax.nn.sigmoid(merge.reshape(b, l, N_BRANCH, D_MODEL))
    mixed = g[:, :, 0] * (ys[0].astype(dt) @ w_branch[0])
    for n in range(1, N_BRANCH):
        mixed = mixed + g[:, :, n] * (ys[n].astype(dt) @ w_branch[n])
    return mixed @ w_out


def _swiglu(h, w_gate, w_up, w_down):
    return (jax.nn.silu(h @ w_gate) * (h @ w_up)) @ w_down


def _moe(h, w_router, w_gate, w_up, w_down):
    logits = (h @ w_router).astype(jnp.float32)
    top_v, top_i = lax.top_k(logits, TOP_K)
    probs = jax.nn.softmax(top_v, axis=-1)
    combine = jnp.sum(jax.nn.one_hot(top_i, N_EXPERTS, dtype=jnp.float32) * probs[..., None], axis=-2).astype(h.dtype)
    out = combine[..., 0:1] * _swiglu(h, w_gate[0], w_up[0], w_down[0])
    for e in range(1, N_EXPERTS):
        out = out + combine[..., e:e + 1] * _swiglu(h, w_gate[e], w_up[e], w_down[e])
    return out


def _channel_mixer(h, l, ffn_w_gate, ffn_w_up, ffn_w_down, moe_router, moe_w_gate, moe_w_up, moe_w_down):
    i = l // 2
    if l % 2 == 0:
        return _swiglu(h, ffn_w_gate[i], ffn_w_up[i], ffn_w_down[i])
    return _moe(h, moe_router[i], moe_w_gate[i], moe_w_up[i], moe_w_down[i])


def setup_inputs(seed: int = 0) -> dict:
    key = jax.random.key(seed)
    ks = jax.random.split(key, 32)
    f32 = jnp.float32
    hh = N_HEADS_MIX

    def nrm(i, shape, scale):
        return jax.random.normal(ks[i], shape, f32) * scale

    i_bias = nrm(9, (DEPTH, 2, hh), 0.1)
    f_bias = 3.0 + 3.0 * jax.random.uniform(ks[10], (DEPTH, 2, hh), f32)
    dt = jnp.exp(jax.random.uniform(ks[12], (DEPTH, 2, hh), f32, math.log(1e-3), math.log(1e-1)))
    return {
        'x': nrm(0, (BATCH, SEQ, D_MODEL), 1.0),
        'c': nrm(1, (BATCH, D_MODEL), 1.0),
        'ctx': nrm(2, (BATCH, CTX_LEN, D_MODEL), 1.0),
        'c_ctx': nrm(3, (D_MODEL,), 1.0),
        'w_ada': nrm(4, (DEPTH, D_MODEL, 6 * D_MODEL), 0.5 * D_MODEL ** -0.5),
        'b_ada': nrm(5, (DEPTH, 6 * D_MODEL), 0.02),
        'w_in': nrm(6, (DEPTH, D_MODEL, N_IN), D_MODEL ** -0.5),
        'conv_w': nrm(7, (DEPTH, CONV_K, 3 * MIX_W), CONV_K ** -0.5),
        'lb_raw': nrm(8, (DEPTH, MIX_W), 0.5),
        'm_gate_bias': jnp.concatenate([i_bias, f_bias], axis=1),
        'gdn_a_log': jnp.log(jax.random.uniform(ks[11], (DEPTH, 2, hh), f32, 1.0, 16.0)),
        'gdn_dt_bias': dt + jnp.log(-jnp.expm1(-dt)),
        'mix_norm': 1.0 + nrm(13, (DEPTH, N_BRANCH, MIX_W), 0.02),
        'w_branch': nrm(14, (DEPTH, N_BRANCH, MIX_W, D_MODEL), DEEPNORM_BETA * MIX_W ** -0.5),
        'w_out': nrm(15, (DEPTH, D_MODEL, D_MODEL), DEEPNORM_BETA * D_MODEL ** -0.5),
        'ln_g': 1.0 + nrm(16, (DEPTH, 2, D_MODEL), 0.02),
        'ln_b': nrm(17, (DEPTH, 2, D_MODEL), 0.02),
        'ffn_w_gate': nrm(18, (N_DENSE, D_MODEL, FFN_DIM), D_MODEL ** -0.5),
        'ffn_w_up': nrm(19, (N_DENSE, D_MODEL, FFN_DIM), DEEPNORM_BETA * D_MODEL ** -0.5),
        'ffn_w_down': nrm(20, (N_DENSE, FFN_DIM, D_MODEL), DEEPNORM_BETA * FFN_DIM ** -0.5),
        'moe_router': nrm(21, (N_MOE, D_MODEL, N_EXPERTS), D_MODEL ** -0.5),
        'moe_w_gate': nrm(22, (N_MOE, N_EXPERTS, D_MODEL, FFN_DIM), D_MODEL ** -0.5),
        'moe_w_up': nrm(23, (N_MOE, N_EXPERTS, D_MODEL, FFN_DIM), DEEPNORM_BETA * D_MODEL ** -0.5),
        'moe_w_down': nrm(24, (N_MOE, N_EXPERTS, FFN_DIM, D_MODEL), DEEPNORM_BETA * FFN_DIM ** -0.5),
    }


def reference(x, c, ctx, c_ctx, w_ada, b_ada, w_in, conv_w, lb_raw, m_gate_bias, gdn_a_log, gdn_dt_bias, mix_norm, w_branch, w_out, ln_g, ln_b, ffn_w_gate, ffn_w_up, ffn_w_down, moe_router, moe_w_gate, moe_w_up, moe_w_down):
    p = jax.nn.softmax(lb_raw.astype(jnp.float32), axis=0)
    lower_bounds = jnp.cumsum(p, axis=0) - p[0]
    s_lat = jax.nn.silu(c)[:, None, :]
    s_ctx = jax.nn.silu(c_ctx)[None, None, :]
    ffn_w = (ffn_w_gate, ffn_w_up, ffn_w_down, moe_router, moe_w_gate, moe_w_up, moe_w_down)
    for l in range(DEPTH):
        mod_l = jnp.split(s_lat @ w_ada[l] + b_ada[l], 6, axis=-1)
        mod_c = jnp.split(s_ctx @ w_ada[l] + b_ada[l], 6, axis=-1)
        mix_p = (w_in[l], conv_w[l], lower_bounds[l], m_gate_bias[l], gdn_a_log[l], gdn_dt_bias[l])
        a_c, b_c, d_c, gates_c = _mixer_inputs(_modulate(ctx, mod_c[0], mod_c[1]), *mix_p)
        a_l, b_l, d_l, gates_l = _mixer_inputs(_modulate(x, mod_l[0], mod_l[1]), *mix_p)
        oa_c, oa_l = _bidir_scan(_gla_scan, a_c, a_l)
        ob_c, ob_l = _bidir_scan(_mlstm_scan, b_c, _map_dirs(_raster_to_colmajor, b_l))
        ob_l = _colmajor_to_raster(ob_l)
        od_c, od_l = _bidir_scan(_gdn_scan, d_c, d_l)
        out_p = (mix_norm[l], w_branch[l], w_out[l])
        x = _post_ln(DEEPNORM_ALPHA * x + mod_l[2] * _merge_branches((oa_l, ob_l, od_l), gates_l, *out_p), ln_g[l, 0], ln_b[l, 0])
        x = _post_ln(DEEPNORM_ALPHA * x + mod_l[5] * _channel_mixer(_modulate(x, mod_l[3], mod_l[4]), l, *ffn_w), ln_g[l, 1], ln_b[l, 1])
        if l < DEPTH - 1:
            ctx = _post_ln(DEEPNORM_ALPHA * ctx + mod_c[2] * _merge_branches((oa_c, ob_c, od_c), gates_c, *out_p), ln_g[l, 0], ln_b[l, 0])
            ctx = _post_ln(DEEPNORM_ALPHA * ctx + mod_c[5] * _channel_mixer(_modulate(ctx, mod_c[3], mod_c[4]), l, *ffn_w), ln_g[l, 1], ln_b[l, 1])
    return x
```

```python
import functools

import jax
import jax.numpy as jnp
from jax import lax
from jax.experimental import pallas as pl
from jax.experimental.pallas import tpu as pltpu

F32 = jnp.float32
BF16 = jnp.bfloat16
HIGHEST = lax.Precision.HIGHEST

HEAD_DIM = 128
N_HEADS = 4
MIX_W = N_HEADS * HEAD_DIM
CHUNK = 64
GRID_W = 64
SUB = 16
N_EXPERTS = 8
LN_EPS = 1e-5
RMS_EPS = 1e-6
L2_EPS = 1e-6
MASK_NEG = -1e30
LANE = 128

COL_MERGE = 0
COL_A = 3 * 1024
COL_B = COL_A + 5 * MIX_W
COL_CQKV = COL_B + 4 * MIX_W
COL_CZ = COL_CQKV + 3 * MIX_W
COL_GATES = COL_CZ + MIX_W
N_PROJ = COL_GATES + 4 * LANE


def _cparams(sem, vmem_mb=None):
    kw = dict(dimension_semantics=sem)
    if vmem_mb is not None:
        kw["vmem_limit_bytes"] = vmem_mb << 20
    return pltpu.CompilerParams(**kw)


def _dot(a, b, precision=None):
    return jnp.dot(a, b, preferred_element_type=F32, precision=precision)


def _dot_nt(a, b, precision=None):
    return lax.dot_general(a, b, (((1,), (1,)), ((), ())), preferred_element_type=F32, precision=precision)


def _dot_tn(a, b, precision=None):
    return lax.dot_general(a, b, (((0,), (0,)), ((), ())), preferred_element_type=F32, precision=precision)


def _ln(x):
    mu = jnp.mean(x, axis=-1, keepdims=True)
    xc = x - mu
    var = jnp.mean(xc * xc, axis=-1, keepdims=True)
    return xc * lax.rsqrt(var + LN_EPS)


def _sigmoid(x):
    return 1.0 / (1.0 + jnp.exp(-x))


def _silu(x):
    return x * _sigmoid(x)


def _softplus(x):
    return jnp.maximum(x, 0.0) + jnp.log(1.0 + jnp.exp(-jnp.abs(x)))


def _log_sigmoid(x):
    return -_softplus(-x)


def _order_mask(n, rev, strict=False):
    r = lax.broadcasted_iota(jnp.int32, (n, n), 0)
    c = lax.broadcasted_iota(jnp.int32, (n, n), 1)
    if rev:
        return (c > r) if strict else (c >= r)
    return (c < r) if strict else (c <= r)


def _cumsum_rows(x, rev):
    m = _order_mask(x.shape[0], rev).astype(F32)
    return _dot(m, x, precision=HIGHEST)


def _ada_kernel(c_ref, w_ref, b_ref, o_ref):
    s = _silu(c_ref[...])
    o_ref[0] = _dot(s, w_ref[0], precision=HIGHEST) + b_ref[0]


def _ada(cvec, w_ada, b_ada, tn=1536):
    depth, d, n6 = w_ada.shape
    rows = cvec.shape[0]
    return pl.pallas_call(
        _ada_kernel,
        grid=(depth, n6 // tn),
        in_specs=[pl.BlockSpec((rows, d), lambda l, n: (0, 0)),
                  pl.BlockSpec((1, d, tn), lambda l, n: (l, 0, n)),
                  pl.BlockSpec((1, 1, tn), lambda l, n: (l, 0, n))],
        out_specs=pl.BlockSpec((1, rows, tn), lambda l, n: (l, 0, n)),
        out_shape=jax.ShapeDtypeStruct((depth, rows, n6), F32),
        compiler_params=_cparams(("parallel", "parallel"), 40),
        name="ada",
    )(cvec, w_ada, b_ada.reshape(depth, 1, n6))


def _lower_bounds_kernel(lb_ref, o_ref):
    x = lb_ref[...]
    depth = x.shape[0]
    mx = jnp.max(x, axis=0, keepdims=True)
    e = jnp.exp(x - mx)
    p = e / jnp.sum(e, axis=0, keepdims=True)
    acc = jnp.zeros_like(p[0:1])
    for l in range(depth):
        o_ref[l:l + 1, :] = acc
        if l + 1 < depth:
            acc = acc + p[l + 1:l + 2]


def _lower_bounds(lb_raw):
    return pl.pallas_call(
        _lower_bounds_kernel,
        out_shape=jax.ShapeDtypeStruct(lb_raw.shape, F32),
        name="lower_bounds",
    )(lb_raw)


def _in_proj_kernel(x_ref, mod_ref, w_ref, o_ref, h_ref):
    @pl.when(pl.program_id(2) == 0)
    def _():
        shift = mod_ref[0, 0:1, :]
        scale = mod_ref[0, 1:2, :]
        h_ref[...] = (_ln(x_ref[0]) * (1.0 + scale) + shift).astype(BF16)

    o_ref[0] = _dot(h_ref[...], w_ref[...])


def _in_proj(x, mod, w, tl, tn=2048):
    b, l, d = x.shape
    n_proj = w.shape[1]
    return pl.pallas_call(
        _in_proj_kernel,
        grid=(b, l // tl, n_proj // tn),
        in_specs=[pl.BlockSpec((1, tl, d), lambda bi, i, n: (bi, i, 0)),
                  pl.BlockSpec((1, 6, d), lambda bi, i, n: (bi, 0, 0)),
                  pl.BlockSpec((d, tn), lambda bi, i, n: (0, n))],
        out_specs=pl.BlockSpec((1, tl, tn), lambda bi, i, n: (bi, i, n)),
        out_shape=jax.ShapeDtypeStruct((b, l, n_proj), F32),
        scratch_shapes=[pltpu.VMEM((tl, d), BF16)],
        compiler_params=_cparams(("parallel", "parallel", "arbitrary"), 48),
        name="in_proj",
    )(x, mod, w)


def _conv_kernel(x_ref, prev_ref, next_ref, w_ref, o_ref):
    i = pl.program_id(1)
    last = pl.num_programs(1) - 1
    x = x_ref[0]
    tl = x.shape[0]
    row = lax.broadcasted_iota(jnp.int32, x.shape, 0)
    before = jnp.where(i > 0, prev_ref[0, 7:8, :], 0.0)
    after = jnp.where(i < last, next_ref[0, 0:1, :], 0.0)
    x_dn = jnp.where(row == 0, before, pltpu.roll(x, 1, axis=0))
    x_up = jnp.where(row == tl - 1, after, pltpu.roll(x, tl - 1, axis=0))
    y = _silu(x_dn * w_ref[0:1, :] + x * w_ref[1:2, :] + x_up * w_ref[2:3, :])
    for j in range(3 * N_HEADS):
        t = y[:, j * HEAD_DIM:(j + 1) * HEAD_DIM]
        if j < 2 * N_HEADS:
            t = t * lax.rsqrt(jnp.sum(t * t, axis=-1, keepdims=True) + L2_EPS)
            if j < N_HEADS:
                t = t * HEAD_DIM ** -0.5
        o_ref[0, :, j * HEAD_DIM:(j + 1) * HEAD_DIM] = t


def _gdn_conv(proj, conv_w, tl):
    b, l, _ = proj.shape
    w3 = 3 * MIX_W
    cb = COL_CQKV // w3
    nb8 = l // 8
    r8 = tl // 8
    return pl.pallas_call(
        _conv_kernel,
        grid=(b, l // tl),
        in_specs=[pl.BlockSpec((1, tl, w3), lambda bi, i: (bi, i, cb)),
                  pl.BlockSpec((1, 8, w3), lambda bi, i: (bi, jnp.maximum(i * r8 - 1, 0), cb)),
                  pl.BlockSpec((1, 8, w3), lambda bi, i: (bi, jnp.minimum((i + 1) * r8, nb8 - 1), cb)),
                  pl.BlockSpec((3, w3), lambda bi, i: (0, 0))],
        out_specs=pl.BlockSpec((1, tl, w3), lambda bi, i: (bi, i, 0)),
        out_shape=jax.ShapeDtypeStruct((b, l, w3), F32),
        compiler_params=_cparams(("parallel", "parallel"), 40),
        name="gdn_conv",
    )(proj, proj, proj, conv_w)


def _hgrn_chunk(q, fpre, v, lb, s_t, emat, rev):
    c = q.shape[0]
    nb = c // SUB
    lf = jnp.log(lb + (1.0 - lb) * _sigmoid(fpre))
    k = (1.0 - lb) * _sigmoid(-fpre)
    g = _cumsum_rows(lf, rev)

    rows = []
    for j in range(nb):
        r0 = j * SUB
        gj = g[r0:r0 + SUB]
        qj = q[r0:r0 + SUB]
        pieces = []
        for s in range(SUB):
            gs = g[r0 + s:r0 + s + 1]
            ks = k[r0 + s:r0 + s + 1]
            pieces.append((jnp.exp(jnp.minimum(gj - gs, 0.0)) * (qj * ks)).astype(BF16))
        rows.append(jnp.concatenate(pieces, axis=1))
    a_diag = _dot(jnp.concatenate(rows, axis=0), emat)

    offs = []
    for i in range(nb):
        r0 = i * SUB
        if (not rev and i == 0) or (rev and i == nb - 1):
            offs.append(jnp.zeros((SUB, c), F32))
            continue
        bi = g[r0 + SUB:r0 + SUB + 1] if rev else g[r0 - 1:r0]
        lhs = (q[r0:r0 + SUB] * jnp.exp(g[r0:r0 + SUB] - bi)).astype(BF16)
        rhs = (k * jnp.exp(jnp.minimum(bi - g, 0.0))).astype(BF16)
        offs.append(_dot_nt(lhs, rhs))
    a_off = jnp.concatenate(offs, axis=0)

    r = lax.broadcasted_iota(jnp.int32, (c, c), 0)
    col = lax.broadcasted_iota(jnp.int32, (c, c), 1)
    shift = SUB.bit_length() - 1
    rb = lax.shift_right_logical(r, shift)
    cb = lax.shift_right_logical(col, shift)
    order = (col >= r) if rev else (col <= r)
    off_side = (cb > rb) if rev else (cb < rb)
    a = jnp.where((rb == cb) & order, a_diag, jnp.where(off_side, a_off, 0.0))

    g_end = g[0:1] if rev else g[c - 1:c]
    vb = v.astype(BF16)
    o = _dot(a.astype(BF16), vb) + _dot_nt((q * jnp.exp(g)).astype(BF16), s_t.astype(BF16))
    kd = (k * jnp.exp(g_end - g)).astype(BF16)
    s_new = s_t * jnp.exp(g_end) + _dot_tn(vb, kd)
    return o, s_new


def _hgrn_kernel(qf_ref, ff_ref, vf_ref, qb_ref, fb_ref, vb_ref, lb_ref, e_ref, s0_ref,
                 of_ref, ob_ref, sout_ref, s_ref):
    n = pl.program_id(2)

    @pl.when(n == 0)
    def _():
        s_ref[...] = s0_ref[0, 0]

    lb = lb_ref[...]
    emat = e_ref[...]
    o, s_new = _hgrn_chunk(qf_ref[0], ff_ref[0], vf_ref[0], lb, s_ref[0], emat, rev=False)
    of_ref[0] = o
    s_ref[0] = s_new
    o, s_new = _hgrn_chunk(qb_ref[0], fb_ref[0], vb_ref[0], lb, s_ref[1], emat, rev=True)
    ob_ref[0] = o
    s_ref[1] = s_new

    @pl.when(n == pl.num_programs(2) - 1)
    def _():
        sout_ref[0, 0] = s_ref[...]


def _hgrn_emat():
    r = jnp.arange(SUB * HEAD_DIM)[:, None] // HEAD_DIM
    c = jnp.arange(CHUNK)[None, :] % SUB
    return (r == c).astype(BF16)


def _hgrn_scan(proj, lb, s0):
    b, l, _ = proj.shape
    nc = l // CHUNK
    ca = COL_A // LANE
    hpb = MIX_W // LANE

    def fwd(col):
        return pl.BlockSpec((1, CHUNK, LANE), lambda bi, h, n: (bi, n, ca + col * hpb + h))

    def bwd(col):
        return pl.BlockSpec((1, CHUNK, LANE), lambda bi, h, n: (bi, nc - 1 - n, ca + col * hpb + h))

    st_spec = pl.BlockSpec((1, 1, 2, HEAD_DIM, HEAD_DIM), lambda bi, h, n: (bi, h, 0, 0, 0))
    return pl.pallas_call(
        _hgrn_kernel,
        grid=(b, N_HEADS, nc),
        in_specs=[fwd(0), fwd(1), fwd(3), bwd(0), bwd(2), bwd(3),
                  pl.BlockSpec((1, LANE), lambda bi, h, n: (0, h)),
                  pl.BlockSpec((SUB * HEAD_DIM, CHUNK), lambda bi, h, n: (0, 0)),
                  st_spec],
        out_specs=[pl.BlockSpec((1, CHUNK, LANE), lambda bi, h, n: (bi, n, h)),
                   pl.BlockSpec((1, CHUNK, LANE), lambda bi, h, n: (bi, nc - 1 - n, h)),
                   st_spec],
        out_shape=[jax.ShapeDtypeStruct((b, l, MIX_W), F32),
                   jax.ShapeDtypeStruct((b, l, MIX_W), F32),
                   jax.ShapeDtypeStruct(s0.shape, F32)],
        scratch_shapes=[pltpu.VMEM((2, HEAD_DIM, HEAD_DIM), F32)],
        compiler_params=_cparams(("parallel", "parallel", "arbitrary"), 40),
        name="hgrn_scan",
    )(proj, proj, proj, proj, proj, proj, lb, _hgrn_emat(), s0)


def _mlstm_chunk(q, k, v, bb_c, bb_r, ic_c, ic_r, cm, nvec, m, rev):
    c = q.shape[0]
    tri = _order_mask(c, rev)
    log_d = jnp.where(tri, bb_c - bb_r + ic_r, MASK_NEG)
    log_inter = bb_c + m
    m_t = jnp.maximum(jnp.max(log_d, axis=1, keepdims=True), log_inter)
    dmat = jnp.where(tri, jnp.exp(log_d - m_t), 0.0)
    inter = jnp.exp(log_inter - m_t)
    qb, kb, vb = q.astype(BF16), k.astype(BF16), v.astype(BF16)
    s = _dot_nt(qb, kb) * dmat
    num = _dot(s.astype(BF16), vb) + inter * _dot(qb, cm.astype(BF16))
    den = jnp.sum(s, axis=1, keepdims=True) + inter * jnp.sum(q * nvec, axis=1, keepdims=True)
    hh = num / jnp.maximum(jnp.abs(den), jnp.exp(-m_t))
    m_new = m_t[0:1] if rev else m_t[c - 1:c]
    bb_end = bb_c[0:1] if rev else bb_c[c - 1:c]
    w = jnp.exp(bb_end - bb_c + ic_c - m_new)
    decay = jnp.exp(bb_end + m - m_new)
    kw = k * w
    cm_new = decay * cm + _dot_tn(kw.astype(BF16), vb)
    n_new = decay * nvec + jnp.sum(kw, axis=0, keepdims=True)
    return hh, cm_new, n_new, m_new


def _mlstm_kernel(qf_ref, kf_ref, vf_ref, gf_ref, qb_ref, kb_ref, vb_ref, gb_ref, bias_ref, cm0_ref, nm0_ref,
                  of_ref, ob_ref, cmout_ref, nmout_ref, cm_ref, nm_ref):
    n = pl.program_id(1)

    @pl.when(n == 0)
    def _():
        cm_ref[...] = cm0_ref[0]
        nm_ref[...] = nm0_ref[0]

    dirs = ((qf_ref, kf_ref, vf_ref, gf_ref, of_ref), (qb_ref, kb_ref, vb_ref, gb_ref, ob_ref))
    for d, (q_ref, k_ref, v_ref, g_ref, o_ref) in enumerate(dirs):
        rev = d == 1
        gates = g_ref[0] + bias_ref[...]
        cs = _cumsum_rows(_log_sigmoid(gates), rev)
        cs_t = cs.T
        gates_t = gates.T
        for h in range(N_HEADS):
            ci = 4 * d + h
            cf = 8 + 4 * d + h
            sl = slice(h * HEAD_DIM, (h + 1) * HEAD_DIM)
            hh, cm_new, n_new, m_new = _mlstm_chunk(
                q_ref[0, :, sl], k_ref[0, :, sl] * HEAD_DIM ** -0.5, v_ref[0, :, sl],
                cs[:, cf:cf + 1], cs_t[cf:cf + 1, :], gates[:, ci:ci + 1], gates_t[ci:ci + 1, :],
                cm_ref[d, h], nm_ref[d, h, 0:1, :], nm_ref[d, h, 1:2, 0:1], rev)
            o_ref[0, :, sl] = hh
            cm_ref[d, h] = cm_new
            nm_ref[d, h, 0:1, :] = n_new
            nm_ref[d, h, 1:2, :] = jnp.broadcast_to(m_new, (1, HEAD_DIM))

    @pl.when(n == pl.num_programs(1) - 1)
    def _():
        cmout_ref[0] = cm_ref[...]
        nmout_ref[0] = nm_ref[...]


def _mlstm_scan(proj, gbias, cm0, nm0, colmajor):
    b, l, n_proj = proj.shape
    nc = l // CHUNK
    u512 = n_proj // MIX_W
    u128 = n_proj // LANE
    cb512 = COL_B // MIX_W
    cg128 = COL_GATES // LANE
    if colmajor:
        assert l == CHUNK * GRID_W
        src = proj.reshape(b, CHUNK, GRID_W * n_proj)
        out_view = (b, CHUNK, GRID_W * MIX_W)

        def in_map(unit, nunits, col, flip):
            return lambda bi, n: (bi, 0, (nc - 1 - n if flip else n) * nunits + col)

        def out_map(flip):
            return lambda bi, n: (bi, 0, nc - 1 - n if flip else n)
    else:
        src = proj
        out_view = (b, l, MIX_W)

        def in_map(unit, nunits, col, flip):
            return lambda bi, n: (bi, nc - 1 - n if flip else n, col)

        def out_map(flip):
            return lambda bi, n: (bi, nc - 1 - n if flip else n, 0)

    def specs(flip):
        return [pl.BlockSpec((1, CHUNK, MIX_W), in_map(MIX_W, u512, cb512 + j, flip)) for j in range(3)] + \
               [pl.BlockSpec((1, CHUNK, LANE), in_map(LANE, u128, cg128, flip))]

    cm_spec = pl.BlockSpec((1, 2, N_HEADS, HEAD_DIM, HEAD_DIM), lambda bi, n: (bi, 0, 0, 0, 0))
    nm_spec = pl.BlockSpec((1, 2, N_HEADS, 8, HEAD_DIM), lambda bi, n: (bi, 0, 0, 0, 0))
    o_f, o_b, cm, nm = pl.pallas_call(
        _mlstm_kernel,
        grid=(b, nc),
        in_specs=specs(False) + specs(True) + [pl.BlockSpec((1, LANE), lambda bi, n: (0, 0)), cm_spec, nm_spec],
        out_specs=[pl.BlockSpec((1, CHUNK, MIX_W), out_map(False)),
                   pl.BlockSpec((1, CHUNK, MIX_W), out_map(True)),
                   cm_spec, nm_spec],
        out_shape=[jax.ShapeDtypeStruct(out_view, F32), jax.ShapeDtypeStruct(out_view, F32),
                   jax.ShapeDtypeStruct(cm0.shape, F32), jax.ShapeDtypeStruct(nm0.shape, F32)],
        scratch_shapes=[pltpu.VMEM((2, N_HEADS, HEAD_DIM, HEAD_DIM), F32),
                        pltpu.VMEM((2, N_HEADS, 8, HEAD_DIM), F32)],
        compiler_params=_cparams(("parallel", "arbitrary"), 40),
        name="mlstm_scan",
    )(*([src] * 8), gbias, cm0, nm0)
    return o_f.reshape(b, l, MIX_W), o_b.reshape(b, l, MIX_W), cm, nm


def _inv_unit_triangular(a):
    n = a.shape[0]
    eye = (lax.broadcasted_iota(jnp.int32, (n, n), 0) == lax.broadcasted_iota(jnp.int32, (n, n), 1)).astype(F32)
    x = eye - a
    p = a
    steps = (n - 1).bit_length() - 1
    for _ in range(steps):
        p = _dot(p, p, precision=HIGHEST)
        x = x + _dot(x, p, precision=HIGHEST)
    return x


def _gdn_chunk(q, k, v, gam_c, gam_r, beta, s, rev):
    c = q.shape[0]
    tri = _order_mask(c, rev)
    strict = _order_mask(c, rev, strict=True)
    ldec = jnp.where(tri, jnp.exp(jnp.minimum(gam_c - gam_r, 0.0)), 0.0)
    kbeta = k * beta
    a = jnp.where(strict, _dot_nt(kbeta, k, precision=HIGHEST) * ldec, 0.0)
    t_inv = _inv_unit_triangular(a)
    eg = jnp.exp(gam_c)
    sol = _dot(t_inv, jnp.concatenate([v * beta, kbeta * eg], axis=1), precision=HIGHEST)
    u = sol[:, :HEAD_DIM]
    w = sol[:, HEAD_DIM:]
    qk = jnp.where(tri, _dot_nt(q.astype(BF16), k.astype(BF16)) * ldec, 0.0)
    g_end = gam_c[0:1] if rev else gam_c[c - 1:c]
    kg = k * jnp.exp(g_end - gam_c)
    sb = s.astype(BF16)
    v_new = u - _dot(w.astype(BF16), sb)
    vnb = v_new.astype(BF16)
    o = _dot((q * eg).astype(BF16), sb) + _dot(qk.astype(BF16), vnb)
    s_new = jnp.exp(g_end) * s + _dot_tn(kg.astype(BF16), vnb)
    return o, s_new


def _gdn_kernel(xf_ref, gf_ref, xb_ref, gb_ref, alog_ref, dtb_ref, s0_ref,
                of_ref, ob_ref, sout_ref, s_ref):
    n = pl.program_id(1)

    @pl.when(n == 0)
    def _():
        s_ref[...] = s0_ref[0]

    neg_a = -jnp.exp(alog_ref[...])
    for d, (x_ref, g_ref, o_ref) in enumerate(((xf_ref, gf_ref, of_ref), (xb_ref, gb_ref, ob_ref))):
        rev = d == 1
        gates = g_ref[0]
        cs = _cumsum_rows(neg_a * _softplus(gates + dtb_ref[...]), rev)
        cs_t = cs.T
        betas = _sigmoid(gates)
        for h in range(N_HEADS):
            ca = 16 + 4 * d + h
            cbeta = 24 + 4 * d + h
            o, s_new = _gdn_chunk(
                x_ref[0, :, h * HEAD_DIM:(h + 1) * HEAD_DIM],
                x_ref[0, :, MIX_W + h * HEAD_DIM:MIX_W + (h + 1) * HEAD_DIM],
                x_ref[0, :, 2 * MIX_W + h * HEAD_DIM:2 * MIX_W + (h + 1) * HEAD_DIM],
                cs[:, ca:ca + 1], cs_t[ca:ca + 1, :], betas[:, cbeta:cbeta + 1], s_ref[d, h], rev)
            o_ref[0, :, h * HEAD_DIM:(h + 1) * HEAD_DIM] = o
            s_ref[d, h] = s_new

    @pl.when(n == pl.num_programs(1) - 1)
    def _():
        sout_ref[0] = s_ref[...]


def _gdn_scan(qkv, proj, alog_row, dtb_row, s0):
    b, l, w3 = qkv.shape
    nc = l // CHUNK
    cg = COL_GATES // LANE
    st_spec = pl.BlockSpec((1, 2, N_HEADS, HEAD_DIM, HEAD_DIM), lambda bi, n: (bi, 0, 0, 0, 0))
    row_spec = pl.BlockSpec((1, LANE), lambda bi, n: (0, 0))
    return pl.pallas_call(
        _gdn_kernel,
        grid=(b, nc),
        in_specs=[pl.BlockSpec((1, CHUNK, w3), lambda bi, n: (bi, n, 0)),
                  pl.BlockSpec((1, CHUNK, LANE), lambda bi, n: (bi, n, cg)),
                  pl.BlockSpec((1, CHUNK, w3), lambda bi, n: (bi, nc - 1 - n, 0)),
                  pl.BlockSpec((1, CHUNK, LANE), lambda bi, n: (bi, nc - 1 - n, cg)),
                  row_spec, row_spec, st_spec],
        out_specs=[pl.BlockSpec((1, CHUNK, MIX_W), lambda bi, n: (bi, n, 0)),
                   pl.BlockSpec((1, CHUNK, MIX_W), lambda bi, n: (bi, nc - 1 - n, 0)),
                   st_spec],
        out_shape=[jax.ShapeDtypeStruct((b, l, MIX_W), F32), jax.ShapeDtypeStruct((b, l, MIX_W), F32),
                   jax.ShapeDtypeStruct(s0.shape, F32)],
        scratch_shapes=[pltpu.VMEM((2, N_HEADS, HEAD_DIM, HEAD_DIM), F32)],
        compiler_params=_cparams(("parallel", "arbitrary"), 40),
        name="gdn_scan",
    )(qkv, proj, qkv, proj, alog_row, dtb_row, s0)


def _merge_kernel(oaf, oab, obf, obb, ocf, ocb, ag, bo, cz, mg, x_ref, mod_ref, gain_ref, wbr_ref, wout_ref,
                  lng_ref, lnb_ref, out_ref, *, alpha):
    d_model = x_ref.shape[-1]
    branches = ((oaf, oab, ag, _silu), (obf, obb, bo, _sigmoid), (ocf, ocb, cz, _silu))
    mixed = None
    for nbr, (of_ref, ob_ref, gate_ref, act) in enumerate(branches):
        o = of_ref[0] + ob_ref[0]
        parts = []
        for h in range(N_HEADS):
            oh = o[:, h * HEAD_DIM:(h + 1) * HEAD_DIM]
            parts.append(oh * lax.rsqrt(jnp.mean(oh * oh, axis=-1, keepdims=True) + RMS_EPS))
        y = jnp.concatenate(parts, axis=1) * gain_ref[nbr:nbr + 1, :] * act(gate_ref[0])
        term = _sigmoid(mg[0, :, nbr * d_model:(nbr + 1) * d_model]) * _dot(y.astype(BF16), wbr_ref[nbr])
        mixed = term if mixed is None else mixed + term
    sub = _dot(mixed.astype(BF16), wout_ref[...])
    z = alpha * x_ref[0] + mod_ref[0, 2:3, :] * sub
    out_ref[0] = _ln(z) * lng_ref[...] + lnb_ref[...]


def _merge(outs, proj, x, mod, gain, w_branch, w_out, ln_g, ln_b, alpha, tl):
    b, l, d = x.shape
    u = MIX_W

    def tok(width, col):
        return pl.BlockSpec((1, tl, width), lambda bi, i: (bi, i, col))

    def const(shape):
        return pl.BlockSpec(shape, lambda bi, i: (0,) * len(shape))

    return pl.pallas_call(
        functools.partial(_merge_kernel, alpha=alpha),
        grid=(b, l // tl),
        in_specs=[tok(u, 0)] * 6 + [
            tok(u, (COL_A + 4 * MIX_W) // u), tok(u, (COL_B + 3 * MIX_W) // u), tok(u, COL_CZ // u),
            tok(3 * d, COL_MERGE // (3 * d)), tok(d, 0),
            pl.BlockSpec((1, 6, d), lambda bi, i: (bi, 0, 0)),
            const((3, u)), const((3, u, d)), const((d, d)), const((1, d)), const((1, d))],
        out_specs=tok(d, 0),
        out_shape=jax.ShapeDtypeStruct((b, l, d), F32),
        compiler_params=_cparams(("parallel", "parallel"), 48),
        name="merge",
    )(*outs, proj, proj, proj, proj, x, mod, gain, w_branch, w_out, ln_g, ln_b)


def _ffn_kernel(x_ref, mod_ref, wg_ref, wu_ref, wd_ref, lng_ref, lnb_ref, out_ref, h_ref, acc_ref, *, alpha):
    f = pl.program_id(2)

    @pl.when(f == 0)
    def _():
        h_ref[...] = (_ln(x_ref[0]) * (1.0 + mod_ref[0, 4:5, :]) + mod_ref[0, 3:4, :]).astype(BF16)
        acc_ref[...] = jnp.zeros_like(acc_ref)

    h = h_ref[...]
    act = _silu(_dot(h, wg_ref[...])) * _dot(h, wu_ref[...])
    acc_ref[...] += _dot(act.astype(BF16), wd_ref[...])

    @pl.when(f == pl.num_programs(2) - 1)
    def _():
        z = alpha * x_ref[0] + mod_ref[0, 5:6, :] * acc_ref[...]
        out_ref[0] = _ln(z) * lng_ref[...] + lnb_ref[...]


def _ffn(x, mod, w_gate, w_up, w_down, ln_g, ln_b, alpha, tl, tf=512):
    b, l, d = x.shape
    f = w_gate.shape[1]
    return pl.pallas_call(
        functools.partial(_ffn_kernel, alpha=alpha),
        grid=(b, l // tl, f // tf),
        in_specs=[pl.BlockSpec((1, tl, d), lambda bi, i, j: (bi, i, 0)),
                  pl.BlockSpec((1, 6, d), lambda bi, i, j: (bi, 0, 0)),
                  pl.BlockSpec((d, tf), lambda bi, i, j: (0, j)),
                  pl.BlockSpec((d, tf), lambda bi, i, j: (0, j)),
                  pl.BlockSpec((tf, d), lambda bi, i, j: (j, 0)),
                  pl.BlockSpec((1, d), lambda bi, i, j: (0, 0)),
                  pl.BlockSpec((1, d), lambda bi, i, j: (0, 0))],
        out_specs=pl.BlockSpec((1, tl, d), lambda bi, i, j: (bi, i, 0)),
        out_shape=jax.ShapeDtypeStruct((b, l, d), F32),
        scratch_shapes=[pltpu.VMEM((tl, d), BF16), pltpu.VMEM((tl, d), F32)],
        compiler_params=_cparams(("parallel", "parallel", "arbitrary"), 48),
        name="ffn",
    )(x, mod, w_gate, w_up, w_down, ln_g, ln_b)


def _top2_combine(logits):
    lane = lax.broadcasted_iota(jnp.int32, logits.shape, 1).astype(F32)
    neg = -jnp.inf
    lg = jnp.where(lane < N_EXPERTS, logits, neg)
    m1 = jnp.max(lg, axis=1, keepdims=True)
    i1 = jnp.min(jnp.where(lg == m1, lane, float(LANE)), axis=1, keepdims=True)
    lg2 = jnp.where(lane == i1, neg, lg)
    m2 = jnp.max(lg2, axis=1, keepdims=True)
    i2 = jnp.min(jnp.where(lg2 == m2, lane, float(LANE)), axis=1, keepdims=True)
    e2 = jnp.exp(m2 - m1)
    p1 = 1.0 / (1.0 + e2)
    return jnp.where(lane == i1, p1, jnp.where(lane == i2, e2 * p1, 0.0))


def _moe_kernel(x_ref, mod_ref, wr_ref, wg_ref, wu_ref, wd_ref, lng_ref, lnb_ref, out_ref,
                h_ref, acc_ref, comb_ref, *, alpha):
    e = pl.program_id(2)
    f = pl.program_id(3)

    @pl.when((e == 0) & (f == 0))
    def _():
        h = _ln(x_ref[0]) * (1.0 + mod_ref[0, 4:5, :]) + mod_ref[0, 3:4, :]
        h_ref[...] = h.astype(BF16)
        comb_ref[...] = _top2_combine(_dot(h, wr_ref[...], precision=HIGHEST))
        acc_ref[...] = jnp.zeros_like(acc_ref)

    lane = lax.broadcasted_iota(jnp.int32, comb_ref.shape, 1)
    weight = jnp.sum(jnp.where(lane == e, comb_ref[...], 0.0), axis=1, keepdims=True)
    h = h_ref[...]
    act = _silu(_dot(h, wg_ref[0])) * _dot(h, wu_ref[0]) * weight
    acc_ref[...] += _dot(act.astype(BF16), wd_ref[0])

    @pl.when((e == pl.num_programs(2) - 1) & (f == pl.num_programs(3) - 1))
    def _():
        z = alpha * x_ref[0] + mod_ref[0, 5:6, :] * acc_ref[...]
        out_ref[0] = _ln(z) * lng_ref[...] + lnb_ref[...]


def _moe(x, mod, w_router, w_gate, w_up, w_down, ln_g, ln_b, alpha, tl, tf=512):
    b, l, d = x.shape
    ne, _, f = w_gate.shape
    return pl.pallas_call(
        functools.partial(_moe_kernel, alpha=alpha),
        grid=(b, l // tl, ne, f // tf),
        in_specs=[pl.BlockSpec((1, tl, d), lambda bi, i, e, j: (bi, i, 0)),
                  pl.BlockSpec((1, 6, d), lambda bi, i, e, j: (bi, 0, 0)),
                  pl.BlockSpec((d, LANE), lambda bi, i, e, j: (0, 0)),
                  pl.BlockSpec((1, d, tf), lambda bi, i, e, j: (e, 0, j)),
                  pl.BlockSpec((1, d, tf), lambda bi, i, e, j: (e, 0, j)),
                  pl.BlockSpec((1, tf, d), lambda bi, i, e, j: (e, j, 0)),
                  pl.BlockSpec((1, d), lambda bi, i, e, j: (0, 0)),
                  pl.BlockSpec((1, d), lambda bi, i, e, j: (0, 0))],
        out_specs=pl.BlockSpec((1, tl, d), lambda bi, i, e, j: (bi, i, 0)),
        out_shape=jax.ShapeDtypeStruct((b, l, d), F32),
        scratch_shapes=[pltpu.VMEM((tl, d), BF16), pltpu.VMEM((tl, d), F32), pltpu.VMEM((tl, LANE), F32)],
        compiler_params=_cparams(("parallel", "parallel", "arbitrary", "arbitrary"), 48),
        name="moe",
    )(x, mod, w_router, w_gate, w_up, w_down, ln_g, ln_b)


def _reorder_w_in(w):
    d = w.shape[0]
    o_a, o_b = 0, 5 * MIX_W
    o_bgt = o_b + 4 * MIX_W
    o_cqkv = o_bgt + 16
    o_cz = o_cqkv + 3 * MIX_W
    o_cgt = o_cz + MIX_W
    o_merge = o_cgt + 16
    parts = [w[:, o_merge:o_merge + 3 * d], w[:, o_a:o_bgt], w[:, o_cqkv:o_cgt],
             w[:, o_bgt:o_bgt + 16], w[:, o_cgt:o_cgt + 16],
             jnp.zeros((d, N_PROJ - COL_GATES - 32), w.dtype)]
    return jnp.concatenate(parts, axis=1)


def _pad_row(vals, offset):
    row = jnp.zeros((1, LANE), F32)
    return lax.dynamic_update_slice(row, vals.reshape(1, -1).astype(F32), (0, offset))


def _tile(l, pref):
    return pref if l % pref == 0 else l


def kernel(x, c, ctx, c_ctx, w_ada, b_ada, w_in, conv_w, lb_raw, m_gate_bias, gdn_a_log, gdn_dt_bias, mix_norm,
           w_branch, w_out, ln_g, ln_b, ffn_w_gate, ffn_w_up, ffn_w_down, moe_router, moe_w_gate, moe_w_up,
           moe_w_down):
    depth, d_model = w_in.shape[0], w_in.shape[1]
    b, l_lat, _ = x.shape
    l_ctx = ctx.shape[1]
    alpha = float((2 * depth) ** 0.25)
    assert d_model == 1024 and w_in.shape[2] == 9 * MIX_W + 3 * MIX_W + MIX_W + 32 + 3 * d_model

    lower = _lower_bounds(lb_raw.astype(F32))
    n_rows = -(-(b + 1) // 8) * 8
    cvec = jnp.concatenate([c, c_ctx[None, :], jnp.zeros((n_rows - b - 1, d_model), F32)], axis=0)
    mods = _ada(cvec, w_ada, b_ada)

    zeros_state = jnp.zeros((b, 2, N_HEADS, HEAD_DIM, HEAD_DIM), F32)
    zeros_hgrn = jnp.zeros((b, N_HEADS, 2, HEAD_DIM, HEAD_DIM), F32)
    zeros_nm = jnp.zeros((b, 2, N_HEADS, 8, HEAD_DIM), F32)

    for l in range(depth):
        mod_l = mods[l, :b].reshape(b, 6, d_model)
        mod_c = jnp.broadcast_to(mods[l, b].reshape(1, 6, d_model), (b, 6, d_model))
        w_in_l = _reorder_w_in(w_in[l]).astype(BF16)
        lb_row = lower[l].reshape(1, MIX_W)
        gbias = _pad_row(m_gate_bias[l], 0)
        alog_row = _pad_row(gdn_a_log[l], 16)
        dtb_row = _pad_row(gdn_dt_bias[l], 16)
        gain = mix_norm[l]
        wbr = w_branch[l].astype(BF16)
        wo = w_out[l].astype(BF16)
        g0, b0 = ln_g[l, 0].reshape(1, -1), ln_b[l, 0].reshape(1, -1)
        g1, b1 = ln_g[l, 1].reshape(1, -1), ln_b[l, 1].reshape(1, -1)

        p_c = _in_proj(ctx, mod_c, w_in_l, _tile(l_ctx, 256))
        p_l = _in_proj(x, mod_l, w_in_l, _tile(l_lat, 1024))
        qkv_c = _gdn_conv(p_c, conv_w[l], _tile(l_ctx, 256))
        qkv_l = _gdn_conv(p_l, conv_w[l], _tile(l_lat, 512))

        oa_c = _hgrn_scan(p_c, lb_row, zeros_hgrn)
        oa_l = _hgrn_scan(p_l, lb_row, oa_c[2])
        ob_c = _mlstm_scan(p_c, gbias, zeros_state, zeros_nm, colmajor=False)
        ob_l = _mlstm_scan(p_l, gbias, ob_c[2], ob_c[3], colmajor=True)
        oc_c = _gdn_scan(qkv_c, p_c, alog_row, dtb_row, zeros_state)
        oc_l = _gdn_scan(qkv_l, p_l, alog_row, dtb_row, oc_c[2])

        def channel_mixer(t, mod, tl):
            i = l // 2
            if l % 2 == 0:
                return _ffn(t, mod, ffn_w_gate[i].astype(BF16), ffn_w_up[i].astype(BF16),
                            ffn_w_down[i].astype(BF16), g1, b1, alpha, tl)
            w_r = jnp.concatenate([moe_router[i], jnp.zeros((d_model, LANE - N_EXPERTS), F32)], axis=1)
            return _moe(t, mod, w_r, moe_w_gate[i].astype(BF16), moe_w_up[i].astype(BF16),
                        moe_w_down[i].astype(BF16), g1, b1, alpha, tl)

        outs_l = (oa_l[0], oa_l[1], ob_l[0], ob_l[1], oc_l[0], oc_l[1])
        x = _merge(outs_l, p_l, x, mod_l, gain, wbr, wo, g0, b0, alpha, _tile(l_lat, 256))
        x = channel_mixer(x, mod_l, _tile(l_lat, 1024))
        if l < depth - 1:
            outs_c = (oa_c[0], oa_c[1], ob_c[0], ob_c[1], oc_c[0], oc_c[1])
            ctx = _merge(outs_c, p_c, ctx, mod_c, gain, wbr, wo, g0, b0, alpha, _tile(l_ctx, 256))
            ctx = channel_mixer(ctx, mod_c, _tile(l_ctx, 256))
    return x
```

```python
import functools

import jax
import jax.numpy as jnp
from jax import lax
from jax.experimental import pallas as pl
from jax.experimental.pallas import tpu as pltpu

F32 = jnp.float32
BF16 = jnp.bfloat16
HIGHEST = lax.Precision.HIGHEST

HEAD_DIM = 128
N_HEADS = 4
MIX_W = N_HEADS * HEAD_DIM
CHUNK = 64
GRID_W = 64
SUB = 16
N_EXPERTS = 8
LN_EPS = 1e-5
RMS_EPS = 1e-6
L2_EPS = 1e-6
MASK_NEG = -1e30
LANE = 128

COL_MERGE = 0
COL_A = 3 * 1024
COL_B = COL_A + 5 * MIX_W
COL_CQKV = COL_B + 4 * MIX_W
COL_CZ = COL_CQKV + 3 * MIX_W
COL_GATES = COL_CZ + MIX_W
N_PROJ = COL_GATES + 4 * LANE


def _cparams(sem, vmem_mb=None):
    kw = dict(dimension_semantics=sem)
    if vmem_mb is not None:
        kw["vmem_limit_bytes"] = vmem_mb << 20
    return pltpu.CompilerParams(**kw)


def _dot(a, b, precision=None):
    return jnp.dot(a, b, preferred_element_type=F32, precision=precision)


def _dot_nt(a, b, precision=None):
    return lax.dot_general(a, b, (((1,), (1,)), ((), ())), preferred_element_type=F32, precision=precision)


def _dot_tn(a, b):
    return lax.dot_general(a, b, (((0,), (0,)), ((), ())), preferred_element_type=F32)


def _ln(x):
    mu = jnp.mean(x, axis=-1, keepdims=True)
    xc = x - mu
    var = jnp.mean(xc * xc, axis=-1, keepdims=True)
    return xc * lax.rsqrt(var + LN_EPS)


def _sigmoid(x):
    return 1.0 / (1.0 + jnp.exp(-x))


def _silu(x):
    return x * _sigmoid(x)


def _softplus(x):
    return jnp.maximum(x, 0.0) + jnp.log(1.0 + jnp.exp(-jnp.abs(x)))


def _log_sigmoid(x):
    return -_softplus(-x)


def _order_mask(n, rev, strict=False):
    r = lax.broadcasted_iota(jnp.int32, (n, n), 0)
    c = lax.broadcasted_iota(jnp.int32, (n, n), 1)
    if rev:
        return (c > r) if strict else (c >= r)
    return (c < r) if strict else (c <= r)


def _cumsum_rows(x, rev):
    m = _order_mask(x.shape[0], rev).astype(F32)
    return _dot(m, x, precision=HIGHEST)


def _ada_kernel(c_ref, w_ref, b_ref, o_ref):
    s = _silu(c_ref[...])
    o_ref[0] = _dot(s, w_ref[0], precision=HIGHEST) + b_ref[0]


def _ada(cvec, w_ada, b_ada, tn=1536):
    depth, d, n6 = w_ada.shape
    rows = cvec.shape[0]
    return pl.pallas_call(
        _ada_kernel,
        grid=(depth, n6 // tn),
        in_specs=[pl.BlockSpec((rows, d), lambda l, n: (0, 0)),
                  pl.BlockSpec((1, d, tn), lambda l, n: (l, 0, n)),
                  pl.BlockSpec((1, 1, tn), lambda l, n: (l, 0, n))],
        out_specs=pl.BlockSpec((1, rows, tn), lambda l, n: (l, 0, n)),
        out_shape=jax.ShapeDtypeStruct((depth, rows, n6), F32),
        compiler_params=_cparams(("parallel", "parallel"), 40),
        name="ada",
    )(cvec, w_ada, b_ada.reshape(depth, 1, n6))


def _lower_bounds_kernel(lb_ref, o_ref):
    x = lb_ref[...]
    depth = x.shape[0]
    mx = jnp.max(x, axis=0, keepdims=True)
    e = jnp.exp(x - mx)
    p = e / jnp.sum(e, axis=0, keepdims=True)
    acc = jnp.zeros_like(p[0:1])
    for l in range(depth):
        o_ref[l:l + 1, :] = acc
        if l + 1 < depth:
            acc = acc + p[l + 1:l + 2]


def _lower_bounds(lb_raw):
    return pl.pallas_call(
        _lower_bounds_kernel,
        out_shape=jax.ShapeDtypeStruct(lb_raw.shape, F32),
        name="lower_bounds",
    )(lb_raw)


def _in_proj_kernel(x_ref, mod_ref, w_ref, o_ref, h_ref):
    @pl.when(pl.program_id(2) == 0)
    def _():
        shift = mod_ref[0, 0:1, :]
        scale = mod_ref[0, 1:2, :]
        h_ref[...] = (_ln(x_ref[0]) * (1.0 + scale) + shift).astype(BF16)

    o_ref[0] = _dot(h_ref[...], w_ref[...])


def _in_proj(x, mod, w, tl, tn=2048):
    b, l, d = x.shape
    n_proj = w.shape[1]
    return pl.pallas_call(
        _in_proj_kernel,
        grid=(b, l // tl, n_proj // tn),
        in_specs=[pl.BlockSpec((1, tl, d), lambda bi, i, n: (bi, i, 0)),
                  pl.BlockSpec((1, 6, d), lambda bi, i, n: (bi, 0, 0)),
                  pl.BlockSpec((d, tn), lambda bi, i, n: (0, n))],
        out_specs=pl.BlockSpec((1, tl, tn), lambda bi, i, n: (bi, i, n)),
        out_shape=jax.ShapeDtypeStruct((b, l, n_proj), F32),
        scratch_shapes=[pltpu.VMEM((tl, d), BF16)],
        compiler_params=_cparams(("parallel", "parallel", "arbitrary"), 48),
        name="in_proj",
    )(x, mod, w)


def _conv_kernel(x_ref, prev_ref, next_ref, w_ref, o_ref):
    i = pl.program_id(1)
    last = pl.num_programs(1) - 1
    x = x_ref[0]
    tl = x.shape[0]
    row = lax.broadcasted_iota(jnp.int32, x.shape, 0)
    before = jnp.where(i > 0, prev_ref[0, 7:8, :], 0.0)
    after = jnp.where(i < last, next_ref[0, 0:1, :], 0.0)
    x_dn = jnp.where(row == 0, before, pltpu.roll(x, 1, axis=0))
    x_up = jnp.where(row == tl - 1, after, pltpu.roll(x, tl - 1, axis=0))
    y = _silu(x_dn * w_ref[0:1, :] + x * w_ref[1:2, :] + x_up * w_ref[2:3, :])
    for j in range(3 * N_HEADS):
        t = y[:, j * HEAD_DIM:(j + 1) * HEAD_DIM]
        if j < 2 * N_HEADS:
            t = t * lax.rsqrt(jnp.sum(t * t, axis=-1, keepdims=True) + L2_EPS)
            if j < N_HEADS:
                t = t * HEAD_DIM ** -0.5
        o_ref[0, :, j * HEAD_DIM:(j + 1) * HEAD_DIM] = t


def _gdn_conv(proj, conv_w, tl):
    b, l, _ = proj.shape
    w3 = 3 * MIX_W
    cb = COL_CQKV // w3
    nb8 = l // 8
    r8 = tl // 8
    return pl.pallas_call(
        _conv_kernel,
        grid=(b, l // tl),
        in_specs=[pl.BlockSpec((1, tl, w3), lambda bi, i: (bi, i, cb)),
                  pl.BlockSpec((1, 8, w3), lambda bi, i: (bi, jnp.maximum(i * r8 - 1, 0), cb)),
                  pl.BlockSpec((1, 8, w3), lambda bi, i: (bi, jnp.minimum((i + 1) * r8, nb8 - 1), cb)),
                  pl.BlockSpec((3, w3), lambda bi, i: (0, 0))],
        out_specs=pl.BlockSpec((1, tl, w3), lambda bi, i: (bi, i, 0)),
        out_shape=jax.ShapeDtypeStruct((b, l, w3), F32),
        compiler_params=_cparams(("parallel", "parallel"), 40),
        name="gdn_conv",
    )(proj, proj, proj, conv_w)


def _hgrn_chunks(probs, states, emat):
    c = probs[0]["q"].shape[0]
    nb = c // SUB
    for p in probs:
        lb = p["lb"]
        p["k"] = (1.0 - lb) * _sigmoid(-p["fpre"])
        p["g"] = _cumsum_rows(jnp.log(lb + (1.0 - lb) * _sigmoid(p["fpre"])), p["rev"])

    walls = []
    for p in probs:
        q, k, g = p["q"], p["k"], p["g"]
        rows = []
        for j in range(nb):
            r0 = j * SUB
            gj = g[r0:r0 + SUB]
            qj = q[r0:r0 + SUB]
            pieces = []
            for s in range(SUB):
                gs = g[r0 + s:r0 + s + 1]
                ks = k[r0 + s:r0 + s + 1]
                pieces.append((jnp.exp(jnp.minimum(gj - gs, 0.0)) * (qj * ks)).astype(BF16))
            rows.append(jnp.concatenate(pieces, axis=1))
        walls.append(jnp.concatenate(rows, axis=0))
    a_diags = [_dot(w, emat) for w in walls]

    a_offs = []
    for p in probs:
        q, k, g, rev = p["q"], p["k"], p["g"], p["rev"]
        offs = []
        for i in range(nb):
            r0 = i * SUB
            if (not rev and i == 0) or (rev and i == nb - 1):
                offs.append(jnp.zeros((SUB, c), F32))
                continue
            bi = g[r0 + SUB:r0 + SUB + 1] if rev else g[r0 - 1:r0]
            lhs = (q[r0:r0 + SUB] * jnp.exp(g[r0:r0 + SUB] - bi)).astype(BF16)
            rhs = (k * jnp.exp(jnp.minimum(bi - g, 0.0))).astype(BF16)
            offs.append(_dot_nt(lhs, rhs))
        a_offs.append(jnp.concatenate(offs, axis=0))

    r = lax.broadcasted_iota(jnp.int32, (c, c), 0)
    col = lax.broadcasted_iota(jnp.int32, (c, c), 1)
    shift = SUB.bit_length() - 1
    rb = lax.shift_right_logical(r, shift)
    cb = lax.shift_right_logical(col, shift)
    inters = [_dot_nt((p["q"] * jnp.exp(p["g"])).astype(BF16), s_t.astype(BF16)) for p, s_t in zip(probs, states)]
    outs, new_states = [], []
    for p, a_diag, a_off, inter, s_t in zip(probs, a_diags, a_offs, inters, states):
        rev, g = p["rev"], p["g"]
        order = (col >= r) if rev else (col <= r)
        off_side = (cb > rb) if rev else (cb < rb)
        a = jnp.where((rb == cb) & order, a_diag, jnp.where(off_side, a_off, 0.0))
        g_end = g[0:1] if rev else g[c - 1:c]
        vb = p["v"].astype(BF16)
        outs.append(_dot(a.astype(BF16), vb) + inter)
        kd = (p["k"] * jnp.exp(g_end - g)).astype(BF16)
        new_states.append(s_t * jnp.exp(g_end) + _dot_tn(vb, kd))
    return outs, new_states


def _hgrn_kernel(qf_ref, ff_ref, vf_ref, qb_ref, fb_ref, vb_ref, lb_ref, e_ref, s0_ref,
                 of_ref, ob_ref, sout_ref, s_ref):
    n = pl.program_id(1)

    @pl.when(n == 0)
    def _():
        s_ref[...] = s0_ref[0]

    probs, where = [], []
    for d, (q_ref, f_ref, v_ref, o_ref) in enumerate(((qf_ref, ff_ref, vf_ref, of_ref),
                                                      (qb_ref, fb_ref, vb_ref, ob_ref))):
        for h in range(N_HEADS):
            sl = slice(h * HEAD_DIM, (h + 1) * HEAD_DIM)
            probs.append(dict(q=q_ref[0, :, sl], fpre=f_ref[0, :, sl], v=v_ref[0, :, sl], lb=lb_ref[:, sl],
                              rev=d == 1))
            where.append((o_ref, sl, h, d))
    outs, new_states = _hgrn_chunks(probs, [s_ref[h, d] for _, _, h, d in where], e_ref[...])
    for (o_ref, sl, h, d), o, s_new in zip(where, outs, new_states):
        o_ref[0, :, sl] = o
        s_ref[h, d] = s_new

    @pl.when(n == pl.num_programs(1) - 1)
    def _():
        sout_ref[0] = s_ref[...]


def _hgrn_emat():
    r = jnp.arange(SUB * HEAD_DIM)[:, None] // HEAD_DIM
    c = jnp.arange(CHUNK)[None, :] % SUB
    return (r == c).astype(BF16)


def _hgrn_scan(proj, lb, s0):
    b, l, _ = proj.shape
    nc = l // CHUNK
    ca = COL_A // MIX_W

    def fwd(col):
        return pl.BlockSpec((1, CHUNK, MIX_W), lambda bi, n: (bi, n, ca + col))

    def bwd(col):
        return pl.BlockSpec((1, CHUNK, MIX_W), lambda bi, n: (bi, nc - 1 - n, ca + col))

    st_spec = pl.BlockSpec((1, N_HEADS, 2, HEAD_DIM, HEAD_DIM), lambda bi, n: (bi, 0, 0, 0, 0))
    return pl.pallas_call(
        _hgrn_kernel,
        grid=(b, nc),
        in_specs=[fwd(0), fwd(1), fwd(3), bwd(0), bwd(2), bwd(3),
                  pl.BlockSpec((1, MIX_W), lambda bi, n: (0, 0)),
                  pl.BlockSpec((SUB * HEAD_DIM, CHUNK), lambda bi, n: (0, 0)),
                  st_spec],
        out_specs=[pl.BlockSpec((1, CHUNK, MIX_W), lambda bi, n: (bi, n, 0)),
                   pl.BlockSpec((1, CHUNK, MIX_W), lambda bi, n: (bi, nc - 1 - n, 0)),
                   st_spec],
        out_shape=[jax.ShapeDtypeStruct((b, l, MIX_W), F32),
                   jax.ShapeDtypeStruct((b, l, MIX_W), F32),
                   jax.ShapeDtypeStruct(s0.shape, F32)],
        scratch_shapes=[pltpu.VMEM((N_HEADS, 2, HEAD_DIM, HEAD_DIM), F32)],
        compiler_params=_cparams(("parallel", "arbitrary"), 40),
        name="hgrn_scan",
    )(proj, proj, proj, proj, proj, proj, lb, _hgrn_emat(), s0)


def _mlstm_chunks(probs, states):
    c = probs[0]["q"].shape[0]
    for p, (cm, nvec, m) in zip(probs, states):
        tri = _order_mask(c, p["rev"])
        log_d = jnp.where(tri, p["bb_c"] - p["bb_r"] + p["ic_r"], MASK_NEG)
        log_inter = p["bb_c"] + m
        m_t = jnp.maximum(jnp.max(log_d, axis=1, keepdims=True), log_inter)
        p["m_t"] = m_t
        p["dmat"] = jnp.where(tri, jnp.exp(log_d - m_t), 0.0)
        p["inter"] = jnp.exp(log_inter - m_t)
        p["qb"], p["kb"], p["vb"] = p["q"].astype(BF16), p["k"].astype(BF16), p["v"].astype(BF16)
    scores = [_dot_nt(p["qb"], p["kb"]) * p["dmat"] for p in probs]
    carried = [_dot(p["qb"], cm.astype(BF16)) for p, (cm, _, _) in zip(probs, states)]
    local = [_dot(s.astype(BF16), p["vb"]) for p, s in zip(probs, scores)]
    outs, new_states = [], []
    for p, s, car, loc, (cm, nvec, m) in zip(probs, scores, carried, local, states):
        m_t, inter = p["m_t"], p["inter"]
        den = jnp.sum(s, axis=1, keepdims=True) + inter * jnp.sum(p["q"] * nvec, axis=1, keepdims=True)
        outs.append((loc + inter * car) / jnp.maximum(jnp.abs(den), jnp.exp(-m_t)))
        m_new = m_t[0:1] if p["rev"] else m_t[c - 1:c]
        bb_end = p["bb_c"][0:1] if p["rev"] else p["bb_c"][c - 1:c]
        decay = jnp.exp(bb_end + m - m_new)
        kw = p["k"] * jnp.exp(bb_end - p["bb_c"] + p["ic_c"] - m_new)
        new_states.append((decay * cm + _dot_tn(kw.astype(BF16), p["vb"]),
                           decay * nvec + jnp.sum(kw, axis=0, keepdims=True), m_new))
    return outs, new_states


def _column_view(ref):
    rows, group, width = ref.shape[-3:]
    return ref.reshape(rows * group, width), group


def _load_chunk(ref, j, colmajor):
    if colmajor:
        flat, group = _column_view(ref)
        return flat[pl.ds(j, CHUNK, stride=group), :]
    return ref[0, pl.ds(pl.multiple_of(j * CHUNK, CHUNK), CHUNK), :]


def _store_chunk(ref, j, val, colmajor):
    if colmajor:
        flat, group = _column_view(ref)
        flat[pl.ds(j, CHUNK, stride=group), :] = val
    else:
        ref[0, pl.ds(pl.multiple_of(j * CHUNK, CHUNK), CHUNK), :] = val


def _mlstm_kernel(*refs, colmajor, chunks):
    per_dir = 3 * N_HEADS + 1
    fwd, bwd = refs[:per_dir], refs[per_dir:2 * per_dir]
    bias_ref, cm0_ref, nm0_ref, of_ref, ob_ref, cmout_ref, nmout_ref, cm_ref, nm_ref = refs[2 * per_dir:]
    n = pl.program_id(1)

    @pl.when(n == 0)
    def _():
        cm_ref[...] = cm0_ref[0]
        nm_ref[...] = nm0_ref[0]

    def step(i, carry):
        probs, where = [], []
        for d, (in_refs, o_ref) in enumerate(((fwd, of_ref), (bwd, ob_ref))):
            rev = d == 1
            j = chunks - 1 - i if rev else i
            gates = _load_chunk(in_refs[-1], j, colmajor) + bias_ref[...]
            cs = _cumsum_rows(_log_sigmoid(gates), rev)
            cs_t = cs.T
            gates_t = gates.T
            for h in range(N_HEADS):
                ci = 4 * d + h
                cf = 8 + 4 * d + h
                probs.append(dict(
                    q=_load_chunk(in_refs[h], j, colmajor),
                    k=_load_chunk(in_refs[N_HEADS + h], j, colmajor) * HEAD_DIM ** -0.5,
                    v=_load_chunk(in_refs[2 * N_HEADS + h], j, colmajor),
                    bb_c=cs[:, cf:cf + 1], bb_r=cs_t[cf:cf + 1, :],
                    ic_c=gates[:, ci:ci + 1], ic_r=gates_t[ci:ci + 1, :], rev=rev))
                where.append((o_ref, j, d, h))
        states = [(cm_ref[d, h], nm_ref[d, h, 0:1, :], nm_ref[d, h, 1:2, 0:1]) for _, _, d, h in where]
        outs, new_states = _mlstm_chunks(probs, states)
        for (o_ref, j, d, h), o, (cm_new, n_new, m_new) in zip(where, outs, new_states):
            _store_chunk(o_ref.at[h], j, o, colmajor)
            cm_ref[d, h] = cm_new
            nm_ref[d, h, 0:1, :] = n_new
            nm_ref[d, h, 1:2, :] = jnp.broadcast_to(m_new, (1, HEAD_DIM))
        return carry

    lax.fori_loop(0, chunks, step, 0)

    @pl.when(n == pl.num_programs(1) - 1)
    def _():
        cmout_ref[0] = cm_ref[...]
        nmout_ref[0] = nm_ref[...]


def _mlstm_scan(proj, gbias, cm0, nm0, colmajor):
    b, l, n_proj = proj.shape
    cb = COL_B // LANE
    cg = COL_GATES // LANE
    if colmajor:
        chunks = 8
        assert l == CHUNK * GRID_W and GRID_W % chunks == 0
        nb = GRID_W // chunks
        src = proj.reshape(b, CHUNK, GRID_W, n_proj)
        in_blk = (1, CHUNK, chunks, LANE)
        out_blk = (N_HEADS, 1, CHUNK, chunks, LANE)
        out_view = (N_HEADS, b, CHUNK, GRID_W, LANE)

        def in_map(col, flip):
            return lambda bi, n: (bi, 0, nb - 1 - n if flip else n, col)

        def out_map(flip):
            return lambda bi, n: (0, bi, 0, nb - 1 - n if flip else n, 0)
    else:
        chunks = min(l // CHUNK, 4)
        nb = l // (chunks * CHUNK)
        src = proj
        in_blk = (1, chunks * CHUNK, LANE)
        out_blk = (N_HEADS, 1, chunks * CHUNK, LANE)
        out_view = (N_HEADS, b, l, LANE)

        def in_map(col, flip):
            return lambda bi, n: (bi, nb - 1 - n if flip else n, col)

        def out_map(flip):
            return lambda bi, n: (0, bi, nb - 1 - n if flip else n, 0)

    def specs(flip):
        cols = [cb + j for j in range(3 * N_HEADS)] + [cg]
        return [pl.BlockSpec(in_blk, in_map(col, flip)) for col in cols]

    cm_spec = pl.BlockSpec((1, 2, N_HEADS, HEAD_DIM, HEAD_DIM), lambda bi, n: (bi, 0, 0, 0, 0))
    nm_spec = pl.BlockSpec((1, 2, N_HEADS, 8, HEAD_DIM), lambda bi, n: (bi, 0, 0, 0, 0))
    n_in = 2 * (3 * N_HEADS + 1)
    o_f, o_b, cm, nm = pl.pallas_call(
        functools.partial(_mlstm_kernel, colmajor=colmajor, chunks=chunks),
        grid=(b, nb),
        in_specs=specs(False) + specs(True) + [pl.BlockSpec((1, LANE), lambda bi, n: (0, 0)), cm_spec, nm_spec],
        out_specs=[pl.BlockSpec(out_blk, out_map(False)), pl.BlockSpec(out_blk, out_map(True)), cm_spec, nm_spec],
        out_shape=[jax.ShapeDtypeStruct(out_view, F32), jax.ShapeDtypeStruct(out_view, F32),
                   jax.ShapeDtypeStruct(cm0.shape, F32), jax.ShapeDtypeStruct(nm0.shape, F32)],
        scratch_shapes=[pltpu.VMEM((2, N_HEADS, HEAD_DIM, HEAD_DIM), F32),
                        pltpu.VMEM((2, N_HEADS, 8, HEAD_DIM), F32)],
        compiler_params=_cparams(("parallel", "arbitrary"), 40),
        name="mlstm_scan",
    )(*([src] * n_in), gbias, cm0, nm0)
    head_major = (N_HEADS, b, l, LANE)
    return o_f.reshape(head_major), o_b.reshape(head_major), cm, nm


def _split_bf16(x):
    hi = x.astype(BF16)
    return hi, (x - hi.astype(F32)).astype(BF16)


def _dot3(a, b, nt=False):
    f = _dot_nt if nt else _dot
    return f(a[0], b[0]) + (f(a[0], b[1]) + f(a[1], b[0]))


def _inv_unit_triangular(a_list):
    n = a_list[0].shape[0]
    eye = (lax.broadcasted_iota(jnp.int32, (n, n), 0) == lax.broadcasted_iota(jnp.int32, (n, n), 1)).astype(F32)
    xs = [eye - a for a in a_list]
    splits = [_split_bf16(a) for a in a_list]
    ps = [_dot3(s, s) for s in splits]
    steps = (n - 1).bit_length() - 1
    for step in range(steps):
        last = step == steps - 1
        p_split = [_split_bf16(p) for p in ps]
        lhs = xs if last else [jnp.concatenate([x, p], axis=0) for x, p in zip(xs, ps)]
        prods = [_dot3(_split_bf16(l), s) for l, s in zip(lhs, p_split)]
        xs = [x + pr[:n] for x, pr in zip(xs, prods)]
        if not last:
            ps = [pr[n:] for pr in prods]
    return xs


def _gdn_chunks(probs, states):
    c = probs[0]["q"].shape[0]
    a_list, ldecs = [], []
    for p in probs:
        tri = _order_mask(c, p["rev"])
        strict = _order_mask(c, p["rev"], strict=True)
        ldec = jnp.where(tri, jnp.exp(jnp.minimum(p["gam_c"] - p["gam_r"], 0.0)), 0.0)
        p["kbeta"] = p["k"] * p["beta"]
        kk = _dot3(_split_bf16(p["kbeta"]), _split_bf16(p["k"]), nt=True)
        a_list.append(jnp.where(strict, kk * ldec, 0.0))
        ldecs.append(ldec)
    t_invs = _inv_unit_triangular(a_list)
    sols = []
    for p, t_inv in zip(probs, t_invs):
        p["eg"] = jnp.exp(p["gam_c"])
        rhs = jnp.concatenate([p["v"] * p["beta"], p["kbeta"] * p["eg"]], axis=1)
        sols.append(_dot3(_split_bf16(t_inv), _split_bf16(rhs)))
    qks = [_dot_nt(p["q"].astype(BF16), p["k"].astype(BF16)) * ldec for p, ldec in zip(probs, ldecs)]
    inter = [_dot(jnp.concatenate([sol[:, HEAD_DIM:], p["q"] * p["eg"]], axis=0).astype(BF16), s.astype(BF16))
             for p, sol, s in zip(probs, sols, states)]
    outs, new_states = [], []
    for p, sol, qk, it, s in zip(probs, sols, qks, inter, states):
        g_end = p["gam_c"][0:1] if p["rev"] else p["gam_c"][c - 1:c]
        v_new = (sol[:, :HEAD_DIM] - it[:c]).astype(BF16)
        kg = p["k"] * jnp.exp(g_end - p["gam_c"])
        outs.append(it[c:] + _dot(qk.astype(BF16), v_new))
        new_states.append(jnp.exp(g_end) * s + _dot_tn(kg.astype(BF16), v_new))
    return outs, new_states


def _gdn_kernel(xf_ref, gf_ref, xb_ref, gb_ref, alog_ref, dtb_ref, s0_ref,
                of_ref, ob_ref, sout_ref, s_ref):
    n = pl.program_id(1)

    @pl.when(n == 0)
    def _():
        s_ref[...] = s0_ref[0]

    neg_a = -jnp.exp(alog_ref[...])
    probs, where = [], []
    for d, (x_ref, g_ref, o_ref) in enumerate(((xf_ref, gf_ref, of_ref), (xb_ref, gb_ref, ob_ref))):
        rev = d == 1
        gates = g_ref[0]
        cs = _cumsum_rows(neg_a * _softplus(gates + dtb_ref[...]), rev)
        cs_t = cs.T
        betas = _sigmoid(gates)
        for h in range(N_HEADS):
            ca = 16 + 4 * d + h
            cbeta = 24 + 4 * d + h
            probs.append(dict(
                q=x_ref[0, :, h * HEAD_DIM:(h + 1) * HEAD_DIM],
                k=x_ref[0, :, MIX_W + h * HEAD_DIM:MIX_W + (h + 1) * HEAD_DIM],
                v=x_ref[0, :, 2 * MIX_W + h * HEAD_DIM:2 * MIX_W + (h + 1) * HEAD_DIM],
                gam_c=cs[:, ca:ca + 1], gam_r=cs_t[ca:ca + 1, :], beta=betas[:, cbeta:cbeta + 1], rev=rev))
            where.append((o_ref, d, h))
    outs, new_states = _gdn_chunks(probs, [s_ref[d, h] for _, d, h in where])
    for (o_ref, d, h), o, s_new in zip(where, outs, new_states):
        o_ref[0, :, h * HEAD_DIM:(h + 1) * HEAD_DIM] = o
        s_ref[d, h] = s_new

    @pl.when(n == pl.num_programs(1) - 1)
    def _():
        sout_ref[0] = s_ref[...]


def _gdn_scan(qkv, proj, alog_row, dtb_row, s0):
    b, l, w3 = qkv.shape
    nc = l // CHUNK
    cg = COL_GATES // LANE
    st_spec = pl.BlockSpec((1, 2, N_HEADS, HEAD_DIM, HEAD_DIM), lambda bi, n: (bi, 0, 0, 0, 0))
    row_spec = pl.BlockSpec((1, LANE), lambda bi, n: (0, 0))
    return pl.pallas_call(
        _gdn_kernel,
        grid=(b, nc),
        in_specs=[pl.BlockSpec((1, CHUNK, w3), lambda bi, n: (bi, n, 0)),
                  pl.BlockSpec((1, CHUNK, LANE), lambda bi, n: (bi, n, cg)),
                  pl.BlockSpec((1, CHUNK, w3), lambda bi, n: (bi, nc - 1 - n, 0)),
                  pl.BlockSpec((1, CHUNK, LANE), lambda bi, n: (bi, nc - 1 - n, cg)),
                  row_spec, row_spec, st_spec],
        out_specs=[pl.BlockSpec((1, CHUNK, MIX_W), lambda bi, n: (bi, n, 0)),
                   pl.BlockSpec((1, CHUNK, MIX_W), lambda bi, n: (bi, nc - 1 - n, 0)),
                   st_spec],
        out_shape=[jax.ShapeDtypeStruct((b, l, MIX_W), F32), jax.ShapeDtypeStruct((b, l, MIX_W), F32),
                   jax.ShapeDtypeStruct(s0.shape, F32)],
        scratch_shapes=[pltpu.VMEM((2, N_HEADS, HEAD_DIM, HEAD_DIM), F32)],
        compiler_params=_cparams(("parallel", "arbitrary"), 40),
        name="gdn_scan",
    )(qkv, proj, qkv, proj, alog_row, dtb_row, s0)


def _merge_kernel(oaf, oab, obf, obb, ocf, ocb, ag, bo, cz, mg, x_ref, mod_ref, gain_ref, wbr_ref, wout_ref,
                  lng_ref, lnb_ref, out_ref, *, alpha):
    d_model = x_ref.shape[-1]
    branches = ((oaf, oab, ag, _silu), (obf, obb, bo, _sigmoid), (ocf, ocb, cz, _silu))
    mixed = None
    for nbr, (of_ref, ob_ref, gate_ref, act) in enumerate(branches):
        parts = []
        for h in range(N_HEADS):
            if len(of_ref.shape) == 4:
                oh = of_ref[h, 0] + ob_ref[h, 0]
            else:
                oh = of_ref[0, :, h * HEAD_DIM:(h + 1) * HEAD_DIM] + ob_ref[0, :, h * HEAD_DIM:(h + 1) * HEAD_DIM]
            parts.append(oh * lax.rsqrt(jnp.mean(oh * oh, axis=-1, keepdims=True) + RMS_EPS))
        y = jnp.concatenate(parts, axis=1) * gain_ref[nbr:nbr + 1, :] * act(gate_ref[0])
        term = _sigmoid(mg[0, :, nbr * d_model:(nbr + 1) * d_model]) * _dot(y.astype(BF16), wbr_ref[nbr])
        mixed = term if mixed is None else mixed + term
    sub = _dot(mixed.astype(BF16), wout_ref[...])
    z = alpha * x_ref[0] + mod_ref[0, 2:3, :] * sub
    out_ref[0] = _ln(z) * lng_ref[...] + lnb_ref[...]


def _merge(outs, proj, x, mod, gain, w_branch, w_out, ln_g, ln_b, alpha, tl):
    b, l, d = x.shape
    u = MIX_W

    def tok(width, col):
        return pl.BlockSpec((1, tl, width), lambda bi, i: (bi, i, col))

    def const(shape):
        return pl.BlockSpec(shape, lambda bi, i: (0,) * len(shape))

    heads = pl.BlockSpec((N_HEADS, 1, tl, LANE), lambda bi, i: (0, bi, i, 0))
    return pl.pallas_call(
        functools.partial(_merge_kernel, alpha=alpha),
        grid=(b, l // tl),
        in_specs=[tok(u, 0), tok(u, 0), heads, heads, tok(u, 0), tok(u, 0),
            tok(u, (COL_A + 4 * MIX_W) // u), tok(u, (COL_B + 3 * MIX_W) // u), tok(u, COL_CZ // u),
            tok(3 * d, COL_MERGE // (3 * d)), tok(d, 0),
            pl.BlockSpec((1, 6, d), lambda bi, i: (bi, 0, 0)),
            const((3, u)), const((3, u, d)), const((d, d)), const((1, d)), const((1, d))],
        out_specs=tok(d, 0),
        out_shape=jax.ShapeDtypeStruct((b, l, d), F32),
        compiler_params=_cparams(("parallel", "parallel"), 48),
        name="merge",
    )(*outs, proj, proj, proj, proj, x, mod, gain, w_branch, w_out, ln_g, ln_b)


def _ffn_kernel(x_ref, mod_ref, wg_ref, wu_ref, wd_ref, lng_ref, lnb_ref, out_ref, h_ref, acc_ref, *, alpha):
    f = pl.program_id(2)

    @pl.when(f == 0)
    def _():
        h_ref[...] = (_ln(x_ref[0]) * (1.0 + mod_ref[0, 4:5, :]) + mod_ref[0, 3:4, :]).astype(BF16)
        acc_ref[...] = jnp.zeros_like(acc_ref)

    h = h_ref[...]
    act = _silu(_dot(h, wg_ref[...])) * _dot(h, wu_ref[...])
    acc_ref[...] += _dot(act.astype(BF16), wd_ref[...])

    @pl.when(f == pl.num_programs(2) - 1)
    def _():
        z = alpha * x_ref[0] + mod_ref[0, 5:6, :] * acc_ref[...]
        out_ref[0] = _ln(z) * lng_ref[...] + lnb_ref[...]


def _ffn(x, mod, w_gate, w_up, w_down, ln_g, ln_b, alpha, tl, tf=512):
    b, l, d = x.shape
    f = w_gate.shape[1]
    return pl.pallas_call(
        functools.partial(_ffn_kernel, alpha=alpha),
        grid=(b, l // tl, f // tf),
        in_specs=[pl.BlockSpec((1, tl, d), lambda bi, i, j: (bi, i, 0)),
                  pl.BlockSpec((1, 6, d), lambda bi, i, j: (bi, 0, 0)),
                  pl.BlockSpec((d, tf), lambda bi, i, j: (0, j)),
                  pl.BlockSpec((d, tf), lambda bi, i, j: (0, j)),
                  pl.BlockSpec((tf, d), lambda bi, i, j: (j, 0)),
                  pl.BlockSpec((1, d), lambda bi, i, j: (0, 0)),
                  pl.BlockSpec((1, d), lambda bi, i, j: (0, 0))],
        out_specs=pl.BlockSpec((1, tl, d), lambda bi, i, j: (bi, i, 0)),
        out_shape=jax.ShapeDtypeStruct((b, l, d), F32),
        scratch_shapes=[pltpu.VMEM((tl, d), BF16), pltpu.VMEM((tl, d), F32)],
        compiler_params=_cparams(("parallel", "parallel", "arbitrary"), 48),
        name="ffn",
    )(x, mod, w_gate, w_up, w_down, ln_g, ln_b)


def _top2_combine(logits):
    lane = lax.broadcasted_iota(jnp.int32, logits.shape, 1).astype(F32)
    neg = -jnp.inf
    lg = jnp.where(lane < N_EXPERTS, logits, neg)
    m1 = jnp.max(lg, axis=1, keepdims=True)
    i1 = jnp.min(jnp.where(lg == m1, lane, float(LANE)), axis=1, keepdims=True)
    lg2 = jnp.where(lane == i1, neg, lg)
    m2 = jnp.max(lg2, axis=1, keepdims=True)
    i2 = jnp.min(jnp.where(lg2 == m2, lane, float(LANE)), axis=1, keepdims=True)
    e2 = jnp.exp(m2 - m1)
    p1 = 1.0 / (1.0 + e2)
    return jnp.where(lane == i1, p1, jnp.where(lane == i2, e2 * p1, 0.0))


def _moe_kernel(x_ref, mod_ref, wr_ref, wg_ref, wu_ref, wd_ref, lng_ref, lnb_ref, out_ref,
                h_ref, acc_ref, comb_ref, *, alpha):
    e = pl.program_id(2)
    f = pl.program_id(3)

    @pl.when((e == 0) & (f == 0))
    def _():
        h = _ln(x_ref[0]) * (1.0 + mod_ref[0, 4:5, :]) + mod_ref[0, 3:4, :]
        h_ref[...] = h.astype(BF16)
        comb_ref[...] = _top2_combine(_dot(h, wr_ref[...], precision=HIGHEST))
        acc_ref[...] = jnp.zeros_like(acc_ref)

    lane = lax.broadcasted_iota(jnp.int32, comb_ref.shape, 1)
    weight = jnp.sum(jnp.where(lane == e, comb_ref[...], 0.0), axis=1, keepdims=True)
    h = h_ref[...]
    act = _silu(_dot(h, wg_ref[0])) * _dot(h, wu_ref[0]) * weight
    acc_ref[...] += _dot(act.astype(BF16), wd_ref[0])

    @pl.when((e == pl.num_programs(2) - 1) & (f == pl.num_programs(3) - 1))
    def _():
        z = alpha * x_ref[0] + mod_ref[0, 5:6, :] * acc_ref[...]
        out_ref[0] = _ln(z) * lng_ref[...] + lnb_ref[...]


def _moe(x, mod, w_router, w_gate, w_up, w_down, ln_g, ln_b, alpha, tl, tf=512):
    b, l, d = x.shape
    ne, _, f = w_gate.shape
    return pl.pallas_call(
        functools.partial(_moe_kernel, alpha=alpha),
        grid=(b, l // tl, ne, f // tf),
        in_specs=[pl.BlockSpec((1, tl, d), lambda bi, i, e, j: (bi, i, 0)),
                  pl.BlockSpec((1, 6, d), lambda bi, i, e, j: (bi, 0, 0)),
                  pl.BlockSpec((d, LANE), lambda bi, i, e, j: (0, 0)),
                  pl.BlockSpec((1, d, tf), lambda bi, i, e, j: (e, 0, j)),
                  pl.BlockSpec((1, d, tf), lambda bi, i, e, j: (e, 0, j)),
                  pl.BlockSpec((1, tf, d), lambda bi, i, e, j: (e, j, 0)),
                  pl.BlockSpec((1, d), lambda bi, i, e, j: (0, 0)),
                  pl.BlockSpec((1, d), lambda bi, i, e, j: (0, 0))],
        out_specs=pl.BlockSpec((1, tl, d), lambda bi, i, e, j: (bi, i, 0)),
        out_shape=jax.ShapeDtypeStruct((b, l, d), F32),
        scratch_shapes=[pltpu.VMEM((tl, d), BF16), pltpu.VMEM((tl, d), F32), pltpu.VMEM((tl, LANE), F32)],
        compiler_params=_cparams(("parallel", "parallel", "arbitrary", "arbitrary"), 48),
        name="moe",
    )(x, mod, w_router, w_gate, w_up, w_down, ln_g, ln_b)


def _reorder_w_in(w):
    d = w.shape[0]
    o_a, o_b = 0, 5 * MIX_W
    o_bgt = o_b + 4 * MIX_W
    o_cqkv = o_bgt + 16
    o_cz = o_cqkv + 3 * MIX_W
    o_cgt = o_cz + MIX_W
    o_merge = o_cgt + 16
    parts = [w[:, o_merge:o_merge + 3 * d], w[:, o_a:o_bgt], w[:, o_cqkv:o_cgt],
             w[:, o_bgt:o_bgt + 16], w[:, o_cgt:o_cgt + 16],
             jnp.zeros((d, N_PROJ - COL_GATES - 32), w.dtype)]
    return jnp.concatenate(parts, axis=1)


def _pad_row(vals, offset):
    row = jnp.zeros((1, LANE), F32)
    return lax.dynamic_update_slice(row, vals.reshape(1, -1).astype(F32), (0, offset))


def _tile(l, pref):
    return pref if l % pref == 0 else l


def kernel(x, c, ctx, c_ctx, w_ada, b_ada, w_in, conv_w, lb_raw, m_gate_bias, gdn_a_log, gdn_dt_bias, mix_norm,
           w_branch, w_out, ln_g, ln_b, ffn_w_gate, ffn_w_up, ffn_w_down, moe_router, moe_w_gate, moe_w_up,
           moe_w_down):
    depth, d_model = w_in.shape[0], w_in.shape[1]
    b, l_lat, _ = x.shape
    l_ctx = ctx.shape[1]
    alpha = float((2 * depth) ** 0.25)
    assert d_model == 1024 and w_in.shape[2] == 9 * MIX_W + 3 * MIX_W + MIX_W + 32 + 3 * d_model

    lower = _lower_bounds(lb_raw.astype(F32))
    n_rows = -(-(b + 1) // 8) * 8
    cvec = jnp.concatenate([c, c_ctx[None, :], jnp.zeros((n_rows - b - 1, d_model), F32)], axis=0)
    mods = _ada(cvec, w_ada, b_ada)

    zeros_state = jnp.zeros((b, 2, N_HEADS, HEAD_DIM, HEAD_DIM), F32)
    zeros_hgrn = jnp.zeros((b, N_HEADS, 2, HEAD_DIM, HEAD_DIM), F32)
    zeros_nm = jnp.zeros((b, 2, N_HEADS, 8, HEAD_DIM), F32)

    for l in range(depth):
        mod_l = mods[l, :b].reshape(b, 6, d_model)
        mod_c = jnp.broadcast_to(mods[l, b].reshape(1, 6, d_model), (b, 6, d_model))
        w_in_l = _reorder_w_in(w_in[l]).astype(BF16)
        lb_row = lower[l].reshape(1, MIX_W)
        gbias = _pad_row(m_gate_bias[l], 0)
        alog_row = _pad_row(gdn_a_log[l], 16)
        dtb_row = _pad_row(gdn_dt_bias[l], 16)
        gain = mix_norm[l]
        wbr = w_branch[l].astype(BF16)
        wo = w_out[l].astype(BF16)
        g0, b0 = ln_g[l, 0].reshape(1, -1), ln_b[l, 0].reshape(1, -1)
        g1, b1 = ln_g[l, 1].reshape(1, -1), ln_b[l, 1].reshape(1, -1)

        p_c = _in_proj(ctx, mod_c, w_in_l, _tile(l_ctx, 256))
        p_l = _in_proj(x, mod_l, w_in_l, _tile(l_lat, 1024))
        qkv_c = _gdn_conv(p_c, conv_w[l], _tile(l_ctx, 256))
        qkv_l = _gdn_conv(p_l, conv_w[l], _tile(l_lat, 512))

        oa_c = _hgrn_scan(p_c, lb_row, zeros_hgrn)
        oa_l = _hgrn_scan(p_l, lb_row, oa_c[2])
        ob_c = _mlstm_scan(p_c, gbias, zeros_state, zeros_nm, colmajor=False)
        ob_l = _mlstm_scan(p_l, gbias, ob_c[2], ob_c[3], colmajor=True)
        oc_c = _gdn_scan(qkv_c, p_c, alog_row, dtb_row, zeros_state)
        oc_l = _gdn_scan(qkv_l, p_l, alog_row, dtb_row, oc_c[2])

        def channel_mixer(t, mod, tl):
            i = l // 2
            if l % 2 == 0:
                return _ffn(t, mod, ffn_w_gate[i].astype(BF16), ffn_w_up[i].astype(BF16),
                            ffn_w_down[i].astype(BF16), g1, b1, alpha, tl)
            w_r = jnp.concatenate([moe_router[i], jnp.zeros((d_model, LANE - N_EXPERTS), F32)], axis=1)
            return _moe(t, mod, w_r, moe_w_gate[i].astype(BF16), moe_w_up[i].astype(BF16),
                        moe_w_down[i].astype(BF16), g1, b1, alpha, tl)

        outs_l = (oa_l[0], oa_l[1], ob_l[0], ob_l[1], oc_l[0], oc_l[1])
        x = _merge(outs_l, p_l, x, mod_l, gain, wbr, wo, g0, b0, alpha, _tile(l_lat, 256))
        x = channel_mixer(x, mod_l, _tile(l_lat, 1024))
        if l < depth - 1:
            outs_c = (oa_c[0], oa_c[1], ob_c[0], ob_c[1], oc_c[0], oc_c[1])
            ctx = _merge(outs_c, p_c, ctx, mod_c, gain, wbr, wo, g0, b0, alpha, _tile(l_ctx, 256))
            ctx = channel_mixer(ctx, mod_c, _tile(l_ctx, 256))
    return x
```

```python
import functools

import jax
import jax.numpy as jnp
from jax import lax
from jax.experimental import pallas as pl
from jax.experimental.pallas import tpu as pltpu

F32 = jnp.float32
BF16 = jnp.bfloat16
HIGHEST = lax.Precision.HIGHEST

HEAD_DIM = 128
N_HEADS = 4
MIX_W = N_HEADS * HEAD_DIM
CHUNK = 64
GRID_W = 64
SUB = 16
N_EXPERTS = 8
MOE_ROW_BLOCK = 288
LN_EPS = 1e-5
RMS_EPS = 1e-6
L2_EPS = 1e-6
MASK_NEG = -1e30
LOG2_E = 1.4426950408889634
LANE = 128

COL_MERGE = 0
COL_A = 3 * 1024
COL_B = COL_A + 5 * MIX_W
COL_CQKV = COL_B + 4 * MIX_W
COL_CZ = COL_CQKV + 3 * MIX_W
COL_GATES = COL_CZ + MIX_W
N_PROJ = COL_GATES + 4 * LANE


def _cparams(sem, vmem_mb=None):
    kw = dict(dimension_semantics=sem)
    if vmem_mb is not None:
        kw["vmem_limit_bytes"] = vmem_mb << 20
    return pltpu.CompilerParams(**kw)


def _dot(a, b, precision=None):
    return jnp.dot(a, b, preferred_element_type=F32, precision=precision)


def _dot_nt(a, b, precision=None):
    return lax.dot_general(a, b, (((1,), (1,)), ((), ())), preferred_element_type=F32, precision=precision)


def _dot_tn(a, b):
    return lax.dot_general(a, b, (((0,), (0,)), ((), ())), preferred_element_type=F32)


def _ln(x):
    mu = jnp.mean(x, axis=-1, keepdims=True)
    xc = x - mu
    var = jnp.mean(xc * xc, axis=-1, keepdims=True)
    return xc * lax.rsqrt(var + LN_EPS)


def _sigmoid(x):
    return 1.0 / (1.0 + jnp.exp(-x))


def _silu(x):
    return x * _sigmoid(x)


def _softplus(x):
    return jnp.maximum(x, 0.0) + jnp.log(1.0 + jnp.exp(-jnp.abs(x)))


def _log_sigmoid(x):
    return -_softplus(-x)


def _order_mask(n, rev, strict=False):
    r = lax.broadcasted_iota(jnp.int32, (n, n), 0)
    c = lax.broadcasted_iota(jnp.int32, (n, n), 1)
    if rev:
        return (c > r) if strict else (c >= r)
    return (c < r) if strict else (c <= r)


def _cumsum_rows(x, rev):
    m = _order_mask(x.shape[0], rev).astype(F32)
    return _dot(m, x, precision=HIGHEST)


def _ada_kernel(c_ref, w_ref, b_ref, o_ref):
    s = _silu(c_ref[...])
    o_ref[0] = _dot(s, w_ref[0], precision=HIGHEST) + b_ref[0]


def _ada(cvec, w_ada, b_ada, tn=1536):
    depth, d, n6 = w_ada.shape
    rows = cvec.shape[0]
    return pl.pallas_call(
        _ada_kernel,
        grid=(depth, n6 // tn),
        in_specs=[pl.BlockSpec((rows, d), lambda l, n: (0, 0)),
                  pl.BlockSpec((1, d, tn), lambda l, n: (l, 0, n)),
                  pl.BlockSpec((1, 1, tn), lambda l, n: (l, 0, n))],
        out_specs=pl.BlockSpec((1, rows, tn), lambda l, n: (l, 0, n)),
        out_shape=jax.ShapeDtypeStruct((depth, rows, n6), F32),
        compiler_params=_cparams(("parallel", "parallel"), 40),
        name="ada",
    )(cvec, w_ada, b_ada.reshape(depth, 1, n6))


def _lower_bounds_kernel(lb_ref, o_ref):
    x = lb_ref[...]
    depth = x.shape[0]
    mx = jnp.max(x, axis=0, keepdims=True)
    e = jnp.exp(x - mx)
    p = e / jnp.sum(e, axis=0, keepdims=True)
    acc = jnp.zeros_like(p[0:1])
    for l in range(depth):
        o_ref[l:l + 1, :] = acc
        if l + 1 < depth:
            acc = acc + p[l + 1:l + 2]


def _lower_bounds(lb_raw):
    return pl.pallas_call(
        _lower_bounds_kernel,
        out_shape=jax.ShapeDtypeStruct(lb_raw.shape, F32),
        name="lower_bounds",
    )(lb_raw)


def _in_proj_kernel(x_ref, mod_ref, w_ref, o_ref, h_ref):
    @pl.when(pl.program_id(2) == 0)
    def _():
        shift = mod_ref[0, 0:1, :]
        scale = mod_ref[0, 1:2, :]
        h_ref[...] = (_ln(x_ref[0]) * (1.0 + scale) + shift).astype(BF16)

    o_ref[0] = _dot(h_ref[...], w_ref[...])


def _in_proj(x, mod, w, tl, tn=2048):
    b, l, d = x.shape
    n_proj = w.shape[1]
    return pl.pallas_call(
        _in_proj_kernel,
        grid=(b, l // tl, n_proj // tn),
        in_specs=[pl.BlockSpec((1, tl, d), lambda bi, i, n: (bi, i, 0)),
                  pl.BlockSpec((1, 6, d), lambda bi, i, n: (bi, 0, 0)),
                  pl.BlockSpec((d, tn), lambda bi, i, n: (0, n))],
        out_specs=pl.BlockSpec((1, tl, tn), lambda bi, i, n: (bi, i, n)),
        out_shape=jax.ShapeDtypeStruct((b, l, n_proj), F32),
        scratch_shapes=[pltpu.VMEM((tl, d), BF16)],
        compiler_params=_cparams(("parallel", "parallel", "arbitrary"), 48),
        name="in_proj",
    )(x, mod, w)


def _conv_kernel(x_ref, prev_ref, next_ref, w_ref, o_ref):
    i = pl.program_id(1)
    last = pl.num_programs(1) - 1
    x = x_ref[0]
    tl = x.shape[0]
    row = lax.broadcasted_iota(jnp.int32, x.shape, 0)
    before = jnp.where(i > 0, prev_ref[0, 7:8, :], 0.0)
    after = jnp.where(i < last, next_ref[0, 0:1, :], 0.0)
    x_dn = jnp.where(row == 0, before, pltpu.roll(x, 1, axis=0))
    x_up = jnp.where(row == tl - 1, after, pltpu.roll(x, tl - 1, axis=0))
    y = _silu(x_dn * w_ref[0:1, :] + x * w_ref[1:2, :] + x_up * w_ref[2:3, :])
    for j in range(3 * N_HEADS):
        t = y[:, j * HEAD_DIM:(j + 1) * HEAD_DIM]
        if j < 2 * N_HEADS:
            t = t * lax.rsqrt(jnp.sum(t * t, axis=-1, keepdims=True) + L2_EPS)
            if j < N_HEADS:
                t = t * HEAD_DIM ** -0.5
        o_ref[0, :, j * HEAD_DIM:(j + 1) * HEAD_DIM] = t


def _gdn_conv(proj, conv_w, tl):
    b, l, _ = proj.shape
    w3 = 3 * MIX_W
    cb = COL_CQKV // w3
    nb8 = l // 8
    r8 = tl // 8
    return pl.pallas_call(
        _conv_kernel,
        grid=(b, l // tl),
        in_specs=[pl.BlockSpec((1, tl, w3), lambda bi, i: (bi, i, cb)),
                  pl.BlockSpec((1, 8, w3), lambda bi, i: (bi, jnp.maximum(i * r8 - 1, 0), cb)),
                  pl.BlockSpec((1, 8, w3), lambda bi, i: (bi, jnp.minimum((i + 1) * r8, nb8 - 1), cb)),
                  pl.BlockSpec((3, w3), lambda bi, i: (0, 0))],
        out_specs=pl.BlockSpec((1, tl, w3), lambda bi, i: (bi, i, 0)),
        out_shape=jax.ShapeDtypeStruct((b, l, w3), F32),
        compiler_params=_cparams(("parallel", "parallel"), 40),
        name="gdn_conv",
    )(proj, proj, proj, conv_w)


def _hgrn_chunks(probs, states, emat):
    c = probs[0]["q"].shape[0]
    nb = c // SUB
    for p in probs:
        lb = p["lb"]
        p["k"] = (1.0 - lb) * _sigmoid(-p["fpre"])
        p["g"] = _cumsum_rows(jnp.log(lb + (1.0 - lb) * _sigmoid(p["fpre"])), p["rev"])

    walls = []
    for p in probs:
        q = p["q"]
        g2 = p["g"] * LOG2_E
        lk2 = jnp.log(p["k"]) * LOG2_E
        rel = lk2 - g2
        rows = []
        for j in range(nb):
            r0 = j * SUB
            gj = g2[r0:r0 + SUB]
            qj = q[r0:r0 + SUB]
            pieces = []
            for s in range(SUB):
                e2 = jnp.exp2(jnp.minimum(gj + rel[r0 + s:r0 + s + 1], lk2[r0 + s:r0 + s + 1]))
                pieces.append((e2 * qj).astype(BF16))
            rows.append(jnp.concatenate(pieces, axis=1))
        walls.append(jnp.concatenate(rows, axis=0))
    a_diags = [_dot(w, emat) for w in walls]

    a_offs = []
    for p in probs:
        q, k, g, rev = p["q"], p["k"], p["g"], p["rev"]
        offs = []
        for i in range(nb):
            r0 = i * SUB
            if (not rev and i == 0) or (rev and i == nb - 1):
                offs.append(jnp.zeros((SUB, c), F32))
                continue
            bi = g[r0 + SUB:r0 + SUB + 1] if rev else g[r0 - 1:r0]
            lhs = (q[r0:r0 + SUB] * jnp.exp(g[r0:r0 + SUB] - bi)).astype(BF16)
            rhs = (k * jnp.exp(jnp.minimum(bi - g, 0.0))).astype(BF16)
            offs.append(_dot_nt(lhs, rhs))
        a_offs.append(jnp.concatenate(offs, axis=0))

    r = lax.broadcasted_iota(jnp.int32, (c, c), 0)
    col = lax.broadcasted_iota(jnp.int32, (c, c), 1)
    shift = SUB.bit_length() - 1
    rb = lax.shift_right_logical(r, shift)
    cb = lax.shift_right_logical(col, shift)
    inters = [_dot_nt((p["q"] * jnp.exp(p["g"])).astype(BF16), s_t.astype(BF16)) for p, s_t in zip(probs, states)]
    outs, new_states = [], []
    for p, a_diag, a_off, inter, s_t in zip(probs, a_diags, a_offs, inters, states):
        rev, g = p["rev"], p["g"]
        order = (col >= r) if rev else (col <= r)
        off_side = (cb > rb) if rev else (cb < rb)
        a = jnp.where((rb == cb) & order, a_diag, jnp.where(off_side, a_off, 0.0))
        g_end = g[0:1] if rev else g[c - 1:c]
        vb = p["v"].astype(BF16)
        outs.append(_dot(a.astype(BF16), vb) + inter)
        kd = (p["k"] * jnp.exp(g_end - g)).astype(BF16)
        new_states.append(s_t * jnp.exp(g_end) + _dot_tn(vb, kd))
    return outs, new_states


def _hgrn_kernel(qf_ref, ff_ref, vf_ref, qb_ref, fb_ref, vb_ref, lb_ref, e_ref, s0_ref,
                 of_ref, ob_ref, sout_ref, s_ref):
    n = pl.program_id(1)

    @pl.when(n == 0)
    def _():
        s_ref[...] = s0_ref[0]

    probs, where = [], []
    for d, (q_ref, f_ref, v_ref, o_ref) in enumerate(((qf_ref, ff_ref, vf_ref, of_ref),
                                                      (qb_ref, fb_ref, vb_ref, ob_ref))):
        for h in range(N_HEADS):
            sl = slice(h * HEAD_DIM, (h + 1) * HEAD_DIM)
            probs.append(dict(q=q_ref[0, :, sl], fpre=f_ref[0, :, sl], v=v_ref[0, :, sl], lb=lb_ref[:, sl],
                              rev=d == 1))
            where.append((o_ref, sl, h, d))
    outs, new_states = _hgrn_chunks(probs, [s_ref[h, d] for _, _, h, d in where], e_ref[...])
    for (o_ref, sl, h, d), o, s_new in zip(where, outs, new_states):
        o_ref[0, :, sl] = o
        s_ref[h, d] = s_new

    @pl.when(n == pl.num_programs(1) - 1)
    def _():
        sout_ref[0] = s_ref[...]


def _hgrn_emat():
    r = jnp.arange(SUB * HEAD_DIM)[:, None] // HEAD_DIM
    c = jnp.arange(CHUNK)[None, :] % SUB
    return (r == c).astype(BF16)


def _hgrn_scan(proj, lb, s0):
    b, l, _ = proj.shape
    nc = l // CHUNK
    ca = COL_A // MIX_W

    def fwd(col):
        return pl.BlockSpec((1, CHUNK, MIX_W), lambda bi, n: (bi, n, ca + col))

    def bwd(col):
        return pl.BlockSpec((1, CHUNK, MIX_W), lambda bi, n: (bi, nc - 1 - n, ca + col))

    st_spec = pl.BlockSpec((1, N_HEADS, 2, HEAD_DIM, HEAD_DIM), lambda bi, n: (bi, 0, 0, 0, 0))
    return pl.pallas_call(
        _hgrn_kernel,
        grid=(b, nc),
        in_specs=[fwd(0), fwd(1), fwd(3), bwd(0), bwd(2), bwd(3),
                  pl.BlockSpec((1, MIX_W), lambda bi, n: (0, 0)),
                  pl.BlockSpec((SUB * HEAD_DIM, CHUNK), lambda bi, n: (0, 0)),
                  st_spec],
        out_specs=[pl.BlockSpec((1, CHUNK, MIX_W), lambda bi, n: (bi, n, 0)),
                   pl.BlockSpec((1, CHUNK, MIX_W), lambda bi, n: (bi, nc - 1 - n, 0)),
                   st_spec],
        out_shape=[jax.ShapeDtypeStruct((b, l, MIX_W), F32),
                   jax.ShapeDtypeStruct((b, l, MIX_W), F32),
                   jax.ShapeDtypeStruct(s0.shape, F32)],
        scratch_shapes=[pltpu.VMEM((N_HEADS, 2, HEAD_DIM, HEAD_DIM), F32)],
        compiler_params=_cparams(("parallel", "arbitrary"), 40),
        name="hgrn_scan",
    )(proj, proj, proj, proj, proj, proj, lb, _hgrn_emat(), s0)


def _mlstm_chunks(probs, states):
    c = probs[0]["q"].shape[0]
    for p, (cm, nvec, m) in zip(probs, states):
        tri = _order_mask(c, p["rev"])
        log_d = jnp.where(tri, p["bb_c"] - p["bb_r"] + p["ic_r"], MASK_NEG)
        log_inter = p["bb_c"] + m
        m_t = jnp.maximum(jnp.max(log_d, axis=1, keepdims=True), log_inter)
        p["m_t"] = m_t
        p["dmat"] = jnp.where(tri, jnp.exp(log_d - m_t), 0.0)
        p["inter"] = jnp.exp(log_inter - m_t)
        p["qb"], p["kb"], p["vb"] = p["q"].astype(BF16), p["k"].astype(BF16), p["v"].astype(BF16)
    scores = [_dot_nt(p["qb"], p["kb"]) * p["dmat"] for p in probs]
    carried = [_dot(p["qb"], cm.astype(BF16)) for p, (cm, _, _) in zip(probs, states)]
    local = [_dot(s.astype(BF16), p["vb"]) for p, s in zip(probs, scores)]
    outs, new_states = [], []
    for p, s, car, loc, (cm, nvec, m) in zip(probs, scores, carried, local, states):
        m_t, inter = p["m_t"], p["inter"]
        den = jnp.sum(s, axis=1, keepdims=True) + inter * jnp.sum(p["q"] * nvec, axis=1, keepdims=True)
        outs.append((loc + inter * car) / jnp.maximum(jnp.abs(den), jnp.exp(-m_t)))
        m_new = m_t[0:1] if p["rev"] else m_t[c - 1:c]
        bb_end = p["bb_c"][0:1] if p["rev"] else p["bb_c"][c - 1:c]
        decay = jnp.exp(bb_end + m - m_new)
        kw = p["k"] * jnp.exp(bb_end - p["bb_c"] + p["ic_c"] - m_new)
        new_states.append((decay * cm + _dot_tn(kw.astype(BF16), p["vb"]),
                           decay * nvec + jnp.sum(kw, axis=0, keepdims=True), m_new))
    return outs, new_states


def _column_view(ref):
    rows, group, width = ref.shape[-3:]
    return ref.reshape(rows * group, width), group


def _load_chunk(ref, j, colmajor):
    if colmajor:
        flat, group = _column_view(ref)
        return flat[pl.ds(j, CHUNK, stride=group), :]
    return ref[0, pl.ds(pl.multiple_of(j * CHUNK, CHUNK), CHUNK), :]


def _store_chunk(ref, j, val, colmajor):
    if colmajor:
        flat, group = _column_view(ref)
        flat[pl.ds(j, CHUNK, stride=group), :] = val
    else:
        ref[0, pl.ds(pl.multiple_of(j * CHUNK, CHUNK), CHUNK), :] = val


def _mlstm_kernel(*refs, colmajor, chunks):
    per_dir = 3 * N_HEADS + 1
    fwd, bwd = refs[:per_dir], refs[per_dir:2 * per_dir]
    bias_ref, cm0_ref, nm0_ref, of_ref, ob_ref, cmout_ref, nmout_ref, cm_ref, nm_ref = refs[2 * per_dir:]
    n = pl.program_id(1)

    @pl.when(n == 0)
    def _():
        cm_ref[...] = cm0_ref[0]
        nm_ref[...] = nm0_ref[0]

    def step(i, carry):
        probs, where = [], []
        for d, (in_refs, o_ref) in enumerate(((fwd, of_ref), (bwd, ob_ref))):
            rev = d == 1
            j = chunks - 1 - i if rev else i
            gates = _load_chunk(in_refs[-1], j, colmajor) + bias_ref[...]
            cs = _cumsum_rows(_log_sigmoid(gates), rev)
            cs_t = cs.T
            gates_t = gates.T
            for h in range(N_HEADS):
                ci = 4 * d + h
                cf = 8 + 4 * d + h
                probs.append(dict(
                    q=_load_chunk(in_refs[h], j, colmajor),
                    k=_load_chunk(in_refs[N_HEADS + h], j, colmajor) * HEAD_DIM ** -0.5,
                    v=_load_chunk(in_refs[2 * N_HEADS + h], j, colmajor),
                    bb_c=cs[:, cf:cf + 1], bb_r=cs_t[cf:cf + 1, :],
                    ic_c=gates[:, ci:ci + 1], ic_r=gates_t[ci:ci + 1, :], rev=rev))
                where.append((o_ref, j, d, h))
        states = [(cm_ref[d, h], nm_ref[d, h, 0:1, :], nm_ref[d, h, 1:2, 0:1]) for _, _, d, h in where]
        outs, new_states = _mlstm_chunks(probs, states)
        for (o_ref, j, d, h), o, (cm_new, n_new, m_new) in zip(where, outs, new_states):
            _store_chunk(o_ref.at[h], j, o, colmajor)
            cm_ref[d, h] = cm_new
            nm_ref[d, h, 0:1, :] = n_new
            nm_ref[d, h, 1:2, :] = jnp.broadcast_to(m_new, (1, HEAD_DIM))
        return carry

    lax.fori_loop(0, chunks, step, 0)

    @pl.when(n == pl.num_programs(1) - 1)
    def _():
        cmout_ref[0] = cm_ref[...]
        nmout_ref[0] = nm_ref[...]


def _mlstm_scan(proj, gbias, cm0, nm0, colmajor):
    b, l, n_proj = proj.shape
    cb = COL_B // LANE
    cg = COL_GATES // LANE
    if colmajor:
        chunks = 8
        assert l == CHUNK * GRID_W and GRID_W % chunks == 0
        nb = GRID_W // chunks
        src = proj.reshape(b, CHUNK, GRID_W, n_proj)
        in_blk = (1, CHUNK, chunks, LANE)
        out_blk = (N_HEADS, 1, CHUNK, chunks, LANE)
        out_view = (N_HEADS, b, CHUNK, GRID_W, LANE)

        def in_map(col, flip):
            return lambda bi, n: (bi, 0, nb - 1 - n if flip else n, col)

        def out_map(flip):
            return lambda bi, n: (0, bi, 0, nb - 1 - n if flip else n, 0)
    else:
        chunks = min(l // CHUNK, 4)
        nb = l // (chunks * CHUNK)
        src = proj
        in_blk = (1, chunks * CHUNK, LANE)
        out_blk = (N_HEADS, 1, chunks * CHUNK, LANE)
        out_view = (N_HEADS, b, l, LANE)

        def in_map(col, flip):
            return lambda bi, n: (bi, nb - 1 - n if flip else n, col)

        def out_map(flip):
            return lambda bi, n: (0, bi, nb - 1 - n if flip else n, 0)

    def specs(flip):
        cols = [cb + j for j in range(3 * N_HEADS)] + [cg]
        return [pl.BlockSpec(in_blk, in_map(col, flip)) for col in cols]

    cm_spec = pl.BlockSpec((1, 2, N_HEADS, HEAD_DIM, HEAD_DIM), lambda bi, n: (bi, 0, 0, 0, 0))
    nm_spec = pl.BlockSpec((1, 2, N_HEADS, 8, HEAD_DIM), lambda bi, n: (bi, 0, 0, 0, 0))
    n_in = 2 * (3 * N_HEADS + 1)
    o_f, o_b, cm, nm = pl.pallas_call(
        functools.partial(_mlstm_kernel, colmajor=colmajor, chunks=chunks),
        grid=(b, nb),
        in_specs=specs(False) + specs(True) + [pl.BlockSpec((1, LANE), lambda bi, n: (0, 0)), cm_spec, nm_spec],
        out_specs=[pl.BlockSpec(out_blk, out_map(False)), pl.BlockSpec(out_blk, out_map(True)), cm_spec, nm_spec],
        out_shape=[jax.ShapeDtypeStruct(out_view, F32), jax.ShapeDtypeStruct(out_view, F32),
                   jax.ShapeDtypeStruct(cm0.shape, F32), jax.ShapeDtypeStruct(nm0.shape, F32)],
        scratch_shapes=[pltpu.VMEM((2, N_HEADS, HEAD_DIM, HEAD_DIM), F32),
                        pltpu.VMEM((2, N_HEADS, 8, HEAD_DIM), F32)],
        compiler_params=_cparams(("parallel", "arbitrary"), 40),
        name="mlstm_scan",
    )(*([src] * n_in), gbias, cm0, nm0)
    head_major = (N_HEADS, b, l, LANE)
    return o_f.reshape(head_major), o_b.reshape(head_major), cm, nm


def _split_bf16(x):
    hi = x.astype(BF16)
    return hi, (x - hi.astype(F32)).astype(BF16)


def _dot3(a, b, nt=False):
    f = _dot_nt if nt else _dot
    return f(a[0], b[0]) + (f(a[0], b[1]) + f(a[1], b[0]))


def _inv_unit_triangular(a_list):
    n = a_list[0].shape[0]
    eye = (lax.broadcasted_iota(jnp.int32, (n, n), 0) == lax.broadcasted_iota(jnp.int32, (n, n), 1)).astype(F32)
    xs = [eye - a for a in a_list]
    splits = [_split_bf16(a) for a in a_list]
    ps = [_dot3(s, s) for s in splits]
    steps = (n - 1).bit_length() - 1
    for step in range(steps):
        last = step == steps - 1
        p_split = [_split_bf16(p) for p in ps]
        lhs = xs if last else [jnp.concatenate([x, p], axis=0) for x, p in zip(xs, ps)]
        prods = [_dot3(_split_bf16(l), s) for l, s in zip(lhs, p_split)]
        xs = [x + pr[:n] for x, pr in zip(xs, prods)]
        if not last:
            ps = [pr[n:] for pr in prods]
    return xs


def _gdn_chunks(probs, states):
    c = probs[0]["q"].shape[0]
    a_list, ldecs = [], []
    for p in probs:
        tri = _order_mask(c, p["rev"])
        strict = _order_mask(c, p["rev"], strict=True)
        ldec = jnp.where(tri, jnp.exp(jnp.minimum(p["gam_c"] - p["gam_r"], 0.0)), 0.0)
        p["kbeta"] = p["k"] * p["beta"]
        kk = _dot3(_split_bf16(p["kbeta"]), _split_bf16(p["k"]), nt=True)
        a_list.append(jnp.where(strict, kk * ldec, 0.0))
        ldecs.append(ldec)
    t_invs = _inv_unit_triangular(a_list)
    sols = []
    for p, t_inv in zip(probs, t_invs):
        p["eg"] = jnp.exp(p["gam_c"])
        rhs = jnp.concatenate([p["v"] * p["beta"], p["kbeta"] * p["eg"]], axis=1)
        sols.append(_dot3(_split_bf16(t_inv), _split_bf16(rhs)))
    qks = [_dot_nt(p["q"].astype(BF16), p["k"].astype(BF16)) * ldec for p, ldec in zip(probs, ldecs)]
    inter = [_dot(jnp.concatenate([sol[:, HEAD_DIM:], p["q"] * p["eg"]], axis=0).astype(BF16), s.astype(BF16))
             for p, sol, s in zip(probs, sols, states)]
    outs, new_states = [], []
    for p, sol, qk, it, s in zip(probs, sols, qks, inter, states):
        g_end = p["gam_c"][0:1] if p["rev"] else p["gam_c"][c - 1:c]
        v_new = (sol[:, :HEAD_DIM] - it[:c]).astype(BF16)
        kg = p["k"] * jnp.exp(g_end - p["gam_c"])
        outs.append(it[c:] + _dot(qk.astype(BF16), v_new))
        new_states.append(jnp.exp(g_end) * s + _dot_tn(kg.astype(BF16), v_new))
    return outs, new_states


def _gdn_kernel(xf_ref, gf_ref, xb_ref, gb_ref, alog_ref, dtb_ref, s0_ref,
                of_ref, ob_ref, sout_ref, s_ref):
    n = pl.program_id(1)

    @pl.when(n == 0)
    def _():
        s_ref[...] = s0_ref[0]

    neg_a = -jnp.exp(alog_ref[...])
    probs, where = [], []
    for d, (x_ref, g_ref, o_ref) in enumerate(((xf_ref, gf_ref, of_ref), (xb_ref, gb_ref, ob_ref))):
        rev = d == 1
        gates = g_ref[0]
        cs = _cumsum_rows(neg_a * _softplus(gates + dtb_ref[...]), rev)
        cs_t = cs.T
        betas = _sigmoid(gates)
        for h in range(N_HEADS):
            ca = 16 + 4 * d + h
            cbeta = 24 + 4 * d + h
            probs.append(dict(
                q=x_ref[0, :, h * HEAD_DIM:(h + 1) * HEAD_DIM],
                k=x_ref[0, :, MIX_W + h * HEAD_DIM:MIX_W + (h + 1) * HEAD_DIM],
                v=x_ref[0, :, 2 * MIX_W + h * HEAD_DIM:2 * MIX_W + (h + 1) * HEAD_DIM],
                gam_c=cs[:, ca:ca + 1], gam_r=cs_t[ca:ca + 1, :], beta=betas[:, cbeta:cbeta + 1], rev=rev))
            where.append((o_ref, d, h))
    outs, new_states = _gdn_chunks(probs, [s_ref[d, h] for _, d, h in where])
    for (o_ref, d, h), o, s_new in zip(where, outs, new_states):
        o_ref[0, :, h * HEAD_DIM:(h + 1) * HEAD_DIM] = o
        s_ref[d, h] = s_new

    @pl.when(n == pl.num_programs(1) - 1)
    def _():
        sout_ref[0] = s_ref[...]


def _gdn_scan(qkv, proj, alog_row, dtb_row, s0):
    b, l, w3 = qkv.shape
    nc = l // CHUNK
    cg = COL_GATES // LANE
    st_spec = pl.BlockSpec((1, 2, N_HEADS, HEAD_DIM, HEAD_DIM), lambda bi, n: (bi, 0, 0, 0, 0))
    row_spec = pl.BlockSpec((1, LANE), lambda bi, n: (0, 0))
    return pl.pallas_call(
        _gdn_kernel,
        grid=(b, nc),
        in_specs=[pl.BlockSpec((1, CHUNK, w3), lambda bi, n: (bi, n, 0)),
                  pl.BlockSpec((1, CHUNK, LANE), lambda bi, n: (bi, n, cg)),
                  pl.BlockSpec((1, CHUNK, w3), lambda bi, n: (bi, nc - 1 - n, 0)),
                  pl.BlockSpec((1, CHUNK, LANE), lambda bi, n: (bi, nc - 1 - n, cg)),
                  row_spec, row_spec, st_spec],
        out_specs=[pl.BlockSpec((1, CHUNK, MIX_W), lambda bi, n: (bi, n, 0)),
                   pl.BlockSpec((1, CHUNK, MIX_W), lambda bi, n: (bi, nc - 1 - n, 0)),
                   st_spec],
        out_shape=[jax.ShapeDtypeStruct((b, l, MIX_W), F32), jax.ShapeDtypeStruct((b, l, MIX_W), F32),
                   jax.ShapeDtypeStruct(s0.shape, F32)],
        scratch_shapes=[pltpu.VMEM((2, N_HEADS, HEAD_DIM, HEAD_DIM), F32)],
        compiler_params=_cparams(("parallel", "arbitrary"), 40),
        name="gdn_scan",
    )(qkv, proj, qkv, proj, alog_row, dtb_row, s0)


def _merge_kernel(oaf, oab, obf, obb, ocf, ocb, ag, bo, cz, mg, x_ref, mod_ref, gain_ref, wbr_ref, wout_ref,
                  lng_ref, lnb_ref, out_ref, *, alpha):
    d_model = x_ref.shape[-1]
    branches = ((oaf, oab, ag, _silu), (obf, obb, bo, _sigmoid), (ocf, ocb, cz, _silu))
    mixed = None
    for nbr, (of_ref, ob_ref, gate_ref, act) in enumerate(branches):
        parts = []
        for h in range(N_HEADS):
            if len(of_ref.shape) == 4:
                oh = of_ref[h, 0] + ob_ref[h, 0]
            else:
                oh = of_ref[0, :, h * HEAD_DIM:(h + 1) * HEAD_DIM] + ob_ref[0, :, h * HEAD_DIM:(h + 1) * HEAD_DIM]
            parts.append(oh * lax.rsqrt(jnp.mean(oh * oh, axis=-1, keepdims=True) + RMS_EPS))
        y = jnp.concatenate(parts, axis=1) * gain_ref[nbr:nbr + 1, :] * act(gate_ref[0])
        term = _sigmoid(mg[0, :, nbr * d_model:(nbr + 1) * d_model]) * _dot(y.astype(BF16), wbr_ref[nbr])
        mixed = term if mixed is None else mixed + term
    sub = _dot(mixed.astype(BF16), wout_ref[...])
    z = alpha * x_ref[0] + mod_ref[0, 2:3, :] * sub
    out_ref[0] = _ln(z) * lng_ref[...] + lnb_ref[...]


def _merge(outs, proj, x, mod, gain, w_branch, w_out, ln_g, ln_b, alpha, tl):
    b, l, d = x.shape
    u = MIX_W

    def tok(width, col):
        return pl.BlockSpec((1, tl, width), lambda bi, i: (bi, i, col))

    def const(shape):
        return pl.BlockSpec(shape, lambda bi, i: (0,) * len(shape))

    heads = pl.BlockSpec((N_HEADS, 1, tl, LANE), lambda bi, i: (0, bi, i, 0))
    return pl.pallas_call(
        functools.partial(_merge_kernel, alpha=alpha),
        grid=(b, l // tl),
        in_specs=[tok(u, 0), tok(u, 0), heads, heads, tok(u, 0), tok(u, 0),
            tok(u, (COL_A + 4 * MIX_W) // u), tok(u, (COL_B + 3 * MIX_W) // u), tok(u, COL_CZ // u),
            tok(3 * d, COL_MERGE // (3 * d)), tok(d, 0),
            pl.BlockSpec((1, 6, d), lambda bi, i: (bi, 0, 0)),
            const((3, u)), const((3, u, d)), const((d, d)), const((1, d)), const((1, d))],
        out_specs=tok(d, 0),
        out_shape=jax.ShapeDtypeStruct((b, l, d), F32),
        compiler_params=_cparams(("parallel", "parallel"), 48),
        name="merge",
    )(*outs, proj, proj, proj, proj, x, mod, gain, w_branch, w_out, ln_g, ln_b)


def _ffn_kernel(x_ref, mod_ref, wg_ref, wu_ref, wd_ref, lng_ref, lnb_ref, out_ref, h_ref, acc_ref, *, alpha):
    f = pl.program_id(2)

    @pl.when(f == 0)
    def _():
        h_ref[...] = (_ln(x_ref[0]) * (1.0 + mod_ref[0, 4:5, :]) + mod_ref[0, 3:4, :]).astype(BF16)
        acc_ref[...] = jnp.zeros_like(acc_ref)

    h = h_ref[...]
    act = _silu(_dot(h, wg_ref[...])) * _dot(h, wu_ref[...])
    acc_ref[...] += _dot(act.astype(BF16), wd_ref[...])

    @pl.when(f == pl.num_programs(2) - 1)
    def _():
        z = alpha * x_ref[0] + mod_ref[0, 5:6, :] * acc_ref[...]
        out_ref[0] = _ln(z) * lng_ref[...] + lnb_ref[...]


def _ffn(x, mod, w_gate, w_up, w_down, ln_g, ln_b, alpha, tl, tf=512):
    b, l, d = x.shape
    f = w_gate.shape[1]
    return pl.pallas_call(
        functools.partial(_ffn_kernel, alpha=alpha),
        grid=(b, l // tl, f // tf),
        in_specs=[pl.BlockSpec((1, tl, d), lambda bi, i, j: (bi, i, 0)),
                  pl.BlockSpec((1, 6, d), lambda bi, i, j: (bi, 0, 0)),
                  pl.BlockSpec((d, tf), lambda bi, i, j: (0, j)),
                  pl.BlockSpec((d, tf), lambda bi, i, j: (0, j)),
                  pl.BlockSpec((tf, d), lambda bi, i, j: (j, 0)),
                  pl.BlockSpec((1, d), lambda bi, i, j: (0, 0)),
                  pl.BlockSpec((1, d), lambda bi, i, j: (0, 0))],
        out_specs=pl.BlockSpec((1, tl, d), lambda bi, i, j: (bi, i, 0)),
        out_shape=jax.ShapeDtypeStruct((b, l, d), F32),
        scratch_shapes=[pltpu.VMEM((tl, d), BF16), pltpu.VMEM((tl, d), F32)],
        compiler_params=_cparams(("parallel", "parallel", "arbitrary"), 48),
        name="ffn",
    )(x, mod, w_gate, w_up, w_down, ln_g, ln_b)


def _top2_combine(logits):
    lane = lax.broadcasted_iota(jnp.int32, logits.shape, 1).astype(F32)
    neg = -jnp.inf
    lg = jnp.where(lane < N_EXPERTS, logits, neg)
    m1 = jnp.max(lg, axis=1, keepdims=True)
    i1 = jnp.min(jnp.where(lg == m1, lane, float(LANE)), axis=1, keepdims=True)
    lg2 = jnp.where(lane == i1, neg, lg)
    m2 = jnp.max(lg2, axis=1, keepdims=True)
    i2 = jnp.min(jnp.where(lg2 == m2, lane, float(LANE)), axis=1, keepdims=True)
    e2 = jnp.exp(m2 - m1)
    p1 = 1.0 / (1.0 + e2)
    chosen = (lane == i1) | (lane == i2)
    return jnp.where(lane == i1, p1, jnp.where(lane == i2, e2 * p1, 0.0)), chosen


def _moe_kernel(x_ref, mod_ref, wr_ref, wg_ref, wu_ref, wd_ref, lng_ref, lnb_ref, out_ref,
                h_ref, acc_ref, comb_ref, pos_ref, post_ref, col_ref, xc_ref, yc_ref, cnt_ref, *, alpha, rb):
    e = pl.program_id(2)
    f = pl.program_id(3)
    tl, d = h_ref.shape

    @pl.when((e == 0) & (f == 0))
    def _():
        h = _ln(x_ref[0]) * (1.0 + mod_ref[0, 4:5, :]) + mod_ref[0, 3:4, :]
        h_ref[...] = h.astype(BF16)
        comb, chosen = _top2_combine(_dot(h, wr_ref[...], precision=HIGHEST))
        comb_ref[...] = comb
        before = lax.broadcasted_iota(jnp.int32, (tl, tl), 1) < lax.broadcasted_iota(jnp.int32, (tl, tl), 0)
        rank = _dot(jnp.where(before, 1.0, 0.0).astype(BF16), jnp.where(chosen, 1.0, 0.0).astype(BF16))
        pos = jnp.where(chosen, rank, -1.0)
        pos_ref[...] = pos
        pos_t = pos.T[0:N_EXPERTS, :]
        post_ref[...] = pos_t
        for ex in range(N_EXPERTS):
            cnt_ref[ex] = jnp.sum(jnp.where(pos_t[ex:ex + 1, :] >= 0.0, 1.0, 0.0)).astype(jnp.int32)
        acc_ref[...] = jnp.zeros_like(acc_ref)

    nblk = (cnt_ref[e] + (rb - 1)) // rb

    def rows(r):
        return pl.ds(pl.multiple_of(r * rb, rb), rb)

    @pl.when(f == 0)
    def _():
        lane = lax.broadcasted_iota(jnp.int32, (tl, LANE), 1)
        col_ref[0] = jnp.sum(jnp.where(lane == e, pos_ref[...], 0.0), axis=1, keepdims=True)
        col_ref[1] = jnp.sum(jnp.where(lane == e, comb_ref[...], 0.0), axis=1, keepdims=True)
        pos_row = post_ref[pl.ds(e, 1), :]

        def gather(r, carry):
            slot = (lax.broadcasted_iota(jnp.int32, (rb, tl), 0) + r * rb).astype(F32)
            sel = jnp.where(pos_row == slot, 1.0, 0.0).astype(BF16)
            xc_ref[rows(r), :] = _dot(sel, h_ref[...]).astype(BF16)
            yc_ref[rows(r), :] = jnp.zeros((rb, d), F32)
            return carry

        lax.fori_loop(0, nblk, gather, 0)

    def expert(r, carry):
        xb = xc_ref[rows(r), :]
        act = _silu(_dot(xb, wg_ref[0])) * _dot(xb, wu_ref[0])
        yc_ref[rows(r), :] += _dot(act.astype(BF16), wd_ref[0])
        return carry

    lax.fori_loop(0, nblk, expert, 0)

    @pl.when(f == pl.num_programs(3) - 1)
    def _():
        pos_col = col_ref[0]
        weight = col_ref[1]

        def scatter(r, carry):
            slot = (lax.broadcasted_iota(jnp.int32, (tl, rb), 1) + r * rb).astype(F32)
            sel_t = jnp.where(pos_col == slot, 1.0, 0.0).astype(BF16)
            acc_ref[...] += weight * _dot(sel_t, yc_ref[rows(r), :].astype(BF16))
            return carry

        lax.fori_loop(0, nblk, scatter, 0)

    @pl.when((e == pl.num_programs(2) - 1) & (f == pl.num_programs(3) - 1))
    def _():
        z = alpha * x_ref[0] + mod_ref[0, 5:6, :] * acc_ref[...]
        out_ref[0] = _ln(z) * lng_ref[...] + lnb_ref[...]


def _moe(x, mod, w_router, w_gate, w_up, w_down, ln_g, ln_b, alpha, tl, tf=512):
    b, l, d = x.shape
    ne, _, f = w_gate.shape
    rb = min(MOE_ROW_BLOCK, tl)
    compact_rows = -(-tl // rb) * rb
    return pl.pallas_call(
        functools.partial(_moe_kernel, alpha=alpha, rb=rb),
        grid=(b, l // tl, ne, f // tf),
        in_specs=[pl.BlockSpec((1, tl, d), lambda bi, i, e, j: (bi, i, 0)),
                  pl.BlockSpec((1, 6, d), lambda bi, i, e, j: (bi, 0, 0)),
                  pl.BlockSpec((d, LANE), lambda bi, i, e, j: (0, 0)),
                  pl.BlockSpec((1, d, tf), lambda bi, i, e, j: (e, 0, j)),
                  pl.BlockSpec((1, d, tf), lambda bi, i, e, j: (e, 0, j)),
                  pl.BlockSpec((1, tf, d), lambda bi, i, e, j: (e, j, 0)),
                  pl.BlockSpec((1, d), lambda bi, i, e, j: (0, 0)),
                  pl.BlockSpec((1, d), lambda bi, i, e, j: (0, 0))],
        out_specs=pl.BlockSpec((1, tl, d), lambda bi, i, e, j: (bi, i, 0)),
        out_shape=jax.ShapeDtypeStruct((b, l, d), F32),
        scratch_shapes=[pltpu.VMEM((tl, d), BF16),
                        pltpu.VMEM((tl, d), F32),
                        pltpu.VMEM((tl, LANE), F32),
                        pltpu.VMEM((tl, LANE), F32),
                        pltpu.VMEM((N_EXPERTS, tl), F32),
                        pltpu.VMEM((2, tl, 1), F32),
                        pltpu.VMEM((compact_rows, d), BF16),
                        pltpu.VMEM((compact_rows, d), F32),
                        pltpu.SMEM((N_EXPERTS,), jnp.int32)],
        compiler_params=_cparams(("parallel", "parallel", "arbitrary", "arbitrary"), 56),
        name="moe",
    )(x, mod, w_router, w_gate, w_up, w_down, ln_g, ln_b)


def _reorder_w_in(w):
    d = w.shape[0]
    o_a, o_b = 0, 5 * MIX_W
    o_bgt = o_b + 4 * MIX_W
    o_cqkv = o_bgt + 16
    o_cz = o_cqkv + 3 * MIX_W
    o_cgt = o_cz + MIX_W
    o_merge = o_cgt + 16
    parts = [w[:, o_merge:o_merge + 3 * d], w[:, o_a:o_bgt], w[:, o_cqkv:o_cgt],
             w[:, o_bgt:o_bgt + 16], w[:, o_cgt:o_cgt + 16],
             jnp.zeros((d, N_PROJ - COL_GATES - 32), w.dtype)]
    return jnp.concatenate(parts, axis=1)


def _pad_row(vals, offset):
    row = jnp.zeros((1, LANE), F32)
    return lax.dynamic_update_slice(row, vals.reshape(1, -1).astype(F32), (0, offset))


def _tile(l, pref):
    return pref if l % pref == 0 else l


def kernel(x, c, ctx, c_ctx, w_ada, b_ada, w_in, conv_w, lb_raw, m_gate_bias, gdn_a_log, gdn_dt_bias, mix_norm,
           w_branch, w_out, ln_g, ln_b, ffn_w_gate, ffn_w_up, ffn_w_down, moe_router, moe_w_gate, moe_w_up,
           moe_w_down):
    depth, d_model = w_in.shape[0], w_in.shape[1]
    b, l_lat, _ = x.shape
    l_ctx = ctx.shape[1]
    alpha = float((2 * depth) ** 0.25)
    assert d_model == 1024 and w_in.shape[2] == 9 * MIX_W + 3 * MIX_W + MIX_W + 32 + 3 * d_model

    lower = _lower_bounds(lb_raw.astype(F32))
    n_rows = -(-(b + 1) // 8) * 8
    cvec = jnp.concatenate([c, c_ctx[None, :], jnp.zeros((n_rows - b - 1, d_model), F32)], axis=0)
    mods = _ada(cvec, w_ada, b_ada)

    zeros_state = jnp.zeros((b, 2, N_HEADS, HEAD_DIM, HEAD_DIM), F32)
    zeros_hgrn = jnp.zeros((b, N_HEADS, 2, HEAD_DIM, HEAD_DIM), F32)
    zeros_nm = jnp.zeros((b, 2, N_HEADS, 8, HEAD_DIM), F32)

    pack = 1024 // l_ctx if (1024 % l_ctx == 0 and b % max(1024 // l_ctx, 1) == 0) else 1
    ctx_packed = (b // pack, pack * l_ctx, d_model)

    for l in range(depth):
        mod_l = mods[l, :b].reshape(b, 6, d_model)
        mod_c = jnp.broadcast_to(mods[l, b].reshape(1, 6, d_model), (b, 6, d_model))
        w_in_l = _reorder_w_in(w_in[l]).astype(BF16)
        lb_row = lower[l].reshape(1, MIX_W)
        gbias = _pad_row(m_gate_bias[l], 0)
        alog_row = _pad_row(gdn_a_log[l], 16)
        dtb_row = _pad_row(gdn_dt_bias[l], 16)
        gain = mix_norm[l]
        wbr = w_branch[l].astype(BF16)
        wo = w_out[l].astype(BF16)
        g0, b0 = ln_g[l, 0].reshape(1, -1), ln_b[l, 0].reshape(1, -1)
        g1, b1 = ln_g[l, 1].reshape(1, -1), ln_b[l, 1].reshape(1, -1)

        p_c = _in_proj(ctx.reshape(ctx_packed), mod_c[:ctx_packed[0]], w_in_l, ctx_packed[1])
        p_c = p_c.reshape(b, l_ctx, N_PROJ)
        p_l = _in_proj(x, mod_l, w_in_l, _tile(l_lat, 1024))
        qkv_c = _gdn_conv(p_c, conv_w[l], _tile(l_ctx, 256))
        qkv_l = _gdn_conv(p_l, conv_w[l], _tile(l_lat, 512))

        oa_c = _hgrn_scan(p_c, lb_row, zeros_hgrn)
        oa_l = _hgrn_scan(p_l, lb_row, oa_c[2])
        ob_c = _mlstm_scan(p_c, gbias, zeros_state, zeros_nm, colmajor=False)
        ob_l = _mlstm_scan(p_l, gbias, ob_c[2], ob_c[3], colmajor=True)
        oc_c = _gdn_scan(qkv_c, p_c, alog_row, dtb_row, zeros_state)
        oc_l = _gdn_scan(qkv_l, p_l, alog_row, dtb_row, oc_c[2])

        def channel_mixer(t, mod, tl):
            i = l // 2
            if l % 2 == 0:
                return _ffn(t, mod, ffn_w_gate[i].astype(BF16), ffn_w_up[i].astype(BF16),
                            ffn_w_down[i].astype(BF16), g1, b1, alpha, tl)
            w_r = jnp.concatenate([moe_router[i], jnp.zeros((d_model, LANE - N_EXPERTS), F32)], axis=1)
            return _moe(t, mod, w_r, moe_w_gate[i].astype(BF16), moe_w_up[i].astype(BF16),
                        moe_w_down[i].astype(BF16), g1, b1, alpha, tl)

        outs_l = (oa_l[0], oa_l[1], ob_l[0], ob_l[1], oc_l[0], oc_l[1])
        x = _merge(outs_l, p_l, x, mod_l, gain, wbr, wo, g0, b0, alpha, _tile(l_lat, 256))
        x = channel_mixer(x, mod_l, _tile(l_lat, 1024))
        if l < depth - 1:
            outs_c = (oa_c[0], oa_c[1], ob_c[0], ob_c[1], oc_c[0], oc_c[1])
            ctx = _merge(outs_c, p_c, ctx, mod_c, gain, wbr, wo, g0, b0, alpha, _tile(l_ctx, 256))
            ctx = channel_mixer(ctx.reshape(ctx_packed), mod_c[:ctx_packed[0]], ctx_packed[1]).reshape(ctx.shape)
    return x
```

```python
import functools

import jax
import jax.numpy as jnp
from jax import lax
from jax.experimental import pallas as pl
from jax.experimental.pallas import tpu as pltpu

F32 = jnp.float32
BF16 = jnp.bfloat16
HIGHEST = lax.Precision.HIGHEST

HEAD_DIM = 128
N_HEADS = 4
MIX_W = N_HEADS * HEAD_DIM
CHUNK = 64
GRID_W = 64
SUB = 16
N_EXPERTS = 8
MOE_ROW_BLOCK = 288
LN_EPS = 1e-5
RMS_EPS = 1e-6
L2_EPS = 1e-6
MASK_NEG = -1e30
LOG2_E = 1.4426950408889634
LANE = 128

COL_MERGE = 0
COL_A = 3 * 1024
COL_B = COL_A + 5 * MIX_W
COL_CQKV = COL_B + 4 * MIX_W
COL_CZ = COL_CQKV + 3 * MIX_W
COL_GATES = COL_CZ + MIX_W
N_PROJ = COL_GATES + 4 * LANE


def _cparams(sem, vmem_mb=None):
    kw = dict(dimension_semantics=sem)
    if vmem_mb is not None:
        kw["vmem_limit_bytes"] = vmem_mb << 20
    return pltpu.CompilerParams(**kw)


def _dot(a, b, precision=None):
    return jnp.dot(a, b, preferred_element_type=F32, precision=precision)


def _dot_nt(a, b, precision=None):
    return lax.dot_general(a, b, (((1,), (1,)), ((), ())), preferred_element_type=F32, precision=precision)


def _dot_tn(a, b):
    return lax.dot_general(a, b, (((0,), (0,)), ((), ())), preferred_element_type=F32)


def _ln(x):
    mu = jnp.mean(x, axis=-1, keepdims=True)
    xc = x - mu
    var = jnp.mean(xc * xc, axis=-1, keepdims=True)
    return xc * lax.rsqrt(var + LN_EPS)


def _sigmoid(x):
    return 1.0 / (1.0 + jnp.exp(-x))


def _silu(x):
    return x * _sigmoid(x)


def _softplus(x):
    return jnp.maximum(x, 0.0) + jnp.log(1.0 + jnp.exp(-jnp.abs(x)))


def _log_sigmoid(x):
    return -_softplus(-x)


def _order_mask(n, rev, strict=False):
    r = lax.broadcasted_iota(jnp.int32, (n, n), 0)
    c = lax.broadcasted_iota(jnp.int32, (n, n), 1)
    if rev:
        return (c > r) if strict else (c >= r)
    return (c < r) if strict else (c <= r)


def _cumsum_rows(x, rev):
    m = _order_mask(x.shape[0], rev).astype(F32)
    return _dot(m, x, precision=HIGHEST)


def _ada_kernel(c_ref, w_ref, b_ref, o_ref):
    s = _silu(c_ref[...])
    o_ref[0] = _dot(s, w_ref[0], precision=HIGHEST) + b_ref[0]


def _ada(cvec, w_ada, b_ada, tn=1536):
    depth, d, n6 = w_ada.shape
    rows = cvec.shape[0]
    return pl.pallas_call(
        _ada_kernel,
        grid=(depth, n6 // tn),
        in_specs=[pl.BlockSpec((rows, d), lambda l, n: (0, 0)),
                  pl.BlockSpec((1, d, tn), lambda l, n: (l, 0, n)),
                  pl.BlockSpec((1, 1, tn), lambda l, n: (l, 0, n))],
        out_specs=pl.BlockSpec((1, rows, tn), lambda l, n: (l, 0, n)),
        out_shape=jax.ShapeDtypeStruct((depth, rows, n6), F32),
        compiler_params=_cparams(("parallel", "parallel"), 40),
        name="ada",
    )(cvec, w_ada, b_ada.reshape(depth, 1, n6))


def _lower_bounds_kernel(lb_ref, o_ref):
    x = lb_ref[...]
    depth = x.shape[0]
    mx = jnp.max(x, axis=0, keepdims=True)
    e = jnp.exp(x - mx)
    p = e / jnp.sum(e, axis=0, keepdims=True)
    acc = jnp.zeros_like(p[0:1])
    for l in range(depth):
        o_ref[l:l + 1, :] = acc
        if l + 1 < depth:
            acc = acc + p[l + 1:l + 2]


def _lower_bounds(lb_raw):
    return pl.pallas_call(
        _lower_bounds_kernel,
        out_shape=jax.ShapeDtypeStruct(lb_raw.shape, F32),
        name="lower_bounds",
    )(lb_raw)


def _in_proj_kernel(x_ref, mod_ref, w_ref, o_ref, h_ref):
    @pl.when(pl.program_id(2) == 0)
    def _():
        shift = mod_ref[0, 0:1, :]
        scale = mod_ref[0, 1:2, :]
        h_ref[...] = (_ln(x_ref[0]) * (1.0 + scale) + shift).astype(BF16)

    o_ref[0] = _dot(h_ref[...], w_ref[...])


def _in_proj(x, mod, w, tl, tn=2048):
    b, l, d = x.shape
    n_proj = w.shape[1]
    return pl.pallas_call(
        _in_proj_kernel,
        grid=(b, l // tl, n_proj // tn),
        in_specs=[pl.BlockSpec((1, tl, d), lambda bi, i, n: (bi, i, 0)),
                  pl.BlockSpec((1, 6, d), lambda bi, i, n: (bi, 0, 0)),
                  pl.BlockSpec((d, tn), lambda bi, i, n: (0, n))],
        out_specs=pl.BlockSpec((1, tl, tn), lambda bi, i, n: (bi, i, n)),
        out_shape=jax.ShapeDtypeStruct((b, l, n_proj), F32),
        scratch_shapes=[pltpu.VMEM((tl, d), BF16)],
        compiler_params=_cparams(("parallel", "parallel", "arbitrary"), 48),
        name="in_proj",
    )(x, mod, w)


def _conv_kernel(x_ref, prev_ref, next_ref, w_ref, o_ref):
    i = pl.program_id(1)
    last = pl.num_programs(1) - 1
    x = x_ref[0]
    tl = x.shape[0]
    row = lax.broadcasted_iota(jnp.int32, x.shape, 0)
    before = jnp.where(i > 0, prev_ref[0, 7:8, :], 0.0)
    after = jnp.where(i < last, next_ref[0, 0:1, :], 0.0)
    x_dn = jnp.where(row == 0, before, pltpu.roll(x, 1, axis=0))
    x_up = jnp.where(row == tl - 1, after, pltpu.roll(x, tl - 1, axis=0))
    y = _silu(x_dn * w_ref[0:1, :] + x * w_ref[1:2, :] + x_up * w_ref[2:3, :])
    for j in range(3 * N_HEADS):
        t = y[:, j * HEAD_DIM:(j + 1) * HEAD_DIM]
        if j < 2 * N_HEADS:
            t = t * lax.rsqrt(jnp.sum(t * t, axis=-1, keepdims=True) + L2_EPS)
            if j < N_HEADS:
                t = t * HEAD_DIM ** -0.5
        o_ref[0, :, j * HEAD_DIM:(j + 1) * HEAD_DIM] = t


def _gdn_conv(proj, conv_w, tl):
    b, l, _ = proj.shape
    w3 = 3 * MIX_W
    cb = COL_CQKV // w3
    nb8 = l // 8
    r8 = tl // 8
    return pl.pallas_call(
        _conv_kernel,
        grid=(b, l // tl),
        in_specs=[pl.BlockSpec((1, tl, w3), lambda bi, i: (bi, i, cb)),
                  pl.BlockSpec((1, 8, w3), lambda bi, i: (bi, jnp.maximum(i * r8 - 1, 0), cb)),
                  pl.BlockSpec((1, 8, w3), lambda bi, i: (bi, jnp.minimum((i + 1) * r8, nb8 - 1), cb)),
                  pl.BlockSpec((3, w3), lambda bi, i: (0, 0))],
        out_specs=pl.BlockSpec((1, tl, w3), lambda bi, i: (bi, i, 0)),
        out_shape=jax.ShapeDtypeStruct((b, l, w3), F32),
        compiler_params=_cparams(("parallel", "parallel"), 40),
        name="gdn_conv",
    )(proj, proj, proj, conv_w)


def _hgrn_chunks(probs, states, emat):
    c = probs[0]["q"].shape[0]
    nb = c // SUB
    for p in probs:
        lb = p["lb"]
        p["k"] = (1.0 - lb) * _sigmoid(-p["fpre"])
        p["g"] = _cumsum_rows(jnp.log(lb + (1.0 - lb) * _sigmoid(p["fpre"])), p["rev"])

    walls = []
    for p in probs:
        q = p["q"]
        g2 = p["g"] * LOG2_E
        lk2 = jnp.log(p["k"]) * LOG2_E
        rel = lk2 - g2
        rows = []
        for j in range(nb):
            r0 = j * SUB
            gj = g2[r0:r0 + SUB]
            qj = q[r0:r0 + SUB]
            pieces = []
            for s in range(SUB):
                e2 = jnp.exp2(jnp.minimum(gj + rel[r0 + s:r0 + s + 1], lk2[r0 + s:r0 + s + 1]))
                pieces.append((e2 * qj).astype(BF16))
            rows.append(jnp.concatenate(pieces, axis=1))
        walls.append(jnp.concatenate(rows, axis=0))
    a_diags = [_dot(w, emat) for w in walls]

    a_offs = []
    for p in probs:
        q, k, g, rev = p["q"], p["k"], p["g"], p["rev"]
        offs = []
        for i in range(nb):
            r0 = i * SUB
            if (not rev and i == 0) or (rev and i == nb - 1):
                offs.append(jnp.zeros((SUB, c), F32))
                continue
            bi = g[r0 + SUB:r0 + SUB + 1] if rev else g[r0 - 1:r0]
            lhs = (q[r0:r0 + SUB] * jnp.exp(g[r0:r0 + SUB] - bi)).astype(BF16)
            rhs = (k * jnp.exp(jnp.minimum(bi - g, 0.0))).astype(BF16)
            offs.append(_dot_nt(lhs, rhs))
        a_offs.append(jnp.concatenate(offs, axis=0))

    r = lax.broadcasted_iota(jnp.int32, (c, c), 0)
    col = lax.broadcasted_iota(jnp.int32, (c, c), 1)
    shift = SUB.bit_length() - 1
    rb = lax.shift_right_logical(r, shift)
    cb = lax.shift_right_logical(col, shift)
    inters = [_dot_nt((p["q"] * jnp.exp(p["g"])).astype(BF16), s_t.astype(BF16)) for p, s_t in zip(probs, states)]
    outs, new_states = [], []
    for p, a_diag, a_off, inter, s_t in zip(probs, a_diags, a_offs, inters, states):
        rev, g = p["rev"], p["g"]
        order = (col >= r) if rev else (col <= r)
        off_side = (cb > rb) if rev else (cb < rb)
        a = jnp.where((rb == cb) & order, a_diag, jnp.where(off_side, a_off, 0.0))
        g_end = g[0:1] if rev else g[c - 1:c]
        vb = p["v"].astype(BF16)
        outs.append(_dot(a.astype(BF16), vb) + inter)
        kd = (p["k"] * jnp.exp(g_end - g)).astype(BF16)
        new_states.append(s_t * jnp.exp(g_end) + _dot_tn(vb, kd))
    return outs, new_states


def _hgrn_kernel(qf_ref, ff_ref, vf_ref, qb_ref, fb_ref, vb_ref, lb_ref, e_ref, s0_ref,
                 of_ref, ob_ref, sout_ref, s_ref):
    n = pl.program_id(1)

    @pl.when(n == 0)
    def _():
        s_ref[...] = s0_ref[0]

    probs, where = [], []
    for d, (q_ref, f_ref, v_ref, o_ref) in enumerate(((qf_ref, ff_ref, vf_ref, of_ref),
                                                      (qb_ref, fb_ref, vb_ref, ob_ref))):
        for h in range(N_HEADS):
            sl = slice(h * HEAD_DIM, (h + 1) * HEAD_DIM)
            probs.append(dict(q=q_ref[0, :, sl], fpre=f_ref[0, :, sl], v=v_ref[0, :, sl], lb=lb_ref[:, sl],
                              rev=d == 1))
            where.append((o_ref, sl, h, d))
    outs, new_states = _hgrn_chunks(probs, [s_ref[h, d] for _, _, h, d in where], e_ref[...])
    for (o_ref, sl, h, d), o, s_new in zip(where, outs, new_states):
        o_ref[0, :, sl] = o
        s_ref[h, d] = s_new

    @pl.when(n == pl.num_programs(1) - 1)
    def _():
        sout_ref[0] = s_ref[...]


def _hgrn_emat():
    r = jnp.arange(SUB * HEAD_DIM)[:, None] // HEAD_DIM
    c = jnp.arange(CHUNK)[None, :] % SUB
    return (r == c).astype(BF16)


def _hgrn_scan(proj, lb, s0):
    b, l, _ = proj.shape
    nc = l // CHUNK
    ca = COL_A // MIX_W

    def fwd(col):
        return pl.BlockSpec((1, CHUNK, MIX_W), lambda bi, n: (bi, n, ca + col))

    def bwd(col):
        return pl.BlockSpec((1, CHUNK, MIX_W), lambda bi, n: (bi, nc - 1 - n, ca + col))

    st_spec = pl.BlockSpec((1, N_HEADS, 2, HEAD_DIM, HEAD_DIM), lambda bi, n: (bi, 0, 0, 0, 0))
    return pl.pallas_call(
        _hgrn_kernel,
        grid=(b, nc),
        in_specs=[fwd(0), fwd(1), fwd(3), bwd(0), bwd(2), bwd(3),
                  pl.BlockSpec((1, MIX_W), lambda bi, n: (0, 0)),
                  pl.BlockSpec((SUB * HEAD_DIM, CHUNK), lambda bi, n: (0, 0)),
                  st_spec],
        out_specs=[pl.BlockSpec((1, CHUNK, MIX_W), lambda bi, n: (bi, n, 0)),
                   pl.BlockSpec((1, CHUNK, MIX_W), lambda bi, n: (bi, nc - 1 - n, 0)),
                   st_spec],
        out_shape=[jax.ShapeDtypeStruct((b, l, MIX_W), F32),
                   jax.ShapeDtypeStruct((b, l, MIX_W), F32),
                   jax.ShapeDtypeStruct(s0.shape, F32)],
        scratch_shapes=[pltpu.VMEM((N_HEADS, 2, HEAD_DIM, HEAD_DIM), F32)],
        compiler_params=_cparams(("parallel", "arbitrary"), 40),
        name="hgrn_scan",
    )(proj, proj, proj, proj, proj, proj, lb, _hgrn_emat(), s0)


def _mlstm_chunks(probs, states):
    c = probs[0]["q"].shape[0]
    for p, (cm, nvec, m) in zip(probs, states):
        tri = _order_mask(c, p["rev"])
        log_d = jnp.where(tri, p["bb_c"] - p["bb_r"] + p["ic_r"], MASK_NEG)
        log_inter = p["bb_c"] + m
        m_t = jnp.maximum(jnp.max(log_d, axis=1, keepdims=True), log_inter)
        p["m_t"] = m_t
        p["dmat"] = jnp.where(tri, jnp.exp(log_d - m_t), 0.0)
        p["inter"] = jnp.exp(log_inter - m_t)
        p["qb"], p["kb"], p["vb"] = p["q"].astype(BF16), p["k"].astype(BF16), p["v"].astype(BF16)
    scores = [_dot_nt(p["qb"], p["kb"]) * p["dmat"] for p in probs]
    carried = [_dot(p["qb"], cm.astype(BF16)) for p, (cm, _, _) in zip(probs, states)]
    local = [_dot(s.astype(BF16), p["vb"]) for p, s in zip(probs, scores)]
    outs, new_states = [], []
    for p, s, car, loc, (cm, nvec, m) in zip(probs, scores, carried, local, states):
        m_t, inter = p["m_t"], p["inter"]
        den = jnp.sum(s, axis=1, keepdims=True) + inter * jnp.sum(p["q"] * nvec, axis=1, keepdims=True)
        outs.append((loc + inter * car) / jnp.maximum(jnp.abs(den), jnp.exp(-m_t)))
        m_new = m_t[0:1] if p["rev"] else m_t[c - 1:c]
        bb_end = p["bb_c"][0:1] if p["rev"] else p["bb_c"][c - 1:c]
        decay = jnp.exp(bb_end + m - m_new)
        kw = p["k"] * jnp.exp(bb_end - p["bb_c"] + p["ic_c"] - m_new)
        new_states.append((decay * cm + _dot_tn(kw.astype(BF16), p["vb"]),
                           decay * nvec + jnp.sum(kw, axis=0, keepdims=True), m_new))
    return outs, new_states


def _column_view(ref):
    rows, group, width = ref.shape[-3:]
    return ref.reshape(rows * group, width), group


def _chunk_rows(j):
    start = j * CHUNK
    return pl.ds(start if isinstance(start, int) else pl.multiple_of(start, CHUNK), CHUNK)


def _load_chunk(ref, j, colmajor):
    if colmajor:
        flat, group = _column_view(ref)
        return flat[pl.ds(j, CHUNK, stride=group), :]
    return ref[0, _chunk_rows(j), :]


def _store_chunk(ref, j, val, colmajor):
    if colmajor:
        flat, group = _column_view(ref)
        flat[pl.ds(j, CHUNK, stride=group), :] = val
    else:
        ref[0, _chunk_rows(j), :] = val


def _mlstm_kernel(*refs, colmajor, chunks):
    per_dir = 3 * N_HEADS + 1
    fwd, bwd = refs[:per_dir], refs[per_dir:2 * per_dir]
    bias_ref, cm0_ref, nm0_ref, of_ref, ob_ref, cmout_ref, nmout_ref, cm_ref, nm_ref = refs[2 * per_dir:]
    n = pl.program_id(1)

    @pl.when(n == 0)
    def _():
        cm_ref[...] = cm0_ref[0]
        nm_ref[...] = nm0_ref[0]

    def step(i, carry):
        probs, where = [], []
        for d, (in_refs, o_ref) in enumerate(((fwd, of_ref), (bwd, ob_ref))):
            rev = d == 1
            j = chunks - 1 - i if rev else i
            gates = _load_chunk(in_refs[-1], j, colmajor) + bias_ref[...]
            cs = _cumsum_rows(_log_sigmoid(gates), rev)
            cs_t = cs.T
            gates_t = gates.T
            for h in range(N_HEADS):
                ci = 4 * d + h
                cf = 8 + 4 * d + h
                probs.append(dict(
                    q=_load_chunk(in_refs[h], j, colmajor),
                    k=_load_chunk(in_refs[N_HEADS + h], j, colmajor) * HEAD_DIM ** -0.5,
                    v=_load_chunk(in_refs[2 * N_HEADS + h], j, colmajor),
                    bb_c=cs[:, cf:cf + 1], bb_r=cs_t[cf:cf + 1, :],
                    ic_c=gates[:, ci:ci + 1], ic_r=gates_t[ci:ci + 1, :], rev=rev))
                where.append((o_ref, j, d, h))
        states = [(cm_ref[d, h], nm_ref[d, h, 0:1, :], nm_ref[d, h, 1:2, 0:1]) for _, _, d, h in where]
        outs, new_states = _mlstm_chunks(probs, states)
        for (o_ref, j, d, h), o, (cm_new, n_new, m_new) in zip(where, outs, new_states):
            _store_chunk(o_ref.at[h], j, o, colmajor)
            cm_ref[d, h] = cm_new
            nm_ref[d, h, 0:1, :] = n_new
            nm_ref[d, h, 1:2, :] = jnp.broadcast_to(m_new, (1, HEAD_DIM))
        return carry

    lax.fori_loop(0, chunks, step, 0)

    @pl.when(n == pl.num_programs(1) - 1)
    def _():
        cmout_ref[0] = cm_ref[...]
        nmout_ref[0] = nm_ref[...]


def _mlstm_scan(proj, gbias, cm0, nm0, colmajor):
    b, l, n_proj = proj.shape
    cb = COL_B // LANE
    cg = COL_GATES // LANE
    if colmajor:
        chunks = 8
        assert l == CHUNK * GRID_W and GRID_W % chunks == 0
        nb = GRID_W // chunks
        src = proj.reshape(b, CHUNK, GRID_W, n_proj)
        in_blk = (1, CHUNK, chunks, LANE)
        out_blk = (N_HEADS, 1, CHUNK, chunks, LANE)
        out_view = (N_HEADS, b, CHUNK, GRID_W, LANE)

        def in_map(col, flip):
            return lambda bi, n: (bi, 0, nb - 1 - n if flip else n, col)

        def out_map(flip):
            return lambda bi, n: (0, bi, 0, nb - 1 - n if flip else n, 0)
    else:
        chunks = min(l // CHUNK, 4)
        nb = l // (chunks * CHUNK)
        src = proj
        in_blk = (1, chunks * CHUNK, LANE)
        out_blk = (N_HEADS, 1, chunks * CHUNK, LANE)
        out_view = (N_HEADS, b, l, LANE)

        def in_map(col, flip):
            return lambda bi, n: (bi, nb - 1 - n if flip else n, col)

        def out_map(flip):
            return lambda bi, n: (0, bi, nb - 1 - n if flip else n, 0)

    def specs(flip):
        cols = [cb + j for j in range(3 * N_HEADS)] + [cg]
        return [pl.BlockSpec(in_blk, in_map(col, flip)) for col in cols]

    cm_spec = pl.BlockSpec((1, 2, N_HEADS, HEAD_DIM, HEAD_DIM), lambda bi, n: (bi, 0, 0, 0, 0))
    nm_spec = pl.BlockSpec((1, 2, N_HEADS, 8, HEAD_DIM), lambda bi, n: (bi, 0, 0, 0, 0))
    n_in = 2 * (3 * N_HEADS + 1)
    o_f, o_b, cm, nm = pl.pallas_call(
        functools.partial(_mlstm_kernel, colmajor=colmajor, chunks=chunks),
        grid=(b, nb),
        in_specs=specs(False) + specs(True) + [pl.BlockSpec((1, LANE), lambda bi, n: (0, 0)), cm_spec, nm_spec],
        out_specs=[pl.BlockSpec(out_blk, out_map(False)), pl.BlockSpec(out_blk, out_map(True)), cm_spec, nm_spec],
        out_shape=[jax.ShapeDtypeStruct(out_view, F32), jax.ShapeDtypeStruct(out_view, F32),
                   jax.ShapeDtypeStruct(cm0.shape, F32), jax.ShapeDtypeStruct(nm0.shape, F32)],
        scratch_shapes=[pltpu.VMEM((2, N_HEADS, HEAD_DIM, HEAD_DIM), F32),
                        pltpu.VMEM((2, N_HEADS, 8, HEAD_DIM), F32)],
        compiler_params=_cparams(("parallel", "arbitrary"), 40),
        name="mlstm_scan",
    )(*([src] * n_in), gbias, cm0, nm0)
    head_major = (N_HEADS, b, l, LANE)
    return o_f.reshape(head_major), o_b.reshape(head_major), cm, nm


def _split_bf16(x):
    hi = x.astype(BF16)
    return hi, (x - hi.astype(F32)).astype(BF16)


def _dot3(a, b, nt=False):
    f = _dot_nt if nt else _dot
    return f(a[0], b[0]) + (f(a[0], b[1]) + f(a[1], b[0]))


def _inv_unit_triangular(a_list):
    n = a_list[0].shape[0]
    eye = (lax.broadcasted_iota(jnp.int32, (n, n), 0) == lax.broadcasted_iota(jnp.int32, (n, n), 1)).astype(F32)
    xs = [eye - a for a in a_list]
    splits = [_split_bf16(a) for a in a_list]
    ps = [_dot3(s, s) for s in splits]
    steps = (n - 1).bit_length() - 1
    for step in range(steps):
        last = step == steps - 1
        p_split = [_split_bf16(p) for p in ps]
        lhs = xs if last else [jnp.concatenate([x, p], axis=0) for x, p in zip(xs, ps)]
        prods = [_dot3(_split_bf16(l), s) for l, s in zip(lhs, p_split)]
        xs = [x + pr[:n] for x, pr in zip(xs, prods)]
        if not last:
            ps = [pr[n:] for pr in prods]
    return xs


def _gdn_chunks(probs, states):
    c = probs[0]["q"].shape[0]
    a_list, ldecs = [], []
    for p in probs:
        tri = _order_mask(c, p["rev"])
        strict = _order_mask(c, p["rev"], strict=True)
        ldec = jnp.where(tri, jnp.exp(jnp.minimum(p["gam_c"] - p["gam_r"], 0.0)), 0.0)
        p["kbeta"] = p["k"] * p["beta"]
        kk = _dot3(_split_bf16(p["kbeta"]), _split_bf16(p["k"]), nt=True)
        a_list.append(jnp.where(strict, kk * ldec, 0.0))
        ldecs.append(ldec)
    t_invs = _inv_unit_triangular(a_list)
    sols = []
    for p, t_inv in zip(probs, t_invs):
        p["eg"] = jnp.exp(p["gam_c"])
        rhs = jnp.concatenate([p["v"] * p["beta"], p["kbeta"] * p["eg"]], axis=1)
        sols.append(_dot3(_split_bf16(t_inv), _split_bf16(rhs)))
    qks = [_dot_nt(p["q"].astype(BF16), p["k"].astype(BF16)) * ldec for p, ldec in zip(probs, ldecs)]
    inter = [_dot(jnp.concatenate([sol[:, HEAD_DIM:], p["q"] * p["eg"]], axis=0).astype(BF16), s.astype(BF16))
             for p, sol, s in zip(probs, sols, states)]
    outs, new_states = [], []
    for p, sol, qk, it, s in zip(probs, sols, qks, inter, states):
        g_end = p["gam_c"][0:1] if p["rev"] else p["gam_c"][c - 1:c]
        v_new = (sol[:, :HEAD_DIM] - it[:c]).astype(BF16)
        kg = p["k"] * jnp.exp(g_end - p["gam_c"])
        outs.append(it[c:] + _dot(qk.astype(BF16), v_new))
        new_states.append(jnp.exp(g_end) * s + _dot_tn(kg.astype(BF16), v_new))
    return outs, new_states


def _gdn_kernel(xf_ref, gf_ref, xb_ref, gb_ref, alog_ref, dtb_ref, s0_ref,
                of_ref, ob_ref, sout_ref, s_ref):
    n = pl.program_id(1)

    @pl.when(n == 0)
    def _():
        s_ref[...] = s0_ref[0]

    neg_a = -jnp.exp(alog_ref[...])
    probs, where = [], []
    for d, (x_ref, g_ref, o_ref) in enumerate(((xf_ref, gf_ref, of_ref), (xb_ref, gb_ref, ob_ref))):
        rev = d == 1
        gates = g_ref[0]
        cs = _cumsum_rows(neg_a * _softplus(gates + dtb_ref[...]), rev)
        cs_t = cs.T
        betas = _sigmoid(gates)
        for h in range(N_HEADS):
            ca = 16 + 4 * d + h
            cbeta = 24 + 4 * d + h
            probs.append(dict(
                q=x_ref[0, :, h * HEAD_DIM:(h + 1) * HEAD_DIM],
                k=x_ref[0, :, MIX_W + h * HEAD_DIM:MIX_W + (h + 1) * HEAD_DIM],
                v=x_ref[0, :, 2 * MIX_W + h * HEAD_DIM:2 * MIX_W + (h + 1) * HEAD_DIM],
                gam_c=cs[:, ca:ca + 1], gam_r=cs_t[ca:ca + 1, :], beta=betas[:, cbeta:cbeta + 1], rev=rev))
            where.append((o_ref, d, h))
    outs, new_states = _gdn_chunks(probs, [s_ref[d, h] for _, d, h in where])
    for (o_ref, d, h), o, s_new in zip(where, outs, new_states):
        o_ref[0, :, h * HEAD_DIM:(h + 1) * HEAD_DIM] = o
        s_ref[d, h] = s_new

    @pl.when(n == pl.num_programs(1) - 1)
    def _():
        sout_ref[0] = s_ref[...]


def _gdn_scan(qkv, proj, alog_row, dtb_row, s0):
    b, l, w3 = qkv.shape
    nc = l // CHUNK
    cg = COL_GATES // LANE
    st_spec = pl.BlockSpec((1, 2, N_HEADS, HEAD_DIM, HEAD_DIM), lambda bi, n: (bi, 0, 0, 0, 0))
    row_spec = pl.BlockSpec((1, LANE), lambda bi, n: (0, 0))
    return pl.pallas_call(
        _gdn_kernel,
        grid=(b, nc),
        in_specs=[pl.BlockSpec((1, CHUNK, w3), lambda bi, n: (bi, n, 0)),
                  pl.BlockSpec((1, CHUNK, LANE), lambda bi, n: (bi, n, cg)),
                  pl.BlockSpec((1, CHUNK, w3), lambda bi, n: (bi, nc - 1 - n, 0)),
                  pl.BlockSpec((1, CHUNK, LANE), lambda bi, n: (bi, nc - 1 - n, cg)),
                  row_spec, row_spec, st_spec],
        out_specs=[pl.BlockSpec((1, CHUNK, MIX_W), lambda bi, n: (bi, n, 0)),
                   pl.BlockSpec((1, CHUNK, MIX_W), lambda bi, n: (bi, nc - 1 - n, 0)),
                   st_spec],
        out_shape=[jax.ShapeDtypeStruct((b, l, MIX_W), F32), jax.ShapeDtypeStruct((b, l, MIX_W), F32),
                   jax.ShapeDtypeStruct(s0.shape, F32)],
        scratch_shapes=[pltpu.VMEM((2, N_HEADS, HEAD_DIM, HEAD_DIM), F32)],
        compiler_params=_cparams(("parallel", "arbitrary"), 40),
        name="gdn_scan",
    )(qkv, proj, qkv, proj, alog_row, dtb_row, s0)


def _merge_kernel(oaf, oab, obf, obb, ocf, ocb, ag, bo, cz, mg, x_ref, mod_ref, gain_ref, wbr_ref, wout_ref,
                  lng_ref, lnb_ref, out_ref, *, alpha):
    d_model = x_ref.shape[-1]
    branches = ((oaf, oab, ag, _silu), (obf, obb, bo, _sigmoid), (ocf, ocb, cz, _silu))
    mixed = None
    for nbr, (of_ref, ob_ref, gate_ref, act) in enumerate(branches):
        parts = []
        for h in range(N_HEADS):
            if len(of_ref.shape) == 4:
                oh = of_ref[h, 0] + ob_ref[h, 0]
            else:
                oh = of_ref[0, :, h * HEAD_DIM:(h + 1) * HEAD_DIM] + ob_ref[0, :, h * HEAD_DIM:(h + 1) * HEAD_DIM]
            parts.append(oh * lax.rsqrt(jnp.mean(oh * oh, axis=-1, keepdims=True) + RMS_EPS))
        y = jnp.concatenate(parts, axis=1) * gain_ref[nbr:nbr + 1, :] * act(gate_ref[0])
        term = _sigmoid(mg[0, :, nbr * d_model:(nbr + 1) * d_model]) * _dot(y.astype(BF16), wbr_ref[nbr])
        mixed = term if mixed is None else mixed + term
    sub = _dot(mixed.astype(BF16), wout_ref[...])
    z = alpha * x_ref[0] + mod_ref[0, 2:3, :] * sub
    out_ref[0] = _ln(z) * lng_ref[...] + lnb_ref[...]


def _merge(outs, proj, x, mod, gain, w_branch, w_out, ln_g, ln_b, alpha, tl):
    b, l, d = x.shape
    u = MIX_W

    def tok(width, col):
        return pl.BlockSpec((1, tl, width), lambda bi, i: (bi, i, col))

    def const(shape):
        return pl.BlockSpec(shape, lambda bi, i: (0,) * len(shape))

    heads = pl.BlockSpec((N_HEADS, 1, tl, LANE), lambda bi, i: (0, bi, i, 0))
    return pl.pallas_call(
        functools.partial(_merge_kernel, alpha=alpha),
        grid=(b, l // tl),
        in_specs=[tok(u, 0), tok(u, 0), heads, heads, tok(u, 0), tok(u, 0),
            tok(u, (COL_A + 4 * MIX_W) // u), tok(u, (COL_B + 3 * MIX_W) // u), tok(u, COL_CZ // u),
            tok(3 * d, COL_MERGE // (3 * d)), tok(d, 0),
            pl.BlockSpec((1, 6, d), lambda bi, i: (bi, 0, 0)),
            const((3, u)), const((3, u, d)), const((d, d)), const((1, d)), const((1, d))],
        out_specs=tok(d, 0),
        out_shape=jax.ShapeDtypeStruct((b, l, d), F32),
        compiler_params=_cparams(("parallel", "parallel"), 48),
        name="merge",
    )(*outs, proj, proj, proj, proj, x, mod, gain, w_branch, w_out, ln_g, ln_b)


def _ffn_kernel(x_ref, mod_ref, wg_ref, wu_ref, wd_ref, lng_ref, lnb_ref, out_ref, h_ref, acc_ref, *, alpha):
    f = pl.program_id(2)

    @pl.when(f == 0)
    def _():
        h_ref[...] = (_ln(x_ref[0]) * (1.0 + mod_ref[0, 4:5, :]) + mod_ref[0, 3:4, :]).astype(BF16)
        acc_ref[...] = jnp.zeros_like(acc_ref)

    h = h_ref[...]
    act = _silu(_dot(h, wg_ref[...])) * _dot(h, wu_ref[...])
    acc_ref[...] += _dot(act.astype(BF16), wd_ref[...])

    @pl.when(f == pl.num_programs(2) - 1)
    def _():
        z = alpha * x_ref[0] + mod_ref[0, 5:6, :] * acc_ref[...]
        out_ref[0] = _ln(z) * lng_ref[...] + lnb_ref[...]


def _ffn(x, mod, w_gate, w_up, w_down, ln_g, ln_b, alpha, tl, tf=512):
    b, l, d = x.shape
    f = w_gate.shape[1]
    return pl.pallas_call(
        functools.partial(_ffn_kernel, alpha=alpha),
        grid=(b, l // tl, f // tf),
        in_specs=[pl.BlockSpec((1, tl, d), lambda bi, i, j: (bi, i, 0)),
                  pl.BlockSpec((1, 6, d), lambda bi, i, j: (bi, 0, 0)),
                  pl.BlockSpec((d, tf), lambda bi, i, j: (0, j)),
                  pl.BlockSpec((d, tf), lambda bi, i, j: (0, j)),
                  pl.BlockSpec((tf, d), lambda bi, i, j: (j, 0)),
                  pl.BlockSpec((1, d), lambda bi, i, j: (0, 0)),
                  pl.BlockSpec((1, d), lambda bi, i, j: (0, 0))],
        out_specs=pl.BlockSpec((1, tl, d), lambda bi, i, j: (bi, i, 0)),
        out_shape=jax.ShapeDtypeStruct((b, l, d), F32),
        scratch_shapes=[pltpu.VMEM((tl, d), BF16), pltpu.VMEM((tl, d), F32)],
        compiler_params=_cparams(("parallel", "parallel", "arbitrary"), 48),
        name="ffn",
    )(x, mod, w_gate, w_up, w_down, ln_g, ln_b)


def _top2_combine(logits):
    lane = lax.broadcasted_iota(jnp.int32, logits.shape, 1).astype(F32)
    neg = -jnp.inf
    lg = jnp.where(lane < N_EXPERTS, logits, neg)
    m1 = jnp.max(lg, axis=1, keepdims=True)
    i1 = jnp.min(jnp.where(lg == m1, lane, float(LANE)), axis=1, keepdims=True)
    lg2 = jnp.where(lane == i1, neg, lg)
    m2 = jnp.max(lg2, axis=1, keepdims=True)
    i2 = jnp.min(jnp.where(lg2 == m2, lane, float(LANE)), axis=1, keepdims=True)
    e2 = jnp.exp(m2 - m1)
    p1 = 1.0 / (1.0 + e2)
    chosen = (lane == i1) | (lane == i2)
    return jnp.where(lane == i1, p1, jnp.where(lane == i2, e2 * p1, 0.0)), chosen


def _moe_kernel(x_ref, mod_ref, wr_ref, wgu_ref, wd_ref, lng_ref, lnb_ref, out_ref,
                h_ref, acc_ref, comb_ref, pos_ref, post_ref, col_ref, xc_ref, yc_ref, cnt_ref, *, alpha, rb):
    e = pl.program_id(2)
    f = pl.program_id(3)
    tl, d = h_ref.shape

    @pl.when((e == 0) & (f == 0))
    def _():
        h = _ln(x_ref[0]) * (1.0 + mod_ref[0, 4:5, :]) + mod_ref[0, 3:4, :]
        h_ref[...] = h.astype(BF16)
        comb, chosen = _top2_combine(_dot(h, wr_ref[...], precision=HIGHEST))
        comb_ref[...] = comb
        blk = min(tl, LANE)
        before = jnp.where(_order_mask(blk, False, strict=True), 1.0, 0.0).astype(BF16)
        routed = jnp.where(chosen, 1.0, 0.0)
        total = jnp.zeros((1, LANE), F32)
        ranks = []
        for r0 in range(0, tl, blk):
            part = routed[r0:r0 + blk]
            ranks.append(_dot(before, part.astype(BF16)) + total)
            total = total + jnp.sum(part, axis=0, keepdims=True)
        rank = jnp.concatenate(ranks, axis=0)
        pos = jnp.where(chosen, rank, -1.0)
        pos_ref[...] = pos
        pos_t = pos.T[0:N_EXPERTS, :]
        post_ref[...] = pos_t
        for ex in range(N_EXPERTS):
            cnt_ref[ex] = jnp.sum(jnp.where(pos_t[ex:ex + 1, :] >= 0.0, 1.0, 0.0)).astype(jnp.int32)
        acc_ref[...] = jnp.zeros_like(acc_ref)

    nblk = (cnt_ref[e] + (rb - 1)) // rb

    def rows(r):
        return pl.ds(pl.multiple_of(r * rb, rb), rb)

    @pl.when(f == 0)
    def _():
        lane = lax.broadcasted_iota(jnp.int32, (tl, LANE), 1)
        col_ref[0] = jnp.sum(jnp.where(lane == e, pos_ref[...], 0.0), axis=1, keepdims=True)
        col_ref[1] = jnp.sum(jnp.where(lane == e, comb_ref[...], 0.0), axis=1, keepdims=True)
        pos_row = post_ref[pl.ds(e, 1), :]

        def gather(r, carry):
            slot = (lax.broadcasted_iota(jnp.int32, (rb, tl), 0) + r * rb).astype(F32)
            sel = jnp.where(pos_row == slot, 1.0, 0.0).astype(BF16)
            xc_ref[rows(r), :] = _dot(sel, h_ref[...]).astype(BF16)
            yc_ref[rows(r), :] = jnp.zeros((rb, d), F32)
            return carry

        lax.fori_loop(0, nblk, gather, 0)

    def expert(r, carry):
        xb = xc_ref[rows(r), :]
        gu = _dot(xb, wgu_ref[0, 0])
        half = gu.shape[1] // 2
        act = _silu(gu[:, :half]) * gu[:, half:]
        yc_ref[rows(r), :] += _dot(act.astype(BF16), wd_ref[0])
        return carry

    lax.fori_loop(0, nblk, expert, 0)

    @pl.when(f == pl.num_programs(3) - 1)
    def _():
        pos_col = col_ref[0]
        weight = col_ref[1]

        def scatter(r, carry):
            slot = (lax.broadcasted_iota(jnp.int32, (tl, rb), 1) + r * rb).astype(F32)
            sel_t = jnp.where(pos_col == slot, 1.0, 0.0).astype(BF16)
            acc_ref[...] += weight * _dot(sel_t, yc_ref[rows(r), :].astype(BF16))
            return carry

        lax.fori_loop(0, nblk, scatter, 0)

    @pl.when((e == pl.num_programs(2) - 1) & (f == pl.num_programs(3) - 1))
    def _():
        z = alpha * x_ref[0] + mod_ref[0, 5:6, :] * acc_ref[...]
        out_ref[0] = _ln(z) * lng_ref[...] + lnb_ref[...]


def _pack_gate_up(w_gate, w_up):
    e, d, f = w_gate.shape
    tf = next(t for t in (896, 512, 256, 128) if f % t == 0)

    def tiles(w):
        return w.astype(BF16).reshape(e, d, f // tf, tf).transpose(0, 2, 1, 3)

    return jnp.concatenate([tiles(w_gate), tiles(w_up)], axis=-1)


def _moe(x, mod, w_router, w_gate_up, w_down, ln_g, ln_b, alpha, tl):
    b, l, d = x.shape
    ne, nf, _, tf2 = w_gate_up.shape
    rb = min(MOE_ROW_BLOCK, tl)
    compact_rows = -(-tl // rb) * rb
    return pl.pallas_call(
        functools.partial(_moe_kernel, alpha=alpha, rb=rb),
        grid=(b, l // tl, ne, nf),
        in_specs=[pl.BlockSpec((1, tl, d), lambda bi, i, e, j: (bi, i, 0)),
                  pl.BlockSpec((1, 6, d), lambda bi, i, e, j: (bi, 0, 0)),
                  pl.BlockSpec((d, LANE), lambda bi, i, e, j: (0, 0)),
                  pl.BlockSpec((1, 1, d, tf2), lambda bi, i, e, j: (e, j, 0, 0)),
                  pl.BlockSpec((1, tf2 // 2, d), lambda bi, i, e, j: (e, j, 0)),
                  pl.BlockSpec((1, d), lambda bi, i, e, j: (0, 0)),
                  pl.BlockSpec((1, d), lambda bi, i, e, j: (0, 0))],
        out_specs=pl.BlockSpec((1, tl, d), lambda bi, i, e, j: (bi, i, 0)),
        out_shape=jax.ShapeDtypeStruct((b, l, d), F32),
        scratch_shapes=[pltpu.VMEM((tl, d), BF16),
                        pltpu.VMEM((tl, d), F32),
                        pltpu.VMEM((tl, LANE), F32),
                        pltpu.VMEM((tl, LANE), F32),
                        pltpu.VMEM((N_EXPERTS, tl), F32),
                        pltpu.VMEM((2, tl, 1), F32),
                        pltpu.VMEM((compact_rows, d), BF16),
                        pltpu.VMEM((compact_rows, d), F32),
                        pltpu.SMEM((N_EXPERTS,), jnp.int32)],
        compiler_params=_cparams(("parallel", "parallel", "arbitrary", "arbitrary"), 56),
        name="moe",
    )(x, mod, w_router, w_gate_up, w_down, ln_g, ln_b)


def _reorder_w_in(w):
    d = w.shape[0]
    o_a, o_b = 0, 5 * MIX_W
    o_bgt = o_b + 4 * MIX_W
    o_cqkv = o_bgt + 16
    o_cz = o_cqkv + 3 * MIX_W
    o_cgt = o_cz + MIX_W
    o_merge = o_cgt + 16
    parts = [w[:, o_merge:o_merge + 3 * d], w[:, o_a:o_bgt], w[:, o_cqkv:o_cgt],
             w[:, o_bgt:o_bgt + 16], w[:, o_cgt:o_cgt + 16],
             jnp.zeros((d, N_PROJ - COL_GATES - 32), w.dtype)]
    return jnp.concatenate(parts, axis=1)


def _pad_row(vals, offset):
    row = jnp.zeros((1, LANE), F32)
    return lax.dynamic_update_slice(row, vals.reshape(1, -1).astype(F32), (0, offset))


def _tile(l, pref):
    return pref if l % pref == 0 else l


def kernel(x, c, ctx, c_ctx, w_ada, b_ada, w_in, conv_w, lb_raw, m_gate_bias, gdn_a_log, gdn_dt_bias, mix_norm,
           w_branch, w_out, ln_g, ln_b, ffn_w_gate, ffn_w_up, ffn_w_down, moe_router, moe_w_gate, moe_w_up,
           moe_w_down):
    depth, d_model = w_in.shape[0], w_in.shape[1]
    b, l_lat, _ = x.shape
    l_ctx = ctx.shape[1]
    alpha = float((2 * depth) ** 0.25)
    assert d_model == 1024 and w_in.shape[2] == 9 * MIX_W + 3 * MIX_W + MIX_W + 32 + 3 * d_model

    lower = _lower_bounds(lb_raw.astype(F32))
    n_rows = -(-(b + 1) // 8) * 8
    cvec = jnp.concatenate([c, c_ctx[None, :], jnp.zeros((n_rows - b - 1, d_model), F32)], axis=0)
    mods = _ada(cvec, w_ada, b_ada)

    zeros_state = jnp.zeros((b, 2, N_HEADS, HEAD_DIM, HEAD_DIM), F32)
    zeros_hgrn = jnp.zeros((b, N_HEADS, 2, HEAD_DIM, HEAD_DIM), F32)
    zeros_nm = jnp.zeros((b, 2, N_HEADS, 8, HEAD_DIM), F32)

    pack = 1024 // l_ctx if (1024 % l_ctx == 0 and b % max(1024 // l_ctx, 1) == 0) else 1
    ctx_packed = (b // pack, pack * l_ctx, d_model)

    for l in range(depth):
        mod_l = mods[l, :b].reshape(b, 6, d_model)
        mod_c = jnp.broadcast_to(mods[l, b].reshape(1, 6, d_model), (b, 6, d_model))
        w_in_l = _reorder_w_in(w_in[l]).astype(BF16)
        lb_row = lower[l].reshape(1, MIX_W)
        gbias = _pad_row(m_gate_bias[l], 0)
        alog_row = _pad_row(gdn_a_log[l], 16)
        dtb_row = _pad_row(gdn_dt_bias[l], 16)
        gain = mix_norm[l]
        wbr = w_branch[l].astype(BF16)
        wo = w_out[l].astype(BF16)
        g0, b0 = ln_g[l, 0].reshape(1, -1), ln_b[l, 0].reshape(1, -1)
        g1, b1 = ln_g[l, 1].reshape(1, -1), ln_b[l, 1].reshape(1, -1)

        p_c = _in_proj(ctx.reshape(ctx_packed), mod_c[:ctx_packed[0]], w_in_l, ctx_packed[1])
        p_c = p_c.reshape(b, l_ctx, N_PROJ)
        p_l = _in_proj(x, mod_l, w_in_l, _tile(l_lat, 1024))
        qkv_c = _gdn_conv(p_c, conv_w[l], _tile(l_ctx, 256))
        qkv_l = _gdn_conv(p_l, conv_w[l], _tile(l_lat, 512))

        oa_c = _hgrn_scan(p_c, lb_row, zeros_hgrn)
        oa_l = _hgrn_scan(p_l, lb_row, oa_c[2])
        ob_c = _mlstm_scan(p_c, gbias, zeros_state, zeros_nm, colmajor=False)
        ob_l = _mlstm_scan(p_l, gbias, ob_c[2], ob_c[3], colmajor=True)
        oc_c = _gdn_scan(qkv_c, p_c, alog_row, dtb_row, zeros_state)
        oc_l = _gdn_scan(qkv_l, p_l, alog_row, dtb_row, oc_c[2])

        def channel_mixer(t, mod, tl):
            i = l // 2
            if l % 2 == 0:
                return _ffn(t, mod, ffn_w_gate[i].astype(BF16), ffn_w_up[i].astype(BF16),
                            ffn_w_down[i].astype(BF16), g1, b1, alpha, tl)
            w_r = jnp.concatenate([moe_router[i], jnp.zeros((d_model, LANE - N_EXPERTS), F32)], axis=1)
            return _moe(t, mod, w_r, _pack_gate_up(moe_w_gate[i], moe_w_up[i]), moe_w_down[i].astype(BF16),
                        g1, b1, alpha, tl)

        outs_l = (oa_l[0], oa_l[1], ob_l[0], ob_l[1], oc_l[0], oc_l[1])
        x = _merge(outs_l, p_l, x, mod_l, gain, wbr, wo, g0, b0, alpha, _tile(l_lat, 256))
        x = channel_mixer(x, mod_l, _tile(l_lat, 1024))
        if l < depth - 1:
            outs_c = (oa_c[0], oa_c[1], ob_c[0], ob_c[1], oc_c[0], oc_c[1])
            ctx = _merge(outs_c, p_c, ctx, mod_c, gain, wbr, wo, g0, b0, alpha, _tile(l_ctx, 256))
            ctx = channel_mixer(ctx.reshape(ctx_packed), mod_c[:ctx_packed[0]], ctx_packed[1]).reshape(ctx.shape)
    return x
```

```python
import functools

import jax
import jax.numpy as jnp
from jax import lax
from jax.experimental import pallas as pl
from jax.experimental.pallas import tpu as pltpu

F32 = jnp.float32
BF16 = jnp.bfloat16
HIGHEST = lax.Precision.HIGHEST

HEAD_DIM = 128
N_HEADS = 4
MIX_W = N_HEADS * HEAD_DIM
CHUNK = 64
GRID_W = 64
SUB = 16
MLSTM_GROUP = 2
GDN_BATCH_ROWS = 2
HGRN_GROUP = 8
N_EXPERTS = 8
MOE_ROW_BLOCK = 288
LN_EPS = 1e-5
RMS_EPS = 1e-6
L2_EPS = 1e-6
MASK_NEG = -1e30
LOG2_E = 1.4426950408889634
LANE = 128

COL_MERGE = 0
COL_A = 3 * 1024
COL_B = COL_A + 5 * MIX_W
COL_CQKV = COL_B + 4 * MIX_W
COL_CZ = COL_CQKV + 3 * MIX_W
COL_GATES = COL_CZ + MIX_W
N_PROJ = COL_GATES + LANE
PROJ_TILE = N_PROJ // 7


def _cparams(sem, vmem_mb=None):
    kw = dict(dimension_semantics=sem)
    if vmem_mb is not None:
        kw["vmem_limit_bytes"] = vmem_mb << 20
    return pltpu.CompilerParams(**kw)


def _dot(a, b, precision=None):
    return jnp.dot(a, b, preferred_element_type=F32, precision=precision)


def _dot_nt(a, b, precision=None):
    return lax.dot_general(a, b, (((1,), (1,)), ((), ())), preferred_element_type=F32, precision=precision)


def _dot_tn(a, b):
    return lax.dot_general(a, b, (((0,), (0,)), ((), ())), preferred_element_type=F32)


def _ln(x):
    mu = jnp.mean(x, axis=-1, keepdims=True)
    xc = x - mu
    var = jnp.mean(xc * xc, axis=-1, keepdims=True)
    return xc * lax.rsqrt(var + LN_EPS)


def _sigmoid(x):
    return 1.0 / (1.0 + jnp.exp(-x))


def _silu(x):
    return x * _sigmoid(x)


def _softplus(x):
    return jnp.maximum(x, 0.0) + jnp.log(1.0 + jnp.exp(-jnp.abs(x)))


def _log_sigmoid(x):
    return -_softplus(-x)


def _order_mask(n, rev, strict=False):
    r = lax.broadcasted_iota(jnp.int32, (n, n), 0)
    c = lax.broadcasted_iota(jnp.int32, (n, n), 1)
    if rev:
        return (c > r) if strict else (c >= r)
    return (c < r) if strict else (c <= r)


def _cumsum_rows(x, rev):
    m = _order_mask(x.shape[0], rev).astype(F32)
    return _dot(m, x, precision=HIGHEST)


def _ada_kernel(c_ref, w_ref, b_ref, o_ref):
    s = _silu(c_ref[...])
    o_ref[0] = _dot(s, w_ref[0], precision=HIGHEST) + b_ref[0]


def _ada(cvec, w_ada, b_ada, tn=1536):
    depth, d, n6 = w_ada.shape
    rows = cvec.shape[0]
    return pl.pallas_call(
        _ada_kernel,
        grid=(depth, n6 // tn),
        in_specs=[pl.BlockSpec((rows, d), lambda l, n: (0, 0)),
                  pl.BlockSpec((1, d, tn), lambda l, n: (l, 0, n)),
                  pl.BlockSpec((1, 1, tn), lambda l, n: (l, 0, n))],
        out_specs=pl.BlockSpec((1, rows, tn), lambda l, n: (l, 0, n)),
        out_shape=jax.ShapeDtypeStruct((depth, rows, n6), F32),
        compiler_params=_cparams(("parallel", "parallel"), 40),
        name="ada",
    )(cvec, w_ada, b_ada.reshape(depth, 1, n6))


def _lower_bounds_kernel(lb_ref, o_ref):
    x = lb_ref[...]
    depth = x.shape[0]
    mx = jnp.max(x, axis=0, keepdims=True)
    e = jnp.exp(x - mx)
    p = e / jnp.sum(e, axis=0, keepdims=True)
    acc = jnp.zeros_like(p[0:1])
    for l in range(depth):
        o_ref[l:l + 1, :] = acc
        if l + 1 < depth:
            acc = acc + p[l + 1:l + 2]


def _lower_bounds(lb_raw):
    return pl.pallas_call(
        _lower_bounds_kernel,
        out_shape=jax.ShapeDtypeStruct(lb_raw.shape, F32),
        name="lower_bounds",
    )(lb_raw)


def _in_proj_kernel(x_ref, mod_ref, w_ref, o_ref, h_ref):
    @pl.when(pl.program_id(2) == 0)
    def _():
        shift = mod_ref[0, 0:1, :]
        scale = mod_ref[0, 1:2, :]
        h_ref[...] = (_ln(x_ref[0]) * (1.0 + scale) + shift).astype(BF16)

    o_ref[0] = _dot(h_ref[...], w_ref[...])


def _in_proj(x, mod, w, tl, tn=PROJ_TILE):
    b, l, d = x.shape
    n_proj = w.shape[1]
    return pl.pallas_call(
        _in_proj_kernel,
        grid=(b, l // tl, n_proj // tn),
        in_specs=[pl.BlockSpec((1, tl, d), lambda bi, i, n: (bi, i, 0)),
                  pl.BlockSpec((1, 6, d), lambda bi, i, n: (bi, 0, 0)),
                  pl.BlockSpec((d, tn), lambda bi, i, n: (0, n))],
        out_specs=pl.BlockSpec((1, tl, tn), lambda bi, i, n: (bi, i, n)),
        out_shape=jax.ShapeDtypeStruct((b, l, n_proj), F32),
        scratch_shapes=[pltpu.VMEM((tl, d), BF16)],
        compiler_params=_cparams(("parallel", "parallel", "arbitrary"), 48),
        name="in_proj",
    )(x, mod, w)


def _conv_kernel(x_ref, prev_ref, next_ref, w_ref, o_ref):
    i = pl.program_id(1)
    last = pl.num_programs(1) - 1
    x = x_ref[0]
    tl = x.shape[0]
    row = lax.broadcasted_iota(jnp.int32, x.shape, 0)
    before = jnp.where(i > 0, prev_ref[0, 7:8, :], 0.0)
    after = jnp.where(i < last, next_ref[0, 0:1, :], 0.0)
    x_dn = jnp.where(row == 0, before, pltpu.roll(x, 1, axis=0))
    x_up = jnp.where(row == tl - 1, after, pltpu.roll(x, tl - 1, axis=0))
    y = _silu(x_dn * w_ref[0:1, :] + x * w_ref[1:2, :] + x_up * w_ref[2:3, :])
    for j in range(3 * N_HEADS):
        t = y[:, j * HEAD_DIM:(j + 1) * HEAD_DIM]
        if j < 2 * N_HEADS:
            t = t * lax.rsqrt(jnp.sum(t * t, axis=-1, keepdims=True) + L2_EPS)
            if j < N_HEADS:
                t = t * HEAD_DIM ** -0.5
        o_ref[0, :, j * HEAD_DIM:(j + 1) * HEAD_DIM] = t


def _gdn_conv(proj, conv_w, tl):
    b, l, _ = proj.shape
    w3 = 3 * MIX_W
    cb = COL_CQKV // w3
    nb8 = l // 8
    r8 = tl // 8
    return pl.pallas_call(
        _conv_kernel,
        grid=(b, l // tl),
        in_specs=[pl.BlockSpec((1, tl, w3), lambda bi, i: (bi, i, cb)),
                  pl.BlockSpec((1, 8, w3), lambda bi, i: (bi, jnp.maximum(i * r8 - 1, 0), cb)),
                  pl.BlockSpec((1, 8, w3), lambda bi, i: (bi, jnp.minimum((i + 1) * r8, nb8 - 1), cb)),
                  pl.BlockSpec((3, w3), lambda bi, i: (0, 0))],
        out_specs=pl.BlockSpec((1, tl, w3), lambda bi, i: (bi, i, 0)),
        out_shape=jax.ShapeDtypeStruct((b, l, w3), F32),
        compiler_params=_cparams(("parallel", "parallel"), 40),
        name="gdn_conv",
    )(proj, proj, proj, conv_w)


def _hgrn_chunks(probs, states, emat):
    c = probs[0]["q"].shape[0]
    nb = c // SUB
    for p in probs:
        lb = p["lb"]
        sig = _sigmoid(p["fpre"])
        p["k"] = (1.0 - lb) * (1.0 - sig)
        p["g"] = _cumsum_rows(jnp.log(lb + (1.0 - lb) * sig), p["rev"])

    walls = []
    for p in probs:
        q = p["q"]
        g2 = p["g"] * LOG2_E
        lk2 = jnp.log(p["k"]) * LOG2_E
        rel = lk2 - g2
        rows = []
        half = SUB // 2
        dead = jnp.zeros((half, HEAD_DIM), F32)
        for j in range(nb):
            r0 = j * SUB
            pieces = []
            for s in range(SUB):
                skip = 0 if (s >= half and not p["rev"]) else 1 if (s < half and p["rev"]) else None
                parts = []
                for part in range(2):
                    if part == skip:
                        parts.append(dead)
                        continue
                    t0 = r0 + part * half
                    e2 = jnp.exp2(jnp.minimum(g2[t0:t0 + half] + rel[r0 + s:r0 + s + 1], lk2[r0 + s:r0 + s + 1]))
                    parts.append(e2 * q[t0:t0 + half])
                pieces.append(jnp.concatenate(parts, axis=0).astype(BF16))
            rows.append(jnp.concatenate(pieces, axis=1))
        walls.append(jnp.concatenate(rows, axis=0))
    a_diags = [_dot(w, emat) for w in walls]

    a_offs = []
    for p in probs:
        q, k, g, rev = p["q"], p["k"], p["g"], p["rev"]
        offs = []
        for i in range(nb):
            r0 = i * SUB
            if (not rev and i == 0) or (rev and i == nb - 1):
                offs.append(jnp.zeros((SUB, c), F32))
                continue
            bi = g[r0 + SUB:r0 + SUB + 1] if rev else g[r0 - 1:r0]
            lhs = (q[r0:r0 + SUB] * jnp.exp(g[r0:r0 + SUB] - bi)).astype(BF16)
            rhs = (k * jnp.exp(jnp.minimum(bi - g, 0.0))).astype(BF16)
            offs.append(_dot_nt(lhs, rhs))
        a_offs.append(jnp.concatenate(offs, axis=0))

    r = lax.broadcasted_iota(jnp.int32, (c, c), 0)
    col = lax.broadcasted_iota(jnp.int32, (c, c), 1)
    shift = SUB.bit_length() - 1
    rb = lax.shift_right_logical(r, shift)
    cb = lax.shift_right_logical(col, shift)
    inters = [_dot_nt((p["q"] * jnp.exp(p["g"])).astype(BF16), s_t.astype(BF16)) for p, s_t in zip(probs, states)]
    outs, new_states = [], []
    for p, a_diag, a_off, inter, s_t in zip(probs, a_diags, a_offs, inters, states):
        rev, g = p["rev"], p["g"]
        order = (col >= r) if rev else (col <= r)
        off_side = (cb > rb) if rev else (cb < rb)
        a = jnp.where((rb == cb) & order, a_diag, jnp.where(off_side, a_off, 0.0))
        g_end = g[0:1] if rev else g[c - 1:c]
        vb = p["v"].astype(BF16)
        outs.append(_dot(a.astype(BF16), vb) + inter)
        kd = (p["k"] * jnp.exp(g_end - g)).astype(BF16)
        new_states.append(s_t * jnp.exp(g_end) + _dot_tn(vb, kd))
    return outs, new_states


def _hgrn_kernel(qf_ref, ff_ref, vf_ref, qb_ref, fb_ref, vb_ref, lb_ref, e_ref, s0_ref,
                 of_ref, ob_ref, sout_ref, s_ref):
    n = pl.program_id(1)

    @pl.when(n == 0)
    def _():
        s_ref[...] = s0_ref[0]

    probs, where = [], []
    for d, (q_ref, f_ref, v_ref, o_ref) in enumerate(((qf_ref, ff_ref, vf_ref, of_ref),
                                                      (qb_ref, fb_ref, vb_ref, ob_ref))):
        for h in range(N_HEADS):
            sl = slice(h * HEAD_DIM, (h + 1) * HEAD_DIM)
            probs.append(dict(q=q_ref[0, :, sl], fpre=f_ref[0, :, sl], v=v_ref[0, :, sl], lb=lb_ref[:, sl],
                              rev=d == 1))
            where.append((o_ref, sl, h, d))
    states = [s_ref[h, d] for _, _, h, d in where]
    outs, new_states = [], []
    for g0 in range(0, len(probs), HGRN_GROUP):
        o_g, s_g = _hgrn_chunks(probs[g0:g0 + HGRN_GROUP], states[g0:g0 + HGRN_GROUP], e_ref[...])
        outs += o_g
        new_states += s_g
    for (o_ref, sl, h, d), o, s_new in zip(where, outs, new_states):
        o_ref[0, :, sl] = o
        s_ref[h, d] = s_new

    @pl.when(n == pl.num_programs(1) - 1)
    def _():
        sout_ref[0] = s_ref[...]


def _hgrn_emat():
    r = jnp.arange(SUB * HEAD_DIM)[:, None] // HEAD_DIM
    c = jnp.arange(CHUNK)[None, :] % SUB
    return (r == c).astype(BF16)


def _hgrn_scan(proj, lb, s0):
    b, l, _ = proj.shape
    nc = l // CHUNK
    ca = COL_A // MIX_W

    def fwd(col):
        return pl.BlockSpec((1, CHUNK, MIX_W), lambda bi, n: (bi, n, ca + col))

    def bwd(col):
        return pl.BlockSpec((1, CHUNK, MIX_W), lambda bi, n: (bi, nc - 1 - n, ca + col))

    st_spec = pl.BlockSpec((1, N_HEADS, 2, HEAD_DIM, HEAD_DIM), lambda bi, n: (bi, 0, 0, 0, 0))
    return pl.pallas_call(
        _hgrn_kernel,
        grid=(b, nc),
        in_specs=[fwd(0), fwd(1), fwd(3), bwd(0), bwd(2), bwd(3),
                  pl.BlockSpec((1, MIX_W), lambda bi, n: (0, 0)),
                  pl.BlockSpec((SUB * HEAD_DIM, CHUNK), lambda bi, n: (0, 0)),
                  st_spec],
        out_specs=[pl.BlockSpec((1, CHUNK, MIX_W), lambda bi, n: (bi, n, 0)),
                   pl.BlockSpec((1, CHUNK, MIX_W), lambda bi, n: (bi, nc - 1 - n, 0)),
                   st_spec],
        out_shape=[jax.ShapeDtypeStruct((b, l, MIX_W), F32),
                   jax.ShapeDtypeStruct((b, l, MIX_W), F32),
                   jax.ShapeDtypeStruct(s0.shape, F32)],
        scratch_shapes=[pltpu.VMEM((N_HEADS, 2, HEAD_DIM, HEAD_DIM), F32)],
        compiler_params=_cparams(("parallel", "arbitrary"), 40),
        name="hgrn_scan",
    )(proj, proj, proj, proj, proj, proj, lb, _hgrn_emat(), s0)


def _mlstm_chunks(probs, states):
    c = probs[0]["q"].shape[0]
    for p, (cm, nvec, m) in zip(probs, states):
        tri = _order_mask(c, p["rev"])
        log_d = jnp.where(tri, p["bb_c"] - p["bb_r"] + p["ic_r"], MASK_NEG)
        log_inter = p["bb_c"] + m
        m_t = jnp.maximum(jnp.max(log_d, axis=1, keepdims=True), log_inter)
        p["m_t"] = m_t
        p["dmat"] = jnp.where(tri, jnp.exp(log_d - m_t), 0.0)
        p["inter"] = jnp.exp(log_inter - m_t)
        p["qb"], p["kb"], p["vb"] = p["q"].astype(BF16), p["k"].astype(BF16), p["v"].astype(BF16)
    scores = [_dot_nt(p["qb"], p["kb"]) * p["dmat"] for p in probs]
    carried = [_dot(p["qb"], cm.astype(BF16)) for p, (cm, _, _) in zip(probs, states)]
    local = [_dot(s.astype(BF16), p["vb"]) for p, s in zip(probs, scores)]
    outs, new_states = [], []
    for p, s, car, loc, (cm, nvec, m) in zip(probs, scores, carried, local, states):
        m_t, inter = p["m_t"], p["inter"]
        den = jnp.sum(s, axis=1, keepdims=True) + inter * jnp.sum(p["q"] * nvec, axis=1, keepdims=True)
        outs.append((loc + inter * car) / jnp.maximum(jnp.abs(den), jnp.exp(-m_t)))
        m_new = m_t[0:1] if p["rev"] else m_t[c - 1:c]
        bb_end = p["bb_c"][0:1] if p["rev"] else p["bb_c"][c - 1:c]
        decay = jnp.exp(bb_end + m - m_new)
        kw = p["k"] * jnp.exp(bb_end - p["bb_c"] + p["ic_c"] - m_new)
        new_states.append((decay * cm + _dot_tn(kw.astype(BF16), p["vb"]),
                           decay * nvec + jnp.sum(kw, axis=0, keepdims=True), m_new))
    return outs, new_states


def _column_view(ref):
    rows, group, width = ref.shape[-3:]
    return ref.reshape(rows * group, width), group


def _chunk_rows(j):
    start = j * CHUNK
    return pl.ds(start if isinstance(start, int) else pl.multiple_of(start, CHUNK), CHUNK)


def _load_chunk(ref, j, colmajor):
    if colmajor:
        flat, group = _column_view(ref)
        return flat[pl.ds(j, CHUNK, stride=group), :]
    return ref[0, _chunk_rows(j), :]


def _store_chunk(ref, j, val, colmajor):
    if colmajor:
        flat, group = _column_view(ref)
        flat[pl.ds(j, CHUNK, stride=group), :] = val
    else:
        ref[0, _chunk_rows(j), :] = val


def _mlstm_kernel(*refs, colmajor, chunks):
    per_dir = 3 * N_HEADS + 1
    fwd, bwd = refs[:per_dir], refs[per_dir:2 * per_dir]
    bias_ref, cm0_ref, nm0_ref, of_ref, ob_ref, cmout_ref, nmout_ref, cm_ref, nm_ref = refs[2 * per_dir:]
    n = pl.program_id(1)

    @pl.when(n == 0)
    def _():
        cm_ref[...] = cm0_ref[0]
        nm_ref[...] = nm0_ref[0]

    def step(i, carry):
        probs, where = [], []
        for d, (in_refs, o_ref) in enumerate(((fwd, of_ref), (bwd, ob_ref))):
            rev = d == 1
            j = chunks - 1 - i if rev else i
            gates = _load_chunk(in_refs[-1], j, colmajor) + bias_ref[...]
            cs = _cumsum_rows(_log_sigmoid(gates), rev)
            cs_t = cs.T
            gates_t = gates.T
            for h in range(N_HEADS):
                ci = 4 * d + h
                cf = 8 + 4 * d + h
                probs.append(dict(
                    q=_load_chunk(in_refs[h], j, colmajor),
                    k=_load_chunk(in_refs[N_HEADS + h], j, colmajor) * HEAD_DIM ** -0.5,
                    v=_load_chunk(in_refs[2 * N_HEADS + h], j, colmajor),
                    bb_c=cs[:, cf:cf + 1], bb_r=cs_t[cf:cf + 1, :],
                    ic_c=gates[:, ci:ci + 1], ic_r=gates_t[ci:ci + 1, :], rev=rev))
                where.append((o_ref, j, d, h))
        states = [(cm_ref[d, h], nm_ref[d, h, 0:1, :], nm_ref[d, h, 1:2, 0:1]) for _, _, d, h in where]
        outs, new_states = [], []
        for g0 in range(0, len(probs), MLSTM_GROUP):
            o_g, s_g = _mlstm_chunks(probs[g0:g0 + MLSTM_GROUP], states[g0:g0 + MLSTM_GROUP])
            outs += o_g
            new_states += s_g
        for (o_ref, j, d, h), o, (cm_new, n_new, m_new) in zip(where, outs, new_states):
            _store_chunk(o_ref.at[h], j, o, colmajor)
            cm_ref[d, h] = cm_new
            nm_ref[d, h, 0:1, :] = n_new
            nm_ref[d, h, 1:2, :] = jnp.broadcast_to(m_new, (1, HEAD_DIM))
        return carry

    lax.fori_loop(0, chunks, step, 0)

    @pl.when(n == pl.num_programs(1) - 1)
    def _():
        cmout_ref[0] = cm_ref[...]
        nmout_ref[0] = nm_ref[...]


def _mlstm_scan(proj, gbias, cm0, nm0, colmajor):
    b, l, n_proj = proj.shape
    cb = COL_B // LANE
    cg = COL_GATES // LANE
    if colmajor:
        chunks = 8
        assert l == CHUNK * GRID_W and GRID_W % chunks == 0
        nb = GRID_W // chunks
        src = proj.reshape(b, CHUNK, GRID_W, n_proj)
        in_blk = (1, CHUNK, chunks, LANE)
        out_blk = (N_HEADS, 1, CHUNK, chunks, LANE)
        out_view = (N_HEADS, b, CHUNK, GRID_W, LANE)

        def in_map(col, flip):
            return lambda bi, n: (bi, 0, nb - 1 - n if flip else n, col)

        def out_map(flip):
            return lambda bi, n: (0, bi, 0, nb - 1 - n if flip else n, 0)
    else:
        chunks = min(l // CHUNK, 4)
        nb = l // (chunks * CHUNK)
        src = proj
        in_blk = (1, chunks * CHUNK, LANE)
        out_blk = (N_HEADS, 1, chunks * CHUNK, LANE)
        out_view = (N_HEADS, b, l, LANE)

        def in_map(col, flip):
            return lambda bi, n: (bi, nb - 1 - n if flip else n, col)

        def out_map(flip):
            return lambda bi, n: (0, bi, nb - 1 - n if flip else n, 0)

    def specs(flip):
        cols = [cb + j for j in range(3 * N_HEADS)] + [cg]
        return [pl.BlockSpec(in_blk, in_map(col, flip)) for col in cols]

    cm_spec = pl.BlockSpec((1, 2, N_HEADS, HEAD_DIM, HEAD_DIM), lambda bi, n: (bi, 0, 0, 0, 0))
    nm_spec = pl.BlockSpec((1, 2, N_HEADS, 8, HEAD_DIM), lambda bi, n: (bi, 0, 0, 0, 0))
    n_in = 2 * (3 * N_HEADS + 1)
    o_f, o_b, cm, nm = pl.pallas_call(
        functools.partial(_mlstm_kernel, colmajor=colmajor, chunks=chunks),
        grid=(b, nb),
        in_specs=specs(False) + specs(True) + [pl.BlockSpec((1, LANE), lambda bi, n: (0, 0)), cm_spec, nm_spec],
        out_specs=[pl.BlockSpec(out_blk, out_map(False)), pl.BlockSpec(out_blk, out_map(True)), cm_spec, nm_spec],
        out_shape=[jax.ShapeDtypeStruct(out_view, F32), jax.ShapeDtypeStruct(out_view, F32),
                   jax.ShapeDtypeStruct(cm0.shape, F32), jax.ShapeDtypeStruct(nm0.shape, F32)],
        scratch_shapes=[pltpu.VMEM((2, N_HEADS, HEAD_DIM, HEAD_DIM), F32),
                        pltpu.VMEM((2, N_HEADS, 8, HEAD_DIM), F32)],
        compiler_params=_cparams(("parallel", "arbitrary"), 40),
        name="mlstm_scan",
    )(*([src] * n_in), gbias, cm0, nm0)
    head_major = (N_HEADS, b, l, LANE)
    return o_f.reshape(head_major), o_b.reshape(head_major), cm, nm


def _split_bf16(x):
    hi = x.astype(BF16)
    return hi, (x - hi.astype(F32)).astype(BF16)


def _dot3(a, b, nt=False):
    f = _dot_nt if nt else _dot
    return f(a[0], b[0]) + (f(a[0], b[1]) + f(a[1], b[0]))


def _inv_unit_triangular(a_list):
    n = a_list[0].shape[0]
    eye = (lax.broadcasted_iota(jnp.int32, (n, n), 0) == lax.broadcasted_iota(jnp.int32, (n, n), 1)).astype(F32)
    xs = [eye - a for a in a_list]
    splits = [_split_bf16(a) for a in a_list]
    ps = [_dot3(s, s) for s in splits]
    steps = (n - 1).bit_length() - 1
    for step in range(steps):
        last = step == steps - 1
        p_split = [_split_bf16(p) for p in ps]
        lhs = xs if last else [jnp.concatenate([x, p], axis=0) for x, p in zip(xs, ps)]
        prods = [_dot3(_split_bf16(l), s) for l, s in zip(lhs, p_split)]
        xs = [x + pr[:n] for x, pr in zip(xs, prods)]
        if not last:
            ps = [pr[n:] for pr in prods]
    return xs


def _gdn_chunks(probs, states):
    c = probs[0]["q"].shape[0]
    a_list, ldecs = [], []
    for p in probs:
        tri = _order_mask(c, p["rev"])
        strict = _order_mask(c, p["rev"], strict=True)
        ldec = jnp.where(tri, jnp.exp(jnp.minimum(p["gam_c"] - p["gam_r"], 0.0)), 0.0)
        p["kbeta"] = p["k"] * p["beta"]
        kk = _dot_nt(p["kbeta"].astype(BF16), p["k"].astype(BF16))
        a_list.append(jnp.where(strict, kk * ldec, 0.0))
        ldecs.append(ldec)
    t_invs = _inv_unit_triangular(a_list)
    sols = []
    for p, t_inv in zip(probs, t_invs):
        p["eg"] = jnp.exp(p["gam_c"])
        rhs = jnp.concatenate([p["v"] * p["beta"], p["kbeta"] * p["eg"]], axis=1)
        sols.append(_dot3(_split_bf16(t_inv), _split_bf16(rhs)))
    qks = [_dot_nt(p["q"].astype(BF16), p["k"].astype(BF16)) * ldec for p, ldec in zip(probs, ldecs)]
    inter = [_dot(jnp.concatenate([sol[:, HEAD_DIM:], p["q"] * p["eg"]], axis=0).astype(BF16), s.astype(BF16))
             for p, sol, s in zip(probs, sols, states)]
    outs, new_states = [], []
    for p, sol, qk, it, s in zip(probs, sols, qks, inter, states):
        g_end = p["gam_c"][0:1] if p["rev"] else p["gam_c"][c - 1:c]
        v_new = (sol[:, :HEAD_DIM] - it[:c]).astype(BF16)
        kg = p["k"] * jnp.exp(g_end - p["gam_c"])
        outs.append(it[c:] + _dot(qk.astype(BF16), v_new))
        new_states.append(jnp.exp(g_end) * s + _dot_tn(kg.astype(BF16), v_new))
    return outs, new_states


def _gdn_kernel(xf_ref, gf_ref, xb_ref, gb_ref, alog_ref, dtb_ref, s0_ref,
                of_ref, ob_ref, sout_ref, s_ref):
    n = pl.program_id(1)

    @pl.when(n == 0)
    def _():
        s_ref[...] = s0_ref[...]

    neg_a = -jnp.exp(alog_ref[...])
    probs, where = [], []
    for r in range(xf_ref.shape[0]):
        for d, (x_ref, g_ref, o_ref) in enumerate(((xf_ref, gf_ref, of_ref), (xb_ref, gb_ref, ob_ref))):
            rev = d == 1
            gates = g_ref[r]
            cs = _cumsum_rows(neg_a * _softplus(gates + dtb_ref[...]), rev)
            cs_t = cs.T
            betas = _sigmoid(gates)
            for h in range(N_HEADS):
                ca = 16 + 4 * d + h
                cbeta = 24 + 4 * d + h
                probs.append(dict(
                    q=x_ref[r, :, h * HEAD_DIM:(h + 1) * HEAD_DIM],
                    k=x_ref[r, :, MIX_W + h * HEAD_DIM:MIX_W + (h + 1) * HEAD_DIM],
                    v=x_ref[r, :, 2 * MIX_W + h * HEAD_DIM:2 * MIX_W + (h + 1) * HEAD_DIM],
                    gam_c=cs[:, ca:ca + 1], gam_r=cs_t[ca:ca + 1, :], beta=betas[:, cbeta:cbeta + 1], rev=rev))
                where.append((o_ref, r, d, h))
    outs, new_states = _gdn_chunks(probs, [s_ref[r, d, h] for _, r, d, h in where])
    for (o_ref, r, d, h), o, s_new in zip(where, outs, new_states):
        o_ref[r, :, h * HEAD_DIM:(h + 1) * HEAD_DIM] = o
        s_ref[r, d, h] = s_new

    @pl.when(n == pl.num_programs(1) - 1)
    def _():
        sout_ref[...] = s_ref[...]


def _gdn_scan(qkv, proj, alog_row, dtb_row, s0):
    b, l, w3 = qkv.shape
    nc = l // CHUNK
    cg = COL_GATES // LANE
    rows = GDN_BATCH_ROWS if b % GDN_BATCH_ROWS == 0 else 1
    st_spec = pl.BlockSpec((rows, 2, N_HEADS, HEAD_DIM, HEAD_DIM), lambda bi, n: (bi, 0, 0, 0, 0))
    row_spec = pl.BlockSpec((1, LANE), lambda bi, n: (0, 0))
    return pl.pallas_call(
        _gdn_kernel,
        grid=(b // rows, nc),
        in_specs=[pl.BlockSpec((rows, CHUNK, w3), lambda bi, n: (bi, n, 0)),
                  pl.BlockSpec((rows, CHUNK, LANE), lambda bi, n: (bi, n, cg)),
                  pl.BlockSpec((rows, CHUNK, w3), lambda bi, n: (bi, nc - 1 - n, 0)),
                  pl.BlockSpec((rows, CHUNK, LANE), lambda bi, n: (bi, nc - 1 - n, cg)),
                  row_spec, row_spec, st_spec],
        out_specs=[pl.BlockSpec((rows, CHUNK, MIX_W), lambda bi, n: (bi, n, 0)),
                   pl.BlockSpec((rows, CHUNK, MIX_W), lambda bi, n: (bi, nc - 1 - n, 0)),
                   st_spec],
        out_shape=[jax.ShapeDtypeStruct((b, l, MIX_W), F32), jax.ShapeDtypeStruct((b, l, MIX_W), F32),
                   jax.ShapeDtypeStruct(s0.shape, F32)],
        scratch_shapes=[pltpu.VMEM((rows, 2, N_HEADS, HEAD_DIM, HEAD_DIM), F32)],
        compiler_params=_cparams(("parallel", "arbitrary"), 40),
        name="gdn_scan",
    )(qkv, proj, qkv, proj, alog_row, dtb_row, s0)


def _merge_kernel(oaf, oab, obf, obb, ocf, ocb, ag, bo, cz, mg, x_ref, mod_ref, gain_ref, wbr_ref, wout_ref,
                  lng_ref, lnb_ref, out_ref, *, alpha):
    d_model = x_ref.shape[-1]
    branches = ((oaf, oab, ag, _silu), (obf, obb, bo, _sigmoid), (ocf, ocb, cz, _silu))
    mixed = None
    for nbr, (of_ref, ob_ref, gate_ref, act) in enumerate(branches):
        parts = []
        for h in range(N_HEADS):
            if len(of_ref.shape) == 4:
                oh = of_ref[h, 0] + ob_ref[h, 0]
            else:
                oh = of_ref[0, :, h * HEAD_DIM:(h + 1) * HEAD_DIM] + ob_ref[0, :, h * HEAD_DIM:(h + 1) * HEAD_DIM]
            parts.append(oh * lax.rsqrt(jnp.mean(oh * oh, axis=-1, keepdims=True) + RMS_EPS))
        y = jnp.concatenate(parts, axis=1) * gain_ref[nbr:nbr + 1, :] * act(gate_ref[0])
        term = _sigmoid(mg[0, :, nbr * d_model:(nbr + 1) * d_model]) * _dot(y.astype(BF16), wbr_ref[nbr])
        mixed = term if mixed is None else mixed + term
    sub = _dot(mixed.astype(BF16), wout_ref[...])
    z = alpha * x_ref[0] + mod_ref[0, 2:3, :] * sub
    out_ref[0] = _ln(z) * lng_ref[...] + lnb_ref[...]


def _merge(outs, proj, x, mod, gain, w_branch, w_out, ln_g, ln_b, alpha, tl):
    b, l, d = x.shape
    u = MIX_W

    def tok(width, col):
        return pl.BlockSpec((1, tl, width), lambda bi, i: (bi, i, col))

    def const(shape):
        return pl.BlockSpec(shape, lambda bi, i: (0,) * len(shape))

    heads = pl.BlockSpec((N_HEADS, 1, tl, LANE), lambda bi, i: (0, bi, i, 0))
    return pl.pallas_call(
        functools.partial(_merge_kernel, alpha=alpha),
        grid=(b, l // tl),
        in_specs=[tok(u, 0), tok(u, 0), heads, heads, tok(u, 0), tok(u, 0),
            tok(u, (COL_A + 4 * MIX_W) // u), tok(u, (COL_B + 3 * MIX_W) // u), tok(u, COL_CZ // u),
            tok(3 * d, COL_MERGE // (3 * d)), tok(d, 0),
            pl.BlockSpec((1, 6, d), lambda bi, i: (bi, 0, 0)),
            const((3, u)), const((3, u, d)), const((d, d)), const((1, d)), const((1, d))],
        out_specs=tok(d, 0),
        out_shape=jax.ShapeDtypeStruct((b, l, d), F32),
        compiler_params=_cparams(("parallel", "parallel"), 48),
        name="merge",
    )(*outs, proj, proj, proj, proj, x, mod, gain, w_branch, w_out, ln_g, ln_b)


def _ffn_kernel(x_ref, mod_ref, wg_ref, wu_ref, wd_ref, lng_ref, lnb_ref, out_ref, h_ref, acc_ref, *, alpha):
    f = pl.program_id(2)

    @pl.when(f == 0)
    def _():
        h_ref[...] = (_ln(x_ref[0]) * (1.0 + mod_ref[0, 4:5, :]) + mod_ref[0, 3:4, :]).astype(BF16)
        acc_ref[...] = jnp.zeros_like(acc_ref)

    h = h_ref[...]
    act = _silu(_dot(h, wg_ref[...])) * _dot(h, wu_ref[...])
    acc_ref[...] += _dot(act.astype(BF16), wd_ref[...])

    @pl.when(f == pl.num_programs(2) - 1)
    def _():
        z = alpha * x_ref[0] + mod_ref[0, 5:6, :] * acc_ref[...]
        out_ref[0] = _ln(z) * lng_ref[...] + lnb_ref[...]


def _ffn(x, mod, w_gate, w_up, w_down, ln_g, ln_b, alpha, tl, tf=512):
    b, l, d = x.shape
    f = w_gate.shape[1]
    return pl.pallas_call(
        functools.partial(_ffn_kernel, alpha=alpha),
        grid=(b, l // tl, f // tf),
        in_specs=[pl.BlockSpec((1, tl, d), lambda bi, i, j: (bi, i, 0)),
                  pl.BlockSpec((1, 6, d), lambda bi, i, j: (bi, 0, 0)),
                  pl.BlockSpec((d, tf), lambda bi, i, j: (0, j)),
                  pl.BlockSpec((d, tf), lambda bi, i, j: (0, j)),
                  pl.BlockSpec((tf, d), lambda bi, i, j: (j, 0)),
                  pl.BlockSpec((1, d), lambda bi, i, j: (0, 0)),
                  pl.BlockSpec((1, d), lambda bi, i, j: (0, 0))],
        out_specs=pl.BlockSpec((1, tl, d), lambda bi, i, j: (bi, i, 0)),
        out_shape=jax.ShapeDtypeStruct((b, l, d), F32),
        scratch_shapes=[pltpu.VMEM((tl, d), BF16), pltpu.VMEM((tl, d), F32)],
        compiler_params=_cparams(("parallel", "parallel", "arbitrary"), 48),
        name="ffn",
    )(x, mod, w_gate, w_up, w_down, ln_g, ln_b)


def _top2_combine(logits):
    lane = lax.broadcasted_iota(jnp.int32, logits.shape, 1).astype(F32)
    neg = -jnp.inf
    lg = jnp.where(lane < N_EXPERTS, logits, neg)
    m1 = jnp.max(lg, axis=1, keepdims=True)
    i1 = jnp.min(jnp.where(lg == m1, lane, float(LANE)), axis=1, keepdims=True)
    lg2 = jnp.where(lane == i1, neg, lg)
    m2 = jnp.max(lg2, axis=1, keepdims=True)
    i2 = jnp.min(jnp.where(lg2 == m2, lane, float(LANE)), axis=1, keepdims=True)
    e2 = jnp.exp(m2 - m1)
    p1 = 1.0 / (1.0 + e2)
    chosen = (lane == i1) | (lane == i2)
    return jnp.where(lane == i1, p1, jnp.where(lane == i2, e2 * p1, 0.0)), chosen


def _moe_kernel(x_ref, mod_ref, wr_ref, wgu_ref, wd_ref, lng_ref, lnb_ref, out_ref,
                h_ref, acc_ref, comb_ref, pos_ref, post_ref, col_ref, xc_ref, yc_ref, cnt_ref, *, alpha, rb):
    e = pl.program_id(2)
    f = pl.program_id(3)
    tl, d = h_ref.shape

    @pl.when((e == 0) & (f == 0))
    def _():
        h = _ln(x_ref[0]) * (1.0 + mod_ref[0, 4:5, :]) + mod_ref[0, 3:4, :]
        h_ref[...] = h.astype(BF16)
        comb, chosen = _top2_combine(_dot(h, wr_ref[...], precision=HIGHEST))
        comb_ref[...] = comb
        blk = min(tl, LANE)
        before = jnp.where(_order_mask(blk, False, strict=True), 1.0, 0.0).astype(BF16)
        routed = jnp.where(chosen, 1.0, 0.0)
        total = jnp.zeros((1, LANE), F32)
        ranks = []
        for r0 in range(0, tl, blk):
            part = routed[r0:r0 + blk]
            ranks.append(_dot(before, part.astype(BF16)) + total)
            total = total + jnp.sum(part, axis=0, keepdims=True)
        rank = jnp.concatenate(ranks, axis=0)
        pos = jnp.where(chosen, rank, -1.0)
        pos_ref[...] = pos
        pos_t = pos.T[0:N_EXPERTS, :]
        post_ref[...] = pos_t
        for ex in range(N_EXPERTS):
            cnt_ref[ex] = jnp.sum(jnp.where(pos_t[ex:ex + 1, :] >= 0.0, 1.0, 0.0)).astype(jnp.int32)
        acc_ref[...] = jnp.zeros_like(acc_ref)

    nblk = (cnt_ref[e] + (rb - 1)) // rb

    def rows(r):
        return pl.ds(pl.multiple_of(r * rb, rb), rb)

    @pl.when(f == 0)
    def _():
        lane = lax.broadcasted_iota(jnp.int32, (tl, LANE), 1)
        col_ref[0] = jnp.sum(jnp.where(lane == e, pos_ref[...], 0.0), axis=1, keepdims=True)
        col_ref[1] = jnp.sum(jnp.where(lane == e, comb_ref[...], 0.0), axis=1, keepdims=True)
        pos_row = post_ref[pl.ds(e, 1), :]

        def gather(r, carry):
            slot = (lax.broadcasted_iota(jnp.int32, (rb, tl), 0) + r * rb).astype(F32)
            sel = jnp.where(pos_row == slot, 1.0, 0.0).astype(BF16)
            xc_ref[rows(r), :] = _dot(sel, h_ref[...]).astype(BF16)
            yc_ref[rows(r), :] = jnp.zeros((rb, d), F32)
            return carry

        lax.fori_loop(0, nblk, gather, 0)

    def expert(r, carry):
        xb = xc_ref[rows(r), :]
        gu = _dot(xb, wgu_ref[0, 0])
        half = gu.shape[1] // 2
        act = _silu(gu[:, :half]) * gu[:, half:]
        yc_ref[rows(r), :] += _dot(act.astype(BF16), wd_ref[0])
        return carry

    lax.fori_loop(0, nblk, expert, 0)

    @pl.when(f == pl.num_programs(3) - 1)
    def _():
        pos_col = col_ref[0]
        weight = col_ref[1]

        def scatter(r, carry):
            slot = (lax.broadcasted_iota(jnp.int32, (tl, rb), 1) + r * rb).astype(F32)
            sel_t = jnp.where(pos_col == slot, 1.0, 0.0).astype(BF16)
            acc_ref[...] += weight * _dot(sel_t, yc_ref[rows(r), :].astype(BF16))
            return carry

        lax.fori_loop(0, nblk, scatter, 0)

    @pl.when((e == pl.num_programs(2) - 1) & (f == pl.num_programs(3) - 1))
    def _():
        z = alpha * x_ref[0] + mod_ref[0, 5:6, :] * acc_ref[...]
        out_ref[0] = _ln(z) * lng_ref[...] + lnb_ref[...]


def _pack_gate_up(w_gate, w_up):
    e, d, f = w_gate.shape
    tf = next(t for t in (896, 512, 256, 128) if f % t == 0)

    def tiles(w):
        return w.astype(BF16).reshape(e, d, f // tf, tf).transpose(0, 2, 1, 3)

    return jnp.concatenate([tiles(w_gate), tiles(w_up)], axis=-1)


def _moe(x, mod, w_router, w_gate_up, w_down, ln_g, ln_b, alpha, tl):
    b, l, d = x.shape
    ne, nf, _, tf2 = w_gate_up.shape
    rb = min(MOE_ROW_BLOCK, tl)
    compact_rows = -(-tl // rb) * rb
    return pl.pallas_call(
        functools.partial(_moe_kernel, alpha=alpha, rb=rb),
        grid=(b, l // tl, ne, nf),
        in_specs=[pl.BlockSpec((1, tl, d), lambda bi, i, e, j: (bi, i, 0)),
                  pl.BlockSpec((1, 6, d), lambda bi, i, e, j: (bi, 0, 0)),
                  pl.BlockSpec((d, LANE), lambda bi, i, e, j: (0, 0)),
                  pl.BlockSpec((1, 1, d, tf2), lambda bi, i, e, j: (e, j, 0, 0)),
                  pl.BlockSpec((1, tf2 // 2, d), lambda bi, i, e, j: (e, j, 0)),
                  pl.BlockSpec((1, d), lambda bi, i, e, j: (0, 0)),
                  pl.BlockSpec((1, d), lambda bi, i, e, j: (0, 0))],
        out_specs=pl.BlockSpec((1, tl, d), lambda bi, i, e, j: (bi, i, 0)),
        out_shape=jax.ShapeDtypeStruct((b, l, d), F32),
        scratch_shapes=[pltpu.VMEM((tl, d), BF16),
                        pltpu.VMEM((tl, d), F32),
                        pltpu.VMEM((tl, LANE), F32),
                        pltpu.VMEM((tl, LANE), F32),
                        pltpu.VMEM((N_EXPERTS, tl), F32),
                        pltpu.VMEM((2, tl, 1), F32),
                        pltpu.VMEM((compact_rows, d), BF16),
                        pltpu.VMEM((compact_rows, d), F32),
                        pltpu.SMEM((N_EXPERTS,), jnp.int32)],
        compiler_params=_cparams(("parallel", "parallel", "arbitrary", "arbitrary"), 56),
        name="moe",
    )(x, mod, w_router, w_gate_up, w_down, ln_g, ln_b)


def _reorder_w_in(w):
    d = w.shape[0]
    o_a, o_b = 0, 5 * MIX_W
    o_bgt = o_b + 4 * MIX_W
    o_cqkv = o_bgt + 16
    o_cz = o_cqkv + 3 * MIX_W
    o_cgt = o_cz + MIX_W
    o_merge = o_cgt + 16
    parts = [w[:, o_merge:o_merge + 3 * d], w[:, o_a:o_bgt], w[:, o_cqkv:o_cgt],
             w[:, o_bgt:o_bgt + 16], w[:, o_cgt:o_cgt + 16],
             jnp.zeros((d, N_PROJ - COL_GATES - 32), w.dtype)]
    return jnp.concatenate(parts, axis=1)


def _pad_row(vals, offset):
    row = jnp.zeros((1, LANE), F32)
    return lax.dynamic_update_slice(row, vals.reshape(1, -1).astype(F32), (0, offset))


def _tile(l, pref):
    return pref if l % pref == 0 else l


def kernel(x, c, ctx, c_ctx, w_ada, b_ada, w_in, conv_w, lb_raw, m_gate_bias, gdn_a_log, gdn_dt_bias, mix_norm,
           w_branch, w_out, ln_g, ln_b, ffn_w_gate, ffn_w_up, ffn_w_down, moe_router, moe_w_gate, moe_w_up,
           moe_w_down):
    depth, d_model = w_in.shape[0], w_in.shape[1]
    b, l_lat, _ = x.shape
    l_ctx = ctx.shape[1]
    alpha = float((2 * depth) ** 0.25)
    assert d_model == 1024 and w_in.shape[2] == 9 * MIX_W + 3 * MIX_W + MIX_W + 32 + 3 * d_model

    lower = _lower_bounds(lb_raw.astype(F32))
    n_rows = -(-(b + 1) // 8) * 8
    cvec = jnp.concatenate([c, c_ctx[None, :], jnp.zeros((n_rows - b - 1, d_model), F32)], axis=0)
    mods = _ada(cvec, w_ada, b_ada)

    zeros_state = jnp.zeros((b, 2, N_HEADS, HEAD_DIM, HEAD_DIM), F32)
    zeros_hgrn = jnp.zeros((b, N_HEADS, 2, HEAD_DIM, HEAD_DIM), F32)
    zeros_nm = jnp.zeros((b, 2, N_HEADS, 8, HEAD_DIM), F32)

    pack = 1024 // l_ctx if (1024 % l_ctx == 0 and b % max(1024 // l_ctx, 1) == 0) else 1
    ctx_packed = (b // pack, pack * l_ctx, d_model)

    for l in range(depth):
        mod_l = mods[l, :b].reshape(b, 6, d_model)
        mod_c = jnp.broadcast_to(mods[l, b].reshape(1, 6, d_model), (b, 6, d_model))
        w_in_l = _reorder_w_in(w_in[l]).astype(BF16)
        lb_row = lower[l].reshape(1, MIX_W)
        gbias = _pad_row(m_gate_bias[l], 0)
        alog_row = _pad_row(gdn_a_log[l], 16)
        dtb_row = _pad_row(gdn_dt_bias[l], 16)
        gain = mix_norm[l]
        wbr = w_branch[l].astype(BF16)
        wo = w_out[l].astype(BF16)
        g0, b0 = ln_g[l, 0].reshape(1, -1), ln_b[l, 0].reshape(1, -1)
        g1, b1 = ln_g[l, 1].reshape(1, -1), ln_b[l, 1].reshape(1, -1)

        p_c = _in_proj(ctx.reshape(ctx_packed), mod_c[:ctx_packed[0]], w_in_l, ctx_packed[1])
        p_c = p_c.reshape(b, l_ctx, N_PROJ)
        p_l = _in_proj(x, mod_l, w_in_l, _tile(l_lat, 1024))
        qkv_c = _gdn_conv(p_c, conv_w[l], _tile(l_ctx, 256))
        qkv_l = _gdn_conv(p_l, conv_w[l], _tile(l_lat, 512))

        oa_c = _hgrn_scan(p_c, lb_row, zeros_hgrn)
        oa_l = _hgrn_scan(p_l, lb_row, oa_c[2])
        ob_c = _mlstm_scan(p_c, gbias, zeros_state, zeros_nm, colmajor=False)
        ob_l = _mlstm_scan(p_l, gbias, ob_c[2], ob_c[3], colmajor=True)
        oc_c = _gdn_scan(qkv_c, p_c, alog_row, dtb_row, zeros_state)
        oc_l = _gdn_scan(qkv_l, p_l, alog_row, dtb_row, oc_c[2])

        def channel_mixer(t, mod, tl):
            i = l // 2
            if l % 2 == 0:
                return _ffn(t, mod, ffn_w_gate[i].astype(BF16), ffn_w_up[i].astype(BF16),
                            ffn_w_down[i].astype(BF16), g1, b1, alpha, tl)
            w_r = jnp.concatenate([moe_router[i], jnp.zeros((d_model, LANE - N_EXPERTS), F32)], axis=1)
            return _moe(t, mod, w_r, _pack_gate_up(moe_w_gate[i], moe_w_up[i]), moe_w_down[i].astype(BF16),
                        g1, b1, alpha, tl)

        outs_l = (oa_l[0], oa_l[1], ob_l[0], ob_l[1], oc_l[0], oc_l[1])
        x = _merge(outs_l, p_l, x, mod_l, gain, wbr, wo, g0, b0, alpha, _tile(l_lat, 256))
        x = channel_mixer(x, mod_l, _tile(l_lat, 1024))
        if l < depth - 1:
            outs_c = (oa_c[0], oa_c[1], ob_c[0], ob_c[1], oc_c[0], oc_c[1])
            ctx = _merge(outs_c, p_c, ctx, mod_c, gain, wbr, wo, g0, b0, alpha, _tile(l_ctx, 256))
            ctx = channel_mixer(ctx.reshape(ctx_packed), mod_c[:ctx_packed[0]], ctx_packed[1]).reshape(ctx.shape)
    return x
```

```python
import functools

import jax
import jax.numpy as jnp
from jax import lax
from jax.experimental import pallas as pl
from jax.experimental.pallas import tpu as pltpu

F32 = jnp.float32
BF16 = jnp.bfloat16
HIGHEST = lax.Precision.HIGHEST

HEAD_DIM = 128
N_HEADS = 4
MIX_W = N_HEADS * HEAD_DIM
CHUNK = 64
GRID_W = 64
SUB = 16
MLSTM_GROUP = 2
GDN_BATCH_ROWS = 2
HGRN_GROUP = 8
N_EXPERTS = 8
MOE_ROW_BLOCK = 288
LN_EPS = 1e-5
RMS_EPS = 1e-6
L2_EPS = 1e-6
MASK_NEG = -1e30
LOG2_E = 1.4426950408889634
LANE = 128

COL_MERGE = 0
COL_A = 3 * 1024
COL_B = COL_A + 5 * MIX_W
COL_CQKV = COL_B + 4 * MIX_W
COL_CZ = COL_CQKV + 3 * MIX_W
COL_GATES = COL_CZ + MIX_W
N_PROJ = COL_GATES + 4 * LANE
PROJ_TILE = N_PROJ // 5


def _cparams(sem, vmem_mb=None):
    kw = dict(dimension_semantics=sem)
    if vmem_mb is not None:
        kw["vmem_limit_bytes"] = vmem_mb << 20
    return pltpu.CompilerParams(**kw)


def _dot(a, b, precision=None):
    return jnp.dot(a, b, preferred_element_type=F32, precision=precision)


def _dot_nt(a, b, precision=None):
    return lax.dot_general(a, b, (((1,), (1,)), ((), ())), preferred_element_type=F32, precision=precision)


def _dot_tn(a, b):
    return lax.dot_general(a, b, (((0,), (0,)), ((), ())), preferred_element_type=F32)


def _ln(x):
    mu = jnp.mean(x, axis=-1, keepdims=True)
    xc = x - mu
    var = jnp.mean(xc * xc, axis=-1, keepdims=True)
    return xc * lax.rsqrt(var + LN_EPS)


def _sigmoid(x):
    return 1.0 / (1.0 + jnp.exp(-x))


def _silu(x):
    return x * _sigmoid(x)


def _softplus(x):
    return jnp.maximum(x, 0.0) + jnp.log(1.0 + jnp.exp(-jnp.abs(x)))


def _log_sigmoid(x):
    return -_softplus(-x)


def _order_mask(n, rev, strict=False):
    r = lax.broadcasted_iota(jnp.int32, (n, n), 0)
    c = lax.broadcasted_iota(jnp.int32, (n, n), 1)
    if rev:
        return (c > r) if strict else (c >= r)
    return (c < r) if strict else (c <= r)


def _cumsum_rows(x, rev):
    m = _order_mask(x.shape[0], rev).astype(F32)
    return _dot(m, x, precision=HIGHEST)


def _ada_kernel(c_ref, w_ref, b_ref, o_ref):
    s = _silu(c_ref[...])
    o_ref[0] = _dot(s, w_ref[0], precision=HIGHEST) + b_ref[0]


def _ada(cvec, w_ada, b_ada, tn=1536):
    depth, d, n6 = w_ada.shape
    rows = cvec.shape[0]
    return pl.pallas_call(
        _ada_kernel,
        grid=(depth, n6 // tn),
        in_specs=[pl.BlockSpec((rows, d), lambda l, n: (0, 0)),
                  pl.BlockSpec((1, d, tn), lambda l, n: (l, 0, n)),
                  pl.BlockSpec((1, 1, tn), lambda l, n: (l, 0, n))],
        out_specs=pl.BlockSpec((1, rows, tn), lambda l, n: (l, 0, n)),
        out_shape=jax.ShapeDtypeStruct((depth, rows, n6), F32),
        compiler_params=_cparams(("parallel", "parallel"), 40),
        name="ada",
    )(cvec, w_ada, b_ada.reshape(depth, 1, n6))


def _lower_bounds_kernel(lb_ref, o_ref):
    x = lb_ref[...]
    depth = x.shape[0]
    mx = jnp.max(x, axis=0, keepdims=True)
    e = jnp.exp(x - mx)
    p = e / jnp.sum(e, axis=0, keepdims=True)
    acc = jnp.zeros_like(p[0:1])
    for l in range(depth):
        o_ref[l:l + 1, :] = acc
        if l + 1 < depth:
            acc = acc + p[l + 1:l + 2]


def _lower_bounds(lb_raw):
    return pl.pallas_call(
        _lower_bounds_kernel,
        out_shape=jax.ShapeDtypeStruct(lb_raw.shape, F32),
        name="lower_bounds",
    )(lb_raw)


def _in_proj_kernel(x_ref, mod_ref, w_ref, o_ref, h_ref):
    @pl.when(pl.program_id(2) == 0)
    def _():
        shift = mod_ref[0, 0:1, :]
        scale = mod_ref[0, 1:2, :]
        h_ref[...] = (_ln(x_ref[0]) * (1.0 + scale) + shift).astype(BF16)

    o_ref[0] = _dot(h_ref[...], w_ref[...])


def _in_proj(x, mod, w, tl, tn=PROJ_TILE):
    b, l, d = x.shape
    n_proj = w.shape[1]
    return pl.pallas_call(
        _in_proj_kernel,
        grid=(b, l // tl, n_proj // tn),
        in_specs=[pl.BlockSpec((1, tl, d), lambda bi, i, n: (bi, i, 0)),
                  pl.BlockSpec((1, 6, d), lambda bi, i, n: (bi, 0, 0)),
                  pl.BlockSpec((d, tn), lambda bi, i, n: (0, n))],
        out_specs=pl.BlockSpec((1, tl, tn), lambda bi, i, n: (bi, i, n)),
        out_shape=jax.ShapeDtypeStruct((b, l, n_proj), F32),
        scratch_shapes=[pltpu.VMEM((tl, d), BF16)],
        compiler_params=_cparams(("parallel", "parallel", "arbitrary"), 48),
        name="in_proj",
    )(x, mod, w)


def _conv_kernel(x_ref, prev_ref, next_ref, w_ref, o_ref):
    i = pl.program_id(1)
    last = pl.num_programs(1) - 1
    x = x_ref[0]
    tl = x.shape[0]
    row = lax.broadcasted_iota(jnp.int32, x.shape, 0)
    before = jnp.where(i > 0, prev_ref[0, 7:8, :], 0.0)
    after = jnp.where(i < last, next_ref[0, 0:1, :], 0.0)
    x_dn = jnp.where(row == 0, before, pltpu.roll(x, 1, axis=0))
    x_up = jnp.where(row == tl - 1, after, pltpu.roll(x, tl - 1, axis=0))
    y = _silu(x_dn * w_ref[0:1, :] + x * w_ref[1:2, :] + x_up * w_ref[2:3, :])
    for j in range(3 * N_HEADS):
        t = y[:, j * HEAD_DIM:(j + 1) * HEAD_DIM]
        if j < 2 * N_HEADS:
            t = t * lax.rsqrt(jnp.sum(t * t, axis=-1, keepdims=True) + L2_EPS)
            if j < N_HEADS:
                t = t * HEAD_DIM ** -0.5
        o_ref[0, :, j * HEAD_DIM:(j + 1) * HEAD_DIM] = t


def _gdn_conv(proj, conv_w, tl):
    b, l, _ = proj.shape
    w3 = 3 * MIX_W
    cb = COL_CQKV // w3
    nb8 = l // 8
    r8 = tl // 8
    return pl.pallas_call(
        _conv_kernel,
        grid=(b, l // tl),
        in_specs=[pl.BlockSpec((1, tl, w3), lambda bi, i: (bi, i, cb)),
                  pl.BlockSpec((1, 8, w3), lambda bi, i: (bi, jnp.maximum(i * r8 - 1, 0), cb)),
                  pl.BlockSpec((1, 8, w3), lambda bi, i: (bi, jnp.minimum((i + 1) * r8, nb8 - 1), cb)),
                  pl.BlockSpec((3, w3), lambda bi, i: (0, 0))],
        out_specs=pl.BlockSpec((1, tl, w3), lambda bi, i: (bi, i, 0)),
        out_shape=jax.ShapeDtypeStruct((b, l, w3), F32),
        compiler_params=_cparams(("parallel", "parallel"), 40),
        name="gdn_conv",
    )(proj, proj, proj, conv_w)


def _hgrn_chunks(probs, states, emat):
    c = probs[0]["q"].shape[0]
    nb = c // SUB
    for p in probs:
        lb = p["lb"]
        sig = _sigmoid(p["fpre"])
        p["k"] = (1.0 - lb) * (1.0 - sig)
        p["g"] = _cumsum_rows(jnp.log(lb + (1.0 - lb) * sig), p["rev"])

    walls = []
    for p in probs:
        q = p["q"]
        g2 = p["g"] * LOG2_E
        lk2 = jnp.log(p["k"]) * LOG2_E
        rel = lk2 - g2
        rows = []
        half = SUB // 2
        dead = jnp.zeros((half, HEAD_DIM), F32)
        for j in range(nb):
            r0 = j * SUB
            pieces = []
            for s in range(SUB):
                skip = 0 if (s >= half and not p["rev"]) else 1 if (s < half and p["rev"]) else None
                parts = []
                for part in range(2):
                    if part == skip:
                        parts.append(dead)
                        continue
                    t0 = r0 + part * half
                    e2 = jnp.exp2(jnp.minimum(g2[t0:t0 + half] + rel[r0 + s:r0 + s + 1], lk2[r0 + s:r0 + s + 1]))
                    parts.append(e2 * q[t0:t0 + half])
                pieces.append(jnp.concatenate(parts, axis=0).astype(BF16))
            rows.append(jnp.concatenate(pieces, axis=1))
        walls.append(jnp.concatenate(rows, axis=0))
    a_diags = [_dot(w, emat) for w in walls]

    a_offs = []
    for p in probs:
        q, k, g, rev = p["q"], p["k"], p["g"], p["rev"]
        offs = []
        for i in range(nb):
            r0 = i * SUB
            if (not rev and i == 0) or (rev and i == nb - 1):
                offs.append(jnp.zeros((SUB, c), F32))
                continue
            bi = g[r0 + SUB:r0 + SUB + 1] if rev else g[r0 - 1:r0]
            lhs = (q[r0:r0 + SUB] * jnp.exp(g[r0:r0 + SUB] - bi)).astype(BF16)
            rhs = (k * jnp.exp(jnp.minimum(bi - g, 0.0))).astype(BF16)
            offs.append(_dot_nt(lhs, rhs))
        a_offs.append(jnp.concatenate(offs, axis=0))

    r = lax.broadcasted_iota(jnp.int32, (c, c), 0)
    col = lax.broadcasted_iota(jnp.int32, (c, c), 1)
    shift = SUB.bit_length() - 1
    rb = lax.shift_right_logical(r, shift)
    cb = lax.shift_right_logical(col, shift)
    inters = [_dot_nt((p["q"] * jnp.exp(p["g"])).astype(BF16), s_t.astype(BF16)) for p, s_t in zip(probs, states)]
    outs, new_states = [], []
    for p, a_diag, a_off, inter, s_t in zip(probs, a_diags, a_offs, inters, states):
        rev, g = p["rev"], p["g"]
        order = (col >= r) if rev else (col <= r)
        off_side = (cb > rb) if rev else (cb < rb)
        a = jnp.where((rb == cb) & order, a_diag, jnp.where(off_side, a_off, 0.0))
        g_end = g[0:1] if rev else g[c - 1:c]
        vb = p["v"].astype(BF16)
        outs.append(_dot(a.astype(BF16), vb) + inter)
        kd = (p["k"] * jnp.exp(g_end - g)).astype(BF16)
        new_states.append(s_t * jnp.exp(g_end) + _dot_tn(vb, kd))
    return outs, new_states


def _hgrn_kernel(qf_ref, ff_ref, vf_ref, qb_ref, fb_ref, vb_ref, lb_ref, e_ref, s0_ref,
                 of_ref, ob_ref, sout_ref, s_ref):
    n = pl.program_id(1)

    @pl.when(n == 0)
    def _():
        s_ref[...] = s0_ref[0]

    probs, where = [], []
    for d, (q_ref, f_ref, v_ref, o_ref) in enumerate(((qf_ref, ff_ref, vf_ref, of_ref),
                                                      (qb_ref, fb_ref, vb_ref, ob_ref))):
        for h in range(N_HEADS):
            sl = slice(h * HEAD_DIM, (h + 1) * HEAD_DIM)
            probs.append(dict(q=q_ref[0, :, sl], fpre=f_ref[0, :, sl], v=v_ref[0, :, sl], lb=lb_ref[:, sl],
                              rev=d == 1))
            where.append((o_ref, sl, h, d))
    states = [s_ref[h, d] for _, _, h, d in where]
    outs, new_states = [], []
    for g0 in range(0, len(probs), HGRN_GROUP):
        o_g, s_g = _hgrn_chunks(probs[g0:g0 + HGRN_GROUP], states[g0:g0 + HGRN_GROUP], e_ref[...])
        outs += o_g
        new_states += s_g
    for (o_ref, sl, h, d), o, s_new in zip(where, outs, new_states):
        o_ref[0, :, sl] = o
        s_ref[h, d] = s_new

    @pl.when(n == pl.num_programs(1) - 1)
    def _():
        sout_ref[0] = s_ref[...]


def _hgrn_emat():
    r = jnp.arange(SUB * HEAD_DIM)[:, None] // HEAD_DIM
    c = jnp.arange(CHUNK)[None, :] % SUB
    return (r == c).astype(BF16)


def _hgrn_scan(proj, lb, s0):
    b, l, _ = proj.shape
    nc = l // CHUNK
    ca = COL_A // MIX_W

    def fwd(col):
        return pl.BlockSpec((1, CHUNK, MIX_W), lambda bi, n: (bi, n, ca + col))

    def bwd(col):
        return pl.BlockSpec((1, CHUNK, MIX_W), lambda bi, n: (bi, nc - 1 - n, ca + col))

    st_spec = pl.BlockSpec((1, N_HEADS, 2, HEAD_DIM, HEAD_DIM), lambda bi, n: (bi, 0, 0, 0, 0))
    return pl.pallas_call(
        _hgrn_kernel,
        grid=(b, nc),
        in_specs=[fwd(0), fwd(1), fwd(3), bwd(0), bwd(2), bwd(3),
                  pl.BlockSpec((1, MIX_W), lambda bi, n: (0, 0)),
                  pl.BlockSpec((SUB * HEAD_DIM, CHUNK), lambda bi, n: (0, 0)),
                  st_spec],
        out_specs=[pl.BlockSpec((1, CHUNK, MIX_W), lambda bi, n: (bi, n, 0)),
                   pl.BlockSpec((1, CHUNK, MIX_W), lambda bi, n: (bi, nc - 1 - n, 0)),
                   st_spec],
        out_shape=[jax.ShapeDtypeStruct((b, l, MIX_W), F32),
                   jax.ShapeDtypeStruct((b, l, MIX_W), F32),
                   jax.ShapeDtypeStruct(s0.shape, F32)],
        scratch_shapes=[pltpu.VMEM((N_HEADS, 2, HEAD_DIM, HEAD_DIM), F32)],
        compiler_params=_cparams(("parallel", "arbitrary"), 40),
        name="hgrn_scan",
    )(proj, proj, proj, proj, proj, proj, lb, _hgrn_emat(), s0)


def _mlstm_chunks(probs, states):
    c = probs[0]["q"].shape[0]
    dv = probs[0]["v"].shape[1]
    ones = jnp.ones((c, dv), BF16)
    for p, (cmn, m) in zip(probs, states):
        tri = _order_mask(c, p["rev"])
        log_d = jnp.where(tri, p["bb_c"] - p["bb_r"] + p["ic_r"], MASK_NEG)
        log_inter = p["bb_c"] + m
        m_t = jnp.maximum(jnp.max(log_d, axis=1, keepdims=True), log_inter)
        p["m_t"] = m_t
        p["dmat"] = jnp.where(tri, jnp.exp(log_d - m_t), 0.0)
        p["inter"] = jnp.exp(log_inter - m_t)
        p["qb"], p["kb"] = p["q"].astype(BF16), p["k"].astype(BF16)
        p["v1"] = jnp.concatenate([p["v"].astype(BF16), ones], axis=1)
    scores = [_dot_nt(p["qb"], p["kb"]) * p["dmat"] for p in probs]
    carried = [_dot(p["qb"], cmn.astype(BF16)) for p, (cmn, _) in zip(probs, states)]
    local = [_dot(s.astype(BF16), p["v1"]) for p, s in zip(probs, scores)]
    outs, new_states = [], []
    for p, car, loc, (cmn, m) in zip(probs, carried, local, states):
        m_t = p["m_t"]
        both = loc + p["inter"] * car
        outs.append(both[:, :dv] / jnp.maximum(jnp.abs(both[:, dv:]), jnp.exp(-m_t)))
        m_new = m_t[0:1] if p["rev"] else m_t[c - 1:c]
        bb_end = p["bb_c"][0:1] if p["rev"] else p["bb_c"][c - 1:c]
        decay = jnp.exp(bb_end + m - m_new)
        kw = p["k"] * jnp.exp(bb_end - p["bb_c"] + p["ic_c"] - m_new)
        new_states.append((decay * cmn + _dot_tn(kw.astype(BF16), p["v1"]), m_new))
    return outs, new_states


def _column_view(ref):
    rows, group, width = ref.shape[-3:]
    return ref.reshape(rows * group, width), group


def _chunk_rows(j):
    start = j * CHUNK
    return pl.ds(start if isinstance(start, int) else pl.multiple_of(start, CHUNK), CHUNK)


def _load_chunk(ref, j, colmajor):
    if colmajor:
        flat, group = _column_view(ref)
        return flat[pl.ds(j, CHUNK, stride=group), :]
    return ref[0, _chunk_rows(j), :]


def _store_chunk(ref, j, val, colmajor):
    if colmajor:
        flat, group = _column_view(ref)
        flat[pl.ds(j, CHUNK, stride=group), :] = val
    else:
        ref[0, _chunk_rows(j), :] = val


def _columns_as_rows(x, first):
    row = lax.broadcasted_iota(jnp.int32, (8, LANE), 0)
    lane = lax.broadcasted_iota(jnp.int32, (8, LANE), 1)
    sel = jnp.where(lane == row + first, 1.0, 0.0).astype(BF16)
    hi = x.astype(BF16)
    rest = x - hi.astype(F32)
    mid = rest.astype(BF16)
    lo = (rest - mid.astype(F32)).astype(BF16)
    return _dot_nt(sel, hi) + (_dot_nt(sel, mid) + _dot_nt(sel, lo))


def _mlstm_kernel(*refs, colmajor, chunks):
    per_dir = 3 * N_HEADS + 1
    fwd, bwd = refs[:per_dir], refs[per_dir:2 * per_dir]
    bias_ref, cm0_ref, nm0_ref, of_ref, ob_ref, cmout_ref, nmout_ref, cm_ref, nm_ref = refs[2 * per_dir:]
    n = pl.program_id(1)

    @pl.when(n == 0)
    def _():
        cm_ref[...] = cm0_ref[0]
        nm_ref[...] = nm0_ref[0]

    def step(i, carry):
        probs, where = [], []
        for d, (in_refs, o_ref) in enumerate(((fwd, of_ref), (bwd, ob_ref))):
            rev = d == 1
            j = chunks - 1 - i if rev else i
            gates = _load_chunk(in_refs[-1], j, colmajor) + bias_ref[...]
            cs = _cumsum_rows(_log_sigmoid(gates), rev)
            cs_rows = _columns_as_rows(cs, 8 + 4 * d)
            gate_rows = _columns_as_rows(gates, 4 * d)
            for h in range(N_HEADS):
                ci = 4 * d + h
                cf = 8 + 4 * d + h
                probs.append(dict(
                    q=_load_chunk(in_refs[h], j, colmajor),
                    k=_load_chunk(in_refs[N_HEADS + h], j, colmajor) * HEAD_DIM ** -0.5,
                    v=_load_chunk(in_refs[2 * N_HEADS + h], j, colmajor),
                    bb_c=cs[:, cf:cf + 1], bb_r=cs_rows[h:h + 1, :],
                    ic_c=gates[:, ci:ci + 1], ic_r=gate_rows[h:h + 1, :], rev=rev))
                where.append((o_ref, j, d, h))
        states = [(cm_ref[d, h], nm_ref[d, h, 0:1, 0:1]) for _, _, d, h in where]
        outs, new_states = [], []
        for g0 in range(0, len(probs), MLSTM_GROUP):
            o_g, s_g = _mlstm_chunks(probs[g0:g0 + MLSTM_GROUP], states[g0:g0 + MLSTM_GROUP])
            outs += o_g
            new_states += s_g
        for (o_ref, j, d, h), o, (cmn_new, m_new) in zip(where, outs, new_states):
            _store_chunk(o_ref.at[h], j, o, colmajor)
            cm_ref[d, h] = cmn_new
            nm_ref[d, h, 0:1, :] = jnp.broadcast_to(m_new, (1, HEAD_DIM))
        return carry

    lax.fori_loop(0, chunks, step, 0)

    @pl.when(n == pl.num_programs(1) - 1)
    def _():
        cmout_ref[0] = cm_ref[...]
        nmout_ref[0] = nm_ref[...]


def _mlstm_scan(proj, gbias, cm0, nm0, colmajor):
    b, l, n_proj = proj.shape
    cb = COL_B // LANE
    cg = COL_GATES // LANE
    if colmajor:
        chunks = 8
        assert l == CHUNK * GRID_W and GRID_W % chunks == 0
        nb = GRID_W // chunks
        src = proj.reshape(b, CHUNK, GRID_W, n_proj)
        in_blk = (1, CHUNK, chunks, LANE)
        out_blk = (N_HEADS, 1, CHUNK, chunks, LANE)
        out_view = (N_HEADS, b, CHUNK, GRID_W, LANE)

        def in_map(col, flip):
            return lambda bi, n: (bi, 0, nb - 1 - n if flip else n, col)

        def out_map(flip):
            return lambda bi, n: (0, bi, 0, nb - 1 - n if flip else n, 0)
    else:
        chunks = min(l // CHUNK, 4)
        nb = l // (chunks * CHUNK)
        src = proj
        in_blk = (1, chunks * CHUNK, LANE)
        out_blk = (N_HEADS, 1, chunks * CHUNK, LANE)
        out_view = (N_HEADS, b, l, LANE)

        def in_map(col, flip):
            return lambda bi, n: (bi, nb - 1 - n if flip else n, col)

        def out_map(flip):
            return lambda bi, n: (0, bi, nb - 1 - n if flip else n, 0)

    def specs(flip):
        cols = [cb + j for j in range(3 * N_HEADS)] + [cg]
        return [pl.BlockSpec(in_blk, in_map(col, flip)) for col in cols]

    cm_spec = pl.BlockSpec((1, 2, N_HEADS, HEAD_DIM, 2 * HEAD_DIM), lambda bi, n: (bi, 0, 0, 0, 0))
    nm_spec = pl.BlockSpec((1, 2, N_HEADS, 8, HEAD_DIM), lambda bi, n: (bi, 0, 0, 0, 0))
    n_in = 2 * (3 * N_HEADS + 1)
    o_f, o_b, cm, nm = pl.pallas_call(
        functools.partial(_mlstm_kernel, colmajor=colmajor, chunks=chunks),
        grid=(b, nb),
        in_specs=specs(False) + specs(True) + [pl.BlockSpec((1, LANE), lambda bi, n: (0, 0)), cm_spec, nm_spec],
        out_specs=[pl.BlockSpec(out_blk, out_map(False)), pl.BlockSpec(out_blk, out_map(True)), cm_spec, nm_spec],
        out_shape=[jax.ShapeDtypeStruct(out_view, F32), jax.ShapeDtypeStruct(out_view, F32),
                   jax.ShapeDtypeStruct(cm0.shape, F32), jax.ShapeDtypeStruct(nm0.shape, F32)],
        scratch_shapes=[pltpu.VMEM((2, N_HEADS, HEAD_DIM, 2 * HEAD_DIM), F32),
                        pltpu.VMEM((2, N_HEADS, 8, HEAD_DIM), F32)],
        compiler_params=_cparams(("parallel", "arbitrary"), 40),
        name="mlstm_scan",
    )(*([src] * n_in), gbias, cm0, nm0)
    head_major = (N_HEADS, b, l, LANE)
    return o_f.reshape(head_major), o_b.reshape(head_major), cm, nm


def _split_bf16(x):
    hi = x.astype(BF16)
    return hi, (x - hi.astype(F32)).astype(BF16)


def _dot3(a, b, nt=False):
    f = _dot_nt if nt else _dot
    return f(a[0], b[0]) + (f(a[0], b[1]) + f(a[1], b[0]))


def _inv_unit_triangular(a_list):
    n = a_list[0].shape[0]
    eye = (lax.broadcasted_iota(jnp.int32, (n, n), 0) == lax.broadcasted_iota(jnp.int32, (n, n), 1)).astype(F32)
    xs = [eye - a for a in a_list]
    splits = [_split_bf16(a) for a in a_list]
    ps = [_dot3(s, s) for s in splits]
    steps = (n - 1).bit_length() - 1
    for step in range(steps):
        last = step == steps - 1
        p_split = [_split_bf16(p) for p in ps]
        lhs = xs if last else [jnp.concatenate([x, p], axis=0) for x, p in zip(xs, ps)]
        prods = [_dot3(_split_bf16(l), s) for l, s in zip(lhs, p_split)]
        xs = [x + pr[:n] for x, pr in zip(xs, prods)]
        if not last:
            ps = [pr[n:] for pr in prods]
    return xs


def _gdn_chunks(probs, states):
    c = probs[0]["q"].shape[0]
    a_list, ldecs = [], []
    for p in probs:
        tri = _order_mask(c, p["rev"])
        strict = _order_mask(c, p["rev"], strict=True)
        ldec = jnp.where(tri, jnp.exp(jnp.minimum(p["gam_c"] - p["gam_r"], 0.0)), 0.0)
        p["kbeta"] = p["k"] * p["beta"]
        kk = _dot_nt(p["kbeta"].astype(BF16), p["k"].astype(BF16))
        a_list.append(jnp.where(strict, kk * ldec, 0.0))
        ldecs.append(ldec)
    t_invs = _inv_unit_triangular(a_list)
    sols = []
    for p, t_inv in zip(probs, t_invs):
        p["eg"] = jnp.exp(p["gam_c"])
        rhs = jnp.concatenate([p["v"] * p["beta"], p["kbeta"] * p["eg"]], axis=1)
        sols.append(_dot3(_split_bf16(t_inv), _split_bf16(rhs)))
    qks = [_dot_nt(p["q"].astype(BF16), p["k"].astype(BF16)) * ldec for p, ldec in zip(probs, ldecs)]
    inter = [_dot(jnp.concatenate([sol[:, HEAD_DIM:], p["q"] * p["eg"]], axis=0).astype(BF16), s.astype(BF16))
             for p, sol, s in zip(probs, sols, states)]
    outs, new_states = [], []
    for p, sol, qk, it, s in zip(probs, sols, qks, inter, states):
        g_end = p["gam_c"][0:1] if p["rev"] else p["gam_c"][c - 1:c]
        v_new = (sol[:, :HEAD_DIM] - it[:c]).astype(BF16)
        kg = p["k"] * jnp.exp(g_end - p["gam_c"])
        outs.append(it[c:] + _dot(qk.astype(BF16), v_new))
        new_states.append(jnp.exp(g_end) * s + _dot_tn(kg.astype(BF16), v_new))
    return outs, new_states


def _gdn_kernel(xf_ref, gf_ref, xb_ref, gb_ref, alog_ref, dtb_ref, s0_ref,
                of_ref, ob_ref, sout_ref, s_ref):
    n = pl.program_id(1)

    @pl.when(n == 0)
    def _():
        s_ref[...] = s0_ref[...]

    neg_a = -jnp.exp(alog_ref[...])
    probs, where = [], []
    for r in range(xf_ref.shape[0]):
        for d, (x_ref, g_ref, o_ref) in enumerate(((xf_ref, gf_ref, of_ref), (xb_ref, gb_ref, ob_ref))):
            rev = d == 1
            gates = g_ref[r]
            cs = _cumsum_rows(neg_a * _softplus(gates + dtb_ref[...]), rev)
            cs_t = cs.T
            betas = _sigmoid(gates)
            for h in range(N_HEADS):
                ca = 16 + 4 * d + h
                cbeta = 24 + 4 * d + h
                probs.append(dict(
                    q=x_ref[r, :, h * HEAD_DIM:(h + 1) * HEAD_DIM],
                    k=x_ref[r, :, MIX_W + h * HEAD_DIM:MIX_W + (h + 1) * HEAD_DIM],
                    v=x_ref[r, :, 2 * MIX_W + h * HEAD_DIM:2 * MIX_W + (h + 1) * HEAD_DIM],
                    gam_c=cs[:, ca:ca + 1], gam_r=cs_t[ca:ca + 1, :], beta=betas[:, cbeta:cbeta + 1], rev=rev))
                where.append((o_ref, r, d, h))
    outs, new_states = _gdn_chunks(probs, [s_ref[r, d, h] for _, r, d, h in where])
    for (o_ref, r, d, h), o, s_new in zip(where, outs, new_states):
        o_ref[r, :, h * HEAD_DIM:(h + 1) * HEAD_DIM] = o
        s_ref[r, d, h] = s_new

    @pl.when(n == pl.num_programs(1) - 1)
    def _():
        sout_ref[...] = s_ref[...]


def _gdn_scan(qkv, proj, alog_row, dtb_row, s0):
    b, l, w3 = qkv.shape
    nc = l // CHUNK
    cg = COL_GATES // LANE
    rows = GDN_BATCH_ROWS if b % GDN_BATCH_ROWS == 0 else 1
    st_spec = pl.BlockSpec((rows, 2, N_HEADS, HEAD_DIM, HEAD_DIM), lambda bi, n: (bi, 0, 0, 0, 0))
    row_spec = pl.BlockSpec((1, LANE), lambda bi, n: (0, 0))
    return pl.pallas_call(
        _gdn_kernel,
        grid=(b // rows, nc),
        in_specs=[pl.BlockSpec((rows, CHUNK, w3), lambda bi, n: (bi, n, 0)),
                  pl.BlockSpec((rows, CHUNK, LANE), lambda bi, n: (bi, n, cg)),
                  pl.BlockSpec((rows, CHUNK, w3), lambda bi, n: (bi, nc - 1 - n, 0)),
                  pl.BlockSpec((rows, CHUNK, LANE), lambda bi, n: (bi, nc - 1 - n, cg)),
                  row_spec, row_spec, st_spec],
        out_specs=[pl.BlockSpec((rows, CHUNK, MIX_W), lambda bi, n: (bi, n, 0)),
                   pl.BlockSpec((rows, CHUNK, MIX_W), lambda bi, n: (bi, nc - 1 - n, 0)),
                   st_spec],
        out_shape=[jax.ShapeDtypeStruct((b, l, MIX_W), F32), jax.ShapeDtypeStruct((b, l, MIX_W), F32),
                   jax.ShapeDtypeStruct(s0.shape, F32)],
        scratch_shapes=[pltpu.VMEM((rows, 2, N_HEADS, HEAD_DIM, HEAD_DIM), F32)],
        compiler_params=_cparams(("parallel", "arbitrary"), 40),
        name="gdn_scan",
    )(qkv, proj, qkv, proj, alog_row, dtb_row, s0)


def _merge_kernel(oaf, oab, obf, obb, ocf, ocb, ag, bo, cz, mg, x_ref, mod_ref, gain_ref, wbr_ref, wout_ref,
                  lng_ref, lnb_ref, out_ref, *, alpha):
    d_model = x_ref.shape[-1]
    branches = ((oaf, oab, ag, _silu), (obf, obb, bo, _sigmoid), (ocf, ocb, cz, _silu))
    mixed = None
    for nbr, (of_ref, ob_ref, gate_ref, act) in enumerate(branches):
        parts = []
        for h in range(N_HEADS):
            if len(of_ref.shape) == 4:
                oh = of_ref[h, 0] + ob_ref[h, 0]
            else:
                oh = of_ref[0, :, h * HEAD_DIM:(h + 1) * HEAD_DIM] + ob_ref[0, :, h * HEAD_DIM:(h + 1) * HEAD_DIM]
            parts.append(oh * lax.rsqrt(jnp.mean(oh * oh, axis=-1, keepdims=True) + RMS_EPS))
        y = jnp.concatenate(parts, axis=1) * gain_ref[nbr:nbr + 1, :] * act(gate_ref[0])
        term = _sigmoid(mg[0, :, nbr * d_model:(nbr + 1) * d_model]) * _dot(y.astype(BF16), wbr_ref[nbr])
        mixed = term if mixed is None else mixed + term
    sub = _dot(mixed.astype(BF16), wout_ref[...])
    z = alpha * x_ref[0] + mod_ref[0, 2:3, :] * sub
    out_ref[0] = _ln(z) * lng_ref[...] + lnb_ref[...]


def _merge(outs, proj, x, mod, gain, w_branch, w_out, ln_g, ln_b, alpha, tl):
    b, l, d = x.shape
    u = MIX_W

    def tok(width, col):
        return pl.BlockSpec((1, tl, width), lambda bi, i: (bi, i, col))

    def const(shape):
        return pl.BlockSpec(shape, lambda bi, i: (0,) * len(shape))

    heads = pl.BlockSpec((N_HEADS, 1, tl, LANE), lambda bi, i: (0, bi, i, 0))
    return pl.pallas_call(
        functools.partial(_merge_kernel, alpha=alpha),
        grid=(b, l // tl),
        in_specs=[tok(u, 0), tok(u, 0), heads, heads, tok(u, 0), tok(u, 0),
            tok(u, (COL_A + 4 * MIX_W) // u), tok(u, (COL_B + 3 * MIX_W) // u), tok(u, COL_CZ // u),
            tok(3 * d, COL_MERGE // (3 * d)), tok(d, 0),
            pl.BlockSpec((1, 6, d), lambda bi, i: (bi, 0, 0)),
            const((3, u)), const((3, u, d)), const((d, d)), const((1, d)), const((1, d))],
        out_specs=tok(d, 0),
        out_shape=jax.ShapeDtypeStruct((b, l, d), F32),
        compiler_params=_cparams(("parallel", "parallel"), 48),
        name="merge",
    )(*outs, proj, proj, proj, proj, x, mod, gain, w_branch, w_out, ln_g, ln_b)


def _ffn_kernel(x_ref, mod_ref, wg_ref, wu_ref, wd_ref, lng_ref, lnb_ref, out_ref, h_ref, acc_ref, *, alpha):
    f = pl.program_id(2)

    @pl.when(f == 0)
    def _():
        h_ref[...] = (_ln(x_ref[0]) * (1.0 + mod_ref[0, 4:5, :]) + mod_ref[0, 3:4, :]).astype(BF16)
        acc_ref[...] = jnp.zeros_like(acc_ref)

    h = h_ref[...]
    act = _silu(_dot(h, wg_ref[...])) * _dot(h, wu_ref[...])
    acc_ref[...] += _dot(act.astype(BF16), wd_ref[...])

    @pl.when(f == pl.num_programs(2) - 1)
    def _():
        z = alpha * x_ref[0] + mod_ref[0, 5:6, :] * acc_ref[...]
        out_ref[0] = _ln(z) * lng_ref[...] + lnb_ref[...]


def _ffn(x, mod, w_gate, w_up, w_down, ln_g, ln_b, alpha, tl, tf=512):
    b, l, d = x.shape
    f = w_gate.shape[1]
    return pl.pallas_call(
        functools.partial(_ffn_kernel, alpha=alpha),
        grid=(b, l // tl, f // tf),
        in_specs=[pl.BlockSpec((1, tl, d), lambda bi, i, j: (bi, i, 0)),
                  pl.BlockSpec((1, 6, d), lambda bi, i, j: (bi, 0, 0)),
                  pl.BlockSpec((d, tf), lambda bi, i, j: (0, j)),
                  pl.BlockSpec((d, tf), lambda bi, i, j: (0, j)),
                  pl.BlockSpec((tf, d), lambda bi, i, j: (j, 0)),
                  pl.BlockSpec((1, d), lambda bi, i, j: (0, 0)),
                  pl.BlockSpec((1, d), lambda bi, i, j: (0, 0))],
        out_specs=pl.BlockSpec((1, tl, d), lambda bi, i, j: (bi, i, 0)),
        out_shape=jax.ShapeDtypeStruct((b, l, d), F32),
        scratch_shapes=[pltpu.VMEM((tl, d), BF16), pltpu.VMEM((tl, d), F32)],
        compiler_params=_cparams(("parallel", "parallel", "arbitrary"), 48),
        name="ffn",
    )(x, mod, w_gate, w_up, w_down, ln_g, ln_b)


def _top2_combine(logits):
    lane = lax.broadcasted_iota(jnp.int32, logits.shape, 1).astype(F32)
    neg = -jnp.inf
    lg = jnp.where(lane < N_EXPERTS, logits, neg)
    m1 = jnp.max(lg, axis=1, keepdims=True)
    i1 = jnp.min(jnp.where(lg == m1, lane, float(LANE)), axis=1, keepdims=True)
    lg2 = jnp.where(lane == i1, neg, lg)
    m2 = jnp.max(lg2, axis=1, keepdims=True)
    i2 = jnp.min(jnp.where(lg2 == m2, lane, float(LANE)), axis=1, keepdims=True)
    e2 = jnp.exp(m2 - m1)
    p1 = 1.0 / (1.0 + e2)
    chosen = (lane == i1) | (lane == i2)
    return jnp.where(lane == i1, p1, jnp.where(lane == i2, e2 * p1, 0.0)), chosen


def _moe_kernel(x_ref, mod_ref, wr_ref, wgu_ref, wd_ref, lng_ref, lnb_ref, out_ref,
                h_ref, acc_ref, comb_ref, pos_ref, post_ref, col_ref, xc_ref, yc_ref, cnt_ref, *, alpha, rb):
    e = pl.program_id(2)
    f = pl.program_id(3)
    tl, d = h_ref.shape

    @pl.when((e == 0) & (f == 0))
    def _():
        h = _ln(x_ref[0]) * (1.0 + mod_ref[0, 4:5, :]) + mod_ref[0, 3:4, :]
        h_ref[...] = h.astype(BF16)
        comb, chosen = _top2_combine(_dot(h, wr_ref[...], precision=HIGHEST))
        comb_ref[...] = comb
        blk = min(tl, LANE)
        before = jnp.where(_order_mask(blk, False, strict=True), 1.0, 0.0).astype(BF16)
        routed = jnp.where(chosen, 1.0, 0.0)
        total = jnp.zeros((1, LANE), F32)
        ranks = []
        for r0 in range(0, tl, blk):
            part = routed[r0:r0 + blk]
            ranks.append(_dot(before, part.astype(BF16)) + total)
            total = total + jnp.sum(part, axis=0, keepdims=True)
        rank = jnp.concatenate(ranks, axis=0)
        pos = jnp.where(chosen, rank, -1.0)
        pos_ref[...] = pos
        pos_t = pos.T[0:N_EXPERTS, :]
        post_ref[...] = pos_t
        for ex in range(N_EXPERTS):
            cnt_ref[ex] = jnp.sum(jnp.where(pos_t[ex:ex + 1, :] >= 0.0, 1.0, 0.0)).astype(jnp.int32)
        acc_ref[...] = jnp.zeros_like(acc_ref)

    nblk = (cnt_ref[e] + (rb - 1)) // rb

    def rows(r):
        return pl.ds(pl.multiple_of(r * rb, rb), rb)

    @pl.when(f == 0)
    def _():
        lane = lax.broadcasted_iota(jnp.int32, (tl, LANE), 1)
        col_ref[0] = jnp.sum(jnp.where(lane == e, pos_ref[...], 0.0), axis=1, keepdims=True)
        col_ref[1] = jnp.sum(jnp.where(lane == e, comb_ref[...], 0.0), axis=1, keepdims=True)
        pos_row = post_ref[pl.ds(e, 1), :]

        def gather(r, carry):
            slot = (lax.broadcasted_iota(jnp.int32, (rb, tl), 0) + r * rb).astype(F32)
            sel = jnp.where(pos_row == slot, 1.0, 0.0).astype(BF16)
            xc_ref[rows(r), :] = _dot(sel, h_ref[...]).astype(BF16)
            yc_ref[rows(r), :] = jnp.zeros((rb, d), F32)
            return carry

        lax.fori_loop(0, nblk, gather, 0)

    def expert(r, carry):
        xb = xc_ref[rows(r), :]
        gu = _dot(xb, wgu_ref[0, 0])
        half = gu.shape[1] // 2
        act = _silu(gu[:, :half]) * gu[:, half:]
        yc_ref[rows(r), :] += _dot(act.astype(BF16), wd_ref[0])
        return carry

    lax.fori_loop(0, nblk, expert, 0)

    @pl.when(f == pl.num_programs(3) - 1)
    def _():
        pos_col = col_ref[0]
        weight = col_ref[1]

        def scatter(r, carry):
            slot = (lax.broadcasted_iota(jnp.int32, (tl, rb), 1) + r * rb).astype(F32)
            sel_t = jnp.where(pos_col == slot, 1.0, 0.0).astype(BF16)
            acc_ref[...] += weight * _dot(sel_t, yc_ref[rows(r), :].astype(BF16))
            return carry

        lax.fori_loop(0, nblk, scatter, 0)

    @pl.when((e == pl.num_programs(2) - 1) & (f == pl.num_programs(3) - 1))
    def _():
        z = alpha * x_ref[0] + mod_ref[0, 5:6, :] * acc_ref[...]
        out_ref[0] = _ln(z) * lng_ref[...] + lnb_ref[...]


def _pack_gate_up(w_gate, w_up):
    e, d, f = w_gate.shape
    tf = next(t for t in (896, 512, 256, 128) if f % t == 0)

    def tiles(w):
        return w.astype(BF16).reshape(e, d, f // tf, tf).transpose(0, 2, 1, 3)

    return jnp.concatenate([tiles(w_gate), tiles(w_up)], axis=-1)


def _moe(x, mod, w_router, w_gate_up, w_down, ln_g, ln_b, alpha, tl):
    b, l, d = x.shape
    ne, nf, _, tf2 = w_gate_up.shape
    rb = min(MOE_ROW_BLOCK, tl)
    compact_rows = -(-tl // rb) * rb
    return pl.pallas_call(
        functools.partial(_moe_kernel, alpha=alpha, rb=rb),
        grid=(b, l // tl, ne, nf),
        in_specs=[pl.BlockSpec((1, tl, d), lambda bi, i, e, j: (bi, i, 0)),
                  pl.BlockSpec((1, 6, d), lambda bi, i, e, j: (bi, 0, 0)),
                  pl.BlockSpec((d, LANE), lambda bi, i, e, j: (0, 0)),
                  pl.BlockSpec((1, 1, d, tf2), lambda bi, i, e, j: (e, j, 0, 0)),
                  pl.BlockSpec((1, tf2 // 2, d), lambda bi, i, e, j: (e, j, 0)),
                  pl.BlockSpec((1, d), lambda bi, i, e, j: (0, 0)),
                  pl.BlockSpec((1, d), lambda bi, i, e, j: (0, 0))],
        out_specs=pl.BlockSpec((1, tl, d), lambda bi, i, e, j: (bi, i, 0)),
        out_shape=jax.ShapeDtypeStruct((b, l, d), F32),
        scratch_shapes=[pltpu.VMEM((tl, d), BF16),
                        pltpu.VMEM((tl, d), F32),
                        pltpu.VMEM((tl, LANE), F32),
                        pltpu.VMEM((tl, LANE), F32),
                        pltpu.VMEM((N_EXPERTS, tl), F32),
                        pltpu.VMEM((2, tl, 1), F32),
                        pltpu.VMEM((compact_rows, d), BF16),
                        pltpu.VMEM((compact_rows, d), F32),
                        pltpu.SMEM((N_EXPERTS,), jnp.int32)],
        compiler_params=_cparams(("parallel", "parallel", "arbitrary", "arbitrary"), 56),
        name="moe",
    )(x, mod, w_router, w_gate_up, w_down, ln_g, ln_b)


def _reorder_w_in(w):
    d = w.shape[0]
    o_a, o_b = 0, 5 * MIX_W
    o_bgt = o_b + 4 * MIX_W
    o_cqkv = o_bgt + 16
    o_cz = o_cqkv + 3 * MIX_W
    o_cgt = o_cz + MIX_W
    o_merge = o_cgt + 16
    parts = [w[:, o_merge:o_merge + 3 * d], w[:, o_a:o_bgt], w[:, o_cqkv:o_cgt],
             w[:, o_bgt:o_bgt + 16], w[:, o_cgt:o_cgt + 16],
             jnp.zeros((d, N_PROJ - COL_GATES - 32), w.dtype)]
    return jnp.concatenate(parts, axis=1)


def _pad_row(vals, offset):
    row = jnp.zeros((1, LANE), F32)
    return lax.dynamic_update_slice(row, vals.reshape(1, -1).astype(F32), (0, offset))


def _tile(l, pref):
    return pref if l % pref == 0 else l


def kernel(x, c, ctx, c_ctx, w_ada, b_ada, w_in, conv_w, lb_raw, m_gate_bias, gdn_a_log, gdn_dt_bias, mix_norm,
           w_branch, w_out, ln_g, ln_b, ffn_w_gate, ffn_w_up, ffn_w_down, moe_router, moe_w_gate, moe_w_up,
           moe_w_down):
    depth, d_model = w_in.shape[0], w_in.shape[1]
    b, l_lat, _ = x.shape
    l_ctx = ctx.shape[1]
    alpha = float((2 * depth) ** 0.25)
    assert d_model == 1024 and w_in.shape[2] == 9 * MIX_W + 3 * MIX_W + MIX_W + 32 + 3 * d_model

    lower = _lower_bounds(lb_raw.astype(F32))
    n_rows = -(-(b + 1) // 8) * 8
    cvec = jnp.concatenate([c, c_ctx[None, :], jnp.zeros((n_rows - b - 1, d_model), F32)], axis=0)
    mods = _ada(cvec, w_ada, b_ada)

    zeros_state = jnp.zeros((b, 2, N_HEADS, HEAD_DIM, HEAD_DIM), F32)
    zeros_hgrn = jnp.zeros((b, N_HEADS, 2, HEAD_DIM, HEAD_DIM), F32)
    zeros_nm = jnp.zeros((b, 2, N_HEADS, 8, HEAD_DIM), F32)
    zeros_cmn = jnp.zeros((b, 2, N_HEADS, HEAD_DIM, 2 * HEAD_DIM), F32)

    pack = 1024 // l_ctx if (1024 % l_ctx == 0 and b % max(1024 // l_ctx, 1) == 0) else 1
    ctx_packed = (b // pack, pack * l_ctx, d_model)

    for l in range(depth):
        mod_l = mods[l, :b].reshape(b, 6, d_model)
        mod_c = jnp.broadcast_to(mods[l, b].reshape(1, 6, d_model), (b, 6, d_model))
        w_in_l = _reorder_w_in(w_in[l]).astype(BF16)
        lb_row = lower[l].reshape(1, MIX_W)
        gbias = _pad_row(m_gate_bias[l], 0)
        alog_row = _pad_row(gdn_a_log[l], 16)
        dtb_row = _pad_row(gdn_dt_bias[l], 16)
        gain = mix_norm[l]
        wbr = w_branch[l].astype(BF16)
        wo = w_out[l].astype(BF16)
        g0, b0 = ln_g[l, 0].reshape(1, -1), ln_b[l, 0].reshape(1, -1)
        g1, b1 = ln_g[l, 1].reshape(1, -1), ln_b[l, 1].reshape(1, -1)

        p_c = _in_proj(ctx.reshape(ctx_packed), mod_c[:ctx_packed[0]], w_in_l, ctx_packed[1])
        p_c = p_c.reshape(b, l_ctx, N_PROJ)
        p_l = _in_proj(x, mod_l, w_in_l, _tile(l_lat, 1024))
        qkv_c = _gdn_conv(p_c, conv_w[l], _tile(l_ctx, 256))
        qkv_l = _gdn_conv(p_l, conv_w[l], _tile(l_lat, 512))

        oa_c = _hgrn_scan(p_c, lb_row, zeros_hgrn)
        oa_l = _hgrn_scan(p_l, lb_row, oa_c[2])
        ob_c = _mlstm_scan(p_c, gbias, zeros_cmn, zeros_nm, colmajor=False)
        ob_l = _mlstm_scan(p_l, gbias, ob_c[2], ob_c[3], colmajor=True)
        oc_c = _gdn_scan(qkv_c, p_c, alog_row, dtb_row, zeros_state)
        oc_l = _gdn_scan(qkv_l, p_l, alog_row, dtb_row, oc_c[2])

        def channel_mixer(t, mod, tl):
            i = l // 2
            if l % 2 == 0:
                return _ffn(t, mod, ffn_w_gate[i].astype(BF16), ffn_w_up[i].astype(BF16),
                            ffn_w_down[i].astype(BF16), g1, b1, alpha, tl)
            w_r = jnp.concatenate([moe_router[i], jnp.zeros((d_model, LANE - N_EXPERTS), F32)], axis=1)
            return _moe(t, mod, w_r, _pack_gate_up(moe_w_gate[i], moe_w_up[i]), moe_w_down[i].astype(BF16),
                        g1, b1, alpha, tl)

        outs_l = (oa_l[0], oa_l[1], ob_l[0], ob_l[1], oc_l[0], oc_l[1])
        x = _merge(outs_l, p_l, x, mod_l, gain, wbr, wo, g0, b0, alpha, _tile(l_lat, 256))
        x = channel_mixer(x, mod_l, _tile(l_lat, 1024))
        if l < depth - 1:
            outs_c = (oa_c[0], oa_c[1], ob_c[0], ob_c[1], oc_c[0], oc_c[1])
            ctx = _merge(outs_c, p_c, ctx, mod_c, gain, wbr, wo, g0, b0, alpha, _tile(l_ctx, 256))
            ctx = channel_mixer(ctx.reshape(ctx_packed), mod_c[:ctx_packed[0]], ctx_packed[1]).reshape(ctx.shape)
    return x
```

```python
import functools

import jax
import jax.numpy as jnp
from jax import lax
from jax.experimental import pallas as pl
from jax.experimental.pallas import tpu as pltpu

F32 = jnp.float32
BF16 = jnp.bfloat16
HIGHEST = lax.Precision.HIGHEST

HEAD_DIM = 128
N_HEADS = 4
MIX_W = N_HEADS * HEAD_DIM
CHUNK = 64
GRID_W = 64
SUB = 16
MLSTM_GROUP = 2
GDN_BATCH_ROWS = 2
HGRN_GROUP = 8
HGRN_BATCH_ROWS = 2
N_EXPERTS = 8
MOE_PHASE_ROWS = 512
MOE_VMEM_MB = 60
LN_EPS = 1e-5
RMS_EPS = 1e-6
L2_EPS = 1e-6
MASK_NEG = -1e30
LOG2_E = 1.4426950408889634
LANE = 128

COL_MERGE = 0
COL_A = 3 * 1024
COL_B = COL_A + 5 * MIX_W
COL_CQKV = COL_B + 4 * MIX_W
COL_CZ = COL_CQKV + 3 * MIX_W
COL_GATES = COL_CZ + MIX_W
N_PROJ = COL_GATES + 4 * LANE
PROJ_TILE = N_PROJ // 5


def _cparams(sem, vmem_mb=None):
    kw = dict(dimension_semantics=sem)
    if vmem_mb is not None:
        kw["vmem_limit_bytes"] = vmem_mb << 20
    return pltpu.CompilerParams(**kw)


def _dot(a, b, precision=None):
    return jnp.dot(a, b, preferred_element_type=F32, precision=precision)


def _dot_nt(a, b, precision=None):
    return lax.dot_general(a, b, (((1,), (1,)), ((), ())), preferred_element_type=F32, precision=precision)


def _dot_tn(a, b):
    return lax.dot_general(a, b, (((0,), (0,)), ((), ())), preferred_element_type=F32)


def _ln(x):
    mu = jnp.mean(x, axis=-1, keepdims=True)
    xc = x - mu
    var = jnp.mean(xc * xc, axis=-1, keepdims=True)
    return xc * lax.rsqrt(var + LN_EPS)


def _sigmoid(x):
    return 1.0 / (1.0 + jnp.exp(-x))


def _silu(x):
    return x * _sigmoid(x)


def _softplus(x):
    return jnp.maximum(x, 0.0) + jnp.log(1.0 + jnp.exp(-jnp.abs(x)))


def _log_sigmoid(x):
    return -_softplus(-x)


def _order_mask(n, rev, strict=False):
    r = lax.broadcasted_iota(jnp.int32, (n, n), 0)
    c = lax.broadcasted_iota(jnp.int32, (n, n), 1)
    if rev:
        return (c > r) if strict else (c >= r)
    return (c < r) if strict else (c <= r)


def _cumsum_rows(x, rev):
    m = _order_mask(x.shape[0], rev).astype(F32)
    return _dot(m, x, precision=HIGHEST)


def _ada_kernel(c_ref, w_ref, b_ref, o_ref):
    s = _silu(c_ref[...])
    o_ref[0] = _dot(s, w_ref[0], precision=HIGHEST) + b_ref[0]


def _ada(cvec, w_ada, b_ada, tn=1536):
    depth, d, n6 = w_ada.shape
    rows = cvec.shape[0]
    return pl.pallas_call(
        _ada_kernel,
        grid=(depth, n6 // tn),
        in_specs=[pl.BlockSpec((rows, d), lambda l, n: (0, 0)),
                  pl.BlockSpec((1, d, tn), lambda l, n: (l, 0, n)),
                  pl.BlockSpec((1, 1, tn), lambda l, n: (l, 0, n))],
        out_specs=pl.BlockSpec((1, rows, tn), lambda l, n: (l, 0, n)),
        out_shape=jax.ShapeDtypeStruct((depth, rows, n6), F32),
        compiler_params=_cparams(("parallel", "parallel"), 40),
        name="ada",
    )(cvec, w_ada, b_ada.reshape(depth, 1, n6))


def _lower_bounds_kernel(lb_ref, o_ref):
    x = lb_ref[...]
    depth = x.shape[0]
    mx = jnp.max(x, axis=0, keepdims=True)
    e = jnp.exp(x - mx)
    p = e / jnp.sum(e, axis=0, keepdims=True)
    acc = jnp.zeros_like(p[0:1])
    for l in range(depth):
        o_ref[l:l + 1, :] = acc
        if l + 1 < depth:
            acc = acc + p[l + 1:l + 2]


def _lower_bounds(lb_raw):
    return pl.pallas_call(
        _lower_bounds_kernel,
        out_shape=jax.ShapeDtypeStruct(lb_raw.shape, F32),
        name="lower_bounds",
    )(lb_raw)


def _in_proj_kernel(x_ref, mod_ref, w_ref, o_ref, h_ref):
    @pl.when(pl.program_id(2) == 0)
    def _():
        shift = mod_ref[0, 0:1, :]
        scale = mod_ref[0, 1:2, :]
        h_ref[...] = (_ln(x_ref[0]) * (1.0 + scale) + shift).astype(BF16)

    o_ref[0] = _dot(h_ref[...], w_ref[...])


def _in_proj(x, mod, w, tl, tn=PROJ_TILE):
    b, l, d = x.shape
    n_proj = w.shape[1]
    return pl.pallas_call(
        _in_proj_kernel,
        grid=(b, l // tl, n_proj // tn),
        in_specs=[pl.BlockSpec((1, tl, d), lambda bi, i, n: (bi, i, 0)),
                  pl.BlockSpec((1, 6, d), lambda bi, i, n: (bi, 0, 0)),
                  pl.BlockSpec((d, tn), lambda bi, i, n: (0, n))],
        out_specs=pl.BlockSpec((1, tl, tn), lambda bi, i, n: (bi, i, n)),
        out_shape=jax.ShapeDtypeStruct((b, l, n_proj), F32),
        scratch_shapes=[pltpu.VMEM((tl, d), BF16)],
        compiler_params=_cparams(("parallel", "parallel", "arbitrary"), 48),
        name="in_proj",
    )(x, mod, w)


def _conv_kernel(x_ref, prev_ref, next_ref, w_ref, o_ref):
    i = pl.program_id(1)
    last = pl.num_programs(1) - 1
    x = x_ref[0]
    tl = x.shape[0]
    row = lax.broadcasted_iota(jnp.int32, x.shape, 0)
    before = jnp.where(i > 0, prev_ref[0, 7:8, :], 0.0)
    after = jnp.where(i < last, next_ref[0, 0:1, :], 0.0)
    x_dn = jnp.where(row == 0, before, pltpu.roll(x, 1, axis=0))
    x_up = jnp.where(row == tl - 1, after, pltpu.roll(x, tl - 1, axis=0))
    y = _silu(x_dn * w_ref[0:1, :] + x * w_ref[1:2, :] + x_up * w_ref[2:3, :])
    for j in range(3 * N_HEADS):
        t = y[:, j * HEAD_DIM:(j + 1) * HEAD_DIM]
        if j < 2 * N_HEADS:
            t = t * lax.rsqrt(jnp.sum(t * t, axis=-1, keepdims=True) + L2_EPS)
            if j < N_HEADS:
                t = t * HEAD_DIM ** -0.5
        o_ref[0, :, j * HEAD_DIM:(j + 1) * HEAD_DIM] = t


def _gdn_conv(proj, conv_w, tl):
    b, l, _ = proj.shape
    w3 = 3 * MIX_W
    cb = COL_CQKV // w3
    nb8 = l // 8
    r8 = tl // 8
    return pl.pallas_call(
        _conv_kernel,
        grid=(b, l // tl),
        in_specs=[pl.BlockSpec((1, tl, w3), lambda bi, i: (bi, i, cb)),
                  pl.BlockSpec((1, 8, w3), lambda bi, i: (bi, jnp.maximum(i * r8 - 1, 0), cb)),
                  pl.BlockSpec((1, 8, w3), lambda bi, i: (bi, jnp.minimum((i + 1) * r8, nb8 - 1), cb)),
                  pl.BlockSpec((3, w3), lambda bi, i: (0, 0))],
        out_specs=pl.BlockSpec((1, tl, w3), lambda bi, i: (bi, i, 0)),
        out_shape=jax.ShapeDtypeStruct((b, l, w3), F32),
        compiler_params=_cparams(("parallel", "parallel"), 40),
        name="gdn_conv",
    )(proj, proj, proj, conv_w)


def _hgrn_chunks(probs, states, emat):
    c = probs[0]["q"].shape[0]
    nb = c // SUB
    for p in probs:
        lb = p["lb"]
        sig = _sigmoid(p["fpre"])
        p["k"] = (1.0 - lb) * (1.0 - sig)
        p["g"] = _cumsum_rows(jnp.log(lb + (1.0 - lb) * sig), p["rev"])

    walls = []
    for p in probs:
        q = p["q"]
        g2 = p["g"] * LOG2_E
        lk2 = jnp.log(p["k"]) * LOG2_E
        rel = lk2 - g2
        rows = []
        half = SUB // 2
        dead = jnp.zeros((half, HEAD_DIM), F32)
        for j in range(nb):
            r0 = j * SUB
            pieces = []
            for s in range(SUB):
                skip = 0 if (s >= half and not p["rev"]) else 1 if (s < half and p["rev"]) else None
                parts = []
                for part in range(2):
                    if part == skip:
                        parts.append(dead)
                        continue
                    t0 = r0 + part * half
                    e2 = jnp.exp2(jnp.minimum(g2[t0:t0 + half] + rel[r0 + s:r0 + s + 1], lk2[r0 + s:r0 + s + 1]))
                    parts.append(e2 * q[t0:t0 + half])
                pieces.append(jnp.concatenate(parts, axis=0).astype(BF16))
            rows.append(jnp.concatenate(pieces, axis=1))
        walls.append(jnp.concatenate(rows, axis=0))
    a_diags = [_dot(w, emat) for w in walls]

    a_offs = []
    for p in probs:
        q, k, g, rev = p["q"], p["k"], p["g"], p["rev"]
        offs = []
        for i in range(nb):
            r0 = i * SUB
            if (not rev and i == 0) or (rev and i == nb - 1):
                offs.append(jnp.zeros((SUB, c), F32))
                continue
            bi = g[r0 + SUB:r0 + SUB + 1] if rev else g[r0 - 1:r0]
            lhs = (q[r0:r0 + SUB] * jnp.exp(g[r0:r0 + SUB] - bi)).astype(BF16)
            rhs = (k * jnp.exp(jnp.minimum(bi - g, 0.0))).astype(BF16)
            offs.append(_dot_nt(lhs, rhs))
        a_offs.append(jnp.concatenate(offs, axis=0))

    r = lax.broadcasted_iota(jnp.int32, (c, c), 0)
    col = lax.broadcasted_iota(jnp.int32, (c, c), 1)
    shift = SUB.bit_length() - 1
    rb = lax.shift_right_logical(r, shift)
    cb = lax.shift_right_logical(col, shift)
    inters = [_dot_nt((p["q"] * jnp.exp(p["g"])).astype(BF16), s_t.astype(BF16)) for p, s_t in zip(probs, states)]
    outs, new_states = [], []
    for p, a_diag, a_off, inter, s_t in zip(probs, a_diags, a_offs, inters, states):
        rev, g = p["rev"], p["g"]
        order = (col >= r) if rev else (col <= r)
        off_side = (cb > rb) if rev else (cb < rb)
        a = jnp.where((rb == cb) & order, a_diag, jnp.where(off_side, a_off, 0.0))
        g_end = g[0:1] if rev else g[c - 1:c]
        vb = p["v"].astype(BF16)
        outs.append(_dot(a.astype(BF16), vb) + inter)
        kd = (p["k"] * jnp.exp(g_end - g)).astype(BF16)
        new_states.append(s_t * jnp.exp(g_end) + _dot_tn(vb, kd))
    return outs, new_states


def _hgrn_kernel(qf_ref, ff_ref, vf_ref, qb_ref, fb_ref, vb_ref, lb_ref, e_ref, s0_ref,
                 of_ref, ob_ref, sout_ref, s_ref):
    n = pl.program_id(1)

    @pl.when(n == 0)
    def _():
        s_ref[...] = s0_ref[...]

    probs, where = [], []
    for r in range(qf_ref.shape[0]):
        for d, (q_ref, f_ref, v_ref, o_ref) in enumerate(((qf_ref, ff_ref, vf_ref, of_ref),
                                                          (qb_ref, fb_ref, vb_ref, ob_ref))):
            for h in range(N_HEADS):
                sl = slice(h * HEAD_DIM, (h + 1) * HEAD_DIM)
                probs.append(dict(q=q_ref[r, :, sl], fpre=f_ref[r, :, sl], v=v_ref[r, :, sl], lb=lb_ref[:, sl],
                                  rev=d == 1))
                where.append((o_ref, sl, r, h, d))
    states = [s_ref[r, h, d] for _, _, r, h, d in where]
    outs, new_states = [], []
    for g0 in range(0, len(probs), HGRN_GROUP):
        o_g, s_g = _hgrn_chunks(probs[g0:g0 + HGRN_GROUP], states[g0:g0 + HGRN_GROUP], e_ref[...])
        outs += o_g
        new_states += s_g
    for (o_ref, sl, r, h, d), o, s_new in zip(where, outs, new_states):
        o_ref[r, :, sl] = o
        s_ref[r, h, d] = s_new

    @pl.when(n == pl.num_programs(1) - 1)
    def _():
        sout_ref[...] = s_ref[...]


def _hgrn_emat():
    r = jnp.arange(SUB * HEAD_DIM)[:, None] // HEAD_DIM
    c = jnp.arange(CHUNK)[None, :] % SUB
    return (r == c).astype(BF16)


def _hgrn_scan(proj, lb, s0):
    b, l, _ = proj.shape
    nc = l // CHUNK
    rows = HGRN_BATCH_ROWS if b % HGRN_BATCH_ROWS == 0 else 1
    ca = COL_A // MIX_W

    def fwd(col):
        return pl.BlockSpec((rows, CHUNK, MIX_W), lambda bi, n: (bi, n, ca + col))

    def bwd(col):
        return pl.BlockSpec((rows, CHUNK, MIX_W), lambda bi, n: (bi, nc - 1 - n, ca + col))

    st_spec = pl.BlockSpec((rows, N_HEADS, 2, HEAD_DIM, HEAD_DIM), lambda bi, n: (bi, 0, 0, 0, 0))
    return pl.pallas_call(
        _hgrn_kernel,
        grid=(b // rows, nc),
        in_specs=[fwd(0), fwd(1), fwd(3), bwd(0), bwd(2), bwd(3),
                  pl.BlockSpec((1, MIX_W), lambda bi, n: (0, 0)),
                  pl.BlockSpec((SUB * HEAD_DIM, CHUNK), lambda bi, n: (0, 0)),
                  st_spec],
        out_specs=[pl.BlockSpec((rows, CHUNK, MIX_W), lambda bi, n: (bi, n, 0)),
                   pl.BlockSpec((rows, CHUNK, MIX_W), lambda bi, n: (bi, nc - 1 - n, 0)),
                   st_spec],
        out_shape=[jax.ShapeDtypeStruct((b, l, MIX_W), F32),
                   jax.ShapeDtypeStruct((b, l, MIX_W), F32),
                   jax.ShapeDtypeStruct(s0.shape, F32)],
        scratch_shapes=[pltpu.VMEM((rows, N_HEADS, 2, HEAD_DIM, HEAD_DIM), F32)],
        compiler_params=_cparams(("parallel", "arbitrary"), 40),
        name="hgrn_scan",
    )(proj, proj, proj, proj, proj, proj, lb, _hgrn_emat(), s0)


def _mlstm_chunks(probs, states):
    c = probs[0]["q"].shape[0]
    dv = probs[0]["v"].shape[1]
    ones = jnp.ones((c, dv), BF16)
    for p, (cmn, m) in zip(probs, states):
        tri = _order_mask(c, p["rev"])
        log_d = jnp.where(tri, p["bb_c"] - p["bb_r"] + p["ic_r"], MASK_NEG)
        log_inter = p["bb_c"] + m
        m_t = jnp.maximum(jnp.max(log_d, axis=1, keepdims=True), log_inter)
        p["m_t"] = m_t
        p["dmat"] = jnp.where(tri, jnp.exp(log_d - m_t), 0.0)
        p["inter"] = jnp.exp(log_inter - m_t)
        p["qb"], p["kb"] = p["q"].astype(BF16), p["k"].astype(BF16)
        p["v1"] = jnp.concatenate([p["v"].astype(BF16), ones], axis=1)
    scores = [_dot_nt(p["qb"], p["kb"]) * p["dmat"] for p in probs]
    carried = [_dot(p["qb"], cmn.astype(BF16)) for p, (cmn, _) in zip(probs, states)]
    local = [_dot(s.astype(BF16), p["v1"]) for p, s in zip(probs, scores)]
    outs, new_states = [], []
    for p, car, loc, (cmn, m) in zip(probs, carried, local, states):
        m_t = p["m_t"]
        both = loc + p["inter"] * car
        outs.append(both[:, :dv] / jnp.maximum(jnp.abs(both[:, dv:]), jnp.exp(-m_t)))
        m_new = m_t[0:1] if p["rev"] else m_t[c - 1:c]
        bb_end = p["bb_c"][0:1] if p["rev"] else p["bb_c"][c - 1:c]
        decay = jnp.exp(bb_end + m - m_new)
        kw = p["k"] * jnp.exp(bb_end - p["bb_c"] + p["ic_c"] - m_new)
        new_states.append((decay * cmn + _dot_tn(kw.astype(BF16), p["v1"]), m_new))
    return outs, new_states


def _column_view(ref):
    rows, group, width = ref.shape[-3:]
    return ref.reshape(rows * group, width), group


def _chunk_rows(j):
    start = j * CHUNK
    return pl.ds(start if isinstance(start, int) else pl.multiple_of(start, CHUNK), CHUNK)


def _load_chunk(ref, j, colmajor):
    if colmajor:
        flat, group = _column_view(ref)
        return flat[pl.ds(j, CHUNK, stride=group), :]
    return ref[0, _chunk_rows(j), :]


def _store_chunk(ref, j, val, colmajor):
    if colmajor:
        flat, group = _column_view(ref)
        flat[pl.ds(j, CHUNK, stride=group), :] = val
    else:
        ref[0, _chunk_rows(j), :] = val


def _columns_as_rows(x, first):
    row = lax.broadcasted_iota(jnp.int32, (8, LANE), 0)
    lane = lax.broadcasted_iota(jnp.int32, (8, LANE), 1)
    sel = jnp.where(lane == row + first, 1.0, 0.0).astype(BF16)
    hi = x.astype(BF16)
    rest = x - hi.astype(F32)
    mid = rest.astype(BF16)
    lo = (rest - mid.astype(F32)).astype(BF16)
    return _dot_nt(sel, hi) + (_dot_nt(sel, mid) + _dot_nt(sel, lo))


def _mlstm_kernel(*refs, colmajor, chunks):
    per_dir = 3 * N_HEADS + 1
    fwd, bwd = refs[:per_dir], refs[per_dir:2 * per_dir]
    bias_ref, cm0_ref, nm0_ref, of_ref, ob_ref, cmout_ref, nmout_ref, cm_ref, nm_ref = refs[2 * per_dir:]
    n = pl.program_id(1)

    @pl.when(n == 0)
    def _():
        cm_ref[...] = cm0_ref[0]
        nm_ref[...] = nm0_ref[0]

    def step(i, carry):
        probs, where = [], []
        for d, (in_refs, o_ref) in enumerate(((fwd, of_ref), (bwd, ob_ref))):
            rev = d == 1
            j = chunks - 1 - i if rev else i
            gates = _load_chunk(in_refs[-1], j, colmajor) + bias_ref[...]
            cs = _cumsum_rows(_log_sigmoid(gates), rev)
            cs_rows = _columns_as_rows(cs, 8 + 4 * d)
            gate_rows = _columns_as_rows(gates, 4 * d)
            for h in range(N_HEADS):
                ci = 4 * d + h
                cf = 8 + 4 * d + h
                probs.append(dict(
                    q=_load_chunk(in_refs[h], j, colmajor),
                    k=_load_chunk(in_refs[N_HEADS + h], j, colmajor) * HEAD_DIM ** -0.5,
                    v=_load_chunk(in_refs[2 * N_HEADS + h], j, colmajor),
                    bb_c=cs[:, cf:cf + 1], bb_r=cs_rows[h:h + 1, :],
                    ic_c=gates[:, ci:ci + 1], ic_r=gate_rows[h:h + 1, :], rev=rev))
                where.append((o_ref, j, d, h))
        states = [(cm_ref[d, h], nm_ref[d, h, 0:1, 0:1]) for _, _, d, h in where]
        outs, new_states = [], []
        for g0 in range(0, len(probs), MLSTM_GROUP):
            o_g, s_g = _mlstm_chunks(probs[g0:g0 + MLSTM_GROUP], states[g0:g0 + MLSTM_GROUP])
            outs += o_g
            new_states += s_g
        for (o_ref, j, d, h), o, (cmn_new, m_new) in zip(where, outs, new_states):
            _store_chunk(o_ref.at[h], j, o, colmajor)
            cm_ref[d, h] = cmn_new
            nm_ref[d, h, 0:1, :] = jnp.broadcast_to(m_new, (1, HEAD_DIM))
        return carry

    lax.fori_loop(0, chunks, step, 0)

    @pl.when(n == pl.num_programs(1) - 1)
    def _():
        cmout_ref[0] = cm_ref[...]
        nmout_ref[0] = nm_ref[...]


def _mlstm_scan(proj, gbias, cm0, nm0, colmajor):
    b, l, n_proj = proj.shape
    cb = COL_B // LANE
    cg = COL_GATES // LANE
    if colmajor:
        chunks = 8
        assert l == CHUNK * GRID_W and GRID_W % chunks == 0
        nb = GRID_W // chunks
        src = proj.reshape(b, CHUNK, GRID_W, n_proj)
        in_blk = (1, CHUNK, chunks, LANE)
        out_blk = (N_HEADS, 1, CHUNK, chunks, LANE)
        out_view = (N_HEADS, b, CHUNK, GRID_W, LANE)

        def in_map(col, flip):
            return lambda bi, n: (bi, 0, nb - 1 - n if flip else n, col)

        def out_map(flip):
            return lambda bi, n: (0, bi, 0, nb - 1 - n if flip else n, 0)
    else:
        chunks = min(l // CHUNK, 4)
        nb = l // (chunks * CHUNK)
        src = proj
        in_blk = (1, chunks * CHUNK, LANE)
        out_blk = (N_HEADS, 1, chunks * CHUNK, LANE)
        out_view = (N_HEADS, b, l, LANE)

        def in_map(col, flip):
            return lambda bi, n: (bi, nb - 1 - n if flip else n, col)

        def out_map(flip):
            return lambda bi, n: (0, bi, nb - 1 - n if flip else n, 0)

    def specs(flip):
        cols = [cb + j for j in range(3 * N_HEADS)] + [cg]
        return [pl.BlockSpec(in_blk, in_map(col, flip)) for col in cols]

    cm_spec = pl.BlockSpec((1, 2, N_HEADS, HEAD_DIM, 2 * HEAD_DIM), lambda bi, n: (bi, 0, 0, 0, 0))
    nm_spec = pl.BlockSpec((1, 2, N_HEADS, 8, HEAD_DIM), lambda bi, n: (bi, 0, 0, 0, 0))
    n_in = 2 * (3 * N_HEADS + 1)
    o_f, o_b, cm, nm = pl.pallas_call(
        functools.partial(_mlstm_kernel, colmajor=colmajor, chunks=chunks),
        grid=(b, nb),
        in_specs=specs(False) + specs(True) + [pl.BlockSpec((1, LANE), lambda bi, n: (0, 0)), cm_spec, nm_spec],
        out_specs=[pl.BlockSpec(out_blk, out_map(False)), pl.BlockSpec(out_blk, out_map(True)), cm_spec, nm_spec],
        out_shape=[jax.ShapeDtypeStruct(out_view, F32), jax.ShapeDtypeStruct(out_view, F32),
                   jax.ShapeDtypeStruct(cm0.shape, F32), jax.ShapeDtypeStruct(nm0.shape, F32)],
        scratch_shapes=[pltpu.VMEM((2, N_HEADS, HEAD_DIM, 2 * HEAD_DIM), F32),
                        pltpu.VMEM((2, N_HEADS, 8, HEAD_DIM), F32)],
        compiler_params=_cparams(("parallel", "arbitrary"), 40),
        name="mlstm_scan",
    )(*([src] * n_in), gbias, cm0, nm0)
    head_major = (N_HEADS, b, l, LANE)
    return o_f.reshape(head_major), o_b.reshape(head_major), cm, nm


def _split_bf16(x):
    hi = x.astype(BF16)
    return hi, (x - hi.astype(F32)).astype(BF16)


def _dot3(a, b, nt=False):
    f = _dot_nt if nt else _dot
    return f(a[0], b[0]) + (f(a[0], b[1]) + f(a[1], b[0]))


def _inv_unit_triangular(a_list):
    n = a_list[0].shape[0]
    eye = (lax.broadcasted_iota(jnp.int32, (n, n), 0) == lax.broadcasted_iota(jnp.int32, (n, n), 1)).astype(F32)
    xs = [eye - a for a in a_list]
    splits = [_split_bf16(a) for a in a_list]
    ps = [_dot3(s, s) for s in splits]
    steps = (n - 1).bit_length() - 1
    for step in range(steps):
        last = step == steps - 1
        p_split = [_split_bf16(p) for p in ps]
        lhs = xs if last else [jnp.concatenate([x, p], axis=0) for x, p in zip(xs, ps)]
        prods = [_dot3(_split_bf16(l), s) for l, s in zip(lhs, p_split)]
        xs = [x + pr[:n] for x, pr in zip(xs, prods)]
        if not last:
            ps = [pr[n:] for pr in prods]
    return xs


def _gdn_chunks(probs, states):
    c = probs[0]["q"].shape[0]
    a_list, ldecs = [], []
    for p in probs:
        tri = _order_mask(c, p["rev"])
        strict = _order_mask(c, p["rev"], strict=True)
        ldec = jnp.where(tri, jnp.exp(jnp.minimum(p["gam_c"] - p["gam_r"], 0.0)), 0.0)
        p["kbeta"] = p["k"] * p["beta"]
        kk = _dot_nt(p["kbeta"].astype(BF16), p["k"].astype(BF16))
        a_list.append(jnp.where(strict, kk * ldec, 0.0))
        ldecs.append(ldec)
    t_invs = _inv_unit_triangular(a_list)
    sols = []
    for p, t_inv in zip(probs, t_invs):
        p["eg"] = jnp.exp(p["gam_c"])
        rhs = jnp.concatenate([p["v"] * p["beta"], p["kbeta"] * p["eg"]], axis=1)
        sols.append(_dot3(_split_bf16(t_inv), _split_bf16(rhs)))
    qks = [_dot_nt(p["q"].astype(BF16), p["k"].astype(BF16)) * ldec for p, ldec in zip(probs, ldecs)]
    inter = [_dot(jnp.concatenate([sol[:, HEAD_DIM:], p["q"] * p["eg"]], axis=0).astype(BF16), s.astype(BF16))
             for p, sol, s in zip(probs, sols, states)]
    outs, new_states = [], []
    for p, sol, qk, it, s in zip(probs, sols, qks, inter, states):
        g_end = p["gam_c"][0:1] if p["rev"] else p["gam_c"][c - 1:c]
        v_new = (sol[:, :HEAD_DIM] - it[:c]).astype(BF16)
        kg = p["k"] * jnp.exp(g_end - p["gam_c"])
        outs.append(it[c:] + _dot(qk.astype(BF16), v_new))
        new_states.append(jnp.exp(g_end) * s + _dot_tn(kg.astype(BF16), v_new))
    return outs, new_states


def _gdn_kernel(xf_ref, gf_ref, xb_ref, gb_ref, alog_ref, dtb_ref, s0_ref,
                of_ref, ob_ref, sout_ref, s_ref):
    n = pl.program_id(1)

    @pl.when(n == 0)
    def _():
        s_ref[...] = s0_ref[...]

    neg_a = -jnp.exp(alog_ref[...])
    probs, where = [], []
    for r in range(xf_ref.shape[0]):
        for d, (x_ref, g_ref, o_ref) in enumerate(((xf_ref, gf_ref, of_ref), (xb_ref, gb_ref, ob_ref))):
            rev = d == 1
            gates = g_ref[r]
            cs = _cumsum_rows(neg_a * _softplus(gates + dtb_ref[...]), rev)
            cs_t = cs.T
            betas = _sigmoid(gates)
            for h in range(N_HEADS):
                ca = 16 + 4 * d + h
                cbeta = 24 + 4 * d + h
                probs.append(dict(
                    q=x_ref[r, :, h * HEAD_DIM:(h + 1) * HEAD_DIM],
                    k=x_ref[r, :, MIX_W + h * HEAD_DIM:MIX_W + (h + 1) * HEAD_DIM],
                    v=x_ref[r, :, 2 * MIX_W + h * HEAD_DIM:2 * MIX_W + (h + 1) * HEAD_DIM],
                    gam_c=cs[:, ca:ca + 1], gam_r=cs_t[ca:ca + 1, :], beta=betas[:, cbeta:cbeta + 1], rev=rev))
                where.append((o_ref, r, d, h))
    outs, new_states = _gdn_chunks(probs, [s_ref[r, d, h] for _, r, d, h in where])
    for (o_ref, r, d, h), o, s_new in zip(where, outs, new_states):
        o_ref[r, :, h * HEAD_DIM:(h + 1) * HEAD_DIM] = o
        s_ref[r, d, h] = s_new

    @pl.when(n == pl.num_programs(1) - 1)
    def _():
        sout_ref[...] = s_ref[...]


def _gdn_scan(qkv, proj, alog_row, dtb_row, s0):
    b, l, w3 = qkv.shape
    nc = l // CHUNK
    cg = COL_GATES // LANE
    rows = GDN_BATCH_ROWS if b % GDN_BATCH_ROWS == 0 else 1
    st_spec = pl.BlockSpec((rows, 2, N_HEADS, HEAD_DIM, HEAD_DIM), lambda bi, n: (bi, 0, 0, 0, 0))
    row_spec = pl.BlockSpec((1, LANE), lambda bi, n: (0, 0))
    return pl.pallas_call(
        _gdn_kernel,
        grid=(b // rows, nc),
        in_specs=[pl.BlockSpec((rows, CHUNK, w3), lambda bi, n: (bi, n, 0)),
                  pl.BlockSpec((rows, CHUNK, LANE), lambda bi, n: (bi, n, cg)),
                  pl.BlockSpec((rows, CHUNK, w3), lambda bi, n: (bi, nc - 1 - n, 0)),
                  pl.BlockSpec((rows, CHUNK, LANE), lambda bi, n: (bi, nc - 1 - n, cg)),
                  row_spec, row_spec, st_spec],
        out_specs=[pl.BlockSpec((rows, CHUNK, MIX_W), lambda bi, n: (bi, n, 0)),
                   pl.BlockSpec((rows, CHUNK, MIX_W), lambda bi, n: (bi, nc - 1 - n, 0)),
                   st_spec],
        out_shape=[jax.ShapeDtypeStruct((b, l, MIX_W), F32), jax.ShapeDtypeStruct((b, l, MIX_W), F32),
                   jax.ShapeDtypeStruct(s0.shape, F32)],
        scratch_shapes=[pltpu.VMEM((rows, 2, N_HEADS, HEAD_DIM, HEAD_DIM), F32)],
        compiler_params=_cparams(("parallel", "arbitrary"), 40),
        name="gdn_scan",
    )(qkv, proj, qkv, proj, alog_row, dtb_row, s0)


def _merge_kernel(oaf, oab, obf, obb, ocf, ocb, ag, bo, cz, mg, x_ref, mod_ref, gain_ref, wbr_ref, wout_ref,
                  lng_ref, lnb_ref, out_ref, *, alpha):
    d_model = x_ref.shape[-1]
    branches = ((oaf, oab, ag, _silu), (obf, obb, bo, _sigmoid), (ocf, ocb, cz, _silu))
    mixed = None
    for nbr, (of_ref, ob_ref, gate_ref, act) in enumerate(branches):
        parts = []
        for h in range(N_HEADS):
            if len(of_ref.shape) == 4:
                oh = of_ref[h, 0] + ob_ref[h, 0]
            else:
                oh = of_ref[0, :, h * HEAD_DIM:(h + 1) * HEAD_DIM] + ob_ref[0, :, h * HEAD_DIM:(h + 1) * HEAD_DIM]
            parts.append(oh * lax.rsqrt(jnp.mean(oh * oh, axis=-1, keepdims=True) + RMS_EPS))
        y = jnp.concatenate(parts, axis=1) * gain_ref[nbr:nbr + 1, :] * act(gate_ref[0])
        term = _sigmoid(mg[0, :, nbr * d_model:(nbr + 1) * d_model]) * _dot(y.astype(BF16), wbr_ref[nbr])
        mixed = term if mixed is None else mixed + term
    sub = _dot(mixed.astype(BF16), wout_ref[...])
    z = alpha * x_ref[0] + mod_ref[0, 2:3, :] * sub
    out_ref[0] = _ln(z) * lng_ref[...] + lnb_ref[...]


def _merge(outs, proj, x, mod, gain, w_branch, w_out, ln_g, ln_b, alpha, tl):
    b, l, d = x.shape
    u = MIX_W

    def tok(width, col):
        return pl.BlockSpec((1, tl, width), lambda bi, i: (bi, i, col))

    def const(shape):
        return pl.BlockSpec(shape, lambda bi, i: (0,) * len(shape))

    heads = pl.BlockSpec((N_HEADS, 1, tl, LANE), lambda bi, i: (0, bi, i, 0))
    return pl.pallas_call(
        functools.partial(_merge_kernel, alpha=alpha),
        grid=(b, l // tl),
        in_specs=[tok(u, 0), tok(u, 0), heads, heads, tok(u, 0), tok(u, 0),
            tok(u, (COL_A + 4 * MIX_W) // u), tok(u, (COL_B + 3 * MIX_W) // u), tok(u, COL_CZ // u),
            tok(3 * d, COL_MERGE // (3 * d)), tok(d, 0),
            pl.BlockSpec((1, 6, d), lambda bi, i: (bi, 0, 0)),
            const((3, u)), const((3, u, d)), const((d, d)), const((1, d)), const((1, d))],
        out_specs=tok(d, 0),
        out_shape=jax.ShapeDtypeStruct((b, l, d), F32),
        compiler_params=_cparams(("parallel", "parallel"), 48),
        name="merge",
    )(*outs, proj, proj, proj, proj, x, mod, gain, w_branch, w_out, ln_g, ln_b)


def _ffn_kernel(x_ref, mod_ref, wg_ref, wu_ref, wd_ref, lng_ref, lnb_ref, out_ref, h_ref, acc_ref, *, alpha):
    f = pl.program_id(2)

    @pl.when(f == 0)
    def _():
        h_ref[...] = (_ln(x_ref[0]) * (1.0 + mod_ref[0, 4:5, :]) + mod_ref[0, 3:4, :]).astype(BF16)
        acc_ref[...] = jnp.zeros_like(acc_ref)

    h = h_ref[...]
    act = _silu(_dot(h, wg_ref[...])) * _dot(h, wu_ref[...])
    acc_ref[...] += _dot(act.astype(BF16), wd_ref[...])

    @pl.when(f == pl.num_programs(2) - 1)
    def _():
        z = alpha * x_ref[0] + mod_ref[0, 5:6, :] * acc_ref[...]
        out_ref[0] = _ln(z) * lng_ref[...] + lnb_ref[...]


def _ffn(x, mod, w_gate, w_up, w_down, ln_g, ln_b, alpha, tl, tf=512):
    b, l, d = x.shape
    f = w_gate.shape[1]
    return pl.pallas_call(
        functools.partial(_ffn_kernel, alpha=alpha),
        grid=(b, l // tl, f // tf),
        in_specs=[pl.BlockSpec((1, tl, d), lambda bi, i, j: (bi, i, 0)),
                  pl.BlockSpec((1, 6, d), lambda bi, i, j: (bi, 0, 0)),
                  pl.BlockSpec((d, tf), lambda bi, i, j: (0, j)),
                  pl.BlockSpec((d, tf), lambda bi, i, j: (0, j)),
                  pl.BlockSpec((tf, d), lambda bi, i, j: (j, 0)),
                  pl.BlockSpec((1, d), lambda bi, i, j: (0, 0)),
                  pl.BlockSpec((1, d), lambda bi, i, j: (0, 0))],
        out_specs=pl.BlockSpec((1, tl, d), lambda bi, i, j: (bi, i, 0)),
        out_shape=jax.ShapeDtypeStruct((b, l, d), F32),
        scratch_shapes=[pltpu.VMEM((tl, d), BF16), pltpu.VMEM((tl, d), F32)],
        compiler_params=_cparams(("parallel", "parallel", "arbitrary"), 48),
        name="ffn",
    )(x, mod, w_gate, w_up, w_down, ln_g, ln_b)


def _top2_combine(logits):
    lane = lax.broadcasted_iota(jnp.int32, logits.shape, 1).astype(F32)
    neg = -jnp.inf
    lg = jnp.where(lane < N_EXPERTS, logits, neg)
    m1 = jnp.max(lg, axis=1, keepdims=True)
    i1 = jnp.min(jnp.where(lg == m1, lane, float(LANE)), axis=1, keepdims=True)
    lg2 = jnp.where(lane == i1, neg, lg)
    m2 = jnp.max(lg2, axis=1, keepdims=True)
    i2 = jnp.min(jnp.where(lg2 == m2, lane, float(LANE)), axis=1, keepdims=True)
    e2 = jnp.exp(m2 - m1)
    p1 = 1.0 / (1.0 + e2)
    chosen = (lane == i1) | (lane == i2)
    return jnp.where(lane == i1, p1, jnp.where(lane == i2, e2 * p1, 0.0)), chosen


def _moe_kernel(x_ref, mod_ref, wr_ref, wgu_ref, wd_ref, lng_ref, lnb_ref, out_ref,
                h_ref, comb_ref, pos_ref, post_ref, col_ref, xc_ref, yc_ref, cnt_ref, *, alpha, rb, pr):
    e = pl.program_id(2)
    f = pl.program_id(3)
    tl, d = h_ref.shape
    nph = tl // pr

    def phase_rows(c):
        return pl.ds(pl.multiple_of(c * pr, pr), pr)

    @pl.when((e == 0) & (f == 0))
    def _():
        blk = min(pr, LANE)
        before = jnp.where(_order_mask(blk, False, strict=True), 1.0, 0.0).astype(BF16)

        def route(c, total):
            rs = phase_rows(c)
            h = _ln(x_ref[0, rs, :]) * (1.0 + mod_ref[0, 4:5, :]) + mod_ref[0, 3:4, :]
            h_ref[rs, :] = h.astype(BF16)
            comb, chosen = _top2_combine(_dot(h, wr_ref[...], precision=HIGHEST))
            comb_ref[rs, :] = comb
            routed = jnp.where(chosen, 1.0, 0.0)
            ranks = []
            for r0 in range(0, pr, blk):
                part = routed[r0:r0 + blk]
                ranks.append(_dot(before, part.astype(BF16)) + total)
                total = total + jnp.sum(part, axis=0, keepdims=True)
            pos = jnp.where(chosen, jnp.concatenate(ranks, axis=0), -1.0)
            pos_ref[rs, :] = pos
            post_ref[c] = pos.T[0:N_EXPERTS, :]
            out_ref[0, rs, :] = jnp.zeros((pr, d), F32)
            return total

        total = lax.fori_loop(0, nph, route, jnp.zeros((1, LANE), F32))
        lane = lax.broadcasted_iota(jnp.int32, (1, LANE), 1)
        for ex in range(N_EXPERTS):
            cnt_ref[ex] = jnp.sum(jnp.where(lane == ex, total, 0.0)).astype(jnp.int32)

    nblk = (cnt_ref[e] + (rb - 1)) // rb

    def rows(r):
        return pl.ds(pl.multiple_of(r * rb, rb), rb)

    @pl.when(f == 0)
    def _():
        lane = lax.broadcasted_iota(jnp.int32, (pr, LANE), 1)
        for c in range(nph):
            rs = slice(c * pr, (c + 1) * pr)
            col_ref[0, rs, :] = jnp.sum(jnp.where(lane == e, pos_ref[rs, :], 0.0), axis=1, keepdims=True)
            col_ref[1, rs, :] = jnp.sum(jnp.where(lane == e, comb_ref[rs, :], 0.0), axis=1, keepdims=True)

        pos_row = jnp.concatenate([post_ref[c, pl.ds(e, 1), :] for c in range(nph)], axis=1)

        def gather(r, carry):
            slot = (lax.broadcasted_iota(jnp.int32, (rb, tl), 0) + r * rb).astype(F32)
            sel = jnp.where(pos_row == slot, 1.0, 0.0).astype(BF16)
            xc_ref[rows(r), :] = _dot(sel, h_ref[...]).astype(BF16)
            yc_ref[rows(r), :] = jnp.zeros((rb, d), F32)
            return carry

        lax.fori_loop(0, nblk, gather, 0)

    def expert(r, carry):
        xb = xc_ref[rows(r), :]
        gu = _dot(xb, wgu_ref[0, 0])
        half = gu.shape[1] // 2
        act = _silu(gu[:, :half]) * gu[:, half:]
        yc_ref[rows(r), :] += _dot(act.astype(BF16), wd_ref[0])
        return carry

    lax.fori_loop(0, nblk, expert, 0)

    @pl.when(f == pl.num_programs(3) - 1)
    def _():
        def scatter(r, carry):
            slot = (lax.broadcasted_iota(jnp.int32, (pr, rb), 1) + r * rb).astype(F32)
            yb = yc_ref[rows(r), :].astype(BF16)
            for c in range(nph):
                rs = slice(c * pr, (c + 1) * pr)
                sel_t = jnp.where(col_ref[0, rs, :] == slot, 1.0, 0.0).astype(BF16)
                out_ref[0, rs, :] += col_ref[1, rs, :] * _dot(sel_t, yb)
            return carry

        lax.fori_loop(0, nblk, scatter, 0)

    @pl.when((e == pl.num_programs(2) - 1) & (f == pl.num_programs(3) - 1))
    def _():
        def finish(c, carry):
            rs = phase_rows(c)
            z = alpha * x_ref[0, rs, :] + mod_ref[0, 5:6, :] * out_ref[0, rs, :]
            out_ref[0, rs, :] = _ln(z) * lng_ref[...] + lnb_ref[...]
            return carry

        lax.fori_loop(0, nph, finish, 0)


def _pack_gate_up(w_gate, w_up):
    e, d, f = w_gate.shape
    tf = next(t for t in (896, 512, 256, 128) if f % t == 0)

    def tiles(w):
        return w.astype(BF16).reshape(e, d, f // tf, tf).transpose(0, 2, 1, 3)

    return jnp.concatenate([tiles(w_gate), tiles(w_up)], axis=-1)


def _moe_row_block(tl):
    return -(-(tl // 4 + tl // 32) // 16) * 16


def _moe(x, mod, w_router, w_gate_up, w_down, ln_g, ln_b, alpha, tl):
    b, l, d = x.shape
    ne, nf, _, tf2 = w_gate_up.shape
    rb = _moe_row_block(tl)
    compact_rows = -(-tl // rb) * rb
    pr = min(tl, MOE_PHASE_ROWS)
    once = dict(pipeline_mode=pl.Buffered(1))
    return pl.pallas_call(
        functools.partial(_moe_kernel, alpha=alpha, rb=rb, pr=pr),
        grid=(b, l // tl, ne, nf),
        in_specs=[pl.BlockSpec((1, tl, d), lambda bi, i, e, j: (bi, i, 0), **once),
                  pl.BlockSpec((1, 6, d), lambda bi, i, e, j: (bi, 0, 0)),
                  pl.BlockSpec((d, LANE), lambda bi, i, e, j: (0, 0)),
                  pl.BlockSpec((1, 1, d, tf2), lambda bi, i, e, j: (e, j, 0, 0)),
                  pl.BlockSpec((1, tf2 // 2, d), lambda bi, i, e, j: (e, j, 0)),
                  pl.BlockSpec((1, d), lambda bi, i, e, j: (0, 0)),
                  pl.BlockSpec((1, d), lambda bi, i, e, j: (0, 0))],
        out_specs=pl.BlockSpec((1, tl, d), lambda bi, i, e, j: (bi, i, 0), **once),
        out_shape=jax.ShapeDtypeStruct((b, l, d), F32),
        scratch_shapes=[pltpu.VMEM((tl, d), BF16),
                        pltpu.VMEM((tl, LANE), F32),
                        pltpu.VMEM((tl, LANE), F32),
                        pltpu.VMEM((tl // pr, N_EXPERTS, pr), F32),
                        pltpu.VMEM((2, tl, 1), F32),
                        pltpu.VMEM((compact_rows, d), BF16),
                        pltpu.VMEM((compact_rows, d), F32),
                        pltpu.SMEM((N_EXPERTS,), jnp.int32)],
        compiler_params=_cparams(("parallel", "parallel", "arbitrary", "arbitrary"), MOE_VMEM_MB),
        name="moe",
    )(x, mod, w_router, w_gate_up, w_down, ln_g, ln_b)


def _reorder_w_in(w):
    d = w.shape[0]
    o_a, o_b = 0, 5 * MIX_W
    o_bgt = o_b + 4 * MIX_W
    o_cqkv = o_bgt + 16
    o_cz = o_cqkv + 3 * MIX_W
    o_cgt = o_cz + MIX_W
    o_merge = o_cgt + 16
    parts = [w[:, o_merge:o_merge + 3 * d], w[:, o_a:o_bgt], w[:, o_cqkv:o_cgt],
             w[:, o_bgt:o_bgt + 16], w[:, o_cgt:o_cgt + 16],
             jnp.zeros((d, N_PROJ - COL_GATES - 32), w.dtype)]
    return jnp.concatenate(parts, axis=1)


def _pad_row(vals, offset):
    row = jnp.zeros((1, LANE), F32)
    return lax.dynamic_update_slice(row, vals.reshape(1, -1).astype(F32), (0, offset))


def _tile(l, pref):
    return pref if l % pref == 0 else l


def kernel(x, c, ctx, c_ctx, w_ada, b_ada, w_in, conv_w, lb_raw, m_gate_bias, gdn_a_log, gdn_dt_bias, mix_norm,
           w_branch, w_out, ln_g, ln_b, ffn_w_gate, ffn_w_up, ffn_w_down, moe_router, moe_w_gate, moe_w_up,
           moe_w_down):
    depth, d_model = w_in.shape[0], w_in.shape[1]
    b, l_lat, _ = x.shape
    l_ctx = ctx.shape[1]
    alpha = float((2 * depth) ** 0.25)
    assert d_model == 1024 and w_in.shape[2] == 9 * MIX_W + 3 * MIX_W + MIX_W + 32 + 3 * d_model

    lower = _lower_bounds(lb_raw.astype(F32))
    n_rows = -(-(b + 1) // 8) * 8
    cvec = jnp.concatenate([c, c_ctx[None, :], jnp.zeros((n_rows - b - 1, d_model), F32)], axis=0)
    mods = _ada(cvec, w_ada, b_ada)

    zeros_state = jnp.zeros((b, 2, N_HEADS, HEAD_DIM, HEAD_DIM), F32)
    zeros_hgrn = jnp.zeros((b, N_HEADS, 2, HEAD_DIM, HEAD_DIM), F32)
    zeros_nm = jnp.zeros((b, 2, N_HEADS, 8, HEAD_DIM), F32)
    zeros_cmn = jnp.zeros((b, 2, N_HEADS, HEAD_DIM, 2 * HEAD_DIM), F32)

    pack = 1024 // l_ctx if (1024 % l_ctx == 0 and b % max(1024 // l_ctx, 1) == 0) else 1
    ctx_packed = (b // pack, pack * l_ctx, d_model)

    for l in range(depth):
        mod_l = mods[l, :b].reshape(b, 6, d_model)
        mod_c = jnp.broadcast_to(mods[l, b].reshape(1, 6, d_model), (b, 6, d_model))
        w_in_l = _reorder_w_in(w_in[l]).astype(BF16)
        lb_row = lower[l].reshape(1, MIX_W)
        gbias = _pad_row(m_gate_bias[l], 0)
        alog_row = _pad_row(gdn_a_log[l], 16)
        dtb_row = _pad_row(gdn_dt_bias[l], 16)
        gain = mix_norm[l]
        wbr = w_branch[l].astype(BF16)
        wo = w_out[l].astype(BF16)
        g0, b0 = ln_g[l, 0].reshape(1, -1), ln_b[l, 0].reshape(1, -1)
        g1, b1 = ln_g[l, 1].reshape(1, -1), ln_b[l, 1].reshape(1, -1)

        p_c = _in_proj(ctx.reshape(ctx_packed), mod_c[:ctx_packed[0]], w_in_l, ctx_packed[1])
        p_c = p_c.reshape(b, l_ctx, N_PROJ)
        p_l = _in_proj(x, mod_l, w_in_l, _tile(l_lat, 1024))
        qkv_c = _gdn_conv(p_c, conv_w[l], _tile(l_ctx, 256))
        qkv_l = _gdn_conv(p_l, conv_w[l], _tile(l_lat, 512))

        oa_c = _hgrn_scan(p_c, lb_row, zeros_hgrn)
        oa_l = _hgrn_scan(p_l, lb_row, oa_c[2])
        ob_c = _mlstm_scan(p_c, gbias, zeros_cmn, zeros_nm, colmajor=False)
        ob_l = _mlstm_scan(p_l, gbias, ob_c[2], ob_c[3], colmajor=True)
        oc_c = _gdn_scan(qkv_c, p_c, alog_row, dtb_row, zeros_state)
        oc_l = _gdn_scan(qkv_l, p_l, alog_row, dtb_row, oc_c[2])

        def channel_mixer(t, mod, tl, tl_moe=None):
            i = l // 2
            if l % 2 == 0:
                return _ffn(t, mod, ffn_w_gate[i].astype(BF16), ffn_w_up[i].astype(BF16),
                            ffn_w_down[i].astype(BF16), g1, b1, alpha, tl)
            w_r = jnp.concatenate([moe_router[i], jnp.zeros((d_model, LANE - N_EXPERTS), F32)], axis=1)
            return _moe(t, mod, w_r, _pack_gate_up(moe_w_gate[i], moe_w_up[i]), moe_w_down[i].astype(BF16),
                        g1, b1, alpha, tl_moe or tl)

        outs_l = (oa_l[0], oa_l[1], ob_l[0], ob_l[1], oc_l[0], oc_l[1])
        x = _merge(outs_l, p_l, x, mod_l, gain, wbr, wo, g0, b0, alpha, _tile(l_lat, 256))
        x = channel_mixer(x, mod_l, _tile(l_lat, 1024), _tile(l_lat, 2048))
        if l < depth - 1:
            outs_c = (oa_c[0], oa_c[1], ob_c[0], ob_c[1], oc_c[0], oc_c[1])
            ctx = _merge(outs_c, p_c, ctx, mod_c, gain, wbr, wo, g0, b0, alpha, _tile(l_ctx, 256))
            ctx = channel_mixer(ctx.reshape(ctx_packed), mod_c[:ctx_packed[0]], ctx_packed[1]).reshape(ctx.shape)
    return x
```

```python
import functools

import jax
import jax.numpy as jnp
from jax import lax
from jax.experimental import pallas as pl
from jax.experimental.pallas import tpu as pltpu

F32 = jnp.float32
BF16 = jnp.bfloat16
HIGHEST = lax.Precision.HIGHEST

HEAD_DIM = 128
N_HEADS = 4
MIX_W = N_HEADS * HEAD_DIM
CHUNK = 64
GRID_W = 64
SUB = 16
MLSTM_GROUP = 2
GDN_BATCH_ROWS = 4
HGRN_GROUP = 8
HGRN_BATCH_ROWS = 2
N_EXPERTS = 8
MOE_ROW_BLOCK = 288
LN_EPS = 1e-5
RMS_EPS = 1e-6
L2_EPS = 1e-6
MASK_NEG = -1e30
LOG2_E = 1.4426950408889634
LANE = 128

COL_MERGE = 0
COL_A = 3 * 1024
COL_B = COL_A + 5 * MIX_W
COL_CQKV = COL_B + 4 * MIX_W
COL_CZ = COL_CQKV + 3 * MIX_W
COL_GATES = COL_CZ + MIX_W
N_PROJ = COL_GATES + 4 * LANE
PROJ_TILE = N_PROJ // 5


def _cparams(sem, vmem_mb=None):
    kw = dict(dimension_semantics=sem)
    if vmem_mb is not None:
        kw["vmem_limit_bytes"] = vmem_mb << 20
    return pltpu.CompilerParams(**kw)


def _dot(a, b, precision=None):
    return jnp.dot(a, b, preferred_element_type=F32, precision=precision)


def _dot_nt(a, b, precision=None):
    return lax.dot_general(a, b, (((1,), (1,)), ((), ())), preferred_element_type=F32, precision=precision)


def _dot_tn(a, b):
    return lax.dot_general(a, b, (((0,), (0,)), ((), ())), preferred_element_type=F32)


def _ln(x):
    mu = jnp.mean(x, axis=-1, keepdims=True)
    xc = x - mu
    var = jnp.mean(xc * xc, axis=-1, keepdims=True)
    return xc * lax.rsqrt(var + LN_EPS)


def _sigmoid(x):
    return 1.0 / (1.0 + jnp.exp(-x))


def _silu(x):
    return x * _sigmoid(x)


def _softplus(x):
    return jnp.maximum(x, 0.0) + jnp.log(1.0 + jnp.exp(-jnp.abs(x)))


def _log_sigmoid(x):
    return -_softplus(-x)


def _order_mask(n, rev, strict=False):
    r = lax.broadcasted_iota(jnp.int32, (n, n), 0)
    c = lax.broadcasted_iota(jnp.int32, (n, n), 1)
    if rev:
        return (c > r) if strict else (c >= r)
    return (c < r) if strict else (c <= r)


def _cumsum_rows(x, rev):
    m = _order_mask(x.shape[0], rev).astype(F32)
    return _dot(m, x, precision=HIGHEST)


def _ada_kernel(c_ref, w_ref, b_ref, o_ref):
    s = _silu(c_ref[...])
    o_ref[0] = _dot(s, w_ref[0], precision=HIGHEST) + b_ref[0]


def _ada(cvec, w_ada, b_ada, tn=1536):
    depth, d, n6 = w_ada.shape
    rows = cvec.shape[0]
    return pl.pallas_call(
        _ada_kernel,
        grid=(depth, n6 // tn),
        in_specs=[pl.BlockSpec((rows, d), lambda l, n: (0, 0)),
                  pl.BlockSpec((1, d, tn), lambda l, n: (l, 0, n)),
                  pl.BlockSpec((1, 1, tn), lambda l, n: (l, 0, n))],
        out_specs=pl.BlockSpec((1, rows, tn), lambda l, n: (l, 0, n)),
        out_shape=jax.ShapeDtypeStruct((depth, rows, n6), F32),
        compiler_params=_cparams(("parallel", "parallel"), 40),
        name="ada",
    )(cvec, w_ada, b_ada.reshape(depth, 1, n6))


def _lower_bounds_kernel(lb_ref, o_ref):
    x = lb_ref[...]
    depth = x.shape[0]
    mx = jnp.max(x, axis=0, keepdims=True)
    e = jnp.exp(x - mx)
    p = e / jnp.sum(e, axis=0, keepdims=True)
    acc = jnp.zeros_like(p[0:1])
    for l in range(depth):
        o_ref[l:l + 1, :] = acc
        if l + 1 < depth:
            acc = acc + p[l + 1:l + 2]


def _lower_bounds(lb_raw):
    return pl.pallas_call(
        _lower_bounds_kernel,
        out_shape=jax.ShapeDtypeStruct(lb_raw.shape, F32),
        name="lower_bounds",
    )(lb_raw)


def _in_proj_kernel(x_ref, mod_ref, w_ref, o_ref, h_ref):
    @pl.when(pl.program_id(2) == 0)
    def _():
        shift = mod_ref[0, 0:1, :]
        scale = mod_ref[0, 1:2, :]
        h_ref[...] = (_ln(x_ref[0]) * (1.0 + scale) + shift).astype(BF16)

    o_ref[0] = _dot(h_ref[...], w_ref[...])


def _in_proj(x, mod, w, tl, tn=PROJ_TILE):
    b, l, d = x.shape
    n_proj = w.shape[1]
    return pl.pallas_call(
        _in_proj_kernel,
        grid=(b, l // tl, n_proj // tn),
        in_specs=[pl.BlockSpec((1, tl, d), lambda bi, i, n: (bi, i, 0)),
                  pl.BlockSpec((1, 6, d), lambda bi, i, n: (bi, 0, 0)),
                  pl.BlockSpec((d, tn), lambda bi, i, n: (0, n))],
        out_specs=pl.BlockSpec((1, tl, tn), lambda bi, i, n: (bi, i, n)),
        out_shape=jax.ShapeDtypeStruct((b, l, n_proj), F32),
        scratch_shapes=[pltpu.VMEM((tl, d), BF16)],
        compiler_params=_cparams(("parallel", "parallel", "arbitrary"), 48),
        name="in_proj",
    )(x, mod, w)


def _conv_kernel(x_ref, prev_ref, next_ref, w_ref, o_ref):
    i = pl.program_id(1)
    last = pl.num_programs(1) - 1
    x = x_ref[0]
    tl = x.shape[0]
    row = lax.broadcasted_iota(jnp.int32, x.shape, 0)
    before = jnp.where(i > 0, prev_ref[0, 7:8, :], 0.0)
    after = jnp.where(i < last, next_ref[0, 0:1, :], 0.0)
    x_dn = jnp.where(row == 0, before, pltpu.roll(x, 1, axis=0))
    x_up = jnp.where(row == tl - 1, after, pltpu.roll(x, tl - 1, axis=0))
    y = _silu(x_dn * w_ref[0:1, :] + x * w_ref[1:2, :] + x_up * w_ref[2:3, :])
    for j in range(3 * N_HEADS):
        t = y[:, j * HEAD_DIM:(j + 1) * HEAD_DIM]
        if j < 2 * N_HEADS:
            t = t * lax.rsqrt(jnp.sum(t * t, axis=-1, keepdims=True) + L2_EPS)
            if j < N_HEADS:
                t = t * HEAD_DIM ** -0.5
        o_ref[0, :, j * HEAD_DIM:(j + 1) * HEAD_DIM] = t


def _gdn_conv(proj, conv_w, tl):
    b, l, _ = proj.shape
    w3 = 3 * MIX_W
    cb = COL_CQKV // w3
    nb8 = l // 8
    r8 = tl // 8
    return pl.pallas_call(
        _conv_kernel,
        grid=(b, l // tl),
        in_specs=[pl.BlockSpec((1, tl, w3), lambda bi, i: (bi, i, cb)),
                  pl.BlockSpec((1, 8, w3), lambda bi, i: (bi, jnp.maximum(i * r8 - 1, 0), cb)),
                  pl.BlockSpec((1, 8, w3), lambda bi, i: (bi, jnp.minimum((i + 1) * r8, nb8 - 1), cb)),
                  pl.BlockSpec((3, w3), lambda bi, i: (0, 0))],
        out_specs=pl.BlockSpec((1, tl, w3), lambda bi, i: (bi, i, 0)),
        out_shape=jax.ShapeDtypeStruct((b, l, w3), F32),
        compiler_params=_cparams(("parallel", "parallel"), 40),
        name="gdn_conv",
    )(proj, proj, proj, conv_w)


def _hgrn_chunks(probs, states, emat):
    c = probs[0]["q"].shape[0]
    nb = c // SUB
    for p in probs:
        lb = p["lb"]
        sig = _sigmoid(p["fpre"])
        p["k"] = (1.0 - lb) * (1.0 - sig)
        p["g"] = _cumsum_rows(jnp.log(lb + (1.0 - lb) * sig), p["rev"])

    walls = []
    for p in probs:
        q = p["q"]
        g2 = p["g"] * LOG2_E
        lk2 = jnp.log(p["k"]) * LOG2_E
        rel = lk2 - g2
        rows = []
        half = SUB // 2
        dead = jnp.zeros((half, HEAD_DIM), F32)
        for j in range(nb):
            r0 = j * SUB
            pieces = []
            for s in range(SUB):
                skip = 0 if (s >= half and not p["rev"]) else 1 if (s < half and p["rev"]) else None
                parts = []
                for part in range(2):
                    if part == skip:
                        parts.append(dead)
                        continue
                    t0 = r0 + part * half
                    e2 = jnp.exp2(jnp.minimum(g2[t0:t0 + half] + rel[r0 + s:r0 + s + 1], lk2[r0 + s:r0 + s + 1]))
                    parts.append(e2 * q[t0:t0 + half])
                pieces.append(jnp.concatenate(parts, axis=0).astype(BF16))
            rows.append(jnp.concatenate(pieces, axis=1))
        walls.append(jnp.concatenate(rows, axis=0))
    a_diags = [_dot(w, emat) for w in walls]

    a_offs = []
    for p in probs:
        q, k, g, rev = p["q"], p["k"], p["g"], p["rev"]
        offs = []
        for i in range(nb):
            r0 = i * SUB
            if (not rev and i == 0) or (rev and i == nb - 1):
                offs.append(jnp.zeros((SUB, c), F32))
                continue
            bi = g[r0 + SUB:r0 + SUB + 1] if rev else g[r0 - 1:r0]
            lhs = (q[r0:r0 + SUB] * jnp.exp(g[r0:r0 + SUB] - bi)).astype(BF16)
            lo, hi = (r0 + SUB, c) if rev else (0, r0)
            live = k[lo:hi] * jnp.exp(bi - g[lo:hi])
            pad = jnp.zeros((c - (hi - lo), HEAD_DIM), F32)
            rhs = jnp.concatenate([pad, live] if rev else [live, pad], axis=0)
            offs.append(_dot_nt(lhs, rhs.astype(BF16)))
        a_offs.append(jnp.concatenate(offs, axis=0))

    r = lax.broadcasted_iota(jnp.int32, (c, c), 0)
    col = lax.broadcasted_iota(jnp.int32, (c, c), 1)
    shift = SUB.bit_length() - 1
    rb = lax.shift_right_logical(r, shift)
    cb = lax.shift_right_logical(col, shift)
    inters = [_dot_nt((p["q"] * jnp.exp(p["g"])).astype(BF16), s_t.astype(BF16)) for p, s_t in zip(probs, states)]
    outs, new_states = [], []
    for p, a_diag, a_off, inter, s_t in zip(probs, a_diags, a_offs, inters, states):
        rev, g = p["rev"], p["g"]
        order = (col >= r) if rev else (col <= r)
        off_side = (cb > rb) if rev else (cb < rb)
        a = jnp.where((rb == cb) & order, a_diag, jnp.where(off_side, a_off, 0.0))
        g_end = g[0:1] if rev else g[c - 1:c]
        vb = p["v"].astype(BF16)
        outs.append(_dot(a.astype(BF16), vb) + inter)
        kd = (p["k"] * jnp.exp(g_end - g)).astype(BF16)
        new_states.append(s_t * jnp.exp(g_end) + _dot_tn(vb, kd))
    return outs, new_states


def _hgrn_kernel(qf_ref, ff_ref, vf_ref, qb_ref, fb_ref, vb_ref, lb_ref, e_ref, s0_ref,
                 of_ref, ob_ref, sout_ref, s_ref):
    n = pl.program_id(1)

    @pl.when(n == 0)
    def _():
        s_ref[...] = s0_ref[...]

    probs, where = [], []
    for r in range(qf_ref.shape[0]):
        for d, (q_ref, f_ref, v_ref, o_ref) in enumerate(((qf_ref, ff_ref, vf_ref, of_ref),
                                                          (qb_ref, fb_ref, vb_ref, ob_ref))):
            for h in range(N_HEADS):
                sl = slice(h * HEAD_DIM, (h + 1) * HEAD_DIM)
                probs.append(dict(q=q_ref[r, :, sl], fpre=f_ref[r, :, sl], v=v_ref[r, :, sl], lb=lb_ref[:, sl],
                                  rev=d == 1))
                where.append((o_ref, sl, r, h, d))
    states = [s_ref[r, h, d] for _, _, r, h, d in where]
    outs, new_states = [], []
    for g0 in range(0, len(probs), HGRN_GROUP):
        o_g, s_g = _hgrn_chunks(probs[g0:g0 + HGRN_GROUP], states[g0:g0 + HGRN_GROUP], e_ref[...])
        outs += o_g
        new_states += s_g
    for (o_ref, sl, r, h, d), o, s_new in zip(where, outs, new_states):
        o_ref[r, :, sl] = o
        s_ref[r, h, d] = s_new

    @pl.when(n == pl.num_programs(1) - 1)
    def _():
        sout_ref[...] = s_ref[...]


def _hgrn_emat():
    r = jnp.arange(SUB * HEAD_DIM)[:, None] // HEAD_DIM
    c = jnp.arange(CHUNK)[None, :] % SUB
    return (r == c).astype(BF16)


def _hgrn_scan(proj, lb, s0):
    b, l, _ = proj.shape
    nc = l // CHUNK
    rows = HGRN_BATCH_ROWS if b % HGRN_BATCH_ROWS == 0 else 1
    ca = COL_A // MIX_W

    def fwd(col):
        return pl.BlockSpec((rows, CHUNK, MIX_W), lambda bi, n: (bi, n, ca + col))

    def bwd(col):
        return pl.BlockSpec((rows, CHUNK, MIX_W), lambda bi, n: (bi, nc - 1 - n, ca + col))

    st_spec = pl.BlockSpec((rows, N_HEADS, 2, HEAD_DIM, HEAD_DIM), lambda bi, n: (bi, 0, 0, 0, 0))
    return pl.pallas_call(
        _hgrn_kernel,
        grid=(b // rows, nc),
        in_specs=[fwd(0), fwd(1), fwd(3), bwd(0), bwd(2), bwd(3),
                  pl.BlockSpec((1, MIX_W), lambda bi, n: (0, 0)),
                  pl.BlockSpec((SUB * HEAD_DIM, CHUNK), lambda bi, n: (0, 0)),
                  st_spec],
        out_specs=[pl.BlockSpec((rows, CHUNK, MIX_W), lambda bi, n: (bi, n, 0)),
                   pl.BlockSpec((rows, CHUNK, MIX_W), lambda bi, n: (bi, nc - 1 - n, 0)),
                   st_spec],
        out_shape=[jax.ShapeDtypeStruct((b, l, MIX_W), F32),
                   jax.ShapeDtypeStruct((b, l, MIX_W), F32),
                   jax.ShapeDtypeStruct(s0.shape, F32)],
        scratch_shapes=[pltpu.VMEM((rows, N_HEADS, 2, HEAD_DIM, HEAD_DIM), F32)],
        compiler_params=_cparams(("parallel", "arbitrary"), 40),
        name="hgrn_scan",
    )(proj, proj, proj, proj, proj, proj, lb, _hgrn_emat(), s0)


def _mlstm_chunks(probs, states):
    c = probs[0]["q"].shape[0]
    dv = probs[0]["v"].shape[1]
    ones = jnp.ones((c, dv), BF16)
    for p, (cmn, m) in zip(probs, states):
        tri = _order_mask(c, p["rev"])
        log_d = jnp.where(tri, p["bb_c"] - p["bb_r"] + p["ic_r"], MASK_NEG)
        log_inter = p["bb_c"] + m
        m_t = jnp.maximum(jnp.max(log_d, axis=1, keepdims=True), log_inter)
        p["m_t"] = m_t
        p["dmat"] = jnp.where(tri, jnp.exp(log_d - m_t), 0.0)
        p["inter"] = jnp.exp(log_inter - m_t)
        p["qb"], p["kb"] = p["q"].astype(BF16), p["k"].astype(BF16)
        p["v1"] = jnp.concatenate([p["v"].astype(BF16), ones], axis=1)
    scores = [_dot_nt(p["qb"], p["kb"]) * p["dmat"] for p in probs]
    carried = [_dot(p["qb"], cmn.astype(BF16)) for p, (cmn, _) in zip(probs, states)]
    local = [_dot(s.astype(BF16), p["v1"]) for p, s in zip(probs, scores)]
    outs, new_states = [], []
    for p, car, loc, (cmn, m) in zip(probs, carried, local, states):
        m_t = p["m_t"]
        both = loc + p["inter"] * car
        outs.append(both[:, :dv] / jnp.maximum(jnp.abs(both[:, dv:]), jnp.exp(-m_t)))
        m_new = m_t[0:1] if p["rev"] else m_t[c - 1:c]
        bb_end = p["bb_c"][0:1] if p["rev"] else p["bb_c"][c - 1:c]
        decay = jnp.exp(bb_end + m - m_new)
        kw = p["k"] * jnp.exp(bb_end - p["bb_c"] + p["ic_c"] - m_new)
        new_states.append((decay * cmn + _dot_tn(kw.astype(BF16), p["v1"]), m_new))
    return outs, new_states


def _column_view(ref):
    rows, group, width = ref.shape[-3:]
    return ref.reshape(rows * group, width), group


def _chunk_rows(j):
    start = j * CHUNK
    return pl.ds(start if isinstance(start, int) else pl.multiple_of(start, CHUNK), CHUNK)


def _load_chunk(ref, j, colmajor):
    if colmajor:
        flat, group = _column_view(ref)
        return flat[pl.ds(j, CHUNK, stride=group), :]
    return ref[0, _chunk_rows(j), :]


def _store_chunk(ref, j, val, colmajor):
    if colmajor:
        flat, group = _column_view(ref)
        flat[pl.ds(j, CHUNK, stride=group), :] = val
    else:
        ref[0, _chunk_rows(j), :] = val


def _columns_as_rows(x, first):
    row = lax.broadcasted_iota(jnp.int32, (8, LANE), 0)
    lane = lax.broadcasted_iota(jnp.int32, (8, LANE), 1)
    sel = jnp.where(lane == row + first, 1.0, 0.0).astype(BF16)
    hi = x.astype(BF16)
    rest = x - hi.astype(F32)
    mid = rest.astype(BF16)
    lo = (rest - mid.astype(F32)).astype(BF16)
    return _dot_nt(sel, hi) + (_dot_nt(sel, mid) + _dot_nt(sel, lo))


def _mlstm_kernel(*refs, colmajor, chunks):
    per_dir = 3 * N_HEADS + 1
    fwd, bwd = refs[:per_dir], refs[per_dir:2 * per_dir]
    bias_ref, cm0_ref, nm0_ref, of_ref, ob_ref, cmout_ref, nmout_ref, cm_ref, nm_ref = refs[2 * per_dir:]
    n = pl.program_id(1)

    @pl.when(n == 0)
    def _():
        cm_ref[...] = cm0_ref[0]
        nm_ref[...] = nm0_ref[0]

    def step(i, carry):
        probs, where = [], []
        for d, (in_refs, o_ref) in enumerate(((fwd, of_ref), (bwd, ob_ref))):
            rev = d == 1
            j = chunks - 1 - i if rev else i
            gates = _load_chunk(in_refs[-1], j, colmajor) + bias_ref[...]
            cs = _cumsum_rows(_log_sigmoid(gates), rev)
            cs_rows = _columns_as_rows(cs, 8 + 4 * d)
            gate_rows = _columns_as_rows(gates, 4 * d)
            for h in range(N_HEADS):
                ci = 4 * d + h
                cf = 8 + 4 * d + h
                probs.append(dict(
                    q=_load_chunk(in_refs[h], j, colmajor),
                    k=_load_chunk(in_refs[N_HEADS + h], j, colmajor) * HEAD_DIM ** -0.5,
                    v=_load_chunk(in_refs[2 * N_HEADS + h], j, colmajor),
                    bb_c=cs[:, cf:cf + 1], bb_r=cs_rows[h:h + 1, :],
                    ic_c=gates[:, ci:ci + 1], ic_r=gate_rows[h:h + 1, :], rev=rev))
                where.append((o_ref, j, d, h))
        states = [(cm_ref[d, h], nm_ref[d, h, 0:1, 0:1]) for _, _, d, h in where]
        outs, new_states = [], []
        for g0 in range(0, len(probs), MLSTM_GROUP):
            o_g, s_g = _mlstm_chunks(probs[g0:g0 + MLSTM_GROUP], states[g0:g0 + MLSTM_GROUP])
            outs += o_g
            new_states += s_g
        for (o_ref, j, d, h), o, (cmn_new, m_new) in zip(where, outs, new_states):
            _store_chunk(o_ref.at[h], j, o, colmajor)
            cm_ref[d, h] = cmn_new
            nm_ref[d, h, 0:1, :] = jnp.broadcast_to(m_new, (1, HEAD_DIM))
        return carry

    lax.fori_loop(0, chunks, step, 0)

    @pl.when(n == pl.num_programs(1) - 1)
    def _():
        cmout_ref[0] = cm_ref[...]
        nmout_ref[0] = nm_ref[...]


def _mlstm_scan(proj, gbias, cm0, nm0, colmajor):
    b, l, n_proj = proj.shape
    cb = COL_B // LANE
    cg = COL_GATES // LANE
    if colmajor:
        chunks = 8
        assert l == CHUNK * GRID_W and GRID_W % chunks == 0
        nb = GRID_W // chunks
        src = proj.reshape(b, CHUNK, GRID_W, n_proj)
        in_blk = (1, CHUNK, chunks, LANE)
        out_blk = (N_HEADS, 1, CHUNK, chunks, LANE)
        out_view = (N_HEADS, b, CHUNK, GRID_W, LANE)

        def in_map(col, flip):
            return lambda bi, n: (bi, 0, nb - 1 - n if flip else n, col)

        def out_map(flip):
            return lambda bi, n: (0, bi, 0, nb - 1 - n if flip else n, 0)
    else:
        chunks = min(l // CHUNK, 4)
        nb = l // (chunks * CHUNK)
        src = proj
        in_blk = (1, chunks * CHUNK, LANE)
        out_blk = (N_HEADS, 1, chunks * CHUNK, LANE)
        out_view = (N_HEADS, b, l, LANE)

        def in_map(col, flip):
            return lambda bi, n: (bi, nb - 1 - n if flip else n, col)

        def out_map(flip):
            return lambda bi, n: (0, bi, nb - 1 - n if flip else n, 0)

    def specs(flip):
        cols = [cb + j for j in range(3 * N_HEADS)] + [cg]
        return [pl.BlockSpec(in_blk, in_map(col, flip)) for col in cols]

    cm_spec = pl.BlockSpec((1, 2, N_HEADS, HEAD_DIM, 2 * HEAD_DIM), lambda bi, n: (bi, 0, 0, 0, 0))
    nm_spec = pl.BlockSpec((1, 2, N_HEADS, 8, HEAD_DIM), lambda bi, n: (bi, 0, 0, 0, 0))
    n_in = 2 * (3 * N_HEADS + 1)
    o_f, o_b, cm, nm = pl.pallas_call(
        functools.partial(_mlstm_kernel, colmajor=colmajor, chunks=chunks),
        grid=(b, nb),
        in_specs=specs(False) + specs(True) + [pl.BlockSpec((1, LANE), lambda bi, n: (0, 0)), cm_spec, nm_spec],
        out_specs=[pl.BlockSpec(out_blk, out_map(False)), pl.BlockSpec(out_blk, out_map(True)), cm_spec, nm_spec],
        out_shape=[jax.ShapeDtypeStruct(out_view, F32), jax.ShapeDtypeStruct(out_view, F32),
                   jax.ShapeDtypeStruct(cm0.shape, F32), jax.ShapeDtypeStruct(nm0.shape, F32)],
        scratch_shapes=[pltpu.VMEM((2, N_HEADS, HEAD_DIM, 2 * HEAD_DIM), F32),
                        pltpu.VMEM((2, N_HEADS, 8, HEAD_DIM), F32)],
        compiler_params=_cparams(("parallel", "arbitrary"), 40),
        name="mlstm_scan",
    )(*([src] * n_in), gbias, cm0, nm0)
    head_major = (N_HEADS, b, l, LANE)
    return o_f.reshape(head_major), o_b.reshape(head_major), cm, nm


def _split_bf16(x):
    hi = x.astype(BF16)
    return hi, (x - hi.astype(F32)).astype(BF16)


def _dot3(a, b, nt=False):
    f = _dot_nt if nt else _dot
    return f(a[0], b[0]) + (f(a[0], b[1]) + f(a[1], b[0]))


def _inv_unit_triangular(a_list):
    n = a_list[0].shape[0]
    eye = (lax.broadcasted_iota(jnp.int32, (n, n), 0) == lax.broadcasted_iota(jnp.int32, (n, n), 1)).astype(F32)
    xs = [eye - a for a in a_list]
    splits = [_split_bf16(a) for a in a_list]
    ps = [_dot3(s, s) for s in splits]
    steps = (n - 1).bit_length() - 1
    for step in range(steps):
        last = step == steps - 1
        p_split = [_split_bf16(p) for p in ps]
        lhs = xs if last else [jnp.concatenate([x, p], axis=0) for x, p in zip(xs, ps)]
        prods = [_dot3(_split_bf16(l), s) for l, s in zip(lhs, p_split)]
        xs = [x + pr[:n] for x, pr in zip(xs, prods)]
        if not last:
            ps = [pr[n:] for pr in prods]
    return xs


def _gdn_chunks(probs, states):
    c = probs[0]["q"].shape[0]
    a_list, ldecs = [], []
    for p in probs:
        tri = _order_mask(c, p["rev"])
        strict = _order_mask(c, p["rev"], strict=True)
        ldec = jnp.where(tri, jnp.exp(jnp.minimum(p["gam_c"] - p["gam_r"], 0.0)), 0.0)
        p["kbeta"] = p["k"] * p["beta"]
        kk = _dot_nt(p["kbeta"].astype(BF16), p["k"].astype(BF16))
        a_list.append(jnp.where(strict, kk * ldec, 0.0))
        ldecs.append(ldec)
    t_invs = _inv_unit_triangular(a_list)
    sols = []
    for p, t_inv in zip(probs, t_invs):
        p["eg"] = jnp.exp(p["gam_c"])
        rhs = jnp.concatenate([p["v"] * p["beta"], p["kbeta"] * p["eg"]], axis=1)
        sols.append(_dot3(_split_bf16(t_inv), _split_bf16(rhs)))
    qks = [_dot_nt(p["q"].astype(BF16), p["k"].astype(BF16)) * ldec for p, ldec in zip(probs, ldecs)]
    inter = [_dot(jnp.concatenate([sol[:, HEAD_DIM:], p["q"] * p["eg"]], axis=0).astype(BF16), s.astype(BF16))
             for p, sol, s in zip(probs, sols, states)]
    outs, new_states = [], []
    for p, sol, qk, it, s in zip(probs, sols, qks, inter, states):
        g_end = p["gam_c"][0:1] if p["rev"] else p["gam_c"][c - 1:c]
        v_new = (sol[:, :HEAD_DIM] - it[:c]).astype(BF16)
        kg = p["k"] * jnp.exp(g_end - p["gam_c"])
        outs.append(it[c:] + _dot(qk.astype(BF16), v_new))
        new_states.append(jnp.exp(g_end) * s + _dot_tn(kg.astype(BF16), v_new))
    return outs, new_states


def _gdn_kernel(xf_ref, gf_ref, xb_ref, gb_ref, alog_ref, dtb_ref, s0_ref,
                of_ref, ob_ref, sout_ref, s_ref):
    n = pl.program_id(1)

    @pl.when(n == 0)
    def _():
        s_ref[...] = s0_ref[...]

    neg_a = -jnp.exp(alog_ref[...])
    probs, where = [], []
    for r in range(xf_ref.shape[0]):
        for d, (x_ref, g_ref, o_ref) in enumerate(((xf_ref, gf_ref, of_ref), (xb_ref, gb_ref, ob_ref))):
            rev = d == 1
            gates = g_ref[r]
            cs = _cumsum_rows(neg_a * _softplus(gates + dtb_ref[...]), rev)
            cs_t = cs.T
            betas = _sigmoid(gates)
            for h in range(N_HEADS):
                ca = 16 + 4 * d + h
                cbeta = 24 + 4 * d + h
                probs.append(dict(
                    q=x_ref[r, :, h * HEAD_DIM:(h + 1) * HEAD_DIM],
                    k=x_ref[r, :, MIX_W + h * HEAD_DIM:MIX_W + (h + 1) * HEAD_DIM],
                    v=x_ref[r, :, 2 * MIX_W + h * HEAD_DIM:2 * MIX_W + (h + 1) * HEAD_DIM],
                    gam_c=cs[:, ca:ca + 1], gam_r=cs_t[ca:ca + 1, :], beta=betas[:, cbeta:cbeta + 1], rev=rev))
                where.append((o_ref, r, d, h))
    outs, new_states = _gdn_chunks(probs, [s_ref[r, d, h] for _, r, d, h in where])
    for (o_ref, r, d, h), o, s_new in zip(where, outs, new_states):
        o_ref[r, :, h * HEAD_DIM:(h + 1) * HEAD_DIM] = o
        s_ref[r, d, h] = s_new

    @pl.when(n == pl.num_programs(1) - 1)
    def _():
        sout_ref[...] = s_ref[...]


def _gdn_scan(qkv, proj, alog_row, dtb_row, s0):
    b, l, w3 = qkv.shape
    nc = l // CHUNK
    cg = COL_GATES // LANE
    rows = GDN_BATCH_ROWS if b % GDN_BATCH_ROWS == 0 else 1
    st_spec = pl.BlockSpec((rows, 2, N_HEADS, HEAD_DIM, HEAD_DIM), lambda bi, n: (bi, 0, 0, 0, 0))
    row_spec = pl.BlockSpec((1, LANE), lambda bi, n: (0, 0))
    return pl.pallas_call(
        _gdn_kernel,
        grid=(b // rows, nc),
        in_specs=[pl.BlockSpec((rows, CHUNK, w3), lambda bi, n: (bi, n, 0)),
                  pl.BlockSpec((rows, CHUNK, LANE), lambda bi, n: (bi, n, cg)),
                  pl.BlockSpec((rows, CHUNK, w3), lambda bi, n: (bi, nc - 1 - n, 0)),
                  pl.BlockSpec((rows, CHUNK, LANE), lambda bi, n: (bi, nc - 1 - n, cg)),
                  row_spec, row_spec, st_spec],
        out_specs=[pl.BlockSpec((rows, CHUNK, MIX_W), lambda bi, n: (bi, n, 0)),
                   pl.BlockSpec((rows, CHUNK, MIX_W), lambda bi, n: (bi, nc - 1 - n, 0)),
                   st_spec],
        out_shape=[jax.ShapeDtypeStruct((b, l, MIX_W), F32), jax.ShapeDtypeStruct((b, l, MIX_W), F32),
                   jax.ShapeDtypeStruct(s0.shape, F32)],
        scratch_shapes=[pltpu.VMEM((rows, 2, N_HEADS, HEAD_DIM, HEAD_DIM), F32)],
        compiler_params=_cparams(("parallel", "arbitrary"), 40),
        name="gdn_scan",
    )(qkv, proj, qkv, proj, alog_row, dtb_row, s0)


def _merge_kernel(oaf, oab, obf, obb, ocf, ocb, ag, bo, cz, mg, x_ref, mod_ref, gain_ref, wbr_ref, wout_ref,
                  lng_ref, lnb_ref, out_ref, *, alpha):
    d_model = x_ref.shape[-1]
    branches = ((oaf, oab, ag, _silu), (obf, obb, bo, _sigmoid), (ocf, ocb, cz, _silu))
    mixed = None
    for nbr, (of_ref, ob_ref, gate_ref, act) in enumerate(branches):
        parts = []
        for h in range(N_HEADS):
            if len(of_ref.shape) == 4:
                oh = of_ref[h, 0] + ob_ref[h, 0]
            else:
                oh = of_ref[0, :, h * HEAD_DIM:(h + 1) * HEAD_DIM] + ob_ref[0, :, h * HEAD_DIM:(h + 1) * HEAD_DIM]
            parts.append(oh * lax.rsqrt(jnp.mean(oh * oh, axis=-1, keepdims=True) + RMS_EPS))
        y = jnp.concatenate(parts, axis=1) * gain_ref[nbr:nbr + 1, :] * act(gate_ref[0])
        term = _sigmoid(mg[0, :, nbr * d_model:(nbr + 1) * d_model]) * _dot(y.astype(BF16), wbr_ref[nbr])
        mixed = term if mixed is None else mixed + term
    sub = _dot(mixed.astype(BF16), wout_ref[...])
    z = alpha * x_ref[0] + mod_ref[0, 2:3, :] * sub
    out_ref[0] = _ln(z) * lng_ref[...] + lnb_ref[...]


def _merge(outs, proj, x, mod, gain, w_branch, w_out, ln_g, ln_b, alpha, tl):
    b, l, d = x.shape
    u = MIX_W

    def tok(width, col):
        return pl.BlockSpec((1, tl, width), lambda bi, i: (bi, i, col))

    def const(shape):
        return pl.BlockSpec(shape, lambda bi, i: (0,) * len(shape))

    heads = pl.BlockSpec((N_HEADS, 1, tl, LANE), lambda bi, i: (0, bi, i, 0))
    return pl.pallas_call(
        functools.partial(_merge_kernel, alpha=alpha),
        grid=(b, l // tl),
        in_specs=[tok(u, 0), tok(u, 0), heads, heads, tok(u, 0), tok(u, 0),
            tok(u, (COL_A + 4 * MIX_W) // u), tok(u, (COL_B + 3 * MIX_W) // u), tok(u, COL_CZ // u),
            tok(3 * d, COL_MERGE // (3 * d)), tok(d, 0),
            pl.BlockSpec((1, 6, d), lambda bi, i: (bi, 0, 0)),
            const((3, u)), const((3, u, d)), const((d, d)), const((1, d)), const((1, d))],
        out_specs=tok(d, 0),
        out_shape=jax.ShapeDtypeStruct((b, l, d), F32),
        compiler_params=_cparams(("parallel", "parallel"), 48),
        name="merge",
    )(*outs, proj, proj, proj, proj, x, mod, gain, w_branch, w_out, ln_g, ln_b)


def _ffn_kernel(x_ref, mod_ref, wg_ref, wu_ref, wd_ref, lng_ref, lnb_ref, out_ref, h_ref, acc_ref, *, alpha):
    f = pl.program_id(2)

    @pl.when(f == 0)
    def _():
        h_ref[...] = (_ln(x_ref[0]) * (1.0 + mod_ref[0, 4:5, :]) + mod_ref[0, 3:4, :]).astype(BF16)
        acc_ref[...] = jnp.zeros_like(acc_ref)

    h = h_ref[...]
    act = _silu(_dot(h, wg_ref[...])) * _dot(h, wu_ref[...])
    acc_ref[...] += _dot(act.astype(BF16), wd_ref[...])

    @pl.when(f == pl.num_programs(2) - 1)
    def _():
        z = alpha * x_ref[0] + mod_ref[0, 5:6, :] * acc_ref[...]
        out_ref[0] = _ln(z) * lng_ref[...] + lnb_ref[...]


def _ffn(x, mod, w_gate, w_up, w_down, ln_g, ln_b, alpha, tl, tf=512):
    b, l, d = x.shape
    f = w_gate.shape[1]
    return pl.pallas_call(
        functools.partial(_ffn_kernel, alpha=alpha),
        grid=(b, l // tl, f // tf),
        in_specs=[pl.BlockSpec((1, tl, d), lambda bi, i, j: (bi, i, 0)),
                  pl.BlockSpec((1, 6, d), lambda bi, i, j: (bi, 0, 0)),
                  pl.BlockSpec((d, tf), lambda bi, i, j: (0, j)),
                  pl.BlockSpec((d, tf), lambda bi, i, j: (0, j)),
                  pl.BlockSpec((tf, d), lambda bi, i, j: (j, 0)),
                  pl.BlockSpec((1, d), lambda bi, i, j: (0, 0)),
                  pl.BlockSpec((1, d), lambda bi, i, j: (0, 0))],
        out_specs=pl.BlockSpec((1, tl, d), lambda bi, i, j: (bi, i, 0)),
        out_shape=jax.ShapeDtypeStruct((b, l, d), F32),
        scratch_shapes=[pltpu.VMEM((tl, d), BF16), pltpu.VMEM((tl, d), F32)],
        compiler_params=_cparams(("parallel", "parallel", "arbitrary"), 48),
        name="ffn",
    )(x, mod, w_gate, w_up, w_down, ln_g, ln_b)


def _top2_combine(logits):
    lane = lax.broadcasted_iota(jnp.int32, logits.shape, 1).astype(F32)
    neg = -jnp.inf
    lg = jnp.where(lane < N_EXPERTS, logits, neg)
    m1 = jnp.max(lg, axis=1, keepdims=True)
    i1 = jnp.min(jnp.where(lg == m1, lane, float(LANE)), axis=1, keepdims=True)
    lg2 = jnp.where(lane == i1, neg, lg)
    m2 = jnp.max(lg2, axis=1, keepdims=True)
    i2 = jnp.min(jnp.where(lg2 == m2, lane, float(LANE)), axis=1, keepdims=True)
    e2 = jnp.exp(m2 - m1)
    p1 = 1.0 / (1.0 + e2)
    chosen = (lane == i1) | (lane == i2)
    return jnp.where(lane == i1, p1, jnp.where(lane == i2, e2 * p1, 0.0)), chosen


def _moe_kernel(x_ref, mod_ref, wr_ref, wgu_ref, wd_ref, lng_ref, lnb_ref, out_ref,
                h_ref, acc_ref, comb_ref, pos_ref, post_ref, col_ref, xc_ref, yc_ref, cnt_ref, *, alpha, rb):
    e = pl.program_id(2)
    f = pl.program_id(3)
    tl, d = h_ref.shape

    @pl.when((e == 0) & (f == 0))
    def _():
        h = _ln(x_ref[0]) * (1.0 + mod_ref[0, 4:5, :]) + mod_ref[0, 3:4, :]
        h_ref[...] = h.astype(BF16)
        comb, chosen = _top2_combine(_dot(h, wr_ref[...], precision=HIGHEST))
        comb_ref[...] = comb
        blk = min(tl, LANE)
        before = jnp.where(_order_mask(blk, False, strict=True), 1.0, 0.0).astype(BF16)
        routed = jnp.where(chosen, 1.0, 0.0)
        total = jnp.zeros((1, LANE), F32)
        ranks = []
        for r0 in range(0, tl, blk):
            part = routed[r0:r0 + blk]
            ranks.append(_dot(before, part.astype(BF16)) + total)
            total = total + jnp.sum(part, axis=0, keepdims=True)
        rank = jnp.concatenate(ranks, axis=0)
        pos = jnp.where(chosen, rank, -1.0)
        pos_ref[...] = pos
        pos_t = pos.T[0:N_EXPERTS, :]
        post_ref[...] = pos_t
        for ex in range(N_EXPERTS):
            cnt_ref[ex] = jnp.sum(jnp.where(pos_t[ex:ex + 1, :] >= 0.0, 1.0, 0.0)).astype(jnp.int32)
        acc_ref[...] = jnp.zeros_like(acc_ref)

    nblk = (cnt_ref[e] + (rb - 1)) // rb

    def rows(r):
        return pl.ds(pl.multiple_of(r * rb, rb), rb)

    @pl.when(f == 0)
    def _():
        lane = lax.broadcasted_iota(jnp.int32, (tl, LANE), 1)
        col_ref[0] = jnp.sum(jnp.where(lane == e, pos_ref[...], 0.0), axis=1, keepdims=True)
        col_ref[1] = jnp.sum(jnp.where(lane == e, comb_ref[...], 0.0), axis=1, keepdims=True)
        pos_row = post_ref[pl.ds(e, 1), :]

        def gather(r, carry):
            slot = (lax.broadcasted_iota(jnp.int32, (rb, tl), 0) + r * rb).astype(F32)
            sel = jnp.where(pos_row == slot, 1.0, 0.0).astype(BF16)
            xc_ref[rows(r), :] = _dot(sel, h_ref[...]).astype(BF16)
            yc_ref[rows(r), :] = jnp.zeros((rb, d), F32)
            return carry

        lax.fori_loop(0, nblk, gather, 0)

    def expert(r, carry):
        xb = xc_ref[rows(r), :]
        gu = _dot(xb, wgu_ref[0, 0])
        half = gu.shape[1] // 2
        act = _silu(gu[:, :half]) * gu[:, half:]
        yc_ref[rows(r), :] += _dot(act.astype(BF16), wd_ref[0])
        return carry

    lax.fori_loop(0, nblk, expert, 0)

    @pl.when(f == pl.num_programs(3) - 1)
    def _():
        pos_col = col_ref[0]
        weight = col_ref[1]

        def scatter(r, carry):
            slot = (lax.broadcasted_iota(jnp.int32, (tl, rb), 1) + r * rb).astype(F32)
            sel_t = jnp.where(pos_col == slot, 1.0, 0.0).astype(BF16)
            acc_ref[...] += weight * _dot(sel_t, yc_ref[rows(r), :].astype(BF16))
            return carry

        lax.fori_loop(0, nblk, scatter, 0)

    @pl.when((e == pl.num_programs(2) - 1) & (f == pl.num_programs(3) - 1))
    def _():
        z = alpha * x_ref[0] + mod_ref[0, 5:6, :] * acc_ref[...]
        out_ref[0] = _ln(z) * lng_ref[...] + lnb_ref[...]


def _pack_gate_up(w_gate, w_up):
    e, d, f = w_gate.shape
    tf = next(t for t in (896, 512, 256, 128) if f % t == 0)

    def tiles(w):
        return w.astype(BF16).reshape(e, d, f // tf, tf).transpose(0, 2, 1, 3)

    return jnp.concatenate([tiles(w_gate), tiles(w_up)], axis=-1)


def _moe(x, mod, w_router, w_gate_up, w_down, ln_g, ln_b, alpha, tl):
    b, l, d = x.shape
    ne, nf, _, tf2 = w_gate_up.shape
    rb = min(MOE_ROW_BLOCK, tl)
    compact_rows = -(-tl // rb) * rb
    return pl.pallas_call(
        functools.partial(_moe_kernel, alpha=alpha, rb=rb),
        grid=(b, l // tl, ne, nf),
        in_specs=[pl.BlockSpec((1, tl, d), lambda bi, i, e, j: (bi, i, 0)),
                  pl.BlockSpec((1, 6, d), lambda bi, i, e, j: (bi, 0, 0)),
                  pl.BlockSpec((d, LANE), lambda bi, i, e, j: (0, 0)),
                  pl.BlockSpec((1, 1, d, tf2), lambda bi, i, e, j: (e, j, 0, 0)),
                  pl.BlockSpec((1, tf2 // 2, d), lambda bi, i, e, j: (e, j, 0)),
                  pl.BlockSpec((1, d), lambda bi, i, e, j: (0, 0)),
                  pl.BlockSpec((1, d), lambda bi, i, e, j: (0, 0))],
        out_specs=pl.BlockSpec((1, tl, d), lambda bi, i, e, j: (bi, i, 0)),
        out_shape=jax.ShapeDtypeStruct((b, l, d), F32),
        scratch_shapes=[pltpu.VMEM((tl, d), BF16),
                        pltpu.VMEM((tl, d), F32),
                        pltpu.VMEM((tl, LANE), F32),
                        pltpu.VMEM((tl, LANE), F32),
                        pltpu.VMEM((N_EXPERTS, tl), F32),
                        pltpu.VMEM((2, tl, 1), F32),
                        pltpu.VMEM((compact_rows, d), BF16),
                        pltpu.VMEM((compact_rows, d), F32),
                        pltpu.SMEM((N_EXPERTS,), jnp.int32)],
        compiler_params=_cparams(("parallel", "parallel", "arbitrary", "arbitrary"), 56),
        name="moe",
    )(x, mod, w_router, w_gate_up, w_down, ln_g, ln_b)


def _reorder_w_in(w):
    d = w.shape[0]
    o_a, o_b = 0, 5 * MIX_W
    o_bgt = o_b + 4 * MIX_W
    o_cqkv = o_bgt + 16
    o_cz = o_cqkv + 3 * MIX_W
    o_cgt = o_cz + MIX_W
    o_merge = o_cgt + 16
    parts = [w[:, o_merge:o_merge + 3 * d], w[:, o_a:o_bgt], w[:, o_cqkv:o_cgt],
             w[:, o_bgt:o_bgt + 16], w[:, o_cgt:o_cgt + 16],
             jnp.zeros((d, N_PROJ - COL_GATES - 32), w.dtype)]
    return jnp.concatenate(parts, axis=1)


def _pad_row(vals, offset):
    row = jnp.zeros((1, LANE), F32)
    return lax.dynamic_update_slice(row, vals.reshape(1, -1).astype(F32), (0, offset))


def _tile(l, pref):
    return pref if l % pref == 0 else l


def kernel(x, c, ctx, c_ctx, w_ada, b_ada, w_in, conv_w, lb_raw, m_gate_bias, gdn_a_log, gdn_dt_bias, mix_norm,
           w_branch, w_out, ln_g, ln_b, ffn_w_gate, ffn_w_up, ffn_w_down, moe_router, moe_w_gate, moe_w_up,
           moe_w_down):
    depth, d_model = w_in.shape[0], w_in.shape[1]
    b, l_lat, _ = x.shape
    l_ctx = ctx.shape[1]
    alpha = float((2 * depth) ** 0.25)
    assert d_model == 1024 and w_in.shape[2] == 9 * MIX_W + 3 * MIX_W + MIX_W + 32 + 3 * d_model

    lower = _lower_bounds(lb_raw.astype(F32))
    n_rows = -(-(b + 1) // 8) * 8
    cvec = jnp.concatenate([c, c_ctx[None, :], jnp.zeros((n_rows - b - 1, d_model), F32)], axis=0)
    mods = _ada(cvec, w_ada, b_ada)

    zeros_state = jnp.zeros((b, 2, N_HEADS, HEAD_DIM, HEAD_DIM), F32)
    zeros_hgrn = jnp.zeros((b, N_HEADS, 2, HEAD_DIM, HEAD_DIM), F32)
    zeros_nm = jnp.zeros((b, 2, N_HEADS, 8, HEAD_DIM), F32)
    zeros_cmn = jnp.zeros((b, 2, N_HEADS, HEAD_DIM, 2 * HEAD_DIM), F32)

    pack = 1024 // l_ctx if (1024 % l_ctx == 0 and b % max(1024 // l_ctx, 1) == 0) else 1
    ctx_packed = (b // pack, pack * l_ctx, d_model)

    for l in range(depth):
        mod_l = mods[l, :b].reshape(b, 6, d_model)
        mod_c = jnp.broadcast_to(mods[l, b].reshape(1, 6, d_model), (b, 6, d_model))
        w_in_l = _reorder_w_in(w_in[l]).astype(BF16)
        lb_row = lower[l].reshape(1, MIX_W)
        gbias = _pad_row(m_gate_bias[l], 0)
        alog_row = _pad_row(gdn_a_log[l], 16)
        dtb_row = _pad_row(gdn_dt_bias[l], 16)
        gain = mix_norm[l]
        wbr = w_branch[l].astype(BF16)
        wo = w_out[l].astype(BF16)
        g0, b0 = ln_g[l, 0].reshape(1, -1), ln_b[l, 0].reshape(1, -1)
        g1, b1 = ln_g[l, 1].reshape(1, -1), ln_b[l, 1].reshape(1, -1)

        p_c = _in_proj(ctx.reshape(ctx_packed), mod_c[:ctx_packed[0]], w_in_l, ctx_packed[1])
        p_c = p_c.reshape(b, l_ctx, N_PROJ)
        p_l = _in_proj(x, mod_l, w_in_l, _tile(l_lat, 1024))
        qkv_c = _gdn_conv(p_c, conv_w[l], _tile(l_ctx, 256))
        qkv_l = _gdn_conv(p_l, conv_w[l], _tile(l_lat, 512))

        oa_c = _hgrn_scan(p_c, lb_row, zeros_hgrn)
        oa_l = _hgrn_scan(p_l, lb_row, oa_c[2])
        ob_c = _mlstm_scan(p_c, gbias, zeros_cmn, zeros_nm, colmajor=False)
        ob_l = _mlstm_scan(p_l, gbias, ob_c[2], ob_c[3], colmajor=True)
        oc_c = _gdn_scan(qkv_c, p_c, alog_row, dtb_row, zeros_state)
        oc_l = _gdn_scan(qkv_l, p_l, alog_row, dtb_row, oc_c[2])

        def channel_mixer(t, mod, tl):
            i = l // 2
            if l % 2 == 0:
                return _ffn(t, mod, ffn_w_gate[i].astype(BF16), ffn_w_up[i].astype(BF16),
                            ffn_w_down[i].astype(BF16), g1, b1, alpha, tl)
            w_r = jnp.concatenate([moe_router[i], jnp.zeros((d_model, LANE - N_EXPERTS), F32)], axis=1)
            return _moe(t, mod, w_r, _pack_gate_up(moe_w_gate[i], moe_w_up[i]), moe_w_down[i].astype(BF16),
                        g1, b1, alpha, tl)

        outs_l = (oa_l[0], oa_l[1], ob_l[0], ob_l[1], oc_l[0], oc_l[1])
        x = _merge(outs_l, p_l, x, mod_l, gain, wbr, wo, g0, b0, alpha, _tile(l_lat, 256))
        x = channel_mixer(x, mod_l, _tile(l_lat, 1024))
        if l < depth - 1:
            outs_c = (oa_c[0], oa_c[1], ob_c[0], ob_c[1], oc_c[0], oc_c[1])
            ctx = _merge(outs_c, p_c, ctx, mod_c, gain, wbr, wo, g0, b0, alpha, _tile(l_ctx, 256))
            ctx = channel_mixer(ctx.reshape(ctx_packed), mod_c[:ctx_packed[0]], ctx_packed[1]).reshape(ctx.shape)
    return x
```

```python
import functools

import jax
import jax.numpy as jnp
from jax import lax
from jax.experimental import pallas as pl
from jax.experimental.pallas import tpu as pltpu

F32 = jnp.float32
BF16 = jnp.bfloat16
HIGHEST = lax.Precision.HIGHEST

HEAD_DIM = 128
N_HEADS = 4
MIX_W = N_HEADS * HEAD_DIM
CHUNK = 64
GRID_W = 64
SUB = 16
MLSTM_GROUP = 2
GDN_BATCH_ROWS = 4
HGRN_GROUP = 8
HGRN_BATCH_ROWS = 2
N_EXPERTS = 8
MOE_ROW_BLOCK = 288
LN_EPS = 1e-5
RMS_EPS = 1e-6
L2_EPS = 1e-6
MASK_NEG = -1e30
LOG2_E = 1.4426950408889634
LANE = 128

LO_MERGE = 0
LO_CQKV = 3 * 1024
LO_AQ = LO_CQKV + 3 * MIX_W
LO_AI = LO_AQ + MIX_W
LO_AG = LO_AI + MIX_W
LO_BO = LO_AG + MIX_W
LO_CZ = LO_BO + MIX_W
N_LO = LO_CZ + MIX_W
LO_TILE = N_LO // 4
HI_AF = 0
HI_B = 2 * MIX_W
HI_GATES = HI_B + 3 * MIX_W
HI_TILE = (HI_GATES + 2 * LANE) // 2
N_HI = 2 * HI_TILE


def _cparams(sem, vmem_mb=None):
    kw = dict(dimension_semantics=sem)
    if vmem_mb is not None:
        kw["vmem_limit_bytes"] = vmem_mb << 20
    return pltpu.CompilerParams(**kw)


def _dot(a, b, precision=None):
    return jnp.dot(a, b, preferred_element_type=F32, precision=precision)


def _dot_nt(a, b, precision=None):
    return lax.dot_general(a, b, (((1,), (1,)), ((), ())), preferred_element_type=F32, precision=precision)


def _dot_tn(a, b):
    return lax.dot_general(a, b, (((0,), (0,)), ((), ())), preferred_element_type=F32)


def _ln(x):
    mu = jnp.mean(x, axis=-1, keepdims=True)
    xc = x - mu
    var = jnp.mean(xc * xc, axis=-1, keepdims=True)
    return xc * lax.rsqrt(var + LN_EPS)


def _sigmoid(x):
    return 1.0 / (1.0 + jnp.exp(-x))


def _silu(x):
    return x * _sigmoid(x)


def _softplus(x):
    return jnp.maximum(x, 0.0) + jnp.log(1.0 + jnp.exp(-jnp.abs(x)))


def _log_sigmoid(x):
    return -_softplus(-x)


def _order_mask(n, rev, strict=False):
    r = lax.broadcasted_iota(jnp.int32, (n, n), 0)
    c = lax.broadcasted_iota(jnp.int32, (n, n), 1)
    if rev:
        return (c > r) if strict else (c >= r)
    return (c < r) if strict else (c <= r)


def _cumsum_rows(x, rev):
    m = _order_mask(x.shape[0], rev).astype(F32)
    return _dot(m, x, precision=HIGHEST)


def _ada_kernel(c_ref, w_ref, b_ref, o_ref):
    s = _silu(c_ref[...])
    o_ref[0] = _dot(s, w_ref[0], precision=HIGHEST) + b_ref[0]


def _ada(cvec, w_ada, b_ada, tn=1536):
    depth, d, n6 = w_ada.shape
    rows = cvec.shape[0]
    return pl.pallas_call(
        _ada_kernel,
        grid=(depth, n6 // tn),
        in_specs=[pl.BlockSpec((rows, d), lambda l, n: (0, 0)),
                  pl.BlockSpec((1, d, tn), lambda l, n: (l, 0, n)),
                  pl.BlockSpec((1, 1, tn), lambda l, n: (l, 0, n))],
        out_specs=pl.BlockSpec((1, rows, tn), lambda l, n: (l, 0, n)),
        out_shape=jax.ShapeDtypeStruct((depth, rows, n6), F32),
        compiler_params=_cparams(("parallel", "parallel"), 40),
        name="ada",
    )(cvec, w_ada, b_ada.reshape(depth, 1, n6))


def _lower_bounds_kernel(lb_ref, o_ref):
    x = lb_ref[...]
    depth = x.shape[0]
    mx = jnp.max(x, axis=0, keepdims=True)
    e = jnp.exp(x - mx)
    p = e / jnp.sum(e, axis=0, keepdims=True)
    acc = jnp.zeros_like(p[0:1])
    for l in range(depth):
        o_ref[l:l + 1, :] = acc
        if l + 1 < depth:
            acc = acc + p[l + 1:l + 2]


def _lower_bounds(lb_raw):
    return pl.pallas_call(
        _lower_bounds_kernel,
        out_shape=jax.ShapeDtypeStruct(lb_raw.shape, F32),
        name="lower_bounds",
    )(lb_raw)


def _in_proj_kernel(x_ref, mod_ref, wlo_ref, whi_ref, lo_ref, hi_ref, h_ref, *, lo_tiles):
    n = pl.program_id(2)

    @pl.when(n == 0)
    def _():
        shift = mod_ref[0, 0:1, :]
        scale = mod_ref[0, 1:2, :]
        h_ref[...] = (_ln(x_ref[0]) * (1.0 + scale) + shift).astype(BF16)

    @pl.when(n < lo_tiles)
    def _():
        lo_ref[0] = _dot(h_ref[...], wlo_ref[...]).astype(BF16)

    @pl.when(n >= lo_tiles)
    def _():
        hi_ref[0] = _dot(h_ref[...], whi_ref[...])


def _in_proj(x, mod, w_lo, w_hi, tl):
    b, l, d = x.shape
    lo_tiles = N_LO // LO_TILE
    hi_tiles = N_HI // HI_TILE
    return pl.pallas_call(
        functools.partial(_in_proj_kernel, lo_tiles=lo_tiles),
        grid=(b, l // tl, lo_tiles + hi_tiles),
        in_specs=[pl.BlockSpec((1, tl, d), lambda bi, i, n: (bi, i, 0)),
                  pl.BlockSpec((1, 6, d), lambda bi, i, n: (bi, 0, 0)),
                  pl.BlockSpec((d, LO_TILE), lambda bi, i, n: (0, jnp.minimum(n, lo_tiles - 1))),
                  pl.BlockSpec((d, HI_TILE), lambda bi, i, n: (0, jnp.maximum(n - lo_tiles, 0)))],
        out_specs=[pl.BlockSpec((1, tl, LO_TILE), lambda bi, i, n: (bi, i, jnp.minimum(n, lo_tiles - 1))),
                   pl.BlockSpec((1, tl, HI_TILE), lambda bi, i, n: (bi, i, jnp.maximum(n - lo_tiles, 0)))],
        out_shape=[jax.ShapeDtypeStruct((b, l, N_LO), BF16), jax.ShapeDtypeStruct((b, l, N_HI), F32)],
        scratch_shapes=[pltpu.VMEM((tl, d), BF16)],
        compiler_params=_cparams(("parallel", "parallel", "arbitrary"), 48),
        name="in_proj",
    )(x, mod, w_lo, w_hi)


def _conv_kernel(x_ref, prev_ref, next_ref, w_ref, o_ref):
    i = pl.program_id(1)
    last = pl.num_programs(1) - 1
    x = x_ref[0].astype(F32)
    tl = x.shape[0]
    halo = prev_ref.shape[1]
    row = lax.broadcasted_iota(jnp.int32, x.shape, 0)
    before = jnp.where(i > 0, prev_ref[0].astype(F32)[halo - 1:halo, :], 0.0)
    after = jnp.where(i < last, next_ref[0].astype(F32)[0:1, :], 0.0)
    x_dn = jnp.where(row == 0, before, pltpu.roll(x, 1, axis=0))
    x_up = jnp.where(row == tl - 1, after, pltpu.roll(x, tl - 1, axis=0))
    y = _silu(x_dn * w_ref[0:1, :] + x * w_ref[1:2, :] + x_up * w_ref[2:3, :])
    for j in range(3 * N_HEADS):
        t = y[:, j * HEAD_DIM:(j + 1) * HEAD_DIM]
        if j < 2 * N_HEADS:
            t = t * lax.rsqrt(jnp.sum(t * t, axis=-1, keepdims=True) + L2_EPS)
            if j < N_HEADS:
                t = t * HEAD_DIM ** -0.5
        o_ref[0, :, j * HEAD_DIM:(j + 1) * HEAD_DIM] = t


def _gdn_conv(proj, conv_w, tl):
    b, l, _ = proj.shape
    w3 = 3 * MIX_W
    cb = LO_CQKV // w3
    halo = 16
    nbh = l // halo
    rh = tl // halo
    return pl.pallas_call(
        _conv_kernel,
        grid=(b, l // tl),
        in_specs=[pl.BlockSpec((1, tl, w3), lambda bi, i: (bi, i, cb)),
                  pl.BlockSpec((1, halo, w3), lambda bi, i: (bi, jnp.maximum(i * rh - 1, 0), cb)),
                  pl.BlockSpec((1, halo, w3), lambda bi, i: (bi, jnp.minimum((i + 1) * rh, nbh - 1), cb)),
                  pl.BlockSpec((3, w3), lambda bi, i: (0, 0))],
        out_specs=pl.BlockSpec((1, tl, w3), lambda bi, i: (bi, i, 0)),
        out_shape=jax.ShapeDtypeStruct((b, l, w3), F32),
        compiler_params=_cparams(("parallel", "parallel"), 40),
        name="gdn_conv",
    )(proj, proj, proj, conv_w)


def _hgrn_chunks(probs, states, emat):
    c = probs[0]["q"].shape[0]
    nb = c // SUB
    for p in probs:
        lb = p["lb"]
        sig = _sigmoid(p["fpre"])
        p["k"] = (1.0 - lb) * (1.0 - sig)
        p["g"] = _cumsum_rows(jnp.log(lb + (1.0 - lb) * sig), p["rev"])

    walls = []
    for p in probs:
        q = p["q"]
        g2 = p["g"] * LOG2_E
        lk2 = jnp.log(p["k"]) * LOG2_E
        rel = lk2 - g2
        rows = []
        half = SUB // 2
        dead = jnp.zeros((half, HEAD_DIM), F32)
        for j in range(nb):
            r0 = j * SUB
            pieces = []
            for s in range(SUB):
                skip = 0 if (s >= half and not p["rev"]) else 1 if (s < half and p["rev"]) else None
                parts = []
                for part in range(2):
                    if part == skip:
                        parts.append(dead)
                        continue
                    t0 = r0 + part * half
                    e2 = jnp.exp2(jnp.minimum(g2[t0:t0 + half] + rel[r0 + s:r0 + s + 1], lk2[r0 + s:r0 + s + 1]))
                    parts.append(e2 * q[t0:t0 + half])
                pieces.append(jnp.concatenate(parts, axis=0).astype(BF16))
            rows.append(jnp.concatenate(pieces, axis=1))
        walls.append(jnp.concatenate(rows, axis=0))
    a_diags = [_dot(w, emat) for w in walls]

    a_offs = []
    for p in probs:
        q, k, g, rev = p["q"], p["k"], p["g"], p["rev"]
        offs = []
        for i in range(nb):
            r0 = i * SUB
            if (not rev and i == 0) or (rev and i == nb - 1):
                offs.append(jnp.zeros((SUB, c), F32))
                continue
            bi = g[r0 + SUB:r0 + SUB + 1] if rev else g[r0 - 1:r0]
            lhs = (q[r0:r0 + SUB] * jnp.exp(g[r0:r0 + SUB] - bi)).astype(BF16)
            lo, hi = (r0 + SUB, c) if rev else (0, r0)
            live = k[lo:hi] * jnp.exp(bi - g[lo:hi])
            pad = jnp.zeros((c - (hi - lo), HEAD_DIM), F32)
            rhs = jnp.concatenate([pad, live] if rev else [live, pad], axis=0)
            offs.append(_dot_nt(lhs, rhs.astype(BF16)))
        a_offs.append(jnp.concatenate(offs, axis=0))

    r = lax.broadcasted_iota(jnp.int32, (c, c), 0)
    col = lax.broadcasted_iota(jnp.int32, (c, c), 1)
    shift = SUB.bit_length() - 1
    rb = lax.shift_right_logical(r, shift)
    cb = lax.shift_right_logical(col, shift)
    inters = [_dot_nt((p["q"] * jnp.exp(p["g"])).astype(BF16), s_t.astype(BF16)) for p, s_t in zip(probs, states)]
    outs, new_states = [], []
    for p, a_diag, a_off, inter, s_t in zip(probs, a_diags, a_offs, inters, states):
        rev, g = p["rev"], p["g"]
        order = (col >= r) if rev else (col <= r)
        off_side = (cb > rb) if rev else (cb < rb)
        a = jnp.where((rb == cb) & order, a_diag, jnp.where(off_side, a_off, 0.0))
        g_end = g[0:1] if rev else g[c - 1:c]
        vb = p["v"].astype(BF16)
        outs.append(_dot(a.astype(BF16), vb) + inter)
        kd = (p["k"] * jnp.exp(g_end - g)).astype(BF16)
        new_states.append(s_t * jnp.exp(g_end) + _dot_tn(vb, kd))
    return outs, new_states


def _hgrn_kernel(qf_ref, ff_ref, vf_ref, qb_ref, fb_ref, vb_ref, lb_ref, e_ref, s0_ref,
                 of_ref, ob_ref, sout_ref, s_ref):
    n = pl.program_id(1)

    @pl.when(n == 0)
    def _():
        s_ref[...] = s0_ref[...]

    probs, where = [], []
    for r in range(qf_ref.shape[0]):
        for d, (q_ref, f_ref, v_ref, o_ref) in enumerate(((qf_ref, ff_ref, vf_ref, of_ref),
                                                          (qb_ref, fb_ref, vb_ref, ob_ref))):
            for h in range(N_HEADS):
                sl = slice(h * HEAD_DIM, (h + 1) * HEAD_DIM)
                probs.append(dict(q=q_ref[r, :, sl].astype(F32), fpre=f_ref[r, :, sl],
                                  v=v_ref[r, :, sl].astype(F32), lb=lb_ref[:, sl], rev=d == 1))
                where.append((o_ref, sl, r, h, d))
    states = [s_ref[r, h, d] for _, _, r, h, d in where]
    outs, new_states = [], []
    for g0 in range(0, len(probs), HGRN_GROUP):
        o_g, s_g = _hgrn_chunks(probs[g0:g0 + HGRN_GROUP], states[g0:g0 + HGRN_GROUP], e_ref[...])
        outs += o_g
        new_states += s_g
    for (o_ref, sl, r, h, d), o, s_new in zip(where, outs, new_states):
        o_ref[r, :, sl] = o
        s_ref[r, h, d] = s_new

    @pl.when(n == pl.num_programs(1) - 1)
    def _():
        sout_ref[...] = s_ref[...]


def _hgrn_emat():
    r = jnp.arange(SUB * HEAD_DIM)[:, None] // HEAD_DIM
    c = jnp.arange(CHUNK)[None, :] % SUB
    return (r == c).astype(BF16)


def _hgrn_scan(p_lo, p_hi, lb, s0):
    b, l, _ = p_lo.shape
    nc = l // CHUNK
    rows = HGRN_BATCH_ROWS if b % HGRN_BATCH_ROWS == 0 else 1
    q_col, i_col, f_col = LO_AQ // MIX_W, LO_AI // MIX_W, HI_AF // MIX_W

    def fwd(col):
        return pl.BlockSpec((rows, CHUNK, MIX_W), lambda bi, n: (bi, n, col))

    def bwd(col):
        return pl.BlockSpec((rows, CHUNK, MIX_W), lambda bi, n: (bi, nc - 1 - n, col))

    st_spec = pl.BlockSpec((rows, N_HEADS, 2, HEAD_DIM, HEAD_DIM), lambda bi, n: (bi, 0, 0, 0, 0))
    return pl.pallas_call(
        _hgrn_kernel,
        grid=(b // rows, nc),
        in_specs=[fwd(q_col), fwd(f_col), fwd(i_col), bwd(q_col), bwd(f_col + 1), bwd(i_col),
                  pl.BlockSpec((1, MIX_W), lambda bi, n: (0, 0)),
                  pl.BlockSpec((SUB * HEAD_DIM, CHUNK), lambda bi, n: (0, 0)),
                  st_spec],
        out_specs=[pl.BlockSpec((rows, CHUNK, MIX_W), lambda bi, n: (bi, n, 0)),
                   pl.BlockSpec((rows, CHUNK, MIX_W), lambda bi, n: (bi, nc - 1 - n, 0)),
                   st_spec],
        out_shape=[jax.ShapeDtypeStruct((b, l, MIX_W), F32),
                   jax.ShapeDtypeStruct((b, l, MIX_W), F32),
                   jax.ShapeDtypeStruct(s0.shape, F32)],
        scratch_shapes=[pltpu.VMEM((rows, N_HEADS, 2, HEAD_DIM, HEAD_DIM), F32)],
        compiler_params=_cparams(("parallel", "arbitrary"), 40),
        name="hgrn_scan",
    )(p_lo, p_hi, p_lo, p_lo, p_hi, p_lo, lb, _hgrn_emat(), s0)


def _mlstm_chunks(probs, states):
    c = probs[0]["q"].shape[0]
    dv = probs[0]["v"].shape[1]
    ones = jnp.ones((c, dv), BF16)
    for p, (cmn, m) in zip(probs, states):
        tri = _order_mask(c, p["rev"])
        log_d = jnp.where(tri, p["bb_c"] - p["bb_r"] + p["ic_r"], MASK_NEG)
        log_inter = p["bb_c"] + m
        m_t = jnp.maximum(jnp.max(log_d, axis=1, keepdims=True), log_inter)
        p["m_t"] = m_t
        p["dmat"] = jnp.where(tri, jnp.exp(log_d - m_t), 0.0)
        p["inter"] = jnp.exp(log_inter - m_t)
        p["qb"], p["kb"] = p["q"].astype(BF16), p["k"].astype(BF16)
        p["v1"] = jnp.concatenate([p["v"].astype(BF16), ones], axis=1)
    scores = [_dot_nt(p["qb"], p["kb"]) * p["dmat"] for p in probs]
    carried = [_dot(p["qb"], cmn.astype(BF16)) for p, (cmn, _) in zip(probs, states)]
    local = [_dot(s.astype(BF16), p["v1"]) for p, s in zip(probs, scores)]
    outs, new_states = [], []
    for p, car, loc, (cmn, m) in zip(probs, carried, local, states):
        m_t = p["m_t"]
        both = loc + p["inter"] * car
        outs.append(both[:, :dv] / jnp.maximum(jnp.abs(both[:, dv:]), jnp.exp(-m_t)))
        m_new = m_t[0:1] if p["rev"] else m_t[c - 1:c]
        bb_end = p["bb_c"][0:1] if p["rev"] else p["bb_c"][c - 1:c]
        decay = jnp.exp(bb_end + m - m_new)
        kw = p["k"] * jnp.exp(bb_end - p["bb_c"] + p["ic_c"] - m_new)
        new_states.append((decay * cmn + _dot_tn(kw.astype(BF16), p["v1"]), m_new))
    return outs, new_states


def _column_view(ref):
    rows, group, width = ref.shape[-3:]
    return ref.reshape(rows * group, width), group


def _chunk_rows(j):
    start = j * CHUNK
    return pl.ds(start if isinstance(start, int) else pl.multiple_of(start, CHUNK), CHUNK)


def _load_chunk(ref, j, colmajor):
    if colmajor:
        flat, group = _column_view(ref)
        return flat[pl.ds(j, CHUNK, stride=group), :]
    return ref[0, _chunk_rows(j), :]


def _store_chunk(ref, j, val, colmajor):
    if colmajor:
        flat, group = _column_view(ref)
        flat[pl.ds(j, CHUNK, stride=group), :] = val
    else:
        ref[0, _chunk_rows(j), :] = val


def _columns_as_rows(x, first):
    row = lax.broadcasted_iota(jnp.int32, (8, LANE), 0)
    lane = lax.broadcasted_iota(jnp.int32, (8, LANE), 1)
    sel = jnp.where(lane == row + first, 1.0, 0.0).astype(BF16)
    hi = x.astype(BF16)
    rest = x - hi.astype(F32)
    mid = rest.astype(BF16)
    lo = (rest - mid.astype(F32)).astype(BF16)
    return _dot_nt(sel, hi) + (_dot_nt(sel, mid) + _dot_nt(sel, lo))


def _mlstm_kernel(*refs, colmajor, chunks):
    per_dir = 3 * N_HEADS + 1
    fwd, bwd = refs[:per_dir], refs[per_dir:2 * per_dir]
    bias_ref, cm0_ref, nm0_ref, of_ref, ob_ref, cmout_ref, nmout_ref, cm_ref, nm_ref = refs[2 * per_dir:]
    n = pl.program_id(1)

    @pl.when(n == 0)
    def _():
        cm_ref[...] = cm0_ref[0]
        nm_ref[...] = nm0_ref[0]

    def step(i, carry):
        probs, where = [], []
        for d, (in_refs, o_ref) in enumerate(((fwd, of_ref), (bwd, ob_ref))):
            rev = d == 1
            j = chunks - 1 - i if rev else i
            gates = _load_chunk(in_refs[-1], j, colmajor) + bias_ref[...]
            cs = _cumsum_rows(_log_sigmoid(gates), rev)
            cs_rows = _columns_as_rows(cs, 8 + 4 * d)
            gate_rows = _columns_as_rows(gates, 4 * d)
            for h in range(N_HEADS):
                ci = 4 * d + h
                cf = 8 + 4 * d + h
                probs.append(dict(
                    q=_load_chunk(in_refs[h], j, colmajor),
                    k=_load_chunk(in_refs[N_HEADS + h], j, colmajor) * HEAD_DIM ** -0.5,
                    v=_load_chunk(in_refs[2 * N_HEADS + h], j, colmajor),
                    bb_c=cs[:, cf:cf + 1], bb_r=cs_rows[h:h + 1, :],
                    ic_c=gates[:, ci:ci + 1], ic_r=gate_rows[h:h + 1, :], rev=rev))
                where.append((o_ref, j, d, h))
        states = [(cm_ref[d, h], nm_ref[d, h, 0:1, 0:1]) for _, _, d, h in where]
        outs, new_states = [], []
        for g0 in range(0, len(probs), MLSTM_GROUP):
            o_g, s_g = _mlstm_chunks(probs[g0:g0 + MLSTM_GROUP], states[g0:g0 + MLSTM_GROUP])
            outs += o_g
            new_states += s_g
        for (o_ref, j, d, h), o, (cmn_new, m_new) in zip(where, outs, new_states):
            _store_chunk(o_ref.at[h], j, o, colmajor)
            cm_ref[d, h] = cmn_new
            nm_ref[d, h, 0:1, :] = jnp.broadcast_to(m_new, (1, HEAD_DIM))
        return carry

    lax.fori_loop(0, chunks, step, 0)

    @pl.when(n == pl.num_programs(1) - 1)
    def _():
        cmout_ref[0] = cm_ref[...]
        nmout_ref[0] = nm_ref[...]


def _mlstm_scan(proj, gbias, cm0, nm0, colmajor):
    b, l, n_proj = proj.shape
    cb = HI_B // LANE
    cg = HI_GATES // LANE
    if colmajor:
        chunks = 8
        assert l == CHUNK * GRID_W and GRID_W % chunks == 0
        nb = GRID_W // chunks
        src = proj.reshape(b, CHUNK, GRID_W, n_proj)
        in_blk = (1, CHUNK, chunks, LANE)
        out_blk = (N_HEADS, 1, CHUNK, chunks, LANE)
        out_view = (N_HEADS, b, CHUNK, GRID_W, LANE)

        def in_map(col, flip):
            return lambda bi, n: (bi, 0, nb - 1 - n if flip else n, col)

        def out_map(flip):
            return lambda bi, n: (0, bi, 0, nb - 1 - n if flip else n, 0)
    else:
        chunks = min(l // CHUNK, 4)
        nb = l // (chunks * CHUNK)
        src = proj
        in_blk = (1, chunks * CHUNK, LANE)
        out_blk = (N_HEADS, 1, chunks * CHUNK, LANE)
        out_view = (N_HEADS, b, l, LANE)

        def in_map(col, flip):
            return lambda bi, n: (bi, nb - 1 - n if flip else n, col)

        def out_map(flip):
            return lambda bi, n: (0, bi, nb - 1 - n if flip else n, 0)

    def specs(flip):
        cols = [cb + j for j in range(3 * N_HEADS)] + [cg]
        return [pl.BlockSpec(in_blk, in_map(col, flip)) for col in cols]

    cm_spec = pl.BlockSpec((1, 2, N_HEADS, HEAD_DIM, 2 * HEAD_DIM), lambda bi, n: (bi, 0, 0, 0, 0))
    nm_spec = pl.BlockSpec((1, 2, N_HEADS, 8, HEAD_DIM), lambda bi, n: (bi, 0, 0, 0, 0))
    n_in = 2 * (3 * N_HEADS + 1)
    o_f, o_b, cm, nm = pl.pallas_call(
        functools.partial(_mlstm_kernel, colmajor=colmajor, chunks=chunks),
        grid=(b, nb),
        in_specs=specs(False) + specs(True) + [pl.BlockSpec((1, LANE), lambda bi, n: (0, 0)), cm_spec, nm_spec],
        out_specs=[pl.BlockSpec(out_blk, out_map(False)), pl.BlockSpec(out_blk, out_map(True)), cm_spec, nm_spec],
        out_shape=[jax.ShapeDtypeStruct(out_view, F32), jax.ShapeDtypeStruct(out_view, F32),
                   jax.ShapeDtypeStruct(cm0.shape, F32), jax.ShapeDtypeStruct(nm0.shape, F32)],
        scratch_shapes=[pltpu.VMEM((2, N_HEADS, HEAD_DIM, 2 * HEAD_DIM), F32),
                        pltpu.VMEM((2, N_HEADS, 8, HEAD_DIM), F32)],
        compiler_params=_cparams(("parallel", "arbitrary"), 40),
        name="mlstm_scan",
    )(*([src] * n_in), gbias, cm0, nm0)
    head_major = (N_HEADS, b, l, LANE)
    return o_f.reshape(head_major), o_b.reshape(head_major), cm, nm


def _split_bf16(x):
    hi = x.astype(BF16)
    return hi, (x - hi.astype(F32)).astype(BF16)


def _dot3(a, b, nt=False):
    f = _dot_nt if nt else _dot
    return f(a[0], b[0]) + (f(a[0], b[1]) + f(a[1], b[0]))


def _inv_unit_triangular(a_list):
    n = a_list[0].shape[0]
    eye = (lax.broadcasted_iota(jnp.int32, (n, n), 0) == lax.broadcasted_iota(jnp.int32, (n, n), 1)).astype(F32)
    xs = [eye - a for a in a_list]
    splits = [_split_bf16(a) for a in a_list]
    ps = [_dot3(s, s) for s in splits]
    steps = (n - 1).bit_length() - 1
    for step in range(steps):
        last = step == steps - 1
        p_split = [_split_bf16(p) for p in ps]
        lhs = xs if last else [jnp.concatenate([x, p], axis=0) for x, p in zip(xs, ps)]
        prods = [_dot3(_split_bf16(l), s) for l, s in zip(lhs, p_split)]
        xs = [x + pr[:n] for x, pr in zip(xs, prods)]
        if not last:
            ps = [pr[n:] for pr in prods]
    return xs


def _gdn_chunks(probs, states):
    c = probs[0]["q"].shape[0]
    a_list, ldecs = [], []
    for p in probs:
        tri = _order_mask(c, p["rev"])
        strict = _order_mask(c, p["rev"], strict=True)
        ldec = jnp.where(tri, jnp.exp(jnp.minimum(p["gam_c"] - p["gam_r"], 0.0)), 0.0)
        p["kbeta"] = p["k"] * p["beta"]
        kk = _dot_nt(p["kbeta"].astype(BF16), p["k"].astype(BF16))
        a_list.append(jnp.where(strict, kk * ldec, 0.0))
        ldecs.append(ldec)
    t_invs = _inv_unit_triangular(a_list)
    sols = []
    for p, t_inv in zip(probs, t_invs):
        p["eg"] = jnp.exp(p["gam_c"])
        rhs = jnp.concatenate([p["v"] * p["beta"], p["kbeta"] * p["eg"]], axis=1)
        sols.append(_dot3(_split_bf16(t_inv), _split_bf16(rhs)))
    qks = [_dot_nt(p["q"].astype(BF16), p["k"].astype(BF16)) * ldec for p, ldec in zip(probs, ldecs)]
    inter = [_dot(jnp.concatenate([sol[:, HEAD_DIM:], p["q"] * p["eg"]], axis=0).astype(BF16), s.astype(BF16))
             for p, sol, s in zip(probs, sols, states)]
    outs, new_states = [], []
    for p, sol, qk, it, s in zip(probs, sols, qks, inter, states):
        g_end = p["gam_c"][0:1] if p["rev"] else p["gam_c"][c - 1:c]
        v_new = (sol[:, :HEAD_DIM] - it[:c]).astype(BF16)
        kg = p["k"] * jnp.exp(g_end - p["gam_c"])
        outs.append(it[c:] + _dot(qk.astype(BF16), v_new))
        new_states.append(jnp.exp(g_end) * s + _dot_tn(kg.astype(BF16), v_new))
    return outs, new_states


def _gdn_kernel(xf_ref, gf_ref, xb_ref, gb_ref, alog_ref, dtb_ref, s0_ref,
                of_ref, ob_ref, sout_ref, s_ref):
    n = pl.program_id(1)

    @pl.when(n == 0)
    def _():
        s_ref[...] = s0_ref[...]

    neg_a = -jnp.exp(alog_ref[...])
    probs, where = [], []
    for r in range(xf_ref.shape[0]):
        for d, (x_ref, g_ref, o_ref) in enumerate(((xf_ref, gf_ref, of_ref), (xb_ref, gb_ref, ob_ref))):
            rev = d == 1
            gates = g_ref[r]
            cs = _cumsum_rows(neg_a * _softplus(gates + dtb_ref[...]), rev)
            cs_t = cs.T
            betas = _sigmoid(gates)
            for h in range(N_HEADS):
                ca = 16 + 4 * d + h
                cbeta = 24 + 4 * d + h
                probs.append(dict(
                    q=x_ref[r, :, h * HEAD_DIM:(h + 1) * HEAD_DIM],
                    k=x_ref[r, :, MIX_W + h * HEAD_DIM:MIX_W + (h + 1) * HEAD_DIM],
                    v=x_ref[r, :, 2 * MIX_W + h * HEAD_DIM:2 * MIX_W + (h + 1) * HEAD_DIM],
                    gam_c=cs[:, ca:ca + 1], gam_r=cs_t[ca:ca + 1, :], beta=betas[:, cbeta:cbeta + 1], rev=rev))
                where.append((o_ref, r, d, h))
    outs, new_states = _gdn_chunks(probs, [s_ref[r, d, h] for _, r, d, h in where])
    for (o_ref, r, d, h), o, s_new in zip(where, outs, new_states):
        o_ref[r, :, h * HEAD_DIM:(h + 1) * HEAD_DIM] = o
        s_ref[r, d, h] = s_new

    @pl.when(n == pl.num_programs(1) - 1)
    def _():
        sout_ref[...] = s_ref[...]


def _gdn_scan(qkv, proj, alog_row, dtb_row, s0):
    b, l, w3 = qkv.shape
    nc = l // CHUNK
    cg = HI_GATES // LANE
    rows = GDN_BATCH_ROWS if b % GDN_BATCH_ROWS == 0 else 1
    st_spec = pl.BlockSpec((rows, 2, N_HEADS, HEAD_DIM, HEAD_DIM), lambda bi, n: (bi, 0, 0, 0, 0))
    row_spec = pl.BlockSpec((1, LANE), lambda bi, n: (0, 0))
    return pl.pallas_call(
        _gdn_kernel,
        grid=(b // rows, nc),
        in_specs=[pl.BlockSpec((rows, CHUNK, w3), lambda bi, n: (bi, n, 0)),
                  pl.BlockSpec((rows, CHUNK, LANE), lambda bi, n: (bi, n, cg)),
                  pl.BlockSpec((rows, CHUNK, w3), lambda bi, n: (bi, nc - 1 - n, 0)),
                  pl.BlockSpec((rows, CHUNK, LANE), lambda bi, n: (bi, nc - 1 - n, cg)),
                  row_spec, row_spec, st_spec],
        out_specs=[pl.BlockSpec((rows, CHUNK, MIX_W), lambda bi, n: (bi, n, 0)),
                   pl.BlockSpec((rows, CHUNK, MIX_W), lambda bi, n: (bi, nc - 1 - n, 0)),
                   st_spec],
        out_shape=[jax.ShapeDtypeStruct((b, l, MIX_W), F32), jax.ShapeDtypeStruct((b, l, MIX_W), F32),
                   jax.ShapeDtypeStruct(s0.shape, F32)],
        scratch_shapes=[pltpu.VMEM((rows, 2, N_HEADS, HEAD_DIM, HEAD_DIM), F32)],
        compiler_params=_cparams(("parallel", "arbitrary"), 40),
        name="gdn_scan",
    )(qkv, proj, qkv, proj, alog_row, dtb_row, s0)


def _merge_kernel(oaf, oab, obf, obb, ocf, ocb, ag, bo, cz, mg, x_ref, mod_ref, gain_ref, wbr_ref, wout_ref,
                  lng_ref, lnb_ref, out_ref, *, alpha):
    d_model = x_ref.shape[-1]
    branches = ((oaf, oab, ag, _silu), (obf, obb, bo, _sigmoid), (ocf, ocb, cz, _silu))
    mixed = None
    for nbr, (of_ref, ob_ref, gate_ref, act) in enumerate(branches):
        parts = []
        for h in range(N_HEADS):
            if len(of_ref.shape) == 4:
                oh = of_ref[h, 0] + ob_ref[h, 0]
            else:
                oh = of_ref[0, :, h * HEAD_DIM:(h + 1) * HEAD_DIM] + ob_ref[0, :, h * HEAD_DIM:(h + 1) * HEAD_DIM]
            parts.append(oh * lax.rsqrt(jnp.mean(oh * oh, axis=-1, keepdims=True) + RMS_EPS))
        y = jnp.concatenate(parts, axis=1) * gain_ref[nbr:nbr + 1, :] * act(gate_ref[0].astype(F32))
        gate = _sigmoid(mg[0, :, nbr * d_model:(nbr + 1) * d_model].astype(F32))
        term = gate * _dot(y.astype(BF16), wbr_ref[nbr])
        mixed = term if mixed is None else mixed + term
    sub = _dot(mixed.astype(BF16), wout_ref[...])
    z = alpha * x_ref[0] + mod_ref[0, 2:3, :] * sub
    out_ref[0] = _ln(z) * lng_ref[...] + lnb_ref[...]


def _merge(outs, proj, x, mod, gain, w_branch, w_out, ln_g, ln_b, alpha, tl):
    b, l, d = x.shape
    u = MIX_W

    def tok(width, col):
        return pl.BlockSpec((1, tl, width), lambda bi, i: (bi, i, col))

    def const(shape):
        return pl.BlockSpec(shape, lambda bi, i: (0,) * len(shape))

    heads = pl.BlockSpec((N_HEADS, 1, tl, LANE), lambda bi, i: (0, bi, i, 0))
    return pl.pallas_call(
        functools.partial(_merge_kernel, alpha=alpha),
        grid=(b, l // tl),
        in_specs=[tok(u, 0), tok(u, 0), heads, heads, tok(u, 0), tok(u, 0),
            tok(u, LO_AG // u), tok(u, LO_BO // u), tok(u, LO_CZ // u),
            tok(3 * d, LO_MERGE // (3 * d)), tok(d, 0),
            pl.BlockSpec((1, 6, d), lambda bi, i: (bi, 0, 0)),
            const((3, u)), const((3, u, d)), const((d, d)), const((1, d)), const((1, d))],
        out_specs=tok(d, 0),
        out_shape=jax.ShapeDtypeStruct((b, l, d), F32),
        compiler_params=_cparams(("parallel", "parallel"), 48),
        name="merge",
    )(*outs, proj, proj, proj, proj, x, mod, gain, w_branch, w_out, ln_g, ln_b)


def _ffn_kernel(x_ref, mod_ref, wg_ref, wu_ref, wd_ref, lng_ref, lnb_ref, out_ref, h_ref, acc_ref, *, alpha):
    f = pl.program_id(2)

    @pl.when(f == 0)
    def _():
        h_ref[...] = (_ln(x_ref[0]) * (1.0 + mod_ref[0, 4:5, :]) + mod_ref[0, 3:4, :]).astype(BF16)
        acc_ref[...] = jnp.zeros_like(acc_ref)

    h = h_ref[...]
    act = _silu(_dot(h, wg_ref[...])) * _dot(h, wu_ref[...])
    acc_ref[...] += _dot(act.astype(BF16), wd_ref[...])

    @pl.when(f == pl.num_programs(2) - 1)
    def _():
        z = alpha * x_ref[0] + mod_ref[0, 5:6, :] * acc_ref[...]
        out_ref[0] = _ln(z) * lng_ref[...] + lnb_ref[...]


def _ffn(x, mod, w_gate, w_up, w_down, ln_g, ln_b, alpha, tl, tf=512):
    b, l, d = x.shape
    f = w_gate.shape[1]
    return pl.pallas_call(
        functools.partial(_ffn_kernel, alpha=alpha),
        grid=(b, l // tl, f // tf),
        in_specs=[pl.BlockSpec((1, tl, d), lambda bi, i, j: (bi, i, 0)),
                  pl.BlockSpec((1, 6, d), lambda bi, i, j: (bi, 0, 0)),
                  pl.BlockSpec((d, tf), lambda bi, i, j: (0, j)),
                  pl.BlockSpec((d, tf), lambda bi, i, j: (0, j)),
                  pl.BlockSpec((tf, d), lambda bi, i, j: (j, 0)),
                  pl.BlockSpec((1, d), lambda bi, i, j: (0, 0)),
                  pl.BlockSpec((1, d), lambda bi, i, j: (0, 0))],
        out_specs=pl.BlockSpec((1, tl, d), lambda bi, i, j: (bi, i, 0)),
        out_shape=jax.ShapeDtypeStruct((b, l, d), F32),
        scratch_shapes=[pltpu.VMEM((tl, d), BF16), pltpu.VMEM((tl, d), F32)],
        compiler_params=_cparams(("parallel", "parallel", "arbitrary"), 48),
        name="ffn",
    )(x, mod, w_gate, w_up, w_down, ln_g, ln_b)


def _top2_combine(logits):
    lane = lax.broadcasted_iota(jnp.int32, logits.shape, 1).astype(F32)
    neg = -jnp.inf
    lg = jnp.where(lane < N_EXPERTS, logits, neg)
    m1 = jnp.max(lg, axis=1, keepdims=True)
    i1 = jnp.min(jnp.where(lg == m1, lane, float(LANE)), axis=1, keepdims=True)
    lg2 = jnp.where(lane == i1, neg, lg)
    m2 = jnp.max(lg2, axis=1, keepdims=True)
    i2 = jnp.min(jnp.where(lg2 == m2, lane, float(LANE)), axis=1, keepdims=True)
    e2 = jnp.exp(m2 - m1)
    p1 = 1.0 / (1.0 + e2)
    chosen = (lane == i1) | (lane == i2)
    return jnp.where(lane == i1, p1, jnp.where(lane == i2, e2 * p1, 0.0)), chosen


def _moe_kernel(x_ref, mod_ref, wr_ref, wgu_ref, wd_ref, lng_ref, lnb_ref, out_ref,
                h_ref, acc_ref, comb_ref, pos_ref, post_ref, col_ref, xc_ref, yc_ref, cnt_ref, *, alpha, rb):
    e = pl.program_id(2)
    f = pl.program_id(3)
    tl, d = h_ref.shape

    @pl.when((e == 0) & (f == 0))
    def _():
        h = _ln(x_ref[0]) * (1.0 + mod_ref[0, 4:5, :]) + mod_ref[0, 3:4, :]
        h_ref[...] = h.astype(BF16)
        comb, chosen = _top2_combine(_dot(h, wr_ref[...], precision=HIGHEST))
        comb_ref[...] = comb
        blk = min(tl, LANE)
        before = jnp.where(_order_mask(blk, False, strict=True), 1.0, 0.0).astype(BF16)
        routed = jnp.where(chosen, 1.0, 0.0)
        total = jnp.zeros((1, LANE), F32)
        ranks = []
        for r0 in range(0, tl, blk):
            part = routed[r0:r0 + blk]
            ranks.append(_dot(before, part.astype(BF16)) + total)
            total = total + jnp.sum(part, axis=0, keepdims=True)
        rank = jnp.concatenate(ranks, axis=0)
        pos = jnp.where(chosen, rank, -1.0)
        pos_ref[...] = pos
        pos_t = pos.T[0:N_EXPERTS, :]
        post_ref[...] = pos_t
        for ex in range(N_EXPERTS):
            cnt_ref[ex] = jnp.sum(jnp.where(pos_t[ex:ex + 1, :] >= 0.0, 1.0, 0.0)).astype(jnp.int32)
        acc_ref[...] = jnp.zeros_like(acc_ref)

    nblk = (cnt_ref[e] + (rb - 1)) // rb

    def rows(r):
        return pl.ds(pl.multiple_of(r * rb, rb), rb)

    @pl.when(f == 0)
    def _():
        lane = lax.broadcasted_iota(jnp.int32, (tl, LANE), 1)
        col_ref[0] = jnp.sum(jnp.where(lane == e, pos_ref[...], 0.0), axis=1, keepdims=True)
        col_ref[1] = jnp.sum(jnp.where(lane == e, comb_ref[...], 0.0), axis=1, keepdims=True)
        pos_row = post_ref[pl.ds(e, 1), :]

        def gather(r, carry):
            slot = (lax.broadcasted_iota(jnp.int32, (rb, tl), 0) + r * rb).astype(F32)
            sel = jnp.where(pos_row == slot, 1.0, 0.0).astype(BF16)
            xc_ref[rows(r), :] = _dot(sel, h_ref[...]).astype(BF16)
            yc_ref[rows(r), :] = jnp.zeros((rb, d), F32)
            return carry

        lax.fori_loop(0, nblk, gather, 0)

    def expert(r, carry):
        xb = xc_ref[rows(r), :]
        gu = _dot(xb, wgu_ref[0, 0])
        half = gu.shape[1] // 2
        act = _silu(gu[:, :half]) * gu[:, half:]
        yc_ref[rows(r), :] += _dot(act.astype(BF16), wd_ref[0])
        return carry

    lax.fori_loop(0, nblk, expert, 0)

    @pl.when(f == pl.num_programs(3) - 1)
    def _():
        pos_col = col_ref[0]
        weight = col_ref[1]

        def scatter(r, carry):
            slot = (lax.broadcasted_iota(jnp.int32, (tl, rb), 1) + r * rb).astype(F32)
            sel_t = jnp.where(pos_col == slot, 1.0, 0.0).astype(BF16)
            acc_ref[...] += weight * _dot(sel_t, yc_ref[rows(r), :].astype(BF16))
            return carry

        lax.fori_loop(0, nblk, scatter, 0)

    @pl.when((e == pl.num_programs(2) - 1) & (f == pl.num_programs(3) - 1))
    def _():
        z = alpha * x_ref[0] + mod_ref[0, 5:6, :] * acc_ref[...]
        out_ref[0] = _ln(z) * lng_ref[...] + lnb_ref[...]


def _pack_gate_up(w_gate, w_up):
    e, d, f = w_gate.shape
    tf = next(t for t in (896, 512, 256, 128) if f % t == 0)

    def tiles(w):
        return w.astype(BF16).reshape(e, d, f // tf, tf).transpose(0, 2, 1, 3)

    return jnp.concatenate([tiles(w_gate), tiles(w_up)], axis=-1)


def _moe(x, mod, w_router, w_gate_up, w_down, ln_g, ln_b, alpha, tl):
    b, l, d = x.shape
    ne, nf, _, tf2 = w_gate_up.shape
    rb = min(MOE_ROW_BLOCK, tl)
    compact_rows = -(-tl // rb) * rb
    return pl.pallas_call(
        functools.partial(_moe_kernel, alpha=alpha, rb=rb),
        grid=(b, l // tl, ne, nf),
        in_specs=[pl.BlockSpec((1, tl, d), lambda bi, i, e, j: (bi, i, 0)),
                  pl.BlockSpec((1, 6, d), lambda bi, i, e, j: (bi, 0, 0)),
                  pl.BlockSpec((d, LANE), lambda bi, i, e, j: (0, 0)),
                  pl.BlockSpec((1, 1, d, tf2), lambda bi, i, e, j: (e, j, 0, 0)),
                  pl.BlockSpec((1, tf2 // 2, d), lambda bi, i, e, j: (e, j, 0)),
                  pl.BlockSpec((1, d), lambda bi, i, e, j: (0, 0)),
                  pl.BlockSpec((1, d), lambda bi, i, e, j: (0, 0))],
        out_specs=pl.BlockSpec((1, tl, d), lambda bi, i, e, j: (bi, i, 0)),
        out_shape=jax.ShapeDtypeStruct((b, l, d), F32),
        scratch_shapes=[pltpu.VMEM((tl, d), BF16),
                        pltpu.VMEM((tl, d), F32),
                        pltpu.VMEM((tl, LANE), F32),
                        pltpu.VMEM((tl, LANE), F32),
                        pltpu.VMEM((N_EXPERTS, tl), F32),
                        pltpu.VMEM((2, tl, 1), F32),
                        pltpu.VMEM((compact_rows, d), BF16),
                        pltpu.VMEM((compact_rows, d), F32),
                        pltpu.SMEM((N_EXPERTS,), jnp.int32)],
        compiler_params=_cparams(("parallel", "parallel", "arbitrary", "arbitrary"), 56),
        name="moe",
    )(x, mod, w_router, w_gate_up, w_down, ln_g, ln_b)


def _reorder_w_in(w):
    d = w.shape[0]
    u = MIX_W
    o_a, o_b = 0, 5 * u
    o_bgt = o_b + 4 * u
    o_cqkv = o_bgt + 16
    o_cz = o_cqkv + 3 * u
    o_cgt = o_cz + u
    o_merge = o_cgt + 16
    lo = [w[:, o_merge:o_merge + 3 * d], w[:, o_cqkv:o_cz], w[:, o_a:o_a + u], w[:, o_a + 3 * u:o_a + 5 * u],
          w[:, o_b + 3 * u:o_bgt], w[:, o_cz:o_cgt]]
    hi = [w[:, o_a + u:o_a + 3 * u], w[:, o_b:o_b + 3 * u], w[:, o_bgt:o_bgt + 16], w[:, o_cgt:o_cgt + 16],
          jnp.zeros((d, N_HI - HI_GATES - 32), w.dtype)]
    return jnp.concatenate(lo, axis=1).astype(BF16), jnp.concatenate(hi, axis=1).astype(BF16)


def _pad_row(vals, offset):
    row = jnp.zeros((1, LANE), F32)
    return lax.dynamic_update_slice(row, vals.reshape(1, -1).astype(F32), (0, offset))


def _tile(l, pref):
    return pref if l % pref == 0 else l


def kernel(x, c, ctx, c_ctx, w_ada, b_ada, w_in, conv_w, lb_raw, m_gate_bias, gdn_a_log, gdn_dt_bias, mix_norm,
           w_branch, w_out, ln_g, ln_b, ffn_w_gate, ffn_w_up, ffn_w_down, moe_router, moe_w_gate, moe_w_up,
           moe_w_down):
    depth, d_model = w_in.shape[0], w_in.shape[1]
    b, l_lat, _ = x.shape
    l_ctx = ctx.shape[1]
    alpha = float((2 * depth) ** 0.25)
    assert d_model == 1024 and w_in.shape[2] == 9 * MIX_W + 3 * MIX_W + MIX_W + 32 + 3 * d_model

    lower = _lower_bounds(lb_raw.astype(F32))
    n_rows = -(-(b + 1) // 8) * 8
    cvec = jnp.concatenate([c, c_ctx[None, :], jnp.zeros((n_rows - b - 1, d_model), F32)], axis=0)
    mods = _ada(cvec, w_ada, b_ada)

    zeros_state = jnp.zeros((b, 2, N_HEADS, HEAD_DIM, HEAD_DIM), F32)
    zeros_hgrn = jnp.zeros((b, N_HEADS, 2, HEAD_DIM, HEAD_DIM), F32)
    zeros_nm = jnp.zeros((b, 2, N_HEADS, 8, HEAD_DIM), F32)
    zeros_cmn = jnp.zeros((b, 2, N_HEADS, HEAD_DIM, 2 * HEAD_DIM), F32)

    pack = 1024 // l_ctx if (1024 % l_ctx == 0 and b % max(1024 // l_ctx, 1) == 0) else 1
    ctx_packed = (b // pack, pack * l_ctx, d_model)

    for l in range(depth):
        mod_l = mods[l, :b].reshape(b, 6, d_model)
        mod_c = jnp.broadcast_to(mods[l, b].reshape(1, 6, d_model), (b, 6, d_model))
        w_lo, w_hi = _reorder_w_in(w_in[l])
        lb_row = lower[l].reshape(1, MIX_W)
        gbias = _pad_row(m_gate_bias[l], 0)
        alog_row = _pad_row(gdn_a_log[l], 16)
        dtb_row = _pad_row(gdn_dt_bias[l], 16)
        gain = mix_norm[l]
        wbr = w_branch[l].astype(BF16)
        wo = w_out[l].astype(BF16)
        g0, b0 = ln_g[l, 0].reshape(1, -1), ln_b[l, 0].reshape(1, -1)
        g1, b1 = ln_g[l, 1].reshape(1, -1), ln_b[l, 1].reshape(1, -1)

        lo_c, hi_c = _in_proj(ctx.reshape(ctx_packed), mod_c[:ctx_packed[0]], w_lo, w_hi, ctx_packed[1])
        lo_c, hi_c = lo_c.reshape(b, l_ctx, N_LO), hi_c.reshape(b, l_ctx, N_HI)
        lo_l, hi_l = _in_proj(x, mod_l, w_lo, w_hi, _tile(l_lat, 1024))
        qkv_c = _gdn_conv(lo_c, conv_w[l], _tile(l_ctx, 256))
        qkv_l = _gdn_conv(lo_l, conv_w[l], _tile(l_lat, 512))

        oa_c = _hgrn_scan(lo_c, hi_c, lb_row, zeros_hgrn)
        oa_l = _hgrn_scan(lo_l, hi_l, lb_row, oa_c[2])
        ob_c = _mlstm_scan(hi_c, gbias, zeros_cmn, zeros_nm, colmajor=False)
        ob_l = _mlstm_scan(hi_l, gbias, ob_c[2], ob_c[3], colmajor=True)
        oc_c = _gdn_scan(qkv_c, hi_c, alog_row, dtb_row, zeros_state)
        oc_l = _gdn_scan(qkv_l, hi_l, alog_row, dtb_row, oc_c[2])

        def channel_mixer(t, mod, tl):
            i = l // 2
            if l % 2 == 0:
                return _ffn(t, mod, ffn_w_gate[i].astype(BF16), ffn_w_up[i].astype(BF16),
                            ffn_w_down[i].astype(BF16), g1, b1, alpha, tl)
            w_r = jnp.concatenate([moe_router[i], jnp.zeros((d_model, LANE - N_EXPERTS), F32)], axis=1)
            return _moe(t, mod, w_r, _pack_gate_up(moe_w_gate[i], moe_w_up[i]), moe_w_down[i].astype(BF16),
                        g1, b1, alpha, tl)

        outs_l = (oa_l[0], oa_l[1], ob_l[0], ob_l[1], oc_l[0], oc_l[1])
        x = _merge(outs_l, lo_l, x, mod_l, gain, wbr, wo, g0, b0, alpha, _tile(l_lat, 256))
        x = channel_mixer(x, mod_l, _tile(l_lat, 1024))
        if l < depth - 1:
            outs_c = (oa_c[0], oa_c[1], ob_c[0], ob_c[1], oc_c[0], oc_c[1])
            ctx = _merge(outs_c, lo_c, ctx, mod_c, gain, wbr, wo, g0, b0, alpha, _tile(l_ctx, 256))
            ctx = channel_mixer(ctx.reshape(ctx_packed), mod_c[:ctx_packed[0]], ctx_packed[1]).reshape(ctx.shape)
    return x
```

```python
import functools

import jax
import jax.numpy as jnp
from jax import lax
from jax.experimental import pallas as pl
from jax.experimental.pallas import tpu as pltpu

F32 = jnp.float32
BF16 = jnp.bfloat16
HIGHEST = lax.Precision.HIGHEST

HEAD_DIM = 128
N_HEADS = 4
MIX_W = N_HEADS * HEAD_DIM
CHUNK = 64
GRID_W = 64
SUB = 16
MLSTM_GROUP = 2
GDN_BATCH_ROWS = 4
HGRN_GROUP = 8
HGRN_BATCH_ROWS = 2
N_EXPERTS = 8
MOE_ROW_BLOCK = 288
LN_EPS = 1e-5
RMS_EPS = 1e-6
L2_EPS = 1e-6
MASK_NEG = -1e30
LOG2_E = 1.4426950408889634
LANE = 128

LO_MERGE = 0
LO_CQKV = 3 * 1024
LO_AQ = LO_CQKV + 3 * MIX_W
LO_AI = LO_AQ + MIX_W
LO_AG = LO_AI + MIX_W
LO_BO = LO_AG + MIX_W
LO_CZ = LO_BO + MIX_W
N_LO = LO_CZ + MIX_W
LO_TILE = N_LO // 4
HI_AF = 0
HI_B = 2 * MIX_W
HI_GATES = HI_B + 3 * MIX_W
HI_TILE = (HI_GATES + 2 * LANE) // 2
N_HI = 2 * HI_TILE


def _cparams(sem, vmem_mb=None):
    kw = dict(dimension_semantics=sem)
    if vmem_mb is not None:
        kw["vmem_limit_bytes"] = vmem_mb << 20
    return pltpu.CompilerParams(**kw)


def _dot(a, b, precision=None):
    return jnp.dot(a, b, preferred_element_type=F32, precision=precision)


def _dot_nt(a, b, precision=None):
    return lax.dot_general(a, b, (((1,), (1,)), ((), ())), preferred_element_type=F32, precision=precision)


def _dot_tn(a, b):
    return lax.dot_general(a, b, (((0,), (0,)), ((), ())), preferred_element_type=F32)


def _ln(x):
    mu = jnp.mean(x, axis=-1, keepdims=True)
    xc = x - mu
    var = jnp.mean(xc * xc, axis=-1, keepdims=True)
    return xc * lax.rsqrt(var + LN_EPS)


def _sigmoid(x):
    return 1.0 / (1.0 + jnp.exp(-x))


def _silu(x):
    return x * _sigmoid(x)


def _softplus(x):
    return jnp.maximum(x, 0.0) + jnp.log(1.0 + jnp.exp(-jnp.abs(x)))


def _log_sigmoid(x):
    return -_softplus(-x)


def _order_mask(n, rev, strict=False):
    r = lax.broadcasted_iota(jnp.int32, (n, n), 0)
    c = lax.broadcasted_iota(jnp.int32, (n, n), 1)
    if rev:
        return (c > r) if strict else (c >= r)
    return (c < r) if strict else (c <= r)


def _cumsum_rows(x, rev):
    m = _order_mask(x.shape[0], rev).astype(F32)
    return _dot(m, x, precision=HIGHEST)


def _ada_kernel(c_ref, w_ref, b_ref, o_ref):
    s = _silu(c_ref[...])
    o_ref[0] = _dot(s, w_ref[0], precision=HIGHEST) + b_ref[0]


def _ada(cvec, w_ada, b_ada, tn=1536):
    depth, d, n6 = w_ada.shape
    rows = cvec.shape[0]
    return pl.pallas_call(
        _ada_kernel,
        grid=(depth, n6 // tn),
        in_specs=[pl.BlockSpec((rows, d), lambda l, n: (0, 0)),
                  pl.BlockSpec((1, d, tn), lambda l, n: (l, 0, n)),
                  pl.BlockSpec((1, 1, tn), lambda l, n: (l, 0, n))],
        out_specs=pl.BlockSpec((1, rows, tn), lambda l, n: (l, 0, n)),
        out_shape=jax.ShapeDtypeStruct((depth, rows, n6), F32),
        compiler_params=_cparams(("parallel", "parallel"), 40),
        name="ada",
    )(cvec, w_ada, b_ada.reshape(depth, 1, n6))


def _lower_bounds_kernel(lb_ref, o_ref):
    x = lb_ref[...]
    depth = x.shape[0]
    mx = jnp.max(x, axis=0, keepdims=True)
    e = jnp.exp(x - mx)
    p = e / jnp.sum(e, axis=0, keepdims=True)
    acc = jnp.zeros_like(p[0:1])
    for l in range(depth):
        o_ref[l:l + 1, :] = acc
        if l + 1 < depth:
            acc = acc + p[l + 1:l + 2]


def _lower_bounds(lb_raw):
    return pl.pallas_call(
        _lower_bounds_kernel,
        out_shape=jax.ShapeDtypeStruct(lb_raw.shape, F32),
        name="lower_bounds",
    )(lb_raw)


def _in_proj_kernel(x_ref, mod_ref, wlo_ref, whi_ref, lo_ref, hi_ref, h_ref, *, lo_tiles):
    n = pl.program_id(2)

    @pl.when(n == 0)
    def _():
        shift = mod_ref[0, 0:1, :]
        scale = mod_ref[0, 1:2, :]
        h_ref[...] = (_ln(x_ref[0]) * (1.0 + scale) + shift).astype(BF16)

    @pl.when(n < lo_tiles)
    def _():
        lo_ref[0] = _dot(h_ref[...], wlo_ref[...]).astype(BF16)

    @pl.when(n >= lo_tiles)
    def _():
        hi_ref[0] = _dot(h_ref[...], whi_ref[...])


def _in_proj(x, mod, w_lo, w_hi, tl):
    b, l, d = x.shape
    lo_tiles = N_LO // LO_TILE
    hi_tiles = N_HI // HI_TILE
    return pl.pallas_call(
        functools.partial(_in_proj_kernel, lo_tiles=lo_tiles),
        grid=(b, l // tl, lo_tiles + hi_tiles),
        in_specs=[pl.BlockSpec((1, tl, d), lambda bi, i, n: (bi, i, 0)),
                  pl.BlockSpec((1, 6, d), lambda bi, i, n: (bi, 0, 0)),
                  pl.BlockSpec((d, LO_TILE), lambda bi, i, n: (0, jnp.minimum(n, lo_tiles - 1))),
                  pl.BlockSpec((d, HI_TILE), lambda bi, i, n: (0, jnp.maximum(n - lo_tiles, 0)))],
        out_specs=[pl.BlockSpec((1, tl, LO_TILE), lambda bi, i, n: (bi, i, jnp.minimum(n, lo_tiles - 1))),
                   pl.BlockSpec((1, tl, HI_TILE), lambda bi, i, n: (bi, i, jnp.maximum(n - lo_tiles, 0)))],
        out_shape=[jax.ShapeDtypeStruct((b, l, N_LO), BF16), jax.ShapeDtypeStruct((b, l, N_HI), F32)],
        scratch_shapes=[pltpu.VMEM((tl, d), BF16)],
        compiler_params=_cparams(("parallel", "parallel", "arbitrary"), 48),
        name="in_proj",
    )(x, mod, w_lo, w_hi)


def _conv_kernel(x_ref, prev_ref, next_ref, w_ref, o_ref):
    i = pl.program_id(1)
    last = pl.num_programs(1) - 1
    x = x_ref[0].astype(F32)
    tl = x.shape[0]
    halo = prev_ref.shape[1]
    row = lax.broadcasted_iota(jnp.int32, x.shape, 0)
    before = jnp.where(i > 0, prev_ref[0].astype(F32)[halo - 1:halo, :], 0.0)
    after = jnp.where(i < last, next_ref[0].astype(F32)[0:1, :], 0.0)
    x_dn = jnp.where(row == 0, before, pltpu.roll(x, 1, axis=0))
    x_up = jnp.where(row == tl - 1, after, pltpu.roll(x, tl - 1, axis=0))
    y = _silu(x_dn * w_ref[0:1, :] + x * w_ref[1:2, :] + x_up * w_ref[2:3, :])
    for j in range(3 * N_HEADS):
        t = y[:, j * HEAD_DIM:(j + 1) * HEAD_DIM]
        if j < 2 * N_HEADS:
            t = t * lax.rsqrt(jnp.sum(t * t, axis=-1, keepdims=True) + L2_EPS)
            if j < N_HEADS:
                t = t * HEAD_DIM ** -0.5
        o_ref[0, :, j * HEAD_DIM:(j + 1) * HEAD_DIM] = t.astype(o_ref.dtype)


def _gdn_conv(proj, conv_w, tl):
    b, l, _ = proj.shape
    w3 = 3 * MIX_W
    cb = LO_CQKV // w3
    halo = 16
    nbh = l // halo
    rh = tl // halo
    return pl.pallas_call(
        _conv_kernel,
        grid=(b, l // tl),
        in_specs=[pl.BlockSpec((1, tl, w3), lambda bi, i: (bi, i, cb)),
                  pl.BlockSpec((1, halo, w3), lambda bi, i: (bi, jnp.maximum(i * rh - 1, 0), cb)),
                  pl.BlockSpec((1, halo, w3), lambda bi, i: (bi, jnp.minimum((i + 1) * rh, nbh - 1), cb)),
                  pl.BlockSpec((3, w3), lambda bi, i: (0, 0))],
        out_specs=pl.BlockSpec((1, tl, w3), lambda bi, i: (bi, i, 0)),
        out_shape=jax.ShapeDtypeStruct((b, l, w3), BF16),
        compiler_params=_cparams(("parallel", "parallel"), 40),
        name="gdn_conv",
    )(proj, proj, proj, conv_w)


def _hgrn_chunks(probs, states, emat):
    c = probs[0]["q"].shape[0]
    nb = c // SUB
    for p in probs:
        lb = p["lb"]
        sig = _sigmoid(p["fpre"])
        p["k"] = (1.0 - lb) * (1.0 - sig)
        p["g"] = _cumsum_rows(jnp.log(lb + (1.0 - lb) * sig), p["rev"])

    walls = []
    for p in probs:
        q = p["q"]
        g2 = p["g"] * LOG2_E
        lk2 = jnp.log(p["k"]) * LOG2_E
        rel = lk2 - g2
        rows = []
        half = SUB // 2
        dead = jnp.zeros((half, HEAD_DIM), F32)
        for j in range(nb):
            r0 = j * SUB
            pieces = []
            for s in range(SUB):
                skip = 0 if (s >= half and not p["rev"]) else 1 if (s < half and p["rev"]) else None
                parts = []
                for part in range(2):
                    if part == skip:
                        parts.append(dead)
                        continue
                    t0 = r0 + part * half
                    e2 = jnp.exp2(jnp.minimum(g2[t0:t0 + half] + rel[r0 + s:r0 + s + 1], lk2[r0 + s:r0 + s + 1]))
                    parts.append(e2 * q[t0:t0 + half])
                pieces.append(jnp.concatenate(parts, axis=0).astype(BF16))
            rows.append(jnp.concatenate(pieces, axis=1))
        walls.append(jnp.concatenate(rows, axis=0))
    a_diags = [_dot(w, emat) for w in walls]

    a_offs = []
    for p in probs:
        q, k, g, rev = p["q"], p["k"], p["g"], p["rev"]
        offs = []
        for i in range(nb):
            r0 = i * SUB
            if (not rev and i == 0) or (rev and i == nb - 1):
                offs.append(jnp.zeros((SUB, c), F32))
                continue
            bi = g[r0 + SUB:r0 + SUB + 1] if rev else g[r0 - 1:r0]
            lhs = (q[r0:r0 + SUB] * jnp.exp(g[r0:r0 + SUB] - bi)).astype(BF16)
            lo, hi = (r0 + SUB, c) if rev else (0, r0)
            live = k[lo:hi] * jnp.exp(bi - g[lo:hi])
            pad = jnp.zeros((c - (hi - lo), HEAD_DIM), F32)
            rhs = jnp.concatenate([pad, live] if rev else [live, pad], axis=0)
            offs.append(_dot_nt(lhs, rhs.astype(BF16)))
        a_offs.append(jnp.concatenate(offs, axis=0))

    r = lax.broadcasted_iota(jnp.int32, (c, c), 0)
    col = lax.broadcasted_iota(jnp.int32, (c, c), 1)
    shift = SUB.bit_length() - 1
    rb = lax.shift_right_logical(r, shift)
    cb = lax.shift_right_logical(col, shift)
    inters = [_dot_nt((p["q"] * jnp.exp(p["g"])).astype(BF16), s_t.astype(BF16)) for p, s_t in zip(probs, states)]
    outs, new_states = [], []
    for p, a_diag, a_off, inter, s_t in zip(probs, a_diags, a_offs, inters, states):
        rev, g = p["rev"], p["g"]
        order = (col >= r) if rev else (col <= r)
        off_side = (cb > rb) if rev else (cb < rb)
        a = jnp.where((rb == cb) & order, a_diag, jnp.where(off_side, a_off, 0.0))
        g_end = g[0:1] if rev else g[c - 1:c]
        vb = p["v"].astype(BF16)
        outs.append(_dot(a.astype(BF16), vb) + inter)
        kd = (p["k"] * jnp.exp(g_end - g)).astype(BF16)
        new_states.append(s_t * jnp.exp(g_end) + _dot_tn(vb, kd))
    return outs, new_states


def _hgrn_kernel(qf_ref, ff_ref, vf_ref, qb_ref, fb_ref, vb_ref, lb_ref, e_ref, s0_ref,
                 of_ref, ob_ref, sout_ref, s_ref):
    n = pl.program_id(1)

    @pl.when(n == 0)
    def _():
        s_ref[...] = s0_ref[...]

    probs, where = [], []
    for r in range(qf_ref.shape[0]):
        for d, (q_ref, f_ref, v_ref, o_ref) in enumerate(((qf_ref, ff_ref, vf_ref, of_ref),
                                                          (qb_ref, fb_ref, vb_ref, ob_ref))):
            for h in range(N_HEADS):
                sl = slice(h * HEAD_DIM, (h + 1) * HEAD_DIM)
                probs.append(dict(q=q_ref[r, :, sl].astype(F32), fpre=f_ref[r, :, sl],
                                  v=v_ref[r, :, sl].astype(F32), lb=lb_ref[:, sl], rev=d == 1))
                where.append((o_ref, sl, r, h, d))
    states = [s_ref[r, h, d] for _, _, r, h, d in where]
    outs, new_states = [], []
    for g0 in range(0, len(probs), HGRN_GROUP):
        o_g, s_g = _hgrn_chunks(probs[g0:g0 + HGRN_GROUP], states[g0:g0 + HGRN_GROUP], e_ref[...])
        outs += o_g
        new_states += s_g
    for (o_ref, sl, r, h, d), o, s_new in zip(where, outs, new_states):
        o_ref[r, :, sl] = o.astype(o_ref.dtype)
        s_ref[r, h, d] = s_new

    @pl.when(n == pl.num_programs(1) - 1)
    def _():
        sout_ref[...] = s_ref[...]


def _hgrn_emat():
    r = jnp.arange(SUB * HEAD_DIM)[:, None] // HEAD_DIM
    c = jnp.arange(CHUNK)[None, :] % SUB
    return (r == c).astype(BF16)


def _hgrn_scan(p_lo, p_hi, lb, s0):
    b, l, _ = p_lo.shape
    nc = l // CHUNK
    rows = HGRN_BATCH_ROWS if b % HGRN_BATCH_ROWS == 0 else 1
    q_col, i_col, f_col = LO_AQ // MIX_W, LO_AI // MIX_W, HI_AF // MIX_W

    def fwd(col):
        return pl.BlockSpec((rows, CHUNK, MIX_W), lambda bi, n: (bi, n, col))

    def bwd(col):
        return pl.BlockSpec((rows, CHUNK, MIX_W), lambda bi, n: (bi, nc - 1 - n, col))

    st_spec = pl.BlockSpec((rows, N_HEADS, 2, HEAD_DIM, HEAD_DIM), lambda bi, n: (bi, 0, 0, 0, 0))
    return pl.pallas_call(
        _hgrn_kernel,
        grid=(b // rows, nc),
        in_specs=[fwd(q_col), fwd(f_col), fwd(i_col), bwd(q_col), bwd(f_col + 1), bwd(i_col),
                  pl.BlockSpec((1, MIX_W), lambda bi, n: (0, 0)),
                  pl.BlockSpec((SUB * HEAD_DIM, CHUNK), lambda bi, n: (0, 0)),
                  st_spec],
        out_specs=[pl.BlockSpec((rows, CHUNK, MIX_W), lambda bi, n: (bi, n, 0)),
                   pl.BlockSpec((rows, CHUNK, MIX_W), lambda bi, n: (bi, nc - 1 - n, 0)),
                   st_spec],
        out_shape=[jax.ShapeDtypeStruct((b, l, MIX_W), BF16),
                   jax.ShapeDtypeStruct((b, l, MIX_W), BF16),
                   jax.ShapeDtypeStruct(s0.shape, F32)],
        scratch_shapes=[pltpu.VMEM((rows, N_HEADS, 2, HEAD_DIM, HEAD_DIM), F32)],
        compiler_params=_cparams(("parallel", "arbitrary"), 40),
        name="hgrn_scan",
    )(p_lo, p_hi, p_lo, p_lo, p_hi, p_lo, lb, _hgrn_emat(), s0)


def _mlstm_chunks(probs, states):
    c = probs[0]["q"].shape[0]
    dv = probs[0]["v"].shape[1]
    ones = jnp.ones((c, dv), BF16)
    for p, (cmn, m) in zip(probs, states):
        tri = _order_mask(c, p["rev"])
        log_d = jnp.where(tri, p["bb_c"] - p["bb_r"] + p["ic_r"], MASK_NEG)
        log_inter = p["bb_c"] + m
        m_t = jnp.maximum(jnp.max(log_d, axis=1, keepdims=True), log_inter)
        p["m_t"] = m_t
        p["dmat"] = jnp.where(tri, jnp.exp(log_d - m_t), 0.0)
        p["inter"] = jnp.exp(log_inter - m_t)
        p["qb"], p["kb"] = p["q"].astype(BF16), p["k"].astype(BF16)
        p["v1"] = jnp.concatenate([p["v"].astype(BF16), ones], axis=1)
    scores = [_dot_nt(p["qb"], p["kb"]) * p["dmat"] for p in probs]
    carried = [_dot(p["qb"], cmn.astype(BF16)) for p, (cmn, _) in zip(probs, states)]
    local = [_dot(s.astype(BF16), p["v1"]) for p, s in zip(probs, scores)]
    outs, new_states = [], []
    for p, car, loc, (cmn, m) in zip(probs, carried, local, states):
        m_t = p["m_t"]
        both = loc + p["inter"] * car
        outs.append(both[:, :dv] / jnp.maximum(jnp.abs(both[:, dv:]), jnp.exp(-m_t)))
        m_new = m_t[0:1] if p["rev"] else m_t[c - 1:c]
        bb_end = p["bb_c"][0:1] if p["rev"] else p["bb_c"][c - 1:c]
        decay = jnp.exp(bb_end + m - m_new)
        kw = p["k"] * jnp.exp(bb_end - p["bb_c"] + p["ic_c"] - m_new)
        new_states.append((decay * cmn + _dot_tn(kw.astype(BF16), p["v1"]), m_new))
    return outs, new_states


def _column_view(ref):
    rows, group, width = ref.shape[-3:]
    return ref.reshape(rows * group, width), group


def _chunk_rows(j):
    start = j * CHUNK
    return pl.ds(start if isinstance(start, int) else pl.multiple_of(start, CHUNK), CHUNK)


def _load_chunk(ref, j, colmajor):
    if colmajor:
        flat, group = _column_view(ref)
        return flat[pl.ds(j, CHUNK, stride=group), :]
    return ref[0, _chunk_rows(j), :]


def _store_chunk(ref, j, val, colmajor):
    if colmajor:
        flat, group = _column_view(ref)
        flat[pl.ds(j, CHUNK, stride=group), :] = val
    else:
        ref[0, _chunk_rows(j), :] = val


def _columns_as_rows(x, first):
    row = lax.broadcasted_iota(jnp.int32, (8, LANE), 0)
    lane = lax.broadcasted_iota(jnp.int32, (8, LANE), 1)
    sel = jnp.where(lane == row + first, 1.0, 0.0).astype(BF16)
    hi = x.astype(BF16)
    rest = x - hi.astype(F32)
    mid = rest.astype(BF16)
    lo = (rest - mid.astype(F32)).astype(BF16)
    return _dot_nt(sel, hi) + (_dot_nt(sel, mid) + _dot_nt(sel, lo))


def _mlstm_kernel(*refs, colmajor, chunks):
    per_dir = 3 * N_HEADS + 1
    fwd, bwd = refs[:per_dir], refs[per_dir:2 * per_dir]
    bias_ref, cm0_ref, nm0_ref, of_ref, ob_ref, cmout_ref, nmout_ref, cm_ref, nm_ref = refs[2 * per_dir:]
    n = pl.program_id(1)

    @pl.when(n == 0)
    def _():
        cm_ref[...] = cm0_ref[0]
        nm_ref[...] = nm0_ref[0]

    def step(i, carry):
        probs, where = [], []
        for d, (in_refs, o_ref) in enumerate(((fwd, of_ref), (bwd, ob_ref))):
            rev = d == 1
            j = chunks - 1 - i if rev else i
            gates = _load_chunk(in_refs[-1], j, colmajor) + bias_ref[...]
            cs = _cumsum_rows(_log_sigmoid(gates), rev)
            cs_rows = _columns_as_rows(cs, 8 + 4 * d)
            gate_rows = _columns_as_rows(gates, 4 * d)
            for h in range(N_HEADS):
                ci = 4 * d + h
                cf = 8 + 4 * d + h
                probs.append(dict(
                    q=_load_chunk(in_refs[h], j, colmajor),
                    k=_load_chunk(in_refs[N_HEADS + h], j, colmajor) * HEAD_DIM ** -0.5,
                    v=_load_chunk(in_refs[2 * N_HEADS + h], j, colmajor),
                    bb_c=cs[:, cf:cf + 1], bb_r=cs_rows[h:h + 1, :],
                    ic_c=gates[:, ci:ci + 1], ic_r=gate_rows[h:h + 1, :], rev=rev))
                where.append((o_ref, j, d, h))
        states = [(cm_ref[d, h], nm_ref[d, h, 0:1, 0:1]) for _, _, d, h in where]
        outs, new_states = [], []
        for g0 in range(0, len(probs), MLSTM_GROUP):
            o_g, s_g = _mlstm_chunks(probs[g0:g0 + MLSTM_GROUP], states[g0:g0 + MLSTM_GROUP])
            outs += o_g
            new_states += s_g
        for (o_ref, j, d, h), o, (cmn_new, m_new) in zip(where, outs, new_states):
            _store_chunk(o_ref.at[h], j, o, colmajor)
            cm_ref[d, h] = cmn_new
            nm_ref[d, h, 0:1, :] = jnp.broadcast_to(m_new, (1, HEAD_DIM))
        return carry

    lax.fori_loop(0, chunks, step, 0)

    @pl.when(n == pl.num_programs(1) - 1)
    def _():
        cmout_ref[0] = cm_ref[...]
        nmout_ref[0] = nm_ref[...]


def _mlstm_scan(proj, gbias, cm0, nm0, colmajor):
    b, l, n_proj = proj.shape
    cb = HI_B // LANE
    cg = HI_GATES // LANE
    if colmajor:
        chunks = 8
        assert l == CHUNK * GRID_W and GRID_W % chunks == 0
        nb = GRID_W // chunks
        src = proj.reshape(b, CHUNK, GRID_W, n_proj)
        in_blk = (1, CHUNK, chunks, LANE)
        out_blk = (N_HEADS, 1, CHUNK, chunks, LANE)
        out_view = (N_HEADS, b, CHUNK, GRID_W, LANE)

        def in_map(col, flip):
            return lambda bi, n: (bi, 0, nb - 1 - n if flip else n, col)

        def out_map(flip):
            return lambda bi, n: (0, bi, 0, nb - 1 - n if flip else n, 0)
    else:
        chunks = min(l // CHUNK, 4)
        nb = l // (chunks * CHUNK)
        src = proj
        in_blk = (1, chunks * CHUNK, LANE)
        out_blk = (N_HEADS, 1, chunks * CHUNK, LANE)
        out_view = (N_HEADS, b, l, LANE)

        def in_map(col, flip):
            return lambda bi, n: (bi, nb - 1 - n if flip else n, col)

        def out_map(flip):
            return lambda bi, n: (0, bi, nb - 1 - n if flip else n, 0)

    def specs(flip):
        cols = [cb + j for j in range(3 * N_HEADS)] + [cg]
        return [pl.BlockSpec(in_blk, in_map(col, flip)) for col in cols]

    cm_spec = pl.BlockSpec((1, 2, N_HEADS, HEAD_DIM, 2 * HEAD_DIM), lambda bi, n: (bi, 0, 0, 0, 0))
    nm_spec = pl.BlockSpec((1, 2, N_HEADS, 8, HEAD_DIM), lambda bi, n: (bi, 0, 0, 0, 0))
    n_in = 2 * (3 * N_HEADS + 1)
    o_f, o_b, cm, nm = pl.pallas_call(
        functools.partial(_mlstm_kernel, colmajor=colmajor, chunks=chunks),
        grid=(b, nb),
        in_specs=specs(False) + specs(True) + [pl.BlockSpec((1, LANE), lambda bi, n: (0, 0)), cm_spec, nm_spec],
        out_specs=[pl.BlockSpec(out_blk, out_map(False)), pl.BlockSpec(out_blk, out_map(True)), cm_spec, nm_spec],
        out_shape=[jax.ShapeDtypeStruct(out_view, F32), jax.ShapeDtypeStruct(out_view, F32),
                   jax.ShapeDtypeStruct(cm0.shape, F32), jax.ShapeDtypeStruct(nm0.shape, F32)],
        scratch_shapes=[pltpu.VMEM((2, N_HEADS, HEAD_DIM, 2 * HEAD_DIM), F32),
                        pltpu.VMEM((2, N_HEADS, 8, HEAD_DIM), F32)],
        compiler_params=_cparams(("parallel", "arbitrary"), 40),
        name="mlstm_scan",
    )(*([src] * n_in), gbias, cm0, nm0)
    head_major = (N_HEADS, b, l, LANE)
    return o_f.reshape(head_major), o_b.reshape(head_major), cm, nm


def _split_bf16(x):
    hi = x.astype(BF16)
    return hi, (x - hi.astype(F32)).astype(BF16)


def _dot3(a, b, nt=False):
    f = _dot_nt if nt else _dot
    return f(a[0], b[0]) + (f(a[0], b[1]) + f(a[1], b[0]))


def _inv_unit_triangular(a_list):
    n = a_list[0].shape[0]
    eye = (lax.broadcasted_iota(jnp.int32, (n, n), 0) == lax.broadcasted_iota(jnp.int32, (n, n), 1)).astype(F32)
    xs = [eye - a for a in a_list]
    splits = [_split_bf16(a) for a in a_list]
    ps = [_dot3(s, s) for s in splits]
    steps = (n - 1).bit_length() - 1
    for step in range(steps):
        last = step == steps - 1
        p_split = [_split_bf16(p) for p in ps]
        lhs = xs if last else [jnp.concatenate([x, p], axis=0) for x, p in zip(xs, ps)]
        prods = [_dot3(_split_bf16(l), s) for l, s in zip(lhs, p_split)]
        xs = [x + pr[:n] for x, pr in zip(xs, prods)]
        if not last:
            ps = [pr[n:] for pr in prods]
    return xs


def _gdn_chunks(probs, states):
    c = probs[0]["q"].shape[0]
    a_list, ldecs = [], []
    for p in probs:
        tri = _order_mask(c, p["rev"])
        strict = _order_mask(c, p["rev"], strict=True)
        ldec = jnp.where(tri, jnp.exp(jnp.minimum(p["gam_c"] - p["gam_r"], 0.0)), 0.0)
        p["kbeta"] = p["k"] * p["beta"]
        kk = _dot_nt(p["kbeta"].astype(BF16), p["k"].astype(BF16))
        a_list.append(jnp.where(strict, kk * ldec, 0.0))
        ldecs.append(ldec)
    t_invs = _inv_unit_triangular(a_list)
    sols = []
    for p, t_inv in zip(probs, t_invs):
        p["eg"] = jnp.exp(p["gam_c"])
        rhs = jnp.concatenate([p["v"] * p["beta"], p["kbeta"] * p["eg"]], axis=1)
        sols.append(_dot3(_split_bf16(t_inv), _split_bf16(rhs)))
    qks = [_dot_nt(p["q"].astype(BF16), p["k"].astype(BF16)) * ldec for p, ldec in zip(probs, ldecs)]
    inter = [_dot(jnp.concatenate([sol[:, HEAD_DIM:], p["q"] * p["eg"]], axis=0).astype(BF16), s.astype(BF16))
             for p, sol, s in zip(probs, sols, states)]
    outs, new_states = [], []
    for p, sol, qk, it, s in zip(probs, sols, qks, inter, states):
        g_end = p["gam_c"][0:1] if p["rev"] else p["gam_c"][c - 1:c]
        v_new = (sol[:, :HEAD_DIM] - it[:c]).astype(BF16)
        kg = p["k"] * jnp.exp(g_end - p["gam_c"])
        outs.append(it[c:] + _dot(qk.astype(BF16), v_new))
        new_states.append(jnp.exp(g_end) * s + _dot_tn(kg.astype(BF16), v_new))
    return outs, new_states


def _gdn_kernel(xf_ref, gf_ref, xb_ref, gb_ref, alog_ref, dtb_ref, s0_ref,
                of_ref, ob_ref, sout_ref, s_ref):
    n = pl.program_id(1)

    @pl.when(n == 0)
    def _():
        s_ref[...] = s0_ref[...]

    neg_a = -jnp.exp(alog_ref[...])
    probs, where = [], []
    for r in range(xf_ref.shape[0]):
        for d, (x_ref, g_ref, o_ref) in enumerate(((xf_ref, gf_ref, of_ref), (xb_ref, gb_ref, ob_ref))):
            rev = d == 1
            gates = g_ref[r]
            cs = _cumsum_rows(neg_a * _softplus(gates + dtb_ref[...]), rev)
            cs_t = cs.T
            betas = _sigmoid(gates)
            for h in range(N_HEADS):
                ca = 16 + 4 * d + h
                cbeta = 24 + 4 * d + h
                probs.append(dict(
                    q=x_ref[r, :, h * HEAD_DIM:(h + 1) * HEAD_DIM].astype(F32),
                    k=x_ref[r, :, MIX_W + h * HEAD_DIM:MIX_W + (h + 1) * HEAD_DIM].astype(F32),
                    v=x_ref[r, :, 2 * MIX_W + h * HEAD_DIM:2 * MIX_W + (h + 1) * HEAD_DIM].astype(F32),
                    gam_c=cs[:, ca:ca + 1], gam_r=cs_t[ca:ca + 1, :], beta=betas[:, cbeta:cbeta + 1], rev=rev))
                where.append((o_ref, r, d, h))
    outs, new_states = _gdn_chunks(probs, [s_ref[r, d, h] for _, r, d, h in where])
    for (o_ref, r, d, h), o, s_new in zip(where, outs, new_states):
        o_ref[r, :, h * HEAD_DIM:(h + 1) * HEAD_DIM] = o.astype(o_ref.dtype)
        s_ref[r, d, h] = s_new

    @pl.when(n == pl.num_programs(1) - 1)
    def _():
        sout_ref[...] = s_ref[...]


def _gdn_scan(qkv, proj, alog_row, dtb_row, s0):
    b, l, w3 = qkv.shape
    nc = l // CHUNK
    cg = HI_GATES // LANE
    rows = GDN_BATCH_ROWS if b % GDN_BATCH_ROWS == 0 else 1
    st_spec = pl.BlockSpec((rows, 2, N_HEADS, HEAD_DIM, HEAD_DIM), lambda bi, n: (bi, 0, 0, 0, 0))
    row_spec = pl.BlockSpec((1, LANE), lambda bi, n: (0, 0))
    return pl.pallas_call(
        _gdn_kernel,
        grid=(b // rows, nc),
        in_specs=[pl.BlockSpec((rows, CHUNK, w3), lambda bi, n: (bi, n, 0)),
                  pl.BlockSpec((rows, CHUNK, LANE), lambda bi, n: (bi, n, cg)),
                  pl.BlockSpec((rows, CHUNK, w3), lambda bi, n: (bi, nc - 1 - n, 0)),
                  pl.BlockSpec((rows, CHUNK, LANE), lambda bi, n: (bi, nc - 1 - n, cg)),
                  row_spec, row_spec, st_spec],
        out_specs=[pl.BlockSpec((rows, CHUNK, MIX_W), lambda bi, n: (bi, n, 0)),
                   pl.BlockSpec((rows, CHUNK, MIX_W), lambda bi, n: (bi, nc - 1 - n, 0)),
                   st_spec],
        out_shape=[jax.ShapeDtypeStruct((b, l, MIX_W), BF16), jax.ShapeDtypeStruct((b, l, MIX_W), BF16),
                   jax.ShapeDtypeStruct(s0.shape, F32)],
        scratch_shapes=[pltpu.VMEM((rows, 2, N_HEADS, HEAD_DIM, HEAD_DIM), F32)],
        compiler_params=_cparams(("parallel", "arbitrary"), 40),
        name="gdn_scan",
    )(qkv, proj, qkv, proj, alog_row, dtb_row, s0)


def _merge_kernel(oaf, oab, obf, obb, ocf, ocb, ag, bo, cz, mg, x_ref, mod_ref, gain_ref, wbr_ref, wout_ref,
                  lng_ref, lnb_ref, out_ref, *, alpha):
    d_model = x_ref.shape[-1]
    branches = ((oaf, oab, ag, _silu), (obf, obb, bo, _sigmoid), (ocf, ocb, cz, _silu))
    mixed = None
    for nbr, (of_ref, ob_ref, gate_ref, act) in enumerate(branches):
        parts = []
        for h in range(N_HEADS):
            if len(of_ref.shape) == 4:
                oh = of_ref[h, 0] + ob_ref[h, 0]
            else:
                sl = slice(h * HEAD_DIM, (h + 1) * HEAD_DIM)
                oh = of_ref[0, :, sl].astype(F32) + ob_ref[0, :, sl].astype(F32)
            parts.append(oh * lax.rsqrt(jnp.mean(oh * oh, axis=-1, keepdims=True) + RMS_EPS))
        y = jnp.concatenate(parts, axis=1) * gain_ref[nbr:nbr + 1, :] * act(gate_ref[0].astype(F32))
        gate = _sigmoid(mg[0, :, nbr * d_model:(nbr + 1) * d_model].astype(F32))
        term = gate * _dot(y.astype(BF16), wbr_ref[nbr])
        mixed = term if mixed is None else mixed + term
    sub = _dot(mixed.astype(BF16), wout_ref[...])
    z = alpha * x_ref[0] + mod_ref[0, 2:3, :] * sub
    out_ref[0] = _ln(z) * lng_ref[...] + lnb_ref[...]


def _merge(outs, proj, x, mod, gain, w_branch, w_out, ln_g, ln_b, alpha, tl):
    b, l, d = x.shape
    u = MIX_W

    def tok(width, col):
        return pl.BlockSpec((1, tl, width), lambda bi, i: (bi, i, col))

    def const(shape):
        return pl.BlockSpec(shape, lambda bi, i: (0,) * len(shape))

    heads = pl.BlockSpec((N_HEADS, 1, tl, LANE), lambda bi, i: (0, bi, i, 0))
    return pl.pallas_call(
        functools.partial(_merge_kernel, alpha=alpha),
        grid=(b, l // tl),
        in_specs=[tok(u, 0), tok(u, 0), heads, heads, tok(u, 0), tok(u, 0),
            tok(u, LO_AG // u), tok(u, LO_BO // u), tok(u, LO_CZ // u),
            tok(3 * d, LO_MERGE // (3 * d)), tok(d, 0),
            pl.BlockSpec((1, 6, d), lambda bi, i: (bi, 0, 0)),
            const((3, u)), const((3, u, d)), const((d, d)), const((1, d)), const((1, d))],
        out_specs=tok(d, 0),
        out_shape=jax.ShapeDtypeStruct((b, l, d), F32),
        compiler_params=_cparams(("parallel", "parallel"), 48),
        name="merge",
    )(*outs, proj, proj, proj, proj, x, mod, gain, w_branch, w_out, ln_g, ln_b)


def _ffn_kernel(x_ref, mod_ref, wgu_ref, wd_ref, lng_ref, lnb_ref, out_ref, h_ref, acc_ref, *, alpha):
    f = pl.program_id(2)

    @pl.when(f == 0)
    def _():
        h_ref[...] = (_ln(x_ref[0]) * (1.0 + mod_ref[0, 4:5, :]) + mod_ref[0, 3:4, :]).astype(BF16)
        acc_ref[...] = jnp.zeros_like(acc_ref)

    gu = _dot(h_ref[...], wgu_ref[0, 0])
    half = gu.shape[1] // 2
    act = _silu(gu[:, :half]) * gu[:, half:]
    acc_ref[...] += _dot(act.astype(BF16), wd_ref[...])

    @pl.when(f == pl.num_programs(2) - 1)
    def _():
        z = alpha * x_ref[0] + mod_ref[0, 5:6, :] * acc_ref[...]
        out_ref[0] = _ln(z) * lng_ref[...] + lnb_ref[...]


def _ffn(x, mod, w_gate_up, w_down, ln_g, ln_b, alpha, tl):
    b, l, d = x.shape
    _, nf, _, tf2 = w_gate_up.shape
    return pl.pallas_call(
        functools.partial(_ffn_kernel, alpha=alpha),
        grid=(b, l // tl, nf),
        in_specs=[pl.BlockSpec((1, tl, d), lambda bi, i, j: (bi, i, 0)),
                  pl.BlockSpec((1, 6, d), lambda bi, i, j: (bi, 0, 0)),
                  pl.BlockSpec((1, 1, d, tf2), lambda bi, i, j: (0, j, 0, 0)),
                  pl.BlockSpec((tf2 // 2, d), lambda bi, i, j: (j, 0)),
                  pl.BlockSpec((1, d), lambda bi, i, j: (0, 0)),
                  pl.BlockSpec((1, d), lambda bi, i, j: (0, 0))],
        out_specs=pl.BlockSpec((1, tl, d), lambda bi, i, j: (bi, i, 0)),
        out_shape=jax.ShapeDtypeStruct((b, l, d), F32),
        scratch_shapes=[pltpu.VMEM((tl, d), BF16), pltpu.VMEM((tl, d), F32)],
        compiler_params=_cparams(("parallel", "parallel", "arbitrary"), 56),
        name="ffn",
    )(x, mod, w_gate_up, w_down, ln_g, ln_b)


def _top2_combine(logits):
    lane = lax.broadcasted_iota(jnp.int32, logits.shape, 1).astype(F32)
    neg = -jnp.inf
    lg = jnp.where(lane < N_EXPERTS, logits, neg)
    m1 = jnp.max(lg, axis=1, keepdims=True)
    i1 = jnp.min(jnp.where(lg == m1, lane, float(LANE)), axis=1, keepdims=True)
    lg2 = jnp.where(lane == i1, neg, lg)
    m2 = jnp.max(lg2, axis=1, keepdims=True)
    i2 = jnp.min(jnp.where(lg2 == m2, lane, float(LANE)), axis=1, keepdims=True)
    e2 = jnp.exp(m2 - m1)
    p1 = 1.0 / (1.0 + e2)
    chosen = (lane == i1) | (lane == i2)
    return jnp.where(lane == i1, p1, jnp.where(lane == i2, e2 * p1, 0.0)), chosen


def _moe_kernel(x_ref, mod_ref, wr_ref, wgu_ref, wd_ref, lng_ref, lnb_ref, out_ref,
                h_ref, acc_ref, comb_ref, pos_ref, post_ref, col_ref, xc_ref, yc_ref, cnt_ref, *, alpha, rb):
    e = pl.program_id(2)
    f = pl.program_id(3)
    tl, d = h_ref.shape

    @pl.when((e == 0) & (f == 0))
    def _():
        h = _ln(x_ref[0]) * (1.0 + mod_ref[0, 4:5, :]) + mod_ref[0, 3:4, :]
        h_ref[...] = h.astype(BF16)
        comb, chosen = _top2_combine(_dot(h, wr_ref[...], precision=HIGHEST))
        comb_ref[...] = comb
        blk = min(tl, LANE)
        before = jnp.where(_order_mask(blk, False, strict=True), 1.0, 0.0).astype(BF16)
        routed = jnp.where(chosen, 1.0, 0.0)
        total = jnp.zeros((1, LANE), F32)
        ranks = []
        for r0 in range(0, tl, blk):
            part = routed[r0:r0 + blk]
            ranks.append(_dot(before, part.astype(BF16)) + total)
            total = total + jnp.sum(part, axis=0, keepdims=True)
        rank = jnp.concatenate(ranks, axis=0)
        pos = jnp.where(chosen, rank, -1.0)
        pos_ref[...] = pos
        pos_t = pos.T[0:N_EXPERTS, :]
        post_ref[...] = pos_t
        for ex in range(N_EXPERTS):
            cnt_ref[ex] = jnp.sum(jnp.where(pos_t[ex:ex + 1, :] >= 0.0, 1.0, 0.0)).astype(jnp.int32)
        acc_ref[...] = jnp.zeros_like(acc_ref)

    nblk = (cnt_ref[e] + (rb - 1)) // rb

    def rows(r):
        return pl.ds(pl.multiple_of(r * rb, rb), rb)

    @pl.when(f == 0)
    def _():
        lane = lax.broadcasted_iota(jnp.int32, (tl, LANE), 1)
        col_ref[0] = jnp.sum(jnp.where(lane == e, pos_ref[...], 0.0), axis=1, keepdims=True)
        col_ref[1] = jnp.sum(jnp.where(lane == e, comb_ref[...], 0.0), axis=1, keepdims=True)
        pos_row = post_ref[pl.ds(e, 1), :]

        def gather(r, carry):
            slot = (lax.broadcasted_iota(jnp.int32, (rb, tl), 0) + r * rb).astype(F32)
            sel = jnp.where(pos_row == slot, 1.0, 0.0).astype(BF16)
            xc_ref[rows(r), :] = _dot(sel, h_ref[...]).astype(BF16)
            yc_ref[rows(r), :] = jnp.zeros((rb, d), F32)
            return carry

        lax.fori_loop(0, nblk, gather, 0)

    def expert(r, carry):
        xb = xc_ref[rows(r), :]
        gu = _dot(xb, wgu_ref[0, 0])
        half = gu.shape[1] // 2
        act = _silu(gu[:, :half]) * gu[:, half:]
        yc_ref[rows(r), :] += _dot(act.astype(BF16), wd_ref[0])
        return carry

    lax.fori_loop(0, nblk, expert, 0)

    @pl.when(f == pl.num_programs(3) - 1)
    def _():
        pos_col = col_ref[0]
        weight = col_ref[1]

        def scatter(r, carry):
            slot = (lax.broadcasted_iota(jnp.int32, (tl, rb), 1) + r * rb).astype(F32)
            sel_t = jnp.where(pos_col == slot, 1.0, 0.0).astype(BF16)
            acc_ref[...] += weight * _dot(sel_t, yc_ref[rows(r), :].astype(BF16))
            return carry

        lax.fori_loop(0, nblk, scatter, 0)

    @pl.when((e == pl.num_programs(2) - 1) & (f == pl.num_programs(3) - 1))
    def _():
        z = alpha * x_ref[0] + mod_ref[0, 5:6, :] * acc_ref[...]
        out_ref[0] = _ln(z) * lng_ref[...] + lnb_ref[...]


def _pack_gate_up(w_gate, w_up):
    e, d, f = w_gate.shape
    tf = next(t for t in (896, 512, 256, 128) if f % t == 0)

    def tiles(w):
        return w.astype(BF16).reshape(e, d, f // tf, tf).transpose(0, 2, 1, 3)

    return jnp.concatenate([tiles(w_gate), tiles(w_up)], axis=-1)


def _moe(x, mod, w_router, w_gate_up, w_down, ln_g, ln_b, alpha, tl):
    b, l, d = x.shape
    ne, nf, _, tf2 = w_gate_up.shape
    rb = min(MOE_ROW_BLOCK, tl)
    compact_rows = -(-tl // rb) * rb
    return pl.pallas_call(
        functools.partial(_moe_kernel, alpha=alpha, rb=rb),
        grid=(b, l // tl, ne, nf),
        in_specs=[pl.BlockSpec((1, tl, d), lambda bi, i, e, j: (bi, i, 0)),
                  pl.BlockSpec((1, 6, d), lambda bi, i, e, j: (bi, 0, 0)),
                  pl.BlockSpec((d, LANE), lambda bi, i, e, j: (0, 0)),
                  pl.BlockSpec((1, 1, d, tf2), lambda bi, i, e, j: (e, j, 0, 0)),
                  pl.BlockSpec((1, tf2 // 2, d), lambda bi, i, e, j: (e, j, 0)),
                  pl.BlockSpec((1, d), lambda bi, i, e, j: (0, 0)),
                  pl.BlockSpec((1, d), lambda bi, i, e, j: (0, 0))],
        out_specs=pl.BlockSpec((1, tl, d), lambda bi, i, e, j: (bi, i, 0)),
        out_shape=jax.ShapeDtypeStruct((b, l, d), F32),
        scratch_shapes=[pltpu.VMEM((tl, d), BF16),
                        pltpu.VMEM((tl, d), F32),
                        pltpu.VMEM((tl, LANE), F32),
                        pltpu.VMEM((tl, LANE), F32),
                        pltpu.VMEM((N_EXPERTS, tl), F32),
                        pltpu.VMEM((2, tl, 1), F32),
                        pltpu.VMEM((compact_rows, d), BF16),
                        pltpu.VMEM((compact_rows, d), F32),
                        pltpu.SMEM((N_EXPERTS,), jnp.int32)],
        compiler_params=_cparams(("parallel", "parallel", "arbitrary", "arbitrary"), 56),
        name="moe",
    )(x, mod, w_router, w_gate_up, w_down, ln_g, ln_b)


def _reorder_w_in(w):
    d = w.shape[0]
    u = MIX_W
    o_a, o_b = 0, 5 * u
    o_bgt = o_b + 4 * u
    o_cqkv = o_bgt + 16
    o_cz = o_cqkv + 3 * u
    o_cgt = o_cz + u
    o_merge = o_cgt + 16
    lo = [w[:, o_merge:o_merge + 3 * d], w[:, o_cqkv:o_cz], w[:, o_a:o_a + u], w[:, o_a + 3 * u:o_a + 5 * u],
          w[:, o_b + 3 * u:o_bgt], w[:, o_cz:o_cgt]]
    hi = [w[:, o_a + u:o_a + 3 * u], w[:, o_b:o_b + 3 * u], w[:, o_bgt:o_bgt + 16], w[:, o_cgt:o_cgt + 16],
          jnp.zeros((d, N_HI - HI_GATES - 32), w.dtype)]
    return jnp.concatenate(lo, axis=1).astype(BF16), jnp.concatenate(hi, axis=1).astype(BF16)


def _pad_row(vals, offset):
    row = jnp.zeros((1, LANE), F32)
    return lax.dynamic_update_slice(row, vals.reshape(1, -1).astype(F32), (0, offset))


def _tile(l, pref):
    return pref if l % pref == 0 else l


def kernel(x, c, ctx, c_ctx, w_ada, b_ada, w_in, conv_w, lb_raw, m_gate_bias, gdn_a_log, gdn_dt_bias, mix_norm,
           w_branch, w_out, ln_g, ln_b, ffn_w_gate, ffn_w_up, ffn_w_down, moe_router, moe_w_gate, moe_w_up,
           moe_w_down):
    depth, d_model = w_in.shape[0], w_in.shape[1]
    b, l_lat, _ = x.shape
    l_ctx = ctx.shape[1]
    alpha = float((2 * depth) ** 0.25)
    assert d_model == 1024 and w_in.shape[2] == 9 * MIX_W + 3 * MIX_W + MIX_W + 32 + 3 * d_model

    lower = _lower_bounds(lb_raw.astype(F32))
    n_rows = -(-(b + 1) // 8) * 8
    cvec = jnp.concatenate([c, c_ctx[None, :], jnp.zeros((n_rows - b - 1, d_model), F32)], axis=0)
    mods = _ada(cvec, w_ada, b_ada)

    zeros_state = jnp.zeros((b, 2, N_HEADS, HEAD_DIM, HEAD_DIM), F32)
    zeros_hgrn = jnp.zeros((b, N_HEADS, 2, HEAD_DIM, HEAD_DIM), F32)
    zeros_nm = jnp.zeros((b, 2, N_HEADS, 8, HEAD_DIM), F32)
    zeros_cmn = jnp.zeros((b, 2, N_HEADS, HEAD_DIM, 2 * HEAD_DIM), F32)

    pack = 1024 // l_ctx if (1024 % l_ctx == 0 and b % max(1024 // l_ctx, 1) == 0) else 1
    ctx_packed = (b // pack, pack * l_ctx, d_model)

    for l in range(depth):
        mod_l = mods[l, :b].reshape(b, 6, d_model)
        mod_c = jnp.broadcast_to(mods[l, b].reshape(1, 6, d_model), (b, 6, d_model))
        w_lo, w_hi = _reorder_w_in(w_in[l])
        lb_row = lower[l].reshape(1, MIX_W)
        gbias = _pad_row(m_gate_bias[l], 0)
        alog_row = _pad_row(gdn_a_log[l], 16)
        dtb_row = _pad_row(gdn_dt_bias[l], 16)
        gain = mix_norm[l]
        wbr = w_branch[l].astype(BF16)
        wo = w_out[l].astype(BF16)
        g0, b0 = ln_g[l, 0].reshape(1, -1), ln_b[l, 0].reshape(1, -1)
        g1, b1 = ln_g[l, 1].reshape(1, -1), ln_b[l, 1].reshape(1, -1)

        lo_c, hi_c = _in_proj(ctx.reshape(ctx_packed), mod_c[:ctx_packed[0]], w_lo, w_hi, ctx_packed[1])
        lo_c, hi_c = lo_c.reshape(b, l_ctx, N_LO), hi_c.reshape(b, l_ctx, N_HI)
        lo_l, hi_l = _in_proj(x, mod_l, w_lo, w_hi, _tile(l_lat, 1024))
        qkv_c = _gdn_conv(lo_c, conv_w[l], _tile(l_ctx, 256))
        qkv_l = _gdn_conv(lo_l, conv_w[l], _tile(l_lat, 512))

        oa_c = _hgrn_scan(lo_c, hi_c, lb_row, zeros_hgrn)
        oa_l = _hgrn_scan(lo_l, hi_l, lb_row, oa_c[2])
        ob_c = _mlstm_scan(hi_c, gbias, zeros_cmn, zeros_nm, colmajor=False)
        ob_l = _mlstm_scan(hi_l, gbias, ob_c[2], ob_c[3], colmajor=True)
        oc_c = _gdn_scan(qkv_c, hi_c, alog_row, dtb_row, zeros_state)
        oc_l = _gdn_scan(qkv_l, hi_l, alog_row, dtb_row, oc_c[2])

        def channel_mixer(t, mod, tl):
            i = l // 2
            if l % 2 == 0:
                return _ffn(t, mod, _pack_gate_up(ffn_w_gate[i][None], ffn_w_up[i][None]),
                            ffn_w_down[i].astype(BF16), g1, b1, alpha, tl)
            w_r = jnp.concatenate([moe_router[i], jnp.zeros((d_model, LANE - N_EXPERTS), F32)], axis=1)
            return _moe(t, mod, w_r, _pack_gate_up(moe_w_gate[i], moe_w_up[i]), moe_w_down[i].astype(BF16),
                        g1, b1, alpha, tl)

        outs_l = (oa_l[0], oa_l[1], ob_l[0], ob_l[1], oc_l[0], oc_l[1])
        x = _merge(outs_l, lo_l, x, mod_l, gain, wbr, wo, g0, b0, alpha, _tile(l_lat, 512))
        x = channel_mixer(x, mod_l, _tile(l_lat, 1024))
        if l < depth - 1:
            outs_c = (oa_c[0], oa_c[1], ob_c[0], ob_c[1], oc_c[0], oc_c[1])
            ctx = _merge(outs_c, lo_c, ctx, mod_c, gain, wbr, wo, g0, b0, alpha, _tile(l_ctx, 256))
            ctx = channel_mixer(ctx.reshape(ctx_packed), mod_c[:ctx_packed[0]], ctx_packed[1]).reshape(ctx.shape)
    return x
```

```python
import functools

import jax
import jax.numpy as jnp
from jax import lax
from jax.experimental import pallas as pl
from jax.experimental.pallas import tpu as pltpu

F32 = jnp.float32
BF16 = jnp.bfloat16
HIGHEST = lax.Precision.HIGHEST

HEAD_DIM = 128
N_HEADS = 4
MIX_W = N_HEADS * HEAD_DIM
CHUNK = 64
GRID_W = 64
SUB = 16
MLSTM_GROUP = 2
GDN_BATCH_ROWS = 4
HGRN_GROUP = 8
HGRN_BATCH_ROWS = 2
N_EXPERTS = 8
MOE_ROW_BLOCK = 288
LN_EPS = 1e-5
RMS_EPS = 1e-6
L2_EPS = 1e-6
MASK_NEG = -1e30
LOG2_E = 1.4426950408889634
LANE = 128

LO_MERGE = 0
LO_CQKV = 3 * 1024
LO_AQ = LO_CQKV + 3 * MIX_W
LO_AI = LO_AQ + MIX_W
LO_AG = LO_AI + MIX_W
LO_BO = LO_AG + MIX_W
LO_CZ = LO_BO + MIX_W
N_LO = LO_CZ + MIX_W
LO_TILE = N_LO // 2
HI_AF = 0
HI_B = 2 * MIX_W
HI_GATES = HI_B + 3 * MIX_W
HI_TILE = (HI_GATES + 2 * LANE) // 2
N_HI = 2 * HI_TILE


def _cparams(sem, vmem_mb=None):
    kw = dict(dimension_semantics=sem)
    if vmem_mb is not None:
        kw["vmem_limit_bytes"] = vmem_mb << 20
    return pltpu.CompilerParams(**kw)


def _dot(a, b, precision=None):
    return jnp.dot(a, b, preferred_element_type=F32, precision=precision)


def _dot_nt(a, b, precision=None):
    return lax.dot_general(a, b, (((1,), (1,)), ((), ())), preferred_element_type=F32, precision=precision)


def _dot_tn(a, b):
    return lax.dot_general(a, b, (((0,), (0,)), ((), ())), preferred_element_type=F32)


def _ln(x):
    mu = jnp.mean(x, axis=-1, keepdims=True)
    xc = x - mu
    var = jnp.mean(xc * xc, axis=-1, keepdims=True)
    return xc * lax.rsqrt(var + LN_EPS)


def _sigmoid(x):
    return 1.0 / (1.0 + jnp.exp(-x))


def _silu(x):
    return x * _sigmoid(x)


def _softplus(x):
    return jnp.maximum(x, 0.0) + jnp.log(1.0 + jnp.exp(-jnp.abs(x)))


def _log_sigmoid(x):
    return -_softplus(-x)


def _order_mask(n, rev, strict=False):
    r = lax.broadcasted_iota(jnp.int32, (n, n), 0)
    c = lax.broadcasted_iota(jnp.int32, (n, n), 1)
    if rev:
        return (c > r) if strict else (c >= r)
    return (c < r) if strict else (c <= r)


def _cumsum_rows(x, rev):
    m = _order_mask(x.shape[0], rev).astype(F32)
    return _dot(m, x, precision=HIGHEST)


def _ada_kernel(c_ref, w_ref, b_ref, o_ref):
    s = _silu(c_ref[...])
    o_ref[0] = _dot(s, w_ref[0], precision=HIGHEST) + b_ref[0]


def _ada(cvec, w_ada, b_ada, tn=1536):
    depth, d, n6 = w_ada.shape
    rows = cvec.shape[0]
    return pl.pallas_call(
        _ada_kernel,
        grid=(depth, n6 // tn),
        in_specs=[pl.BlockSpec((rows, d), lambda l, n: (0, 0)),
                  pl.BlockSpec((1, d, tn), lambda l, n: (l, 0, n)),
                  pl.BlockSpec((1, 1, tn), lambda l, n: (l, 0, n))],
        out_specs=pl.BlockSpec((1, rows, tn), lambda l, n: (l, 0, n)),
        out_shape=jax.ShapeDtypeStruct((depth, rows, n6), F32),
        compiler_params=_cparams(("parallel", "parallel"), 40),
        name="ada",
    )(cvec, w_ada, b_ada.reshape(depth, 1, n6))


def _lower_bounds_kernel(lb_ref, o_ref):
    x = lb_ref[...]
    depth = x.shape[0]
    mx = jnp.max(x, axis=0, keepdims=True)
    e = jnp.exp(x - mx)
    p = e / jnp.sum(e, axis=0, keepdims=True)
    acc = jnp.zeros_like(p[0:1])
    for l in range(depth):
        o_ref[l:l + 1, :] = acc
        if l + 1 < depth:
            acc = acc + p[l + 1:l + 2]


def _lower_bounds(lb_raw):
    return pl.pallas_call(
        _lower_bounds_kernel,
        out_shape=jax.ShapeDtypeStruct(lb_raw.shape, F32),
        name="lower_bounds",
    )(lb_raw)


def _in_proj_kernel(x_ref, mod_ref, wlo_ref, whi_ref, lo_ref, hi_ref, h_ref, *, lo_tiles):
    n = pl.program_id(2)

    @pl.when(n == 0)
    def _():
        shift = mod_ref[0, 0:1, :]
        scale = mod_ref[0, 1:2, :]
        h_ref[...] = (_ln(x_ref[0]) * (1.0 + scale) + shift).astype(BF16)

    @pl.when(n < lo_tiles)
    def _():
        lo_ref[0] = _dot(h_ref[...], wlo_ref[...]).astype(BF16)

    @pl.when(n >= lo_tiles)
    def _():
        hi_ref[0] = _dot(h_ref[...], whi_ref[...])


def _in_proj(x, mod, w_lo, w_hi, tl):
    b, l, d = x.shape
    lo_tiles = N_LO // LO_TILE
    hi_tiles = N_HI // HI_TILE
    return pl.pallas_call(
        functools.partial(_in_proj_kernel, lo_tiles=lo_tiles),
        grid=(b, l // tl, lo_tiles + hi_tiles),
        in_specs=[pl.BlockSpec((1, tl, d), lambda bi, i, n: (bi, i, 0)),
                  pl.BlockSpec((1, 6, d), lambda bi, i, n: (bi, 0, 0)),
                  pl.BlockSpec((d, LO_TILE), lambda bi, i, n: (0, jnp.minimum(n, lo_tiles - 1))),
                  pl.BlockSpec((d, HI_TILE), lambda bi, i, n: (0, jnp.maximum(n - lo_tiles, 0)))],
        out_specs=[pl.BlockSpec((1, tl, LO_TILE), lambda bi, i, n: (bi, i, jnp.minimum(n, lo_tiles - 1))),
                   pl.BlockSpec((1, tl, HI_TILE), lambda bi, i, n: (bi, i, jnp.maximum(n - lo_tiles, 0)))],
        out_shape=[jax.ShapeDtypeStruct((b, l, N_LO), BF16), jax.ShapeDtypeStruct((b, l, N_HI), F32)],
        scratch_shapes=[pltpu.VMEM((tl, d), BF16)],
        compiler_params=_cparams(("parallel", "parallel", "arbitrary"), 60),
        name="in_proj",
    )(x, mod, w_lo, w_hi)


def _conv_kernel(x_ref, prev_ref, next_ref, w_ref, o_ref):
    i = pl.program_id(1)
    last = pl.num_programs(1) - 1
    x = x_ref[0].astype(F32)
    tl = x.shape[0]
    halo = prev_ref.shape[1]
    row = lax.broadcasted_iota(jnp.int32, x.shape, 0)
    before = jnp.where(i > 0, prev_ref[0].astype(F32)[halo - 1:halo, :], 0.0)
    after = jnp.where(i < last, next_ref[0].astype(F32)[0:1, :], 0.0)
    x_dn = jnp.where(row == 0, before, pltpu.roll(x, 1, axis=0))
    x_up = jnp.where(row == tl - 1, after, pltpu.roll(x, tl - 1, axis=0))
    y = _silu(x_dn * w_ref[0:1, :] + x * w_ref[1:2, :] + x_up * w_ref[2:3, :])
    for j in range(3 * N_HEADS):
        t = y[:, j * HEAD_DIM:(j + 1) * HEAD_DIM]
        if j < 2 * N_HEADS:
            t = t * lax.rsqrt(jnp.sum(t * t, axis=-1, keepdims=True) + L2_EPS)
            if j < N_HEADS:
                t = t * HEAD_DIM ** -0.5
        o_ref[0, :, j * HEAD_DIM:(j + 1) * HEAD_DIM] = t.astype(o_ref.dtype)


def _gdn_conv(proj, conv_w, tl):
    b, l, _ = proj.shape
    w3 = 3 * MIX_W
    cb = LO_CQKV // w3
    halo = 16
    nbh = l // halo
    rh = tl // halo
    return pl.pallas_call(
        _conv_kernel,
        grid=(b, l // tl),
        in_specs=[pl.BlockSpec((1, tl, w3), lambda bi, i: (bi, i, cb)),
                  pl.BlockSpec((1, halo, w3), lambda bi, i: (bi, jnp.maximum(i * rh - 1, 0), cb)),
                  pl.BlockSpec((1, halo, w3), lambda bi, i: (bi, jnp.minimum((i + 1) * rh, nbh - 1), cb)),
                  pl.BlockSpec((3, w3), lambda bi, i: (0, 0))],
        out_specs=pl.BlockSpec((1, tl, w3), lambda bi, i: (bi, i, 0)),
        out_shape=jax.ShapeDtypeStruct((b, l, w3), BF16),
        compiler_params=_cparams(("parallel", "parallel"), 40),
        name="gdn_conv",
    )(proj, proj, proj, conv_w)


def _hgrn_chunks(probs, states, emat):
    c = probs[0]["q"].shape[0]
    nb = c // SUB
    for p in probs:
        lb = p["lb"]
        sig = _sigmoid(p["fpre"])
        p["k"] = (1.0 - lb) * (1.0 - sig)
        p["g"] = _cumsum_rows(jnp.log(lb + (1.0 - lb) * sig), p["rev"])

    walls = []
    for p in probs:
        q = p["q"]
        g2 = p["g"] * LOG2_E
        lk2 = jnp.log(p["k"]) * LOG2_E
        rel = lk2 - g2
        rows = []
        half = SUB // 2
        dead = jnp.zeros((half, HEAD_DIM), F32)
        for j in range(nb):
            r0 = j * SUB
            pieces = []
            for s in range(SUB):
                skip = 0 if (s >= half and not p["rev"]) else 1 if (s < half and p["rev"]) else None
                parts = []
                for part in range(2):
                    if part == skip:
                        parts.append(dead)
                        continue
                    t0 = r0 + part * half
                    e2 = jnp.exp2(jnp.minimum(g2[t0:t0 + half] + rel[r0 + s:r0 + s + 1], lk2[r0 + s:r0 + s + 1]))
                    parts.append(e2 * q[t0:t0 + half])
                pieces.append(jnp.concatenate(parts, axis=0).astype(BF16))
            rows.append(jnp.concatenate(pieces, axis=1))
        walls.append(jnp.concatenate(rows, axis=0))
    a_diags = [_dot(w, emat) for w in walls]

    a_offs = []
    for p in probs:
        q, k, g, rev = p["q"], p["k"], p["g"], p["rev"]
        offs = []
        for i in range(nb):
            r0 = i * SUB
            if (not rev and i == 0) or (rev and i == nb - 1):
                offs.append(jnp.zeros((SUB, c), F32))
                continue
            bi = g[r0 + SUB:r0 + SUB + 1] if rev else g[r0 - 1:r0]
            lhs = (q[r0:r0 + SUB] * jnp.exp(g[r0:r0 + SUB] - bi)).astype(BF16)
            lo, hi = (r0 + SUB, c) if rev else (0, r0)
            live = k[lo:hi] * jnp.exp(bi - g[lo:hi])
            pad = jnp.zeros((c - (hi - lo), HEAD_DIM), F32)
            rhs = jnp.concatenate([pad, live] if rev else [live, pad], axis=0)
            offs.append(_dot_nt(lhs, rhs.astype(BF16)))
        a_offs.append(jnp.concatenate(offs, axis=0))

    r = lax.broadcasted_iota(jnp.int32, (c, c), 0)
    col = lax.broadcasted_iota(jnp.int32, (c, c), 1)
    shift = SUB.bit_length() - 1
    rb = lax.shift_right_logical(r, shift)
    cb = lax.shift_right_logical(col, shift)
    inters = [_dot_nt((p["q"] * jnp.exp(p["g"])).astype(BF16), s_t.astype(BF16)) for p, s_t in zip(probs, states)]
    outs, new_states = [], []
    for p, a_diag, a_off, inter, s_t in zip(probs, a_diags, a_offs, inters, states):
        rev, g = p["rev"], p["g"]
        order = (col >= r) if rev else (col <= r)
        off_side = (cb > rb) if rev else (cb < rb)
        a = jnp.where((rb == cb) & order, a_diag, jnp.where(off_side, a_off, 0.0))
        g_end = g[0:1] if rev else g[c - 1:c]
        vb = p["v"].astype(BF16)
        outs.append(_dot(a.astype(BF16), vb) + inter)
        kd = (p["k"] * jnp.exp(g_end - g)).astype(BF16)
        new_states.append(s_t * jnp.exp(g_end) + _dot_tn(vb, kd))
    return outs, new_states


def _hgrn_kernel(qf_ref, ff_ref, vf_ref, qb_ref, fb_ref, vb_ref, lb_ref, e_ref, s0_ref,
                 of_ref, ob_ref, sout_ref, s_ref):
    n = pl.program_id(1)

    @pl.when(n == 0)
    def _():
        s_ref[...] = s0_ref[...]

    probs, where = [], []
    for r in range(qf_ref.shape[0]):
        for d, (q_ref, f_ref, v_ref, o_ref) in enumerate(((qf_ref, ff_ref, vf_ref, of_ref),
                                                          (qb_ref, fb_ref, vb_ref, ob_ref))):
            for h in range(N_HEADS):
                sl = slice(h * HEAD_DIM, (h + 1) * HEAD_DIM)
                probs.append(dict(q=q_ref[r, :, sl].astype(F32), fpre=f_ref[r, :, sl],
                                  v=v_ref[r, :, sl].astype(F32), lb=lb_ref[:, sl], rev=d == 1))
                where.append((o_ref, sl, r, h, d))
    states = [s_ref[r, h, d] for _, _, r, h, d in where]
    outs, new_states = [], []
    for g0 in range(0, len(probs), HGRN_GROUP):
        o_g, s_g = _hgrn_chunks(probs[g0:g0 + HGRN_GROUP], states[g0:g0 + HGRN_GROUP], e_ref[...])
        outs += o_g
        new_states += s_g
    for (o_ref, sl, r, h, d), o, s_new in zip(where, outs, new_states):
        o_ref[r, :, sl] = o.astype(o_ref.dtype)
        s_ref[r, h, d] = s_new

    @pl.when(n == pl.num_programs(1) - 1)
    def _():
        sout_ref[...] = s_ref[...]


def _hgrn_emat():
    r = jnp.arange(SUB * HEAD_DIM)[:, None] // HEAD_DIM
    c = jnp.arange(CHUNK)[None, :] % SUB
    return (r == c).astype(BF16)


def _hgrn_scan(p_lo, p_hi, lb, s0):
    b, l, _ = p_lo.shape
    nc = l // CHUNK
    rows = HGRN_BATCH_ROWS if b % HGRN_BATCH_ROWS == 0 else 1
    q_col, i_col, f_col = LO_AQ // MIX_W, LO_AI // MIX_W, HI_AF // MIX_W

    def fwd(col):
        return pl.BlockSpec((rows, CHUNK, MIX_W), lambda bi, n: (bi, n, col))

    def bwd(col):
        return pl.BlockSpec((rows, CHUNK, MIX_W), lambda bi, n: (bi, nc - 1 - n, col))

    st_spec = pl.BlockSpec((rows, N_HEADS, 2, HEAD_DIM, HEAD_DIM), lambda bi, n: (bi, 0, 0, 0, 0))
    return pl.pallas_call(
        _hgrn_kernel,
        grid=(b // rows, nc),
        in_specs=[fwd(q_col), fwd(f_col), fwd(i_col), bwd(q_col), bwd(f_col + 1), bwd(i_col),
                  pl.BlockSpec((1, MIX_W), lambda bi, n: (0, 0)),
                  pl.BlockSpec((SUB * HEAD_DIM, CHUNK), lambda bi, n: (0, 0)),
                  st_spec],
        out_specs=[pl.BlockSpec((rows, CHUNK, MIX_W), lambda bi, n: (bi, n, 0)),
                   pl.BlockSpec((rows, CHUNK, MIX_W), lambda bi, n: (bi, nc - 1 - n, 0)),
                   st_spec],
        out_shape=[jax.ShapeDtypeStruct((b, l, MIX_W), BF16),
                   jax.ShapeDtypeStruct((b, l, MIX_W), BF16),
                   jax.ShapeDtypeStruct(s0.shape, F32)],
        scratch_shapes=[pltpu.VMEM((rows, N_HEADS, 2, HEAD_DIM, HEAD_DIM), F32)],
        compiler_params=_cparams(("parallel", "arbitrary"), 40),
        name="hgrn_scan",
    )(p_lo, p_hi, p_lo, p_lo, p_hi, p_lo, lb, _hgrn_emat(), s0)


def _mlstm_chunks(probs, states):
    c = probs[0]["q"].shape[0]
    dv = probs[0]["v"].shape[1]
    ones = jnp.ones((c, dv), BF16)
    for p, (cmn, m) in zip(probs, states):
        tri = _order_mask(c, p["rev"])
        log_d = jnp.where(tri, p["bb_c"] - p["bb_r"] + p["ic_r"], MASK_NEG)
        log_inter = p["bb_c"] + m
        m_t = jnp.maximum(jnp.max(log_d, axis=1, keepdims=True), log_inter)
        p["m_t"] = m_t
        p["dmat"] = jnp.where(tri, jnp.exp(log_d - m_t), 0.0)
        p["inter"] = jnp.exp(log_inter - m_t)
        p["qb"], p["kb"] = p["q"].astype(BF16), p["k"].astype(BF16)
        p["v1"] = jnp.concatenate([p["v"].astype(BF16), ones], axis=1)
    scores = [_dot_nt(p["qb"], p["kb"]) * p["dmat"] for p in probs]
    carried = [_dot(p["qb"], cmn.astype(BF16)) for p, (cmn, _) in zip(probs, states)]
    local = [_dot(s.astype(BF16), p["v1"]) for p, s in zip(probs, scores)]
    outs, new_states = [], []
    for p, car, loc, (cmn, m) in zip(probs, carried, local, states):
        m_t = p["m_t"]
        both = loc + p["inter"] * car
        outs.append(both[:, :dv] / jnp.maximum(jnp.abs(both[:, dv:]), jnp.exp(-m_t)))
        m_new = m_t[0:1] if p["rev"] else m_t[c - 1:c]
        bb_end = p["bb_c"][0:1] if p["rev"] else p["bb_c"][c - 1:c]
        decay = jnp.exp(bb_end + m - m_new)
        kw = p["k"] * jnp.exp(bb_end - p["bb_c"] + p["ic_c"] - m_new)
        new_states.append((decay * cmn + _dot_tn(kw.astype(BF16), p["v1"]), m_new))
    return outs, new_states


def _column_view(ref):
    rows, group, width = ref.shape[-3:]
    return ref.reshape(rows * group, width), group


def _chunk_rows(j):
    start = j * CHUNK
    return pl.ds(start if isinstance(start, int) else pl.multiple_of(start, CHUNK), CHUNK)


def _load_chunk(ref, j, colmajor):
    if colmajor:
        flat, group = _column_view(ref)
        return flat[pl.ds(j, CHUNK, stride=group), :]
    return ref[0, _chunk_rows(j), :]


def _store_chunk(ref, j, val, colmajor):
    if colmajor:
        flat, group = _column_view(ref)
        flat[pl.ds(j, CHUNK, stride=group), :] = val
    else:
        ref[0, _chunk_rows(j), :] = val


def _columns_as_rows(x, first):
    row = lax.broadcasted_iota(jnp.int32, (8, LANE), 0)
    lane = lax.broadcasted_iota(jnp.int32, (8, LANE), 1)
    sel = jnp.where(lane == row + first, 1.0, 0.0).astype(BF16)
    hi = x.astype(BF16)
    rest = x - hi.astype(F32)
    mid = rest.astype(BF16)
    lo = (rest - mid.astype(F32)).astype(BF16)
    return _dot_nt(sel, hi) + (_dot_nt(sel, mid) + _dot_nt(sel, lo))


def _mlstm_kernel(*refs, colmajor, chunks):
    per_dir = 3 * N_HEADS + 1
    fwd, bwd = refs[:per_dir], refs[per_dir:2 * per_dir]
    bias_ref, cm0_ref, nm0_ref, of_ref, ob_ref, cmout_ref, nmout_ref, cm_ref, nm_ref = refs[2 * per_dir:]
    n = pl.program_id(1)

    @pl.when(n == 0)
    def _():
        cm_ref[...] = cm0_ref[0]
        nm_ref[...] = nm0_ref[0]

    def step(i, carry):
        probs, where = [], []
        for d, (in_refs, o_ref) in enumerate(((fwd, of_ref), (bwd, ob_ref))):
            rev = d == 1
            j = chunks - 1 - i if rev else i
            gates = _load_chunk(in_refs[-1], j, colmajor) + bias_ref[...]
            cs = _cumsum_rows(_log_sigmoid(gates), rev)
            cs_rows = _columns_as_rows(cs, 8 + 4 * d)
            gate_rows = _columns_as_rows(gates, 4 * d)
            for h in range(N_HEADS):
                ci = 4 * d + h
                cf = 8 + 4 * d + h
                probs.append(dict(
                    q=_load_chunk(in_refs[h], j, colmajor),
                    k=_load_chunk(in_refs[N_HEADS + h], j, colmajor) * HEAD_DIM ** -0.5,
                    v=_load_chunk(in_refs[2 * N_HEADS + h], j, colmajor),
                    bb_c=cs[:, cf:cf + 1], bb_r=cs_rows[h:h + 1, :],
                    ic_c=gates[:, ci:ci + 1], ic_r=gate_rows[h:h + 1, :], rev=rev))
                where.append((o_ref, j, d, h))
        states = [(cm_ref[d, h], nm_ref[d, h, 0:1, 0:1]) for _, _, d, h in where]
        outs, new_states = [], []
        for g0 in range(0, len(probs), MLSTM_GROUP):
            o_g, s_g = _mlstm_chunks(probs[g0:g0 + MLSTM_GROUP], states[g0:g0 + MLSTM_GROUP])
            outs += o_g
            new_states += s_g
        for (o_ref, j, d, h), o, (cmn_new, m_new) in zip(where, outs, new_states):
            _store_chunk(o_ref.at[h], j, o, colmajor)
            cm_ref[d, h] = cmn_new
            nm_ref[d, h, 0:1, :] = jnp.broadcast_to(m_new, (1, HEAD_DIM))
        return carry

    lax.fori_loop(0, chunks, step, 0)

    @pl.when(n == pl.num_programs(1) - 1)
    def _():
        cmout_ref[0] = cm_ref[...]
        nmout_ref[0] = nm_ref[...]


def _mlstm_scan(proj, gbias, cm0, nm0, colmajor):
    b, l, n_proj = proj.shape
    cb = HI_B // LANE
    cg = HI_GATES // LANE
    if colmajor:
        chunks = 8
        assert l == CHUNK * GRID_W and GRID_W % chunks == 0
        nb = GRID_W // chunks
        src = proj.reshape(b, CHUNK, GRID_W, n_proj)
        in_blk = (1, CHUNK, chunks, LANE)
        out_blk = (N_HEADS, 1, CHUNK, chunks, LANE)
        out_view = (N_HEADS, b, CHUNK, GRID_W, LANE)

        def in_map(col, flip):
            return lambda bi, n: (bi, 0, nb - 1 - n if flip else n, col)

        def out_map(flip):
            return lambda bi, n: (0, bi, 0, nb - 1 - n if flip else n, 0)
    else:
        chunks = min(l // CHUNK, 4)
        nb = l // (chunks * CHUNK)
        src = proj
        in_blk = (1, chunks * CHUNK, LANE)
        out_blk = (N_HEADS, 1, chunks * CHUNK, LANE)
        out_view = (N_HEADS, b, l, LANE)

        def in_map(col, flip):
            return lambda bi, n: (bi, nb - 1 - n if flip else n, col)

        def out_map(flip):
            return lambda bi, n: (0, bi, nb - 1 - n if flip else n, 0)

    def specs(flip):
        cols = [cb + j for j in range(3 * N_HEADS)] + [cg]
        return [pl.BlockSpec(in_blk, in_map(col, flip)) for col in cols]

    cm_spec = pl.BlockSpec((1, 2, N_HEADS, HEAD_DIM, 2 * HEAD_DIM), lambda bi, n: (bi, 0, 0, 0, 0))
    nm_spec = pl.BlockSpec((1, 2, N_HEADS, 8, HEAD_DIM), lambda bi, n: (bi, 0, 0, 0, 0))
    n_in = 2 * (3 * N_HEADS + 1)
    o_f, o_b, cm, nm = pl.pallas_call(
        functools.partial(_mlstm_kernel, colmajor=colmajor, chunks=chunks),
        grid=(b, nb),
        in_specs=specs(False) + specs(True) + [pl.BlockSpec((1, LANE), lambda bi, n: (0, 0)), cm_spec, nm_spec],
        out_specs=[pl.BlockSpec(out_blk, out_map(False)), pl.BlockSpec(out_blk, out_map(True)), cm_spec, nm_spec],
        out_shape=[jax.ShapeDtypeStruct(out_view, F32), jax.ShapeDtypeStruct(out_view, F32),
                   jax.ShapeDtypeStruct(cm0.shape, F32), jax.ShapeDtypeStruct(nm0.shape, F32)],
        scratch_shapes=[pltpu.VMEM((2, N_HEADS, HEAD_DIM, 2 * HEAD_DIM), F32),
                        pltpu.VMEM((2, N_HEADS, 8, HEAD_DIM), F32)],
        compiler_params=_cparams(("parallel", "arbitrary"), 40),
        name="mlstm_scan",
    )(*([src] * n_in), gbias, cm0, nm0)
    head_major = (N_HEADS, b, l, LANE)
    return o_f.reshape(head_major), o_b.reshape(head_major), cm, nm


def _split_bf16(x):
    hi = x.astype(BF16)
    return hi, (x - hi.astype(F32)).astype(BF16)


def _dot3(a, b, nt=False):
    f = _dot_nt if nt else _dot
    return f(a[0], b[0]) + (f(a[0], b[1]) + f(a[1], b[0]))


def _inv_unit_triangular(a_list):
    n = a_list[0].shape[0]
    eye = (lax.broadcasted_iota(jnp.int32, (n, n), 0) == lax.broadcasted_iota(jnp.int32, (n, n), 1)).astype(F32)
    xs = [eye - a for a in a_list]
    splits = [_split_bf16(a) for a in a_list]
    ps = [_dot3(s, s) for s in splits]
    steps = (n - 1).bit_length() - 1
    for step in range(steps):
        last = step == steps - 1
        p_split = [_split_bf16(p) for p in ps]
        lhs = xs if last else [jnp.concatenate([x, p], axis=0) for x, p in zip(xs, ps)]
        prods = [_dot3(_split_bf16(l), s) for l, s in zip(lhs, p_split)]
        xs = [x + pr[:n] for x, pr in zip(xs, prods)]
        if not last:
            ps = [pr[n:] for pr in prods]
    return xs


def _gdn_chunks(probs, states):
    c = probs[0]["q"].shape[0]
    a_list, ldecs = [], []
    for p in probs:
        tri = _order_mask(c, p["rev"])
        strict = _order_mask(c, p["rev"], strict=True)
        ldec = jnp.where(tri, jnp.exp(jnp.minimum(p["gam_c"] - p["gam_r"], 0.0)), 0.0)
        p["kbeta"] = p["k"] * p["beta"]
        kk = _dot_nt(p["kbeta"].astype(BF16), p["k"].astype(BF16))
        a_list.append(jnp.where(strict, kk * ldec, 0.0))
        ldecs.append(ldec)
    t_invs = _inv_unit_triangular(a_list)
    sols = []
    for p, t_inv in zip(probs, t_invs):
        p["eg"] = jnp.exp(p["gam_c"])
        rhs = jnp.concatenate([p["v"] * p["beta"], p["kbeta"] * p["eg"]], axis=1)
        sols.append(_dot3(_split_bf16(t_inv), _split_bf16(rhs)))
    qks = [_dot_nt(p["q"].astype(BF16), p["k"].astype(BF16)) * ldec for p, ldec in zip(probs, ldecs)]
    inter = [_dot(jnp.concatenate([sol[:, HEAD_DIM:], p["q"] * p["eg"]], axis=0).astype(BF16), s.astype(BF16))
             for p, sol, s in zip(probs, sols, states)]
    outs, new_states = [], []
    for p, sol, qk, it, s in zip(probs, sols, qks, inter, states):
        g_end = p["gam_c"][0:1] if p["rev"] else p["gam_c"][c - 1:c]
        v_new = (sol[:, :HEAD_DIM] - it[:c]).astype(BF16)
        kg = p["k"] * jnp.exp(g_end - p["gam_c"])
        outs.append(it[c:] + _dot(qk.astype(BF16), v_new))
        new_states.append(jnp.exp(g_end) * s + _dot_tn(kg.astype(BF16), v_new))
    return outs, new_states


def _gdn_kernel(xf_ref, gf_ref, xb_ref, gb_ref, alog_ref, dtb_ref, s0_ref,
                of_ref, ob_ref, sout_ref, s_ref):
    n = pl.program_id(1)

    @pl.when(n == 0)
    def _():
        s_ref[...] = s0_ref[...]

    neg_a = -jnp.exp(alog_ref[...])
    probs, where = [], []
    for r in range(xf_ref.shape[0]):
        for d, (x_ref, g_ref, o_ref) in enumerate(((xf_ref, gf_ref, of_ref), (xb_ref, gb_ref, ob_ref))):
            rev = d == 1
            gates = g_ref[r]
            cs = _cumsum_rows(neg_a * _softplus(gates + dtb_ref[...]), rev)
            cs_t = cs.T
            betas = _sigmoid(gates)
            for h in range(N_HEADS):
                ca = 16 + 4 * d + h
                cbeta = 24 + 4 * d + h
                probs.append(dict(
                    q=x_ref[r, :, h * HEAD_DIM:(h + 1) * HEAD_DIM].astype(F32),
                    k=x_ref[r, :, MIX_W + h * HEAD_DIM:MIX_W + (h + 1) * HEAD_DIM].astype(F32),
                    v=x_ref[r, :, 2 * MIX_W + h * HEAD_DIM:2 * MIX_W + (h + 1) * HEAD_DIM].astype(F32),
                    gam_c=cs[:, ca:ca + 1], gam_r=cs_t[ca:ca + 1, :], beta=betas[:, cbeta:cbeta + 1], rev=rev))
                where.append((o_ref, r, d, h))
    outs, new_states = _gdn_chunks(probs, [s_ref[r, d, h] for _, r, d, h in where])
    for (o_ref, r, d, h), o, s_new in zip(where, outs, new_states):
        o_ref[r, :, h * HEAD_DIM:(h + 1) * HEAD_DIM] = o.astype(o_ref.dtype)
        s_ref[r, d, h] = s_new

    @pl.when(n == pl.num_programs(1) - 1)
    def _():
        sout_ref[...] = s_ref[...]


def _gdn_scan(qkv, proj, alog_row, dtb_row, s0):
    b, l, w3 = qkv.shape
    nc = l // CHUNK
    cg = HI_GATES // LANE
    rows = GDN_BATCH_ROWS if b % GDN_BATCH_ROWS == 0 else 1
    st_spec = pl.BlockSpec((rows, 2, N_HEADS, HEAD_DIM, HEAD_DIM), lambda bi, n: (bi, 0, 0, 0, 0))
    row_spec = pl.BlockSpec((1, LANE), lambda bi, n: (0, 0))
    return pl.pallas_call(
        _gdn_kernel,
        grid=(b // rows, nc),
        in_specs=[pl.BlockSpec((rows, CHUNK, w3), lambda bi, n: (bi, n, 0)),
                  pl.BlockSpec((rows, CHUNK, LANE), lambda bi, n: (bi, n, cg)),
                  pl.BlockSpec((rows, CHUNK, w3), lambda bi, n: (bi, nc - 1 - n, 0)),
                  pl.BlockSpec((rows, CHUNK, LANE), lambda bi, n: (bi, nc - 1 - n, cg)),
                  row_spec, row_spec, st_spec],
        out_specs=[pl.BlockSpec((rows, CHUNK, MIX_W), lambda bi, n: (bi, n, 0)),
                   pl.BlockSpec((rows, CHUNK, MIX_W), lambda bi, n: (bi, nc - 1 - n, 0)),
                   st_spec],
        out_shape=[jax.ShapeDtypeStruct((b, l, MIX_W), BF16), jax.ShapeDtypeStruct((b, l, MIX_W), BF16),
                   jax.ShapeDtypeStruct(s0.shape, F32)],
        scratch_shapes=[pltpu.VMEM((rows, 2, N_HEADS, HEAD_DIM, HEAD_DIM), F32)],
        compiler_params=_cparams(("parallel", "arbitrary"), 40),
        name="gdn_scan",
    )(qkv, proj, qkv, proj, alog_row, dtb_row, s0)


def _merge_kernel(oaf, oab, obf, obb, ocf, ocb, ag, bo, cz, mg, x_ref, mod_ref, gain_ref, wbr_ref, wout_ref,
                  lng_ref, lnb_ref, out_ref, *, alpha):
    d_model = x_ref.shape[-1]
    branches = ((oaf, oab, ag, _silu), (obf, obb, bo, _sigmoid), (ocf, ocb, cz, _silu))
    mixed = None
    for nbr, (of_ref, ob_ref, gate_ref, act) in enumerate(branches):
        parts = []
        for h in range(N_HEADS):
            if len(of_ref.shape) == 4:
                oh = of_ref[h, 0] + ob_ref[h, 0]
            else:
                sl = slice(h * HEAD_DIM, (h + 1) * HEAD_DIM)
                oh = of_ref[0, :, sl].astype(F32) + ob_ref[0, :, sl].astype(F32)
            parts.append(oh * lax.rsqrt(jnp.mean(oh * oh, axis=-1, keepdims=True) + RMS_EPS))
        y = jnp.concatenate(parts, axis=1) * gain_ref[nbr:nbr + 1, :] * act(gate_ref[0].astype(F32))
        gate = _sigmoid(mg[0, :, nbr * d_model:(nbr + 1) * d_model].astype(F32))
        term = gate * _dot(y.astype(BF16), wbr_ref[nbr])
        mixed = term if mixed is None else mixed + term
    sub = _dot(mixed.astype(BF16), wout_ref[...])
    z = alpha * x_ref[0] + mod_ref[0, 2:3, :] * sub
    out_ref[0] = _ln(z) * lng_ref[...] + lnb_ref[...]


def _merge(outs, proj, x, mod, gain, w_branch, w_out, ln_g, ln_b, alpha, tl):
    b, l, d = x.shape
    u = MIX_W

    def tok(width, col):
        return pl.BlockSpec((1, tl, width), lambda bi, i: (bi, i, col))

    def const(shape):
        return pl.BlockSpec(shape, lambda bi, i: (0,) * len(shape))

    heads = pl.BlockSpec((N_HEADS, 1, tl, LANE), lambda bi, i: (0, bi, i, 0))
    return pl.pallas_call(
        functools.partial(_merge_kernel, alpha=alpha),
        grid=(b, l // tl),
        in_specs=[tok(u, 0), tok(u, 0), heads, heads, tok(u, 0), tok(u, 0),
            tok(u, LO_AG // u), tok(u, LO_BO // u), tok(u, LO_CZ // u),
            tok(3 * d, LO_MERGE // (3 * d)), tok(d, 0),
            pl.BlockSpec((1, 6, d), lambda bi, i: (bi, 0, 0)),
            const((3, u)), const((3, u, d)), const((d, d)), const((1, d)), const((1, d))],
        out_specs=tok(d, 0),
        out_shape=jax.ShapeDtypeStruct((b, l, d), F32),
        compiler_params=_cparams(("parallel", "parallel"), 48),
        name="merge",
    )(*outs, proj, proj, proj, proj, x, mod, gain, w_branch, w_out, ln_g, ln_b)


def _ffn_kernel(x_ref, mod_ref, wgu_ref, wd_ref, lng_ref, lnb_ref, out_ref, h_ref, acc_ref, *, alpha):
    f = pl.program_id(2)

    @pl.when(f == 0)
    def _():
        h_ref[...] = (_ln(x_ref[0]) * (1.0 + mod_ref[0, 4:5, :]) + mod_ref[0, 3:4, :]).astype(BF16)
        acc_ref[...] = jnp.zeros_like(acc_ref)

    gu = _dot(h_ref[...], wgu_ref[0, 0])
    half = gu.shape[1] // 2
    act = _silu(gu[:, :half]) * gu[:, half:]
    acc_ref[...] += _dot(act.astype(BF16), wd_ref[...])

    @pl.when(f == pl.num_programs(2) - 1)
    def _():
        z = alpha * x_ref[0] + mod_ref[0, 5:6, :] * acc_ref[...]
        out_ref[0] = _ln(z) * lng_ref[...] + lnb_ref[...]


def _ffn(x, mod, w_gate_up, w_down, ln_g, ln_b, alpha, tl):
    b, l, d = x.shape
    _, nf, _, tf2 = w_gate_up.shape
    return pl.pallas_call(
        functools.partial(_ffn_kernel, alpha=alpha),
        grid=(b, l // tl, nf),
        in_specs=[pl.BlockSpec((1, tl, d), lambda bi, i, j: (bi, i, 0)),
                  pl.BlockSpec((1, 6, d), lambda bi, i, j: (bi, 0, 0)),
                  pl.BlockSpec((1, 1, d, tf2), lambda bi, i, j: (0, j, 0, 0)),
                  pl.BlockSpec((tf2 // 2, d), lambda bi, i, j: (j, 0)),
                  pl.BlockSpec((1, d), lambda bi, i, j: (0, 0)),
                  pl.BlockSpec((1, d), lambda bi, i, j: (0, 0))],
        out_specs=pl.BlockSpec((1, tl, d), lambda bi, i, j: (bi, i, 0)),
        out_shape=jax.ShapeDtypeStruct((b, l, d), F32),
        scratch_shapes=[pltpu.VMEM((tl, d), BF16), pltpu.VMEM((tl, d), F32)],
        compiler_params=_cparams(("parallel", "parallel", "arbitrary"), 56),
        name="ffn",
    )(x, mod, w_gate_up, w_down, ln_g, ln_b)


def _top2_combine(logits):
    lane = lax.broadcasted_iota(jnp.int32, logits.shape, 1).astype(F32)
    neg = -jnp.inf
    lg = jnp.where(lane < N_EXPERTS, logits, neg)
    m1 = jnp.max(lg, axis=1, keepdims=True)
    i1 = jnp.min(jnp.where(lg == m1, lane, float(LANE)), axis=1, keepdims=True)
    lg2 = jnp.where(lane == i1, neg, lg)
    m2 = jnp.max(lg2, axis=1, keepdims=True)
    i2 = jnp.min(jnp.where(lg2 == m2, lane, float(LANE)), axis=1, keepdims=True)
    e2 = jnp.exp(m2 - m1)
    p1 = 1.0 / (1.0 + e2)
    chosen = (lane == i1) | (lane == i2)
    return jnp.where(lane == i1, p1, jnp.where(lane == i2, e2 * p1, 0.0)), chosen


def _moe_kernel(x_ref, mod_ref, wr_ref, wgu_ref, wd_ref, lng_ref, lnb_ref, out_ref,
                h_ref, acc_ref, comb_ref, pos_ref, post_ref, col_ref, xc_ref, yc_ref, cnt_ref, *, alpha, rb):
    e = pl.program_id(2)
    f = pl.program_id(3)
    tl, d = h_ref.shape

    @pl.when((e == 0) & (f == 0))
    def _():
        h = _ln(x_ref[0]) * (1.0 + mod_ref[0, 4:5, :]) + mod_ref[0, 3:4, :]
        h_ref[...] = h.astype(BF16)
        comb, chosen = _top2_combine(_dot(h, wr_ref[...], precision=HIGHEST))
        comb_ref[...] = comb
        blk = min(tl, LANE)
        before = jnp.where(_order_mask(blk, False, strict=True), 1.0, 0.0).astype(BF16)
        routed = jnp.where(chosen, 1.0, 0.0)
        total = jnp.zeros((1, LANE), F32)
        ranks = []
        for r0 in range(0, tl, blk):
            part = routed[r0:r0 + blk]
            ranks.append(_dot(before, part.astype(BF16)) + total)
            total = total + jnp.sum(part, axis=0, keepdims=True)
        rank = jnp.concatenate(ranks, axis=0)
        pos = jnp.where(chosen, rank, -1.0)
        pos_ref[...] = pos
        pos_t = pos.T[0:N_EXPERTS, :]
        post_ref[...] = pos_t
        for ex in range(N_EXPERTS):
            cnt_ref[ex] = jnp.sum(jnp.where(pos_t[ex:ex + 1, :] >= 0.0, 1.0, 0.0)).astype(jnp.int32)
        acc_ref[...] = jnp.zeros_like(acc_ref)

    nblk = (cnt_ref[e] + (rb - 1)) // rb

    def rows(r):
        return pl.ds(pl.multiple_of(r * rb, rb), rb)

    @pl.when(f == 0)
    def _():
        lane = lax.broadcasted_iota(jnp.int32, (tl, LANE), 1)
        col_ref[0] = jnp.sum(jnp.where(lane == e, pos_ref[...], 0.0), axis=1, keepdims=True)
        col_ref[1] = jnp.sum(jnp.where(lane == e, comb_ref[...], 0.0), axis=1, keepdims=True)
        pos_row = post_ref[pl.ds(e, 1), :]

        def gather(r, carry):
            slot = (lax.broadcasted_iota(jnp.int32, (rb, tl), 0) + r * rb).astype(F32)
            sel = jnp.where(pos_row == slot, 1.0, 0.0).astype(BF16)
            xc_ref[rows(r), :] = _dot(sel, h_ref[...]).astype(BF16)
            yc_ref[rows(r), :] = jnp.zeros((rb, d), F32)
            return carry

        lax.fori_loop(0, nblk, gather, 0)

    def expert(r, carry):
        xb = xc_ref[rows(r), :]
        gu = _dot(xb, wgu_ref[0, 0])
        half = gu.shape[1] // 2
        act = _silu(gu[:, :half]) * gu[:, half:]
        yc_ref[rows(r), :] += _dot(act.astype(BF16), wd_ref[0])
        return carry

    lax.fori_loop(0, nblk, expert, 0)

    @pl.when(f == pl.num_programs(3) - 1)
    def _():
        pos_col = col_ref[0]
        weight = col_ref[1]

        def scatter(r, carry):
            slot = (lax.broadcasted_iota(jnp.int32, (tl, rb), 1) + r * rb).astype(F32)
            sel_t = jnp.where(pos_col == slot, 1.0, 0.0).astype(BF16)
            acc_ref[...] += weight * _dot(sel_t, yc_ref[rows(r), :].astype(BF16))
            return carry

        lax.fori_loop(0, nblk, scatter, 0)

    @pl.when((e == pl.num_programs(2) - 1) & (f == pl.num_programs(3) - 1))
    def _():
        z = alpha * x_ref[0] + mod_ref[0, 5:6, :] * acc_ref[...]
        out_ref[0] = _ln(z) * lng_ref[...] + lnb_ref[...]


def _pack_gate_up(w_gate, w_up, widest):
    e, d, f = w_gate.shape
    tf = next(t for t in (1792, 896, 512, 256, 128) if t <= widest and f % t == 0)

    def tiles(w):
        return w.astype(BF16).reshape(e, d, f // tf, tf).transpose(0, 2, 1, 3)

    return jnp.concatenate([tiles(w_gate), tiles(w_up)], axis=-1)


def _moe(x, mod, w_router, w_gate_up, w_down, ln_g, ln_b, alpha, tl):
    b, l, d = x.shape
    ne, nf, _, tf2 = w_gate_up.shape
    rb = min(MOE_ROW_BLOCK, tl)
    compact_rows = -(-tl // rb) * rb
    return pl.pallas_call(
        functools.partial(_moe_kernel, alpha=alpha, rb=rb),
        grid=(b, l // tl, ne, nf),
        in_specs=[pl.BlockSpec((1, tl, d), lambda bi, i, e, j: (bi, i, 0)),
                  pl.BlockSpec((1, 6, d), lambda bi, i, e, j: (bi, 0, 0)),
                  pl.BlockSpec((d, LANE), lambda bi, i, e, j: (0, 0)),
                  pl.BlockSpec((1, 1, d, tf2), lambda bi, i, e, j: (e, j, 0, 0)),
                  pl.BlockSpec((1, tf2 // 2, d), lambda bi, i, e, j: (e, j, 0)),
                  pl.BlockSpec((1, d), lambda bi, i, e, j: (0, 0)),
                  pl.BlockSpec((1, d), lambda bi, i, e, j: (0, 0))],
        out_specs=pl.BlockSpec((1, tl, d), lambda bi, i, e, j: (bi, i, 0)),
        out_shape=jax.ShapeDtypeStruct((b, l, d), F32),
        scratch_shapes=[pltpu.VMEM((tl, d), BF16),
                        pltpu.VMEM((tl, d), F32),
                        pltpu.VMEM((tl, LANE), F32),
                        pltpu.VMEM((tl, LANE), F32),
                        pltpu.VMEM((N_EXPERTS, tl), F32),
                        pltpu.VMEM((2, tl, 1), F32),
                        pltpu.VMEM((compact_rows, d), BF16),
                        pltpu.VMEM((compact_rows, d), F32),
                        pltpu.SMEM((N_EXPERTS,), jnp.int32)],
        compiler_params=_cparams(("parallel", "parallel", "arbitrary", "arbitrary"), 60),
        name="moe",
    )(x, mod, w_router, w_gate_up, w_down, ln_g, ln_b)


def _reorder_w_in(w):
    d = w.shape[0]
    u = MIX_W
    o_a, o_b = 0, 5 * u
    o_bgt = o_b + 4 * u
    o_cqkv = o_bgt + 16
    o_cz = o_cqkv + 3 * u
    o_cgt = o_cz + u
    o_merge = o_cgt + 16
    lo = [w[:, o_merge:o_merge + 3 * d], w[:, o_cqkv:o_cz], w[:, o_a:o_a + u], w[:, o_a + 3 * u:o_a + 5 * u],
          w[:, o_b + 3 * u:o_bgt], w[:, o_cz:o_cgt]]
    hi = [w[:, o_a + u:o_a + 3 * u], w[:, o_b:o_b + 3 * u], w[:, o_bgt:o_bgt + 16], w[:, o_cgt:o_cgt + 16],
          jnp.zeros((d, N_HI - HI_GATES - 32), w.dtype)]
    return jnp.concatenate(lo, axis=1).astype(BF16), jnp.concatenate(hi, axis=1).astype(BF16)


def _pad_row(vals, offset):
    row = jnp.zeros((1, LANE), F32)
    return lax.dynamic_update_slice(row, vals.reshape(1, -1).astype(F32), (0, offset))


def _tile(l, pref):
    return pref if l % pref == 0 else l


def kernel(x, c, ctx, c_ctx, w_ada, b_ada, w_in, conv_w, lb_raw, m_gate_bias, gdn_a_log, gdn_dt_bias, mix_norm,
           w_branch, w_out, ln_g, ln_b, ffn_w_gate, ffn_w_up, ffn_w_down, moe_router, moe_w_gate, moe_w_up,
           moe_w_down):
    depth, d_model = w_in.shape[0], w_in.shape[1]
    b, l_lat, _ = x.shape
    l_ctx = ctx.shape[1]
    alpha = float((2 * depth) ** 0.25)
    assert d_model == 1024 and w_in.shape[2] == 9 * MIX_W + 3 * MIX_W + MIX_W + 32 + 3 * d_model

    lower = _lower_bounds(lb_raw.astype(F32))
    n_rows = -(-(b + 1) // 8) * 8
    cvec = jnp.concatenate([c, c_ctx[None, :], jnp.zeros((n_rows - b - 1, d_model), F32)], axis=0)
    mods = _ada(cvec, w_ada, b_ada)

    zeros_state = jnp.zeros((b, 2, N_HEADS, HEAD_DIM, HEAD_DIM), F32)
    zeros_hgrn = jnp.zeros((b, N_HEADS, 2, HEAD_DIM, HEAD_DIM), F32)
    zeros_nm = jnp.zeros((b, 2, N_HEADS, 8, HEAD_DIM), F32)
    zeros_cmn = jnp.zeros((b, 2, N_HEADS, HEAD_DIM, 2 * HEAD_DIM), F32)

    pack = 1024 // l_ctx if (1024 % l_ctx == 0 and b % max(1024 // l_ctx, 1) == 0) else 1
    ctx_packed = (b // pack, pack * l_ctx, d_model)

    for l in range(depth):
        mod_l = mods[l, :b].reshape(b, 6, d_model)
        mod_c = jnp.broadcast_to(mods[l, b].reshape(1, 6, d_model), (b, 6, d_model))
        w_lo, w_hi = _reorder_w_in(w_in[l])
        lb_row = lower[l].reshape(1, MIX_W)
        gbias = _pad_row(m_gate_bias[l], 0)
        alog_row = _pad_row(gdn_a_log[l], 16)
        dtb_row = _pad_row(gdn_dt_bias[l], 16)
        gain = mix_norm[l]
        wbr = w_branch[l].astype(BF16)
        wo = w_out[l].astype(BF16)
        g0, b0 = ln_g[l, 0].reshape(1, -1), ln_b[l, 0].reshape(1, -1)
        g1, b1 = ln_g[l, 1].reshape(1, -1), ln_b[l, 1].reshape(1, -1)

        lo_c, hi_c = _in_proj(ctx.reshape(ctx_packed), mod_c[:ctx_packed[0]], w_lo, w_hi, ctx_packed[1])
        lo_c, hi_c = lo_c.reshape(b, l_ctx, N_LO), hi_c.reshape(b, l_ctx, N_HI)
        lo_l, hi_l = _in_proj(x, mod_l, w_lo, w_hi, _tile(l_lat, 1024))
        qkv_c = _gdn_conv(lo_c, conv_w[l], _tile(l_ctx, 256))
        qkv_l = _gdn_conv(lo_l, conv_w[l], _tile(l_lat, 512))

        oa_c = _hgrn_scan(lo_c, hi_c, lb_row, zeros_hgrn)
        oa_l = _hgrn_scan(lo_l, hi_l, lb_row, oa_c[2])
        ob_c = _mlstm_scan(hi_c, gbias, zeros_cmn, zeros_nm, colmajor=False)
        ob_l = _mlstm_scan(hi_l, gbias, ob_c[2], ob_c[3], colmajor=True)
        oc_c = _gdn_scan(qkv_c, hi_c, alog_row, dtb_row, zeros_state)
        oc_l = _gdn_scan(qkv_l, hi_l, alog_row, dtb_row, oc_c[2])

        def channel_mixer(t, mod, tl):
            i = l // 2
            if l % 2 == 0:
                return _ffn(t, mod, _pack_gate_up(ffn_w_gate[i][None], ffn_w_up[i][None], 896),
                            ffn_w_down[i].astype(BF16), g1, b1, alpha, tl)
            w_r = jnp.concatenate([moe_router[i], jnp.zeros((d_model, LANE - N_EXPERTS), F32)], axis=1)
            return _moe(t, mod, w_r, _pack_gate_up(moe_w_gate[i], moe_w_up[i], 1792), moe_w_down[i].astype(BF16),
                        g1, b1, alpha, tl)

        outs_l = (oa_l[0], oa_l[1], ob_l[0], ob_l[1], oc_l[0], oc_l[1])
        x = _merge(outs_l, lo_l, x, mod_l, gain, wbr, wo, g0, b0, alpha, _tile(l_lat, 512))
        x = channel_mixer(x, mod_l, _tile(l_lat, 1024))
        if l < depth - 1:
            outs_c = (oa_c[0], oa_c[1], ob_c[0], ob_c[1], oc_c[0], oc_c[1])
            ctx = _merge(outs_c, lo_c, ctx, mod_c, gain, wbr, wo, g0, b0, alpha, _tile(l_ctx, 256))
            ctx = channel_mixer(ctx.reshape(ctx_packed), mod_c[:ctx_packed[0]], ctx_packed[1]).reshape(ctx.shape)
    return x
```

```python
import functools

import jax
import jax.numpy as jnp
from jax import lax
from jax.experimental import pallas as pl
from jax.experimental.pallas import tpu as pltpu

F32 = jnp.float32
BF16 = jnp.bfloat16
HIGHEST = lax.Precision.HIGHEST

HEAD_DIM = 128
N_HEADS = 4
MIX_W = N_HEADS * HEAD_DIM
CHUNK = 64
GRID_W = 64
SUB = 16
MLSTM_GROUP = 2
GDN_BATCH_ROWS = 4
HGRN_GROUP = 8
HGRN_BATCH_ROWS = 2
N_EXPERTS = 8
MOE_ROW_BLOCK = 288
LN_EPS = 1e-5
RMS_EPS = 1e-6
L2_EPS = 1e-6
MASK_NEG = -1e30
LOG2_E = 1.4426950408889634
LANE = 128

LO_MERGE = 0
LO_CQKV = 3 * 1024
LO_AQ = LO_CQKV + 3 * MIX_W
LO_AI = LO_AQ + MIX_W
LO_AG = LO_AI + MIX_W
LO_BO = LO_AG + MIX_W
LO_CZ = LO_BO + MIX_W
N_LO = LO_CZ + MIX_W
LO_TILE = N_LO // 4
HI_AF = 0
HI_B = 2 * MIX_W
HI_GATES = HI_B + 3 * MIX_W
HI_TILE = (HI_GATES + 2 * LANE) // 2
N_HI = 2 * HI_TILE


def _cparams(sem, vmem_mb=None):
    kw = dict(dimension_semantics=sem)
    if vmem_mb is not None:
        kw["vmem_limit_bytes"] = vmem_mb << 20
    return pltpu.CompilerParams(**kw)


def _dot(a, b, precision=None):
    return jnp.dot(a, b, preferred_element_type=F32, precision=precision)


def _dot_nt(a, b, precision=None):
    return lax.dot_general(a, b, (((1,), (1,)), ((), ())), preferred_element_type=F32, precision=precision)


def _dot_tn(a, b):
    return lax.dot_general(a, b, (((0,), (0,)), ((), ())), preferred_element_type=F32)


def _ln(x):
    mu = jnp.mean(x, axis=-1, keepdims=True)
    xc = x - mu
    var = jnp.mean(xc * xc, axis=-1, keepdims=True)
    return xc * lax.rsqrt(var + LN_EPS)


def _sigmoid(x):
    return 1.0 / (1.0 + jnp.exp(-x))


def _silu(x):
    return x * _sigmoid(x)


def _softplus(x):
    return jnp.maximum(x, 0.0) + jnp.log(1.0 + jnp.exp(-jnp.abs(x)))


def _log_sigmoid(x):
    return -_softplus(-x)


def _order_mask(n, rev, strict=False):
    r = lax.broadcasted_iota(jnp.int32, (n, n), 0)
    c = lax.broadcasted_iota(jnp.int32, (n, n), 1)
    if rev:
        return (c > r) if strict else (c >= r)
    return (c < r) if strict else (c <= r)


def _cumsum_rows(x, rev):
    m = _order_mask(x.shape[0], rev).astype(F32)
    return _dot(m, x, precision=HIGHEST)


def _ada_kernel(c_ref, w_ref, b_ref, o_ref):
    s = _silu(c_ref[...])
    o_ref[0] = _dot(s, w_ref[0], precision=HIGHEST) + b_ref[0]


def _ada(cvec, w_ada, b_ada, tn=1536):
    depth, d, n6 = w_ada.shape
    rows = cvec.shape[0]
    return pl.pallas_call(
        _ada_kernel,
        grid=(depth, n6 // tn),
        in_specs=[pl.BlockSpec((rows, d), lambda l, n: (0, 0)),
                  pl.BlockSpec((1, d, tn), lambda l, n: (l, 0, n)),
                  pl.BlockSpec((1, 1, tn), lambda l, n: (l, 0, n))],
        out_specs=pl.BlockSpec((1, rows, tn), lambda l, n: (l, 0, n)),
        out_shape=jax.ShapeDtypeStruct((depth, rows, n6), F32),
        compiler_params=_cparams(("parallel", "parallel"), 40),
        name="ada",
    )(cvec, w_ada, b_ada.reshape(depth, 1, n6))


def _lower_bounds_kernel(lb_ref, o_ref):
    x = lb_ref[...]
    depth = x.shape[0]
    mx = jnp.max(x, axis=0, keepdims=True)
    e = jnp.exp(x - mx)
    p = e / jnp.sum(e, axis=0, keepdims=True)
    acc = jnp.zeros_like(p[0:1])
    for l in range(depth):
        o_ref[l:l + 1, :] = acc
        if l + 1 < depth:
            acc = acc + p[l + 1:l + 2]


def _lower_bounds(lb_raw):
    return pl.pallas_call(
        _lower_bounds_kernel,
        out_shape=jax.ShapeDtypeStruct(lb_raw.shape, F32),
        name="lower_bounds",
    )(lb_raw)


def _in_proj_kernel(x_ref, mod_ref, wlo_ref, whi_ref, lo_ref, hi_ref, h_ref, *, lo_tiles):
    n = pl.program_id(2)

    @pl.when(n == 0)
    def _():
        shift = mod_ref[0, 0:1, :]
        scale = mod_ref[0, 1:2, :]
        h_ref[...] = (_ln(x_ref[0]) * (1.0 + scale) + shift).astype(BF16)

    @pl.when(n < lo_tiles)
    def _():
        lo_ref[0] = _dot(h_ref[...], wlo_ref[...]).astype(BF16)

    @pl.when(n >= lo_tiles)
    def _():
        hi_ref[0] = _dot(h_ref[...], whi_ref[...])


def _in_proj(x, mod, w_lo, w_hi, tl):
    b, l, d = x.shape
    lo_tiles = N_LO // LO_TILE
    hi_tiles = N_HI // HI_TILE
    return pl.pallas_call(
        functools.partial(_in_proj_kernel, lo_tiles=lo_tiles),
        grid=(b, l // tl, lo_tiles + hi_tiles),
        in_specs=[pl.BlockSpec((1, tl, d), lambda bi, i, n: (bi, i, 0)),
                  pl.BlockSpec((1, 6, d), lambda bi, i, n: (bi, 0, 0)),
                  pl.BlockSpec((d, LO_TILE), lambda bi, i, n: (0, jnp.minimum(n, lo_tiles - 1))),
                  pl.BlockSpec((d, HI_TILE), lambda bi, i, n: (0, jnp.maximum(n - lo_tiles, 0)))],
        out_specs=[pl.BlockSpec((1, tl, LO_TILE), lambda bi, i, n: (bi, i, jnp.minimum(n, lo_tiles - 1))),
                   pl.BlockSpec((1, tl, HI_TILE), lambda bi, i, n: (bi, i, jnp.maximum(n - lo_tiles, 0)))],
        out_shape=[jax.ShapeDtypeStruct((b, l, N_LO), BF16), jax.ShapeDtypeStruct((b, l, N_HI), F32)],
        scratch_shapes=[pltpu.VMEM((tl, d), BF16)],
        compiler_params=_cparams(("parallel", "parallel", "arbitrary"), 48),
        name="in_proj",
    )(x, mod, w_lo, w_hi)


def _conv_kernel(x_ref, prev_ref, next_ref, w_ref, o_ref):
    i = pl.program_id(1)
    last = pl.num_programs(1) - 1
    x = x_ref[0].astype(F32)
    tl = x.shape[0]
    halo = prev_ref.shape[1]
    row = lax.broadcasted_iota(jnp.int32, x.shape, 0)
    before = jnp.where(i > 0, prev_ref[0].astype(F32)[halo - 1:halo, :], 0.0)
    after = jnp.where(i < last, next_ref[0].astype(F32)[0:1, :], 0.0)
    x_dn = jnp.where(row == 0, before, pltpu.roll(x, 1, axis=0))
    x_up = jnp.where(row == tl - 1, after, pltpu.roll(x, tl - 1, axis=0))
    y = _silu(x_dn * w_ref[0:1, :] + x * w_ref[1:2, :] + x_up * w_ref[2:3, :])
    for j in range(3 * N_HEADS):
        t = y[:, j * HEAD_DIM:(j + 1) * HEAD_DIM]
        if j < 2 * N_HEADS:
            t = t * lax.rsqrt(jnp.sum(t * t, axis=-1, keepdims=True) + L2_EPS)
            if j < N_HEADS:
                t = t * HEAD_DIM ** -0.5
        o_ref[0, :, j * HEAD_DIM:(j + 1) * HEAD_DIM] = t.astype(o_ref.dtype)


def _gdn_conv(proj, conv_w, tl):
    b, l, _ = proj.shape
    w3 = 3 * MIX_W
    cb = LO_CQKV // w3
    halo = 16
    nbh = l // halo
    rh = tl // halo
    return pl.pallas_call(
        _conv_kernel,
        grid=(b, l // tl),
        in_specs=[pl.BlockSpec((1, tl, w3), lambda bi, i: (bi, i, cb)),
                  pl.BlockSpec((1, halo, w3), lambda bi, i: (bi, jnp.maximum(i * rh - 1, 0), cb)),
                  pl.BlockSpec((1, halo, w3), lambda bi, i: (bi, jnp.minimum((i + 1) * rh, nbh - 1), cb)),
                  pl.BlockSpec((3, w3), lambda bi, i: (0, 0))],
        out_specs=pl.BlockSpec((1, tl, w3), lambda bi, i: (bi, i, 0)),
        out_shape=jax.ShapeDtypeStruct((b, l, w3), BF16),
        compiler_params=_cparams(("parallel", "parallel"), 40),
        name="gdn_conv",
    )(proj, proj, proj, conv_w)


def _hgrn_chunks(probs, states, emat):
    c = probs[0]["q"].shape[0]
    nb = c // SUB
    for p in probs:
        lb = p["lb"]
        sig = _sigmoid(p["fpre"])
        p["k"] = (1.0 - lb) * (1.0 - sig)
        p["g"] = _cumsum_rows(jnp.log(lb + (1.0 - lb) * sig), p["rev"])

    walls = []
    for p in probs:
        q = p["q"]
        g2 = p["g"] * LOG2_E
        lk2 = jnp.log(p["k"]) * LOG2_E
        rel = lk2 - g2
        rows = []
        half = SUB // 2
        dead = jnp.zeros((half, HEAD_DIM), F32)
        for j in range(nb):
            r0 = j * SUB
            pieces = []
            for s in range(SUB):
                skip = 0 if (s >= half and not p["rev"]) else 1 if (s < half and p["rev"]) else None
                parts = []
                for part in range(2):
                    if part == skip:
                        parts.append(dead)
                        continue
                    t0 = r0 + part * half
                    e2 = jnp.exp2(jnp.minimum(g2[t0:t0 + half] + rel[r0 + s:r0 + s + 1], lk2[r0 + s:r0 + s + 1]))
                    parts.append(e2 * q[t0:t0 + half])
                pieces.append(jnp.concatenate(parts, axis=0).astype(BF16))
            rows.append(jnp.concatenate(pieces, axis=1))
        walls.append(jnp.concatenate(rows, axis=0))
    a_diags = [_dot(w, emat) for w in walls]

    a_offs = []
    for p in probs:
        q, k, g, rev = p["q"], p["k"], p["g"], p["rev"]
        offs = []
        for i in range(nb):
            r0 = i * SUB
            if (not rev and i == 0) or (rev and i == nb - 1):
                offs.append(jnp.zeros((SUB, c), F32))
                continue
            bi = g[r0 + SUB:r0 + SUB + 1] if rev else g[r0 - 1:r0]
            lhs = (q[r0:r0 + SUB] * jnp.exp(g[r0:r0 + SUB] - bi)).astype(BF16)
            lo, hi = (r0 + SUB, c) if rev else (0, r0)
            live = k[lo:hi] * jnp.exp(bi - g[lo:hi])
            pad = jnp.zeros((c - (hi - lo), HEAD_DIM), F32)
            rhs = jnp.concatenate([pad, live] if rev else [live, pad], axis=0)
            offs.append(_dot_nt(lhs, rhs.astype(BF16)))
        a_offs.append(jnp.concatenate(offs, axis=0))

    r = lax.broadcasted_iota(jnp.int32, (c, c), 0)
    col = lax.broadcasted_iota(jnp.int32, (c, c), 1)
    shift = SUB.bit_length() - 1
    rb = lax.shift_right_logical(r, shift)
    cb = lax.shift_right_logical(col, shift)
    inters = [_dot_nt((p["q"] * jnp.exp(p["g"])).astype(BF16), s_t.astype(BF16)) for p, s_t in zip(probs, states)]
    outs, new_states = [], []
    for p, a_diag, a_off, inter, s_t in zip(probs, a_diags, a_offs, inters, states):
        rev, g = p["rev"], p["g"]
        order = (col >= r) if rev else (col <= r)
        off_side = (cb > rb) if rev else (cb < rb)
        a = jnp.where((rb == cb) & order, a_diag, jnp.where(off_side, a_off, 0.0))
        g_end = g[0:1] if rev else g[c - 1:c]
        vb = p["v"].astype(BF16)
        outs.append(_dot(a.astype(BF16), vb) + inter)
        kd = (p["k"] * jnp.exp(g_end - g)).astype(BF16)
        new_states.append(s_t * jnp.exp(g_end) + _dot_tn(vb, kd))
    return outs, new_states


def _hgrn_kernel(qf_ref, ff_ref, vf_ref, qb_ref, fb_ref, vb_ref, lb_ref, e_ref, s0_ref,
                 of_ref, ob_ref, sout_ref, s_ref):
    n = pl.program_id(1)

    @pl.when(n == 0)
    def _():
        s_ref[...] = s0_ref[...]

    probs, where = [], []
    for r in range(qf_ref.shape[0]):
        for d, (q_ref, f_ref, v_ref, o_ref) in enumerate(((qf_ref, ff_ref, vf_ref, of_ref),
                                                          (qb_ref, fb_ref, vb_ref, ob_ref))):
            for h in range(N_HEADS):
                sl = slice(h * HEAD_DIM, (h + 1) * HEAD_DIM)
                probs.append(dict(q=q_ref[r, :, sl].astype(F32), fpre=f_ref[r, :, sl],
                                  v=v_ref[r, :, sl].astype(F32), lb=lb_ref[:, sl], rev=d == 1))
                where.append((o_ref, sl, r, h, d))
    states = [s_ref[r, h, d] for _, _, r, h, d in where]
    outs, new_states = [], []
    for g0 in range(0, len(probs), HGRN_GROUP):
        o_g, s_g = _hgrn_chunks(probs[g0:g0 + HGRN_GROUP], states[g0:g0 + HGRN_GROUP], e_ref[...])
        outs += o_g
        new_states += s_g
    for (o_ref, sl, r, h, d), o, s_new in zip(where, outs, new_states):
        o_ref[r, :, sl] = o.astype(o_ref.dtype)
        s_ref[r, h, d] = s_new

    @pl.when(n == pl.num_programs(1) - 1)
    def _():
        sout_ref[...] = s_ref[...]


def _hgrn_emat():
    r = jnp.arange(SUB * HEAD_DIM)[:, None] // HEAD_DIM
    c = jnp.arange(CHUNK)[None, :] % SUB
    return (r == c).astype(BF16)


def _hgrn_scan(p_lo, p_hi, lb, s0):
    b, l, _ = p_lo.shape
    nc = l // CHUNK
    rows = HGRN_BATCH_ROWS if b % HGRN_BATCH_ROWS == 0 else 1
    q_col, i_col, f_col = LO_AQ // MIX_W, LO_AI // MIX_W, HI_AF // MIX_W

    def fwd(col):
        return pl.BlockSpec((rows, CHUNK, MIX_W), lambda bi, n: (bi, n, col))

    def bwd(col):
        return pl.BlockSpec((rows, CHUNK, MIX_W), lambda bi, n: (bi, nc - 1 - n, col))

    st_spec = pl.BlockSpec((rows, N_HEADS, 2, HEAD_DIM, HEAD_DIM), lambda bi, n: (bi, 0, 0, 0, 0))
    return pl.pallas_call(
        _hgrn_kernel,
        grid=(b // rows, nc),
        in_specs=[fwd(q_col), fwd(f_col), fwd(i_col), bwd(q_col), bwd(f_col + 1), bwd(i_col),
                  pl.BlockSpec((1, MIX_W), lambda bi, n: (0, 0)),
                  pl.BlockSpec((SUB * HEAD_DIM, CHUNK), lambda bi, n: (0, 0)),
                  st_spec],
        out_specs=[pl.BlockSpec((rows, CHUNK, MIX_W), lambda bi, n: (bi, n, 0)),
                   pl.BlockSpec((rows, CHUNK, MIX_W), lambda bi, n: (bi, nc - 1 - n, 0)),
                   st_spec],
        out_shape=[jax.ShapeDtypeStruct((b, l, MIX_W), BF16),
                   jax.ShapeDtypeStruct((b, l, MIX_W), BF16),
                   jax.ShapeDtypeStruct(s0.shape, F32)],
        scratch_shapes=[pltpu.VMEM((rows, N_HEADS, 2, HEAD_DIM, HEAD_DIM), F32)],
        compiler_params=_cparams(("parallel", "arbitrary"), 40),
        name="hgrn_scan",
    )(p_lo, p_hi, p_lo, p_lo, p_hi, p_lo, lb, _hgrn_emat(), s0)


def _mlstm_chunks(probs, states):
    c = probs[0]["q"].shape[0]
    dv = probs[0]["v"].shape[1]
    ones = jnp.ones((c, dv), BF16)
    for p, (cmn, m) in zip(probs, states):
        tri = _order_mask(c, p["rev"])
        log_d = jnp.where(tri, p["bb_c"] - p["bb_r"] + p["ic_r"], MASK_NEG)
        log_inter = p["bb_c"] + m
        m_t = jnp.maximum(jnp.max(log_d, axis=1, keepdims=True), log_inter)
        p["m_t"] = m_t
        p["dmat"] = jnp.where(tri, jnp.exp(log_d - m_t), 0.0)
        p["inter"] = jnp.exp(log_inter - m_t)
        p["qb"], p["kb"] = p["q"].astype(BF16), p["k"].astype(BF16)
        p["v1"] = jnp.concatenate([p["v"].astype(BF16), ones], axis=1)
    scores = [_dot_nt(p["qb"], p["kb"]) * p["dmat"] for p in probs]
    carried = [_dot(p["qb"], cmn.astype(BF16)) for p, (cmn, _) in zip(probs, states)]
    local = [_dot(s.astype(BF16), p["v1"]) for p, s in zip(probs, scores)]
    outs, new_states = [], []
    for p, car, loc, (cmn, m) in zip(probs, carried, local, states):
        m_t = p["m_t"]
        both = loc + p["inter"] * car
        outs.append(both[:, :dv] / jnp.maximum(jnp.abs(both[:, dv:]), jnp.exp(-m_t)))
        m_new = m_t[0:1] if p["rev"] else m_t[c - 1:c]
        bb_end = p["bb_c"][0:1] if p["rev"] else p["bb_c"][c - 1:c]
        decay = jnp.exp(bb_end + m - m_new)
        kw = p["k"] * jnp.exp(bb_end - p["bb_c"] + p["ic_c"] - m_new)
        new_states.append((decay * cmn + _dot_tn(kw.astype(BF16), p["v1"]), m_new))
    return outs, new_states


def _column_view(ref):
    rows, group, width = ref.shape[-3:]
    return ref.reshape(rows * group, width), group


def _chunk_rows(j):
    start = j * CHUNK
    return pl.ds(start if isinstance(start, int) else pl.multiple_of(start, CHUNK), CHUNK)


def _load_chunk(ref, j, colmajor):
    if colmajor:
        flat, group = _column_view(ref)
        return flat[pl.ds(j, CHUNK, stride=group), :]
    return ref[0, _chunk_rows(j), :]


def _store_chunk(ref, j, val, colmajor):
    if colmajor:
        flat, group = _column_view(ref)
        flat[pl.ds(j, CHUNK, stride=group), :] = val
    else:
        ref[0, _chunk_rows(j), :] = val


def _columns_as_rows(x, first):
    row = lax.broadcasted_iota(jnp.int32, (8, LANE), 0)
    lane = lax.broadcasted_iota(jnp.int32, (8, LANE), 1)
    sel = jnp.where(lane == row + first, 1.0, 0.0).astype(BF16)
    hi = x.astype(BF16)
    rest = x - hi.astype(F32)
    mid = rest.astype(BF16)
    lo = (rest - mid.astype(F32)).astype(BF16)
    return _dot_nt(sel, hi) + (_dot_nt(sel, mid) + _dot_nt(sel, lo))


def _mlstm_kernel(*refs, colmajor, chunks):
    per_dir = 3 * N_HEADS + 1
    fwd, bwd = refs[:per_dir], refs[per_dir:2 * per_dir]
    bias_ref, cm0_ref, nm0_ref, of_ref, ob_ref, cmout_ref, nmout_ref, cm_ref, nm_ref = refs[2 * per_dir:]
    n = pl.program_id(1)

    @pl.when(n == 0)
    def _():
        cm_ref[...] = cm0_ref[0]
        nm_ref[...] = nm0_ref[0]

    def step(i, carry):
        probs, where = [], []
        for d, (in_refs, o_ref) in enumerate(((fwd, of_ref), (bwd, ob_ref))):
            rev = d == 1
            j = chunks - 1 - i if rev else i
            gates = _load_chunk(in_refs[-1], j, colmajor) + bias_ref[...]
            cs = _cumsum_rows(_log_sigmoid(gates), rev)
            cs_rows = _columns_as_rows(cs, 8 + 4 * d)
            gate_rows = _columns_as_rows(gates, 4 * d)
            for h in range(N_HEADS):
                ci = 4 * d + h
                cf = 8 + 4 * d + h
                probs.append(dict(
                    q=_load_chunk(in_refs[h], j, colmajor),
                    k=_load_chunk(in_refs[N_HEADS + h], j, colmajor) * HEAD_DIM ** -0.5,
                    v=_load_chunk(in_refs[2 * N_HEADS + h], j, colmajor),
                    bb_c=cs[:, cf:cf + 1], bb_r=cs_rows[h:h + 1, :],
                    ic_c=gates[:, ci:ci + 1], ic_r=gate_rows[h:h + 1, :], rev=rev))
                where.append((o_ref, j, d, h))
        states = [(cm_ref[d, h], nm_ref[d, h, 0:1, 0:1]) for _, _, d, h in where]
        outs, new_states = [], []
        for g0 in range(0, len(probs), MLSTM_GROUP):
            o_g, s_g = _mlstm_chunks(probs[g0:g0 + MLSTM_GROUP], states[g0:g0 + MLSTM_GROUP])
            outs += o_g
            new_states += s_g
        for (o_ref, j, d, h), o, (cmn_new, m_new) in zip(where, outs, new_states):
            _store_chunk(o_ref.at[h], j, o, colmajor)
            cm_ref[d, h] = cmn_new
            nm_ref[d, h, 0:1, :] = jnp.broadcast_to(m_new, (1, HEAD_DIM))
        return carry

    lax.fori_loop(0, chunks, step, 0)

    @pl.when(n == pl.num_programs(1) - 1)
    def _():
        cmout_ref[0] = cm_ref[...]
        nmout_ref[0] = nm_ref[...]


def _mlstm_scan(proj, gbias, cm0, nm0, colmajor):
    b, l, n_proj = proj.shape
    cb = HI_B // LANE
    cg = HI_GATES // LANE
    if colmajor:
        chunks = 8
        assert l == CHUNK * GRID_W and GRID_W % chunks == 0
        nb = GRID_W // chunks
        src = proj.reshape(b, CHUNK, GRID_W, n_proj)
        in_blk = (1, CHUNK, chunks, LANE)
        out_blk = (N_HEADS, 1, CHUNK, chunks, LANE)
        out_view = (N_HEADS, b, CHUNK, GRID_W, LANE)

        def in_map(col, flip):
            return lambda bi, n: (bi, 0, nb - 1 - n if flip else n, col)

        def out_map(flip):
            return lambda bi, n: (0, bi, 0, nb - 1 - n if flip else n, 0)
    else:
        chunks = min(l // CHUNK, 4)
        nb = l // (chunks * CHUNK)
        src = proj
        in_blk = (1, chunks * CHUNK, LANE)
        out_blk = (N_HEADS, 1, chunks * CHUNK, LANE)
        out_view = (N_HEADS, b, l, LANE)

        def in_map(col, flip):
            return lambda bi, n: (bi, nb - 1 - n if flip else n, col)

        def out_map(flip):
            return lambda bi, n: (0, bi, nb - 1 - n if flip else n, 0)

    def specs(flip):
        cols = [cb + j for j in range(3 * N_HEADS)] + [cg]
        return [pl.BlockSpec(in_blk, in_map(col, flip)) for col in cols]

    cm_spec = pl.BlockSpec((1, 2, N_HEADS, HEAD_DIM, 2 * HEAD_DIM), lambda bi, n: (bi, 0, 0, 0, 0))
    nm_spec = pl.BlockSpec((1, 2, N_HEADS, 8, HEAD_DIM), lambda bi, n: (bi, 0, 0, 0, 0))
    n_in = 2 * (3 * N_HEADS + 1)
    o_f, o_b, cm, nm = pl.pallas_call(
        functools.partial(_mlstm_kernel, colmajor=colmajor, chunks=chunks),
        grid=(b, nb),
        in_specs=specs(False) + specs(True) + [pl.BlockSpec((1, LANE), lambda bi, n: (0, 0)), cm_spec, nm_spec],
        out_specs=[pl.BlockSpec(out_blk, out_map(False)), pl.BlockSpec(out_blk, out_map(True)), cm_spec, nm_spec],
        out_shape=[jax.ShapeDtypeStruct(out_view, F32), jax.ShapeDtypeStruct(out_view, F32),
                   jax.ShapeDtypeStruct(cm0.shape, F32), jax.ShapeDtypeStruct(nm0.shape, F32)],
        scratch_shapes=[pltpu.VMEM((2, N_HEADS, HEAD_DIM, 2 * HEAD_DIM), F32),
                        pltpu.VMEM((2, N_HEADS, 8, HEAD_DIM), F32)],
        compiler_params=_cparams(("parallel", "arbitrary"), 40),
        name="mlstm_scan",
    )(*([src] * n_in), gbias, cm0, nm0)
    head_major = (N_HEADS, b, l, LANE)
    return o_f.reshape(head_major), o_b.reshape(head_major), cm, nm


def _split_bf16(x):
    hi = x.astype(BF16)
    return hi, (x - hi.astype(F32)).astype(BF16)


def _dot3(a, b, nt=False):
    f = _dot_nt if nt else _dot
    return f(a[0], b[0]) + (f(a[0], b[1]) + f(a[1], b[0]))


def _inv_unit_triangular(a_list):
    n = a_list[0].shape[0]
    eye = (lax.broadcasted_iota(jnp.int32, (n, n), 0) == lax.broadcasted_iota(jnp.int32, (n, n), 1)).astype(F32)
    xs = [eye - a for a in a_list]
    splits = [_split_bf16(a) for a in a_list]
    ps = [_dot3(s, s) for s in splits]
    steps = (n - 1).bit_length() - 1
    for step in range(steps):
        last = step == steps - 1
        p_split = [_split_bf16(p) for p in ps]
        lhs = xs if last else [jnp.concatenate([x, p], axis=0) for x, p in zip(xs, ps)]
        prods = [_dot3(_split_bf16(l), s) for l, s in zip(lhs, p_split)]
        xs = [x + pr[:n] for x, pr in zip(xs, prods)]
        if not last:
            ps = [pr[n:] for pr in prods]
    return xs


def _gdn_chunks(probs, states):
    c = probs[0]["q"].shape[0]
    a_list, ldecs = [], []
    for p in probs:
        tri = _order_mask(c, p["rev"])
        strict = _order_mask(c, p["rev"], strict=True)
        ldec = jnp.where(tri, jnp.exp(jnp.minimum(p["gam_c"] - p["gam_r"], 0.0)), 0.0)
        p["kbeta"] = p["k"] * p["beta"]
        kk = _dot_nt(p["kbeta"].astype(BF16), p["k"].astype(BF16))
        a_list.append(jnp.where(strict, kk * ldec, 0.0))
        ldecs.append(ldec)
    t_invs = _inv_unit_triangular(a_list)
    sols = []
    for p, t_inv in zip(probs, t_invs):
        p["eg"] = jnp.exp(p["gam_c"])
        rhs = jnp.concatenate([p["v"] * p["beta"], p["kbeta"] * p["eg"]], axis=1)
        sols.append(_dot3(_split_bf16(t_inv), _split_bf16(rhs)))
    qks = [_dot_nt(p["q"].astype(BF16), p["k"].astype(BF16)) * ldec for p, ldec in zip(probs, ldecs)]
    inter = [_dot(jnp.concatenate([sol[:, HEAD_DIM:], p["q"] * p["eg"]], axis=0).astype(BF16), s.astype(BF16))
             for p, sol, s in zip(probs, sols, states)]
    outs, new_states = [], []
    for p, sol, qk, it, s in zip(probs, sols, qks, inter, states):
        g_end = p["gam_c"][0:1] if p["rev"] else p["gam_c"][c - 1:c]
        v_new = (sol[:, :HEAD_DIM] - it[:c]).astype(BF16)
        kg = p["k"] * jnp.exp(g_end - p["gam_c"])
        outs.append(it[c:] + _dot(qk.astype(BF16), v_new))
        new_states.append(jnp.exp(g_end) * s + _dot_tn(kg.astype(BF16), v_new))
    return outs, new_states


def _gdn_kernel(xf_ref, gf_ref, xb_ref, gb_ref, alog_ref, dtb_ref, s0_ref,
                of_ref, ob_ref, sout_ref, s_ref):
    n = pl.program_id(1)

    @pl.when(n == 0)
    def _():
        s_ref[...] = s0_ref[...]

    neg_a = -jnp.exp(alog_ref[...])
    probs, where = [], []
    for r in range(xf_ref.shape[0]):
        for d, (x_ref, g_ref, o_ref) in enumerate(((xf_ref, gf_ref, of_ref), (xb_ref, gb_ref, ob_ref))):
            rev = d == 1
            gates = g_ref[r]
            cs = _cumsum_rows(neg_a * _softplus(gates + dtb_ref[...]), rev)
            cs_t = cs.T
            betas = _sigmoid(gates)
            for h in range(N_HEADS):
                ca = 16 + 4 * d + h
                cbeta = 24 + 4 * d + h
                probs.append(dict(
                    q=x_ref[r, :, h * HEAD_DIM:(h + 1) * HEAD_DIM].astype(F32),
                    k=x_ref[r, :, MIX_W + h * HEAD_DIM:MIX_W + (h + 1) * HEAD_DIM].astype(F32),
                    v=x_ref[r, :, 2 * MIX_W + h * HEAD_DIM:2 * MIX_W + (h + 1) * HEAD_DIM].astype(F32),
                    gam_c=cs[:, ca:ca + 1], gam_r=cs_t[ca:ca + 1, :], beta=betas[:, cbeta:cbeta + 1], rev=rev))
                where.append((o_ref, r, d, h))
    outs, new_states = _gdn_chunks(probs, [s_ref[r, d, h] for _, r, d, h in where])
    for (o_ref, r, d, h), o, s_new in zip(where, outs, new_states):
        o_ref[r, :, h * HEAD_DIM:(h + 1) * HEAD_DIM] = o.astype(o_ref.dtype)
        s_ref[r, d, h] = s_new

    @pl.when(n == pl.num_programs(1) - 1)
    def _():
        sout_ref[...] = s_ref[...]


def _gdn_scan(qkv, proj, alog_row, dtb_row, s0):
    b, l, w3 = qkv.shape
    nc = l // CHUNK
    cg = HI_GATES // LANE
    rows = GDN_BATCH_ROWS if b % GDN_BATCH_ROWS == 0 else 1
    st_spec = pl.BlockSpec((rows, 2, N_HEADS, HEAD_DIM, HEAD_DIM), lambda bi, n: (bi, 0, 0, 0, 0))
    row_spec = pl.BlockSpec((1, LANE), lambda bi, n: (0, 0))
    return pl.pallas_call(
        _gdn_kernel,
        grid=(b // rows, nc),
        in_specs=[pl.BlockSpec((rows, CHUNK, w3), lambda bi, n: (bi, n, 0)),
                  pl.BlockSpec((rows, CHUNK, LANE), lambda bi, n: (bi, n, cg)),
                  pl.BlockSpec((rows, CHUNK, w3), lambda bi, n: (bi, nc - 1 - n, 0)),
                  pl.BlockSpec((rows, CHUNK, LANE), lambda bi, n: (bi, nc - 1 - n, cg)),
                  row_spec, row_spec, st_spec],
        out_specs=[pl.BlockSpec((rows, CHUNK, MIX_W), lambda bi, n: (bi, n, 0)),
                   pl.BlockSpec((rows, CHUNK, MIX_W), lambda bi, n: (bi, nc - 1 - n, 0)),
                   st_spec],
        out_shape=[jax.ShapeDtypeStruct((b, l, MIX_W), BF16), jax.ShapeDtypeStruct((b, l, MIX_W), BF16),
                   jax.ShapeDtypeStruct(s0.shape, F32)],
        scratch_shapes=[pltpu.VMEM((rows, 2, N_HEADS, HEAD_DIM, HEAD_DIM), F32)],
        compiler_params=_cparams(("parallel", "arbitrary"), 40),
        name="gdn_scan",
    )(qkv, proj, qkv, proj, alog_row, dtb_row, s0)


def _merge_kernel(oaf, oab, obf, obb, ocf, ocb, ag, bo, cz, mg, x_ref, mod_ref, gain_ref, wbr_ref, wout_ref,
                  lng_ref, lnb_ref, out_ref, *, alpha):
    d_model = x_ref.shape[-1]
    branches = ((oaf, oab, ag, _silu), (obf, obb, bo, _sigmoid), (ocf, ocb, cz, _silu))
    mixed = None
    for nbr, (of_ref, ob_ref, gate_ref, act) in enumerate(branches):
        parts = []
        for h in range(N_HEADS):
            if len(of_ref.shape) == 4:
                oh = of_ref[h, 0] + ob_ref[h, 0]
            else:
                sl = slice(h * HEAD_DIM, (h + 1) * HEAD_DIM)
                oh = of_ref[0, :, sl].astype(F32) + ob_ref[0, :, sl].astype(F32)
            parts.append(oh * lax.rsqrt(jnp.mean(oh * oh, axis=-1, keepdims=True) + RMS_EPS))
        y = jnp.concatenate(parts, axis=1) * gain_ref[nbr:nbr + 1, :] * act(gate_ref[0].astype(F32))
        gate = _sigmoid(mg[0, :, nbr * d_model:(nbr + 1) * d_model].astype(F32))
        term = gate * _dot(y.astype(BF16), wbr_ref[nbr])
        mixed = term if mixed is None else mixed + term
    sub = _dot(mixed.astype(BF16), wout_ref[...])
    z = alpha * x_ref[0] + mod_ref[0, 2:3, :] * sub
    out_ref[0] = _ln(z) * lng_ref[...] + lnb_ref[...]


def _merge(outs, proj, x, mod, gain, w_branch, w_out, ln_g, ln_b, alpha, tl):
    b, l, d = x.shape
    u = MIX_W

    def tok(width, col):
        return pl.BlockSpec((1, tl, width), lambda bi, i: (bi, i, col))

    def const(shape):
        return pl.BlockSpec(shape, lambda bi, i: (0,) * len(shape))

    heads = pl.BlockSpec((N_HEADS, 1, tl, LANE), lambda bi, i: (0, bi, i, 0))
    return pl.pallas_call(
        functools.partial(_merge_kernel, alpha=alpha),
        grid=(b, l // tl),
        in_specs=[tok(u, 0), tok(u, 0), heads, heads, tok(u, 0), tok(u, 0),
            tok(u, LO_AG // u), tok(u, LO_BO // u), tok(u, LO_CZ // u),
            tok(3 * d, LO_MERGE // (3 * d)), tok(d, 0),
            pl.BlockSpec((1, 6, d), lambda bi, i: (bi, 0, 0)),
            const((3, u)), const((3, u, d)), const((d, d)), const((1, d)), const((1, d))],
        out_specs=tok(d, 0),
        out_shape=jax.ShapeDtypeStruct((b, l, d), F32),
        compiler_params=_cparams(("parallel", "parallel"), 48),
        name="merge",
    )(*outs, proj, proj, proj, proj, x, mod, gain, w_branch, w_out, ln_g, ln_b)


def _ffn_kernel(x_ref, mod_ref, wgu_ref, wd_ref, lng_ref, lnb_ref, out_ref, h_ref, acc_ref, *, alpha):
    f = pl.program_id(2)

    @pl.when(f == 0)
    def _():
        h_ref[...] = (_ln(x_ref[0]) * (1.0 + mod_ref[0, 4:5, :]) + mod_ref[0, 3:4, :]).astype(BF16)
        acc_ref[...] = jnp.zeros_like(acc_ref)

    gu = _dot(h_ref[...], wgu_ref[0, 0])
    half = gu.shape[1] // 2
    act = _silu(gu[:, :half]) * gu[:, half:]
    acc_ref[...] += _dot(act.astype(BF16), wd_ref[...])

    @pl.when(f == pl.num_programs(2) - 1)
    def _():
        z = alpha * x_ref[0] + mod_ref[0, 5:6, :] * acc_ref[...]
        out_ref[0] = _ln(z) * lng_ref[...] + lnb_ref[...]


def _ffn(x, mod, w_gate_up, w_down, ln_g, ln_b, alpha, tl):
    b, l, d = x.shape
    _, nf, _, tf2 = w_gate_up.shape
    return pl.pallas_call(
        functools.partial(_ffn_kernel, alpha=alpha),
        grid=(b, l // tl, nf),
        in_specs=[pl.BlockSpec((1, tl, d), lambda bi, i, j: (bi, i, 0)),
                  pl.BlockSpec((1, 6, d), lambda bi, i, j: (bi, 0, 0)),
                  pl.BlockSpec((1, 1, d, tf2), lambda bi, i, j: (0, j, 0, 0)),
                  pl.BlockSpec((tf2 // 2, d), lambda bi, i, j: (j, 0)),
                  pl.BlockSpec((1, d), lambda bi, i, j: (0, 0)),
                  pl.BlockSpec((1, d), lambda bi, i, j: (0, 0))],
        out_specs=pl.BlockSpec((1, tl, d), lambda bi, i, j: (bi, i, 0)),
        out_shape=jax.ShapeDtypeStruct((b, l, d), F32),
        scratch_shapes=[pltpu.VMEM((tl, d), BF16), pltpu.VMEM((tl, d), F32)],
        compiler_params=_cparams(("parallel", "parallel", "arbitrary"), 56),
        name="ffn",
    )(x, mod, w_gate_up, w_down, ln_g, ln_b)


def _top2_combine(logits):
    lane = lax.broadcasted_iota(jnp.int32, logits.shape, 1).astype(F32)
    neg = -jnp.inf
    lg = jnp.where(lane < N_EXPERTS, logits, neg)
    m1 = jnp.max(lg, axis=1, keepdims=True)
    i1 = jnp.min(jnp.where(lg == m1, lane, float(LANE)), axis=1, keepdims=True)
    lg2 = jnp.where(lane == i1, neg, lg)
    m2 = jnp.max(lg2, axis=1, keepdims=True)
    i2 = jnp.min(jnp.where(lg2 == m2, lane, float(LANE)), axis=1, keepdims=True)
    e2 = jnp.exp(m2 - m1)
    p1 = 1.0 / (1.0 + e2)
    chosen = (lane == i1) | (lane == i2)
    return jnp.where(lane == i1, p1, jnp.where(lane == i2, e2 * p1, 0.0)), chosen


def _moe_kernel(x_ref, mod_ref, wr_ref, wgu_ref, wd_ref, lng_ref, lnb_ref, out_ref,
                h_ref, acc_ref, comb_ref, pos_ref, post_ref, col_ref, xc_ref, yc_ref, cnt_ref, *, alpha, rb):
    e = pl.program_id(2)
    f = pl.program_id(3)
    tl, d = h_ref.shape

    @pl.when((e == 0) & (f == 0))
    def _():
        h = _ln(x_ref[0]) * (1.0 + mod_ref[0, 4:5, :]) + mod_ref[0, 3:4, :]
        h_ref[...] = h.astype(BF16)
        comb, chosen = _top2_combine(_dot(h, wr_ref[...], precision=HIGHEST))
        comb_ref[...] = comb
        blk = min(tl, LANE)
        before = jnp.where(_order_mask(blk, False, strict=True), 1.0, 0.0).astype(BF16)
        routed = jnp.where(chosen, 1.0, 0.0)
        total = jnp.zeros((1, LANE), F32)
        ranks = []
        for r0 in range(0, tl, blk):
            part = routed[r0:r0 + blk]
            ranks.append(_dot(before, part.astype(BF16)) + total)
            total = total + jnp.sum(part, axis=0, keepdims=True)
        rank = jnp.concatenate(ranks, axis=0)
        pos = jnp.where(chosen, rank, -1.0)
        pos_ref[...] = pos
        pos_t = pos.T[0:N_EXPERTS, :]
        post_ref[...] = pos_t
        for ex in range(N_EXPERTS):
            cnt_ref[ex] = jnp.sum(jnp.where(pos_t[ex:ex + 1, :] >= 0.0, 1.0, 0.0)).astype(jnp.int32)
        acc_ref[...] = jnp.zeros_like(acc_ref)

    nblk = (cnt_ref[e] + (rb - 1)) // rb

    def rows(r):
        return pl.ds(pl.multiple_of(r * rb, rb), rb)

    @pl.when(f == 0)
    def _():
        lane = lax.broadcasted_iota(jnp.int32, (tl, LANE), 1)
        col_ref[0] = jnp.sum(jnp.where(lane == e, pos_ref[...], 0.0), axis=1, keepdims=True)
        col_ref[1] = jnp.sum(jnp.where(lane == e, comb_ref[...], 0.0), axis=1, keepdims=True)
        pos_row = post_ref[pl.ds(e, 1), :]

        def gather(r, carry):
            slot = (lax.broadcasted_iota(jnp.int32, (rb, tl), 0) + r * rb).astype(F32)
            sel = jnp.where(pos_row == slot, 1.0, 0.0).astype(BF16)
            xc_ref[rows(r), :] = _dot(sel, h_ref[...]).astype(BF16)
            yc_ref[rows(r), :] = jnp.zeros((rb, d), F32)
            return carry

        lax.fori_loop(0, nblk, gather, 0)

    def expert(r, carry):
        xb = xc_ref[rows(r), :]
        gu = _dot(xb, wgu_ref[0, 0])
        half = gu.shape[1] // 2
        act = _silu(gu[:, :half]) * gu[:, half:]
        yc_ref[rows(r), :] += _dot(act.astype(BF16), wd_ref[0])
        return carry

    lax.fori_loop(0, nblk, expert, 0)

    @pl.when(f == pl.num_programs(3) - 1)
    def _():
        pos_col = col_ref[0]
        weight = col_ref[1]

        def scatter(r, carry):
            slot = (lax.broadcasted_iota(jnp.int32, (tl, rb), 1) + r * rb).astype(F32)
            sel_t = jnp.where(pos_col == slot, 1.0, 0.0).astype(BF16)
            acc_ref[...] += weight * _dot(sel_t, yc_ref[rows(r), :].astype(BF16))
            return carry

        lax.fori_loop(0, nblk, scatter, 0)

    @pl.when((e == pl.num_programs(2) - 1) & (f == pl.num_programs(3) - 1))
    def _():
        z = alpha * x_ref[0] + mod_ref[0, 5:6, :] * acc_ref[...]
        out_ref[0] = _ln(z) * lng_ref[...] + lnb_ref[...]


def _pack_gate_up(w_gate, w_up, widest):
    e, d, f = w_gate.shape
    tf = next(t for t in (1792, 896, 512, 256, 128) if t <= widest and f % t == 0)

    def tiles(w):
        return w.astype(BF16).reshape(e, d, f // tf, tf).transpose(0, 2, 1, 3)

    return jnp.concatenate([tiles(w_gate), tiles(w_up)], axis=-1)


def _moe(x, mod, w_router, w_gate_up, w_down, ln_g, ln_b, alpha, tl):
    b, l, d = x.shape
    ne, nf, _, tf2 = w_gate_up.shape
    rb = min(MOE_ROW_BLOCK, tl)
    compact_rows = -(-tl // rb) * rb
    return pl.pallas_call(
        functools.partial(_moe_kernel, alpha=alpha, rb=rb),
        grid=(b, l // tl, ne, nf),
        in_specs=[pl.BlockSpec((1, tl, d), lambda bi, i, e, j: (bi, i, 0)),
                  pl.BlockSpec((1, 6, d), lambda bi, i, e, j: (bi, 0, 0)),
                  pl.BlockSpec((d, LANE), lambda bi, i, e, j: (0, 0)),
                  pl.BlockSpec((1, 1, d, tf2), lambda bi, i, e, j: (e, j, 0, 0)),
                  pl.BlockSpec((1, tf2 // 2, d), lambda bi, i, e, j: (e, j, 0)),
                  pl.BlockSpec((1, d), lambda bi, i, e, j: (0, 0)),
                  pl.BlockSpec((1, d), lambda bi, i, e, j: (0, 0))],
        out_specs=pl.BlockSpec((1, tl, d), lambda bi, i, e, j: (bi, i, 0)),
        out_shape=jax.ShapeDtypeStruct((b, l, d), F32),
        scratch_shapes=[pltpu.VMEM((tl, d), BF16),
                        pltpu.VMEM((tl, d), F32),
                        pltpu.VMEM((tl, LANE), F32),
                        pltpu.VMEM((tl, LANE), F32),
                        pltpu.VMEM((N_EXPERTS, tl), F32),
                        pltpu.VMEM((2, tl, 1), F32),
                        pltpu.VMEM((compact_rows, d), BF16),
                        pltpu.VMEM((compact_rows, d), F32),
                        pltpu.SMEM((N_EXPERTS,), jnp.int32)],
        compiler_params=_cparams(("parallel", "parallel", "arbitrary", "arbitrary"), 60),
        name="moe",
    )(x, mod, w_router, w_gate_up, w_down, ln_g, ln_b)


def _reorder_w_in(w):
    d = w.shape[0]
    u = MIX_W
    o_a, o_b = 0, 5 * u
    o_bgt = o_b + 4 * u
    o_cqkv = o_bgt + 16
    o_cz = o_cqkv + 3 * u
    o_cgt = o_cz + u
    o_merge = o_cgt + 16
    lo = [w[:, o_merge:o_merge + 3 * d], w[:, o_cqkv:o_cz], w[:, o_a:o_a + u], w[:, o_a + 3 * u:o_a + 5 * u],
          w[:, o_b + 3 * u:o_bgt], w[:, o_cz:o_cgt]]
    hi = [w[:, o_a + u:o_a + 3 * u], w[:, o_b:o_b + 3 * u], w[:, o_bgt:o_bgt + 16], w[:, o_cgt:o_cgt + 16],
          jnp.zeros((d, N_HI - HI_GATES - 32), w.dtype)]
    return jnp.concatenate(lo, axis=1).astype(BF16), jnp.concatenate(hi, axis=1).astype(BF16)


def _pad_row(vals, offset):
    row = jnp.zeros((1, LANE), F32)
    return lax.dynamic_update_slice(row, vals.reshape(1, -1).astype(F32), (0, offset))


def _tile(l, pref):
    return pref if l % pref == 0 else l


def kernel(x, c, ctx, c_ctx, w_ada, b_ada, w_in, conv_w, lb_raw, m_gate_bias, gdn_a_log, gdn_dt_bias, mix_norm,
           w_branch, w_out, ln_g, ln_b, ffn_w_gate, ffn_w_up, ffn_w_down, moe_router, moe_w_gate, moe_w_up,
           moe_w_down):
    depth, d_model = w_in.shape[0], w_in.shape[1]
    b, l_lat, _ = x.shape
    l_ctx = ctx.shape[1]
    alpha = float((2 * depth) ** 0.25)
    assert d_model == 1024 and w_in.shape[2] == 9 * MIX_W + 3 * MIX_W + MIX_W + 32 + 3 * d_model

    lower = _lower_bounds(lb_raw.astype(F32))
    n_rows = -(-(b + 1) // 8) * 8
    cvec = jnp.concatenate([c, c_ctx[None, :], jnp.zeros((n_rows - b - 1, d_model), F32)], axis=0)
    mods = _ada(cvec, w_ada, b_ada)

    zeros_state = jnp.zeros((b, 2, N_HEADS, HEAD_DIM, HEAD_DIM), F32)
    zeros_hgrn = jnp.zeros((b, N_HEADS, 2, HEAD_DIM, HEAD_DIM), F32)
    zeros_nm = jnp.zeros((b, 2, N_HEADS, 8, HEAD_DIM), F32)
    zeros_cmn = jnp.zeros((b, 2, N_HEADS, HEAD_DIM, 2 * HEAD_DIM), F32)

    pack = 1024 // l_ctx if (1024 % l_ctx == 0 and b % max(1024 // l_ctx, 1) == 0) else 1
    ctx_packed = (b // pack, pack * l_ctx, d_model)

    for l in range(depth):
        mod_l = mods[l, :b].reshape(b, 6, d_model)
        mod_c = jnp.broadcast_to(mods[l, b].reshape(1, 6, d_model), (b, 6, d_model))
        w_lo, w_hi = _reorder_w_in(w_in[l])
        lb_row = lower[l].reshape(1, MIX_W)
        gbias = _pad_row(m_gate_bias[l], 0)
        alog_row = _pad_row(gdn_a_log[l], 16)
        dtb_row = _pad_row(gdn_dt_bias[l], 16)
        gain = mix_norm[l]
        wbr = w_branch[l].astype(BF16)
        wo = w_out[l].astype(BF16)
        g0, b0 = ln_g[l, 0].reshape(1, -1), ln_b[l, 0].reshape(1, -1)
        g1, b1 = ln_g[l, 1].reshape(1, -1), ln_b[l, 1].reshape(1, -1)

        lo_c, hi_c = _in_proj(ctx.reshape(ctx_packed), mod_c[:ctx_packed[0]], w_lo, w_hi, ctx_packed[1])
        lo_c, hi_c = lo_c.reshape(b, l_ctx, N_LO), hi_c.reshape(b, l_ctx, N_HI)
        lo_l, hi_l = _in_proj(x, mod_l, w_lo, w_hi, _tile(l_lat, 1024))
        qkv_c = _gdn_conv(lo_c, conv_w[l], _tile(l_ctx, 256))
        qkv_l = _gdn_conv(lo_l, conv_w[l], _tile(l_lat, 512))

        oa_c = _hgrn_scan(lo_c, hi_c, lb_row, zeros_hgrn)
        oa_l = _hgrn_scan(lo_l, hi_l, lb_row, oa_c[2])
        ob_c = _mlstm_scan(hi_c, gbias, zeros_cmn, zeros_nm, colmajor=False)
        ob_l = _mlstm_scan(hi_l, gbias, ob_c[2], ob_c[3], colmajor=True)
        oc_c = _gdn_scan(qkv_c, hi_c, alog_row, dtb_row, zeros_state)
        oc_l = _gdn_scan(qkv_l, hi_l, alog_row, dtb_row, oc_c[2])

        def channel_mixer(t, mod, tl):
            i = l // 2
            if l % 2 == 0:
                return _ffn(t, mod, _pack_gate_up(ffn_w_gate[i][None], ffn_w_up[i][None], 896),
                            ffn_w_down[i].astype(BF16), g1, b1, alpha, tl)
            w_r = jnp.concatenate([moe_router[i], jnp.zeros((d_model, LANE - N_EXPERTS), F32)], axis=1)
            return _moe(t, mod, w_r, _pack_gate_up(moe_w_gate[i], moe_w_up[i], 1792), moe_w_down[i].astype(BF16),
                        g1, b1, alpha, tl)

        outs_l = (oa_l[0], oa_l[1], ob_l[0], ob_l[1], oc_l[0], oc_l[1])
        x = _merge(outs_l, lo_l, x, mod_l, gain, wbr, wo, g0, b0, alpha, _tile(l_lat, 512))
        x = channel_mixer(x, mod_l, _tile(l_lat, 1024))
        if l < depth - 1:
            outs_c = (oa_c[0], oa_c[1], ob_c[0], ob_c[1], oc_c[0], oc_c[1])
            ctx = _merge(outs_c, lo_c, ctx, mod_c, gain, wbr, wo, g0, b0, alpha, _tile(l_ctx, 256))
            ctx = channel_mixer(ctx.reshape(ctx_packed), mod_c[:ctx_packed[0]], ctx_packed[1]).reshape(ctx.shape)
    return x
```

```python
import functools

import jax
import jax.numpy as jnp
from jax import lax
from jax.experimental import pallas as pl
from jax.experimental.pallas import tpu as pltpu

F32 = jnp.float32
BF16 = jnp.bfloat16
HIGHEST = lax.Precision.HIGHEST

HEAD_DIM = 128
N_HEADS = 4
MIX_W = N_HEADS * HEAD_DIM
CHUNK = 64
GRID_W = 64
SUB = 16
MLSTM_GROUP = 2
GDN_BATCH_ROWS = 4
HGRN_GROUP = 8
HGRN_BATCH_ROWS = 4
N_EXPERTS = 8
MOE_ROW_BLOCK = 288
LN_EPS = 1e-5
RMS_EPS = 1e-6
L2_EPS = 1e-6
MASK_NEG = -1e30
LOG2_E = 1.4426950408889634
LANE = 128

LO_MERGE = 0
LO_CQKV = 3 * 1024
LO_AQ = LO_CQKV + 3 * MIX_W
LO_AI = LO_AQ + MIX_W
LO_AG = LO_AI + MIX_W
LO_BO = LO_AG + MIX_W
LO_CZ = LO_BO + MIX_W
N_LO = LO_CZ + MIX_W
LO_TILE = N_LO // 4
HI_AF = 0
HI_B = 2 * MIX_W
HI_GATES = HI_B + 3 * MIX_W
HI_TILE = (HI_GATES + 2 * LANE) // 2
N_HI = 2 * HI_TILE


def _cparams(sem, vmem_mb=None):
    kw = dict(dimension_semantics=sem)
    if vmem_mb is not None:
        kw["vmem_limit_bytes"] = vmem_mb << 20
    return pltpu.CompilerParams(**kw)


def _dot(a, b, precision=None):
    return jnp.dot(a, b, preferred_element_type=F32, precision=precision)


def _dot_nt(a, b, precision=None):
    return lax.dot_general(a, b, (((1,), (1,)), ((), ())), preferred_element_type=F32, precision=precision)


def _dot_tn(a, b):
    return lax.dot_general(a, b, (((0,), (0,)), ((), ())), preferred_element_type=F32)


def _ln(x):
    mu = jnp.mean(x, axis=-1, keepdims=True)
    xc = x - mu
    var = jnp.mean(xc * xc, axis=-1, keepdims=True)
    return xc * lax.rsqrt(var + LN_EPS)


def _sigmoid(x):
    return 1.0 / (1.0 + jnp.exp(-x))


def _silu(x):
    return x * _sigmoid(x)


def _softplus(x):
    return jnp.maximum(x, 0.0) + jnp.log(1.0 + jnp.exp(-jnp.abs(x)))


def _log_sigmoid(x):
    return -_softplus(-x)


def _order_mask(n, rev, strict=False):
    r = lax.broadcasted_iota(jnp.int32, (n, n), 0)
    c = lax.broadcasted_iota(jnp.int32, (n, n), 1)
    if rev:
        return (c > r) if strict else (c >= r)
    return (c < r) if strict else (c <= r)


def _cumsum_rows(x, rev):
    m = _order_mask(x.shape[0], rev).astype(F32)
    return _dot(m, x, precision=HIGHEST)


def _ada_kernel(c_ref, w_ref, b_ref, o_ref):
    s = _silu(c_ref[...])
    o_ref[0] = _dot(s, w_ref[0], precision=HIGHEST) + b_ref[0]


def _ada(cvec, w_ada, b_ada, tn=1536):
    depth, d, n6 = w_ada.shape
    rows = cvec.shape[0]
    return pl.pallas_call(
        _ada_kernel,
        grid=(depth, n6 // tn),
        in_specs=[pl.BlockSpec((rows, d), lambda l, n: (0, 0)),
                  pl.BlockSpec((1, d, tn), lambda l, n: (l, 0, n)),
                  pl.BlockSpec((1, 1, tn), lambda l, n: (l, 0, n))],
        out_specs=pl.BlockSpec((1, rows, tn), lambda l, n: (l, 0, n)),
        out_shape=jax.ShapeDtypeStruct((depth, rows, n6), F32),
        compiler_params=_cparams(("parallel", "parallel"), 40),
        name="ada",
    )(cvec, w_ada, b_ada.reshape(depth, 1, n6))


def _lower_bounds_kernel(lb_ref, o_ref):
    x = lb_ref[...]
    depth = x.shape[0]
    mx = jnp.max(x, axis=0, keepdims=True)
    e = jnp.exp(x - mx)
    p = e / jnp.sum(e, axis=0, keepdims=True)
    acc = jnp.zeros_like(p[0:1])
    for l in range(depth):
        o_ref[l:l + 1, :] = acc
        if l + 1 < depth:
            acc = acc + p[l + 1:l + 2]


def _lower_bounds(lb_raw):
    return pl.pallas_call(
        _lower_bounds_kernel,
        out_shape=jax.ShapeDtypeStruct(lb_raw.shape, F32),
        name="lower_bounds",
    )(lb_raw)


def _in_proj_kernel(x_ref, mod_ref, wlo_ref, whi_ref, lo_ref, hi_ref, h_ref, *, lo_tiles):
    n = pl.program_id(2)

    @pl.when(n == 0)
    def _():
        shift = mod_ref[0, 0:1, :]
        scale = mod_ref[0, 1:2, :]
        h_ref[...] = (_ln(x_ref[0]) * (1.0 + scale) + shift).astype(BF16)

    @pl.when(n < lo_tiles)
    def _():
        lo_ref[0] = _dot(h_ref[...], wlo_ref[...]).astype(BF16)

    @pl.when(n >= lo_tiles)
    def _():
        hi_ref[0] = _dot(h_ref[...], whi_ref[...])


def _in_proj(x, mod, w_lo, w_hi, tl):
    b, l, d = x.shape
    lo_tiles = N_LO // LO_TILE
    hi_tiles = N_HI // HI_TILE
    return pl.pallas_call(
        functools.partial(_in_proj_kernel, lo_tiles=lo_tiles),
        grid=(b, l // tl, lo_tiles + hi_tiles),
        in_specs=[pl.BlockSpec((1, tl, d), lambda bi, i, n: (bi, i, 0)),
                  pl.BlockSpec((1, 6, d), lambda bi, i, n: (bi, 0, 0)),
                  pl.BlockSpec((d, LO_TILE), lambda bi, i, n: (0, jnp.minimum(n, lo_tiles - 1))),
                  pl.BlockSpec((d, HI_TILE), lambda bi, i, n: (0, jnp.maximum(n - lo_tiles, 0)))],
        out_specs=[pl.BlockSpec((1, tl, LO_TILE), lambda bi, i, n: (bi, i, jnp.minimum(n, lo_tiles - 1))),
                   pl.BlockSpec((1, tl, HI_TILE), lambda bi, i, n: (bi, i, jnp.maximum(n - lo_tiles, 0)))],
        out_shape=[jax.ShapeDtypeStruct((b, l, N_LO), BF16), jax.ShapeDtypeStruct((b, l, N_HI), F32)],
        scratch_shapes=[pltpu.VMEM((tl, d), BF16)],
        compiler_params=_cparams(("parallel", "parallel", "arbitrary"), 48),
        name="in_proj",
    )(x, mod, w_lo, w_hi)


def _conv_kernel(x_ref, prev_ref, next_ref, w_ref, o_ref):
    i = pl.program_id(1)
    last = pl.num_programs(1) - 1
    x = x_ref[0].astype(F32)
    tl = x.shape[0]
    halo = prev_ref.shape[1]
    row = lax.broadcasted_iota(jnp.int32, x.shape, 0)
    before = jnp.where(i > 0, prev_ref[0].astype(F32)[halo - 1:halo, :], 0.0)
    after = jnp.where(i < last, next_ref[0].astype(F32)[0:1, :], 0.0)
    x_dn = jnp.where(row == 0, before, pltpu.roll(x, 1, axis=0))
    x_up = jnp.where(row == tl - 1, after, pltpu.roll(x, tl - 1, axis=0))
    y = _silu(x_dn * w_ref[0:1, :] + x * w_ref[1:2, :] + x_up * w_ref[2:3, :])
    for j in range(3 * N_HEADS):
        t = y[:, j * HEAD_DIM:(j + 1) * HEAD_DIM]
        if j < 2 * N_HEADS:
            t = t * lax.rsqrt(jnp.sum(t * t, axis=-1, keepdims=True) + L2_EPS)
            if j < N_HEADS:
                t = t * HEAD_DIM ** -0.5
        o_ref[0, :, j * HEAD_DIM:(j + 1) * HEAD_DIM] = t.astype(o_ref.dtype)


def _gdn_conv(proj, conv_w, tl):
    b, l, _ = proj.shape
    w3 = 3 * MIX_W
    cb = LO_CQKV // w3
    halo = 16
    nbh = l // halo
    rh = tl // halo
    return pl.pallas_call(
        _conv_kernel,
        grid=(b, l // tl),
        in_specs=[pl.BlockSpec((1, tl, w3), lambda bi, i: (bi, i, cb)),
                  pl.BlockSpec((1, halo, w3), lambda bi, i: (bi, jnp.maximum(i * rh - 1, 0), cb)),
                  pl.BlockSpec((1, halo, w3), lambda bi, i: (bi, jnp.minimum((i + 1) * rh, nbh - 1), cb)),
                  pl.BlockSpec((3, w3), lambda bi, i: (0, 0))],
        out_specs=pl.BlockSpec((1, tl, w3), lambda bi, i: (bi, i, 0)),
        out_shape=jax.ShapeDtypeStruct((b, l, w3), BF16),
        compiler_params=_cparams(("parallel", "parallel"), 40),
        name="gdn_conv",
    )(proj, proj, proj, conv_w)


def _hgrn_chunks(probs, states, emat):
    c = probs[0]["q"].shape[0]
    nb = c // SUB
    for p in probs:
        lb = p["lb"]
        sig = _sigmoid(p["fpre"])
        p["k"] = (1.0 - lb) * (1.0 - sig)
        p["g"] = _cumsum_rows(jnp.log(lb + (1.0 - lb) * sig), p["rev"])

    walls = []
    for p in probs:
        q = p["q"]
        g2 = p["g"] * LOG2_E
        lk2 = jnp.log(p["k"]) * LOG2_E
        rel = lk2 - g2
        rows = []
        half = SUB // 2
        dead = jnp.zeros((half, HEAD_DIM), F32)
        for j in range(nb):
            r0 = j * SUB
            pieces = []
            for s in range(SUB):
                skip = 0 if (s >= half and not p["rev"]) else 1 if (s < half and p["rev"]) else None
                parts = []
                for part in range(2):
                    if part == skip:
                        parts.append(dead)
                        continue
                    t0 = r0 + part * half
                    e2 = jnp.exp2(jnp.minimum(g2[t0:t0 + half] + rel[r0 + s:r0 + s + 1], lk2[r0 + s:r0 + s + 1]))
                    parts.append(e2 * q[t0:t0 + half])
                pieces.append(jnp.concatenate(parts, axis=0).astype(BF16))
            rows.append(jnp.concatenate(pieces, axis=1))
        walls.append(jnp.concatenate(rows, axis=0))
    a_diags = [_dot(w, emat) for w in walls]

    a_offs = []
    for p in probs:
        q, k, g, rev = p["q"], p["k"], p["g"], p["rev"]
        offs = []
        for i in range(nb):
            r0 = i * SUB
            if (not rev and i == 0) or (rev and i == nb - 1):
                offs.append(jnp.zeros((SUB, c), F32))
                continue
            bi = g[r0 + SUB:r0 + SUB + 1] if rev else g[r0 - 1:r0]
            lhs = (q[r0:r0 + SUB] * jnp.exp(g[r0:r0 + SUB] - bi)).astype(BF16)
            lo, hi = (r0 + SUB, c) if rev else (0, r0)
            live = k[lo:hi] * jnp.exp(bi - g[lo:hi])
            pad = jnp.zeros((c - (hi - lo), HEAD_DIM), F32)
            rhs = jnp.concatenate([pad, live] if rev else [live, pad], axis=0)
            offs.append(_dot_nt(lhs, rhs.astype(BF16)))
        a_offs.append(jnp.concatenate(offs, axis=0))

    r = lax.broadcasted_iota(jnp.int32, (c, c), 0)
    col = lax.broadcasted_iota(jnp.int32, (c, c), 1)
    shift = SUB.bit_length() - 1
    rb = lax.shift_right_logical(r, shift)
    cb = lax.shift_right_logical(col, shift)
    inters = [_dot_nt((p["q"] * jnp.exp(p["g"])).astype(BF16), s_t.astype(BF16)) for p, s_t in zip(probs, states)]
    outs, new_states = [], []
    for p, a_diag, a_off, inter, s_t in zip(probs, a_diags, a_offs, inters, states):
        rev, g = p["rev"], p["g"]
        order = (col >= r) if rev else (col <= r)
        off_side = (cb > rb) if rev else (cb < rb)
        a = jnp.where((rb == cb) & order, a_diag, jnp.where(off_side, a_off, 0.0))
        g_end = g[0:1] if rev else g[c - 1:c]
        vb = p["v"].astype(BF16)
        outs.append(_dot(a.astype(BF16), vb) + inter)
        kd = (p["k"] * jnp.exp(g_end - g)).astype(BF16)
        new_states.append(s_t * jnp.exp(g_end) + _dot_tn(vb, kd))
    return outs, new_states


def _hgrn_kernel(qf_ref, ff_ref, vf_ref, qb_ref, fb_ref, vb_ref, lb_ref, e_ref, s0_ref,
                 of_ref, ob_ref, sout_ref, s_ref):
    n = pl.program_id(1)

    @pl.when(n == 0)
    def _():
        s_ref[...] = s0_ref[...]

    probs, where = [], []
    for r in range(qf_ref.shape[0]):
        for d, (q_ref, f_ref, v_ref, o_ref) in enumerate(((qf_ref, ff_ref, vf_ref, of_ref),
                                                          (qb_ref, fb_ref, vb_ref, ob_ref))):
            for h in range(N_HEADS):
                sl = slice(h * HEAD_DIM, (h + 1) * HEAD_DIM)
                probs.append(dict(q=q_ref[r, :, sl].astype(F32), fpre=f_ref[r, :, sl],
                                  v=v_ref[r, :, sl].astype(F32), lb=lb_ref[:, sl], rev=d == 1))
                where.append((o_ref, sl, r, h, d))
    states = [s_ref[r, h, d] for _, _, r, h, d in where]
    outs, new_states = [], []
    for g0 in range(0, len(probs), HGRN_GROUP):
        o_g, s_g = _hgrn_chunks(probs[g0:g0 + HGRN_GROUP], states[g0:g0 + HGRN_GROUP], e_ref[...])
        outs += o_g
        new_states += s_g
    for (o_ref, sl, r, h, d), o, s_new in zip(where, outs, new_states):
        o_ref[r, :, sl] = o.astype(o_ref.dtype)
        s_ref[r, h, d] = s_new

    @pl.when(n == pl.num_programs(1) - 1)
    def _():
        sout_ref[...] = s_ref[...]


def _hgrn_emat():
    r = jnp.arange(SUB * HEAD_DIM)[:, None] // HEAD_DIM
    c = jnp.arange(CHUNK)[None, :] % SUB
    return (r == c).astype(BF16)


def _hgrn_scan(p_lo, p_hi, lb, s0):
    b, l, _ = p_lo.shape
    nc = l // CHUNK
    rows = HGRN_BATCH_ROWS if b % HGRN_BATCH_ROWS == 0 else 1
    q_col, i_col, f_col = LO_AQ // MIX_W, LO_AI // MIX_W, HI_AF // MIX_W

    def fwd(col):
        return pl.BlockSpec((rows, CHUNK, MIX_W), lambda bi, n: (bi, n, col))

    def bwd(col):
        return pl.BlockSpec((rows, CHUNK, MIX_W), lambda bi, n: (bi, nc - 1 - n, col))

    st_spec = pl.BlockSpec((rows, N_HEADS, 2, HEAD_DIM, HEAD_DIM), lambda bi, n: (bi, 0, 0, 0, 0))
    return pl.pallas_call(
        _hgrn_kernel,
        grid=(b // rows, nc),
        in_specs=[fwd(q_col), fwd(f_col), fwd(i_col), bwd(q_col), bwd(f_col + 1), bwd(i_col),
                  pl.BlockSpec((1, MIX_W), lambda bi, n: (0, 0)),
                  pl.BlockSpec((SUB * HEAD_DIM, CHUNK), lambda bi, n: (0, 0)),
                  st_spec],
        out_specs=[pl.BlockSpec((rows, CHUNK, MIX_W), lambda bi, n: (bi, n, 0)),
                   pl.BlockSpec((rows, CHUNK, MIX_W), lambda bi, n: (bi, nc - 1 - n, 0)),
                   st_spec],
        out_shape=[jax.ShapeDtypeStruct((b, l, MIX_W), BF16),
                   jax.ShapeDtypeStruct((b, l, MIX_W), BF16),
                   jax.ShapeDtypeStruct(s0.shape, F32)],
        scratch_shapes=[pltpu.VMEM((rows, N_HEADS, 2, HEAD_DIM, HEAD_DIM), F32)],
        compiler_params=_cparams(("parallel", "arbitrary"), 40),
        name="hgrn_scan",
    )(p_lo, p_hi, p_lo, p_lo, p_hi, p_lo, lb, _hgrn_emat(), s0)


def _mlstm_chunks(probs, states):
    c = probs[0]["q"].shape[0]
    dv = probs[0]["v"].shape[1]
    ones = jnp.ones((c, dv), BF16)
    for p, (cmn, m) in zip(probs, states):
        tri = _order_mask(c, p["rev"])
        log_d = jnp.where(tri, p["bb_c"] - p["bb_r"] + p["ic_r"], MASK_NEG)
        log_inter = p["bb_c"] + m
        m_t = jnp.maximum(jnp.max(log_d, axis=1, keepdims=True), log_inter)
        p["m_t"] = m_t
        p["dmat"] = jnp.where(tri, jnp.exp(log_d - m_t), 0.0)
        p["inter"] = jnp.exp(log_inter - m_t)
        p["qb"], p["kb"] = p["q"].astype(BF16), p["k"].astype(BF16)
        p["v1"] = jnp.concatenate([p["v"].astype(BF16), ones], axis=1)
    scores = [_dot_nt(p["qb"], p["kb"]) * p["dmat"] for p in probs]
    carried = [_dot(p["qb"], cmn.astype(BF16)) for p, (cmn, _) in zip(probs, states)]
    local = [_dot(s.astype(BF16), p["v1"]) for p, s in zip(probs, scores)]
    outs, new_states = [], []
    for p, car, loc, (cmn, m) in zip(probs, carried, local, states):
        m_t = p["m_t"]
        both = loc + p["inter"] * car
        outs.append(both[:, :dv] / jnp.maximum(jnp.abs(both[:, dv:]), jnp.exp(-m_t)))
        m_new = m_t[0:1] if p["rev"] else m_t[c - 1:c]
        bb_end = p["bb_c"][0:1] if p["rev"] else p["bb_c"][c - 1:c]
        decay = jnp.exp(bb_end + m - m_new)
        kw = p["k"] * jnp.exp(bb_end - p["bb_c"] + p["ic_c"] - m_new)
        new_states.append((decay * cmn + _dot_tn(kw.astype(BF16), p["v1"]), m_new))
    return outs, new_states


def _column_view(ref):
    rows, group, width = ref.shape[-3:]
    return ref.reshape(rows * group, width), group


def _chunk_rows(j):
    start = j * CHUNK
    return pl.ds(start if isinstance(start, int) else pl.multiple_of(start, CHUNK), CHUNK)


def _load_chunk(ref, j, colmajor):
    if colmajor:
        flat, group = _column_view(ref)
        return flat[pl.ds(j, CHUNK, stride=group), :]
    return ref[0, _chunk_rows(j), :]


def _store_chunk(ref, j, val, colmajor):
    if colmajor:
        flat, group = _column_view(ref)
        flat[pl.ds(j, CHUNK, stride=group), :] = val
    else:
        ref[0, _chunk_rows(j), :] = val


def _columns_as_rows(x, first):
    row = lax.broadcasted_iota(jnp.int32, (8, LANE), 0)
    lane = lax.broadcasted_iota(jnp.int32, (8, LANE), 1)
    sel = jnp.where(lane == row + first, 1.0, 0.0).astype(BF16)
    hi = x.astype(BF16)
    rest = x - hi.astype(F32)
    mid = rest.astype(BF16)
    lo = (rest - mid.astype(F32)).astype(BF16)
    return _dot_nt(sel, hi) + (_dot_nt(sel, mid) + _dot_nt(sel, lo))


def _mlstm_kernel(*refs, colmajor, chunks):
    per_dir = 3 * N_HEADS + 1
    fwd, bwd = refs[:per_dir], refs[per_dir:2 * per_dir]
    bias_ref, cm0_ref, nm0_ref, of_ref, ob_ref, cmout_ref, nmout_ref, cm_ref, nm_ref = refs[2 * per_dir:]
    n = pl.program_id(1)

    @pl.when(n == 0)
    def _():
        cm_ref[...] = cm0_ref[0]
        nm_ref[...] = nm0_ref[0]

    def step(i, carry):
        probs, where = [], []
        for d, (in_refs, o_ref) in enumerate(((fwd, of_ref), (bwd, ob_ref))):
            rev = d == 1
            j = chunks - 1 - i if rev else i
            gates = _load_chunk(in_refs[-1], j, colmajor) + bias_ref[...]
            cs = _cumsum_rows(_log_sigmoid(gates), rev)
            cs_rows = _columns_as_rows(cs, 8 + 4 * d)
            gate_rows = _columns_as_rows(gates, 4 * d)
            for h in range(N_HEADS):
                ci = 4 * d + h
                cf = 8 + 4 * d + h
                probs.append(dict(
                    q=_load_chunk(in_refs[h], j, colmajor),
                    k=_load_chunk(in_refs[N_HEADS + h], j, colmajor) * HEAD_DIM ** -0.5,
                    v=_load_chunk(in_refs[2 * N_HEADS + h], j, colmajor),
                    bb_c=cs[:, cf:cf + 1], bb_r=cs_rows[h:h + 1, :],
                    ic_c=gates[:, ci:ci + 1], ic_r=gate_rows[h:h + 1, :], rev=rev))
                where.append((o_ref, j, d, h))
        states = [(cm_ref[d, h], nm_ref[d, h, 0:1, 0:1]) for _, _, d, h in where]
        outs, new_states = [], []
        for g0 in range(0, len(probs), MLSTM_GROUP):
            o_g, s_g = _mlstm_chunks(probs[g0:g0 + MLSTM_GROUP], states[g0:g0 + MLSTM_GROUP])
            outs += o_g
            new_states += s_g
        for (o_ref, j, d, h), o, (cmn_new, m_new) in zip(where, outs, new_states):
            _store_chunk(o_ref.at[h], j, o, colmajor)
            cm_ref[d, h] = cmn_new
            nm_ref[d, h, 0:1, :] = jnp.broadcast_to(m_new, (1, HEAD_DIM))
        return carry

    lax.fori_loop(0, chunks, step, 0)

    @pl.when(n == pl.num_programs(1) - 1)
    def _():
        cmout_ref[0] = cm_ref[...]
        nmout_ref[0] = nm_ref[...]


def _mlstm_scan(proj, gbias, cm0, nm0, colmajor):
    b, l, n_proj = proj.shape
    cb = HI_B // LANE
    cg = HI_GATES // LANE
    if colmajor:
        chunks = 8
        assert l == CHUNK * GRID_W and GRID_W % chunks == 0
        nb = GRID_W // chunks
        src = proj.reshape(b, CHUNK, GRID_W, n_proj)
        in_blk = (1, CHUNK, chunks, LANE)
        out_blk = (N_HEADS, 1, CHUNK, chunks, LANE)
        out_view = (N_HEADS, b, CHUNK, GRID_W, LANE)

        def in_map(col, flip):
            return lambda bi, n: (bi, 0, nb - 1 - n if flip else n, col)

        def out_map(flip):
            return lambda bi, n: (0, bi, 0, nb - 1 - n if flip else n, 0)
    else:
        chunks = min(l // CHUNK, 4)
        nb = l // (chunks * CHUNK)
        src = proj
        in_blk = (1, chunks * CHUNK, LANE)
        out_blk = (N_HEADS, 1, chunks * CHUNK, LANE)
        out_view = (N_HEADS, b, l, LANE)

        def in_map(col, flip):
            return lambda bi, n: (bi, nb - 1 - n if flip else n, col)

        def out_map(flip):
            return lambda bi, n: (0, bi, nb - 1 - n if flip else n, 0)

    def specs(flip):
        cols = [cb + j for j in range(3 * N_HEADS)] + [cg]
        return [pl.BlockSpec(in_blk, in_map(col, flip)) for col in cols]

    cm_spec = pl.BlockSpec((1, 2, N_HEADS, HEAD_DIM, 2 * HEAD_DIM), lambda bi, n: (bi, 0, 0, 0, 0))
    nm_spec = pl.BlockSpec((1, 2, N_HEADS, 8, HEAD_DIM), lambda bi, n: (bi, 0, 0, 0, 0))
    n_in = 2 * (3 * N_HEADS + 1)
    o_f, o_b, cm, nm = pl.pallas_call(
        functools.partial(_mlstm_kernel, colmajor=colmajor, chunks=chunks),
        grid=(b, nb),
        in_specs=specs(False) + specs(True) + [pl.BlockSpec((1, LANE), lambda bi, n: (0, 0)), cm_spec, nm_spec],
        out_specs=[pl.BlockSpec(out_blk, out_map(False)), pl.BlockSpec(out_blk, out_map(True)), cm_spec, nm_spec],
        out_shape=[jax.ShapeDtypeStruct(out_view, F32), jax.ShapeDtypeStruct(out_view, F32),
                   jax.ShapeDtypeStruct(cm0.shape, F32), jax.ShapeDtypeStruct(nm0.shape, F32)],
        scratch_shapes=[pltpu.VMEM((2, N_HEADS, HEAD_DIM, 2 * HEAD_DIM), F32),
                        pltpu.VMEM((2, N_HEADS, 8, HEAD_DIM), F32)],
        compiler_params=_cparams(("parallel", "arbitrary"), 40),
        name="mlstm_scan",
    )(*([src] * n_in), gbias, cm0, nm0)
    head_major = (N_HEADS, b, l, LANE)
    return o_f.reshape(head_major), o_b.reshape(head_major), cm, nm


def _split_bf16(x):
    hi = x.astype(BF16)
    return hi, (x - hi.astype(F32)).astype(BF16)


def _dot3(a, b, nt=False):
    f = _dot_nt if nt else _dot
    return f(a[0], b[0]) + (f(a[0], b[1]) + f(a[1], b[0]))


def _inv_unit_triangular(a_list):
    n = a_list[0].shape[0]
    eye = (lax.broadcasted_iota(jnp.int32, (n, n), 0) == lax.broadcasted_iota(jnp.int32, (n, n), 1)).astype(F32)
    xs = [eye - a for a in a_list]
    splits = [_split_bf16(a) for a in a_list]
    ps = [_dot3(s, s) for s in splits]
    steps = (n - 1).bit_length() - 1
    for step in range(steps):
        last = step == steps - 1
        p_split = [_split_bf16(p) for p in ps]
        lhs = xs if last else [jnp.concatenate([x, p], axis=0) for x, p in zip(xs, ps)]
        prods = [_dot3(_split_bf16(l), s) for l, s in zip(lhs, p_split)]
        xs = [x + pr[:n] for x, pr in zip(xs, prods)]
        if not last:
            ps = [pr[n:] for pr in prods]
    return xs


def _gdn_chunks(probs, states):
    c = probs[0]["q"].shape[0]
    a_list, ldecs = [], []
    for p in probs:
        tri = _order_mask(c, p["rev"])
        strict = _order_mask(c, p["rev"], strict=True)
        ldec = jnp.where(tri, jnp.exp(jnp.minimum(p["gam_c"] - p["gam_r"], 0.0)), 0.0)
        p["kbeta"] = p["k"] * p["beta"]
        kk = _dot_nt(p["kbeta"].astype(BF16), p["k"].astype(BF16))
        a_list.append(jnp.where(strict, kk * ldec, 0.0))
        ldecs.append(ldec)
    t_invs = _inv_unit_triangular(a_list)
    sols = []
    for p, t_inv in zip(probs, t_invs):
        p["eg"] = jnp.exp(p["gam_c"])
        rhs = jnp.concatenate([p["v"] * p["beta"], p["kbeta"] * p["eg"]], axis=1)
        sols.append(_dot3(_split_bf16(t_inv), _split_bf16(rhs)))
    qks = [_dot_nt(p["q"].astype(BF16), p["k"].astype(BF16)) * ldec for p, ldec in zip(probs, ldecs)]
    inter = [_dot(jnp.concatenate([sol[:, HEAD_DIM:], p["q"] * p["eg"]], axis=0).astype(BF16), s.astype(BF16))
             for p, sol, s in zip(probs, sols, states)]
    outs, new_states = [], []
    for p, sol, qk, it, s in zip(probs, sols, qks, inter, states):
        g_end = p["gam_c"][0:1] if p["rev"] else p["gam_c"][c - 1:c]
        v_new = (sol[:, :HEAD_DIM] - it[:c]).astype(BF16)
        kg = p["k"] * jnp.exp(g_end - p["gam_c"])
        outs.append(it[c:] + _dot(qk.astype(BF16), v_new))
        new_states.append(jnp.exp(g_end) * s + _dot_tn(kg.astype(BF16), v_new))
    return outs, new_states


def _gdn_kernel(xf_ref, gf_ref, xb_ref, gb_ref, alog_ref, dtb_ref, s0_ref,
                of_ref, ob_ref, sout_ref, s_ref):
    n = pl.program_id(1)

    @pl.when(n == 0)
    def _():
        s_ref[...] = s0_ref[...]

    neg_a = -jnp.exp(alog_ref[...])
    probs, where = [], []
    for r in range(xf_ref.shape[0]):
        for d, (x_ref, g_ref, o_ref) in enumerate(((xf_ref, gf_ref, of_ref), (xb_ref, gb_ref, ob_ref))):
            rev = d == 1
            gates = g_ref[r]
            cs = _cumsum_rows(neg_a * _softplus(gates + dtb_ref[...]), rev)
            cs_t = cs.T
            betas = _sigmoid(gates)
            for h in range(N_HEADS):
                ca = 16 + 4 * d + h
                cbeta = 24 + 4 * d + h
                probs.append(dict(
                    q=x_ref[r, :, h * HEAD_DIM:(h + 1) * HEAD_DIM].astype(F32),
                    k=x_ref[r, :, MIX_W + h * HEAD_DIM:MIX_W + (h + 1) * HEAD_DIM].astype(F32),
                    v=x_ref[r, :, 2 * MIX_W + h * HEAD_DIM:2 * MIX_W + (h + 1) * HEAD_DIM].astype(F32),
                    gam_c=cs[:, ca:ca + 1], gam_r=cs_t[ca:ca + 1, :], beta=betas[:, cbeta:cbeta + 1], rev=rev))
                where.append((o_ref, r, d, h))
    outs, new_states = _gdn_chunks(probs, [s_ref[r, d, h] for _, r, d, h in where])
    for (o_ref, r, d, h), o, s_new in zip(where, outs, new_states):
        o_ref[r, :, h * HEAD_DIM:(h + 1) * HEAD_DIM] = o.astype(o_ref.dtype)
        s_ref[r, d, h] = s_new

    @pl.when(n == pl.num_programs(1) - 1)
    def _():
        sout_ref[...] = s_ref[...]


def _gdn_scan(qkv, proj, alog_row, dtb_row, s0):
    b, l, w3 = qkv.shape
    nc = l // CHUNK
    cg = HI_GATES // LANE
    rows = GDN_BATCH_ROWS if b % GDN_BATCH_ROWS == 0 else 1
    st_spec = pl.BlockSpec((rows, 2, N_HEADS, HEAD_DIM, HEAD_DIM), lambda bi, n: (bi, 0, 0, 0, 0))
    row_spec = pl.BlockSpec((1, LANE), lambda bi, n: (0, 0))
    return pl.pallas_call(
        _gdn_kernel,
        grid=(b // rows, nc),
        in_specs=[pl.BlockSpec((rows, CHUNK, w3), lambda bi, n: (bi, n, 0)),
                  pl.BlockSpec((rows, CHUNK, LANE), lambda bi, n: (bi, n, cg)),
                  pl.BlockSpec((rows, CHUNK, w3), lambda bi, n: (bi, nc - 1 - n, 0)),
                  pl.BlockSpec((rows, CHUNK, LANE), lambda bi, n: (bi, nc - 1 - n, cg)),
                  row_spec, row_spec, st_spec],
        out_specs=[pl.BlockSpec((rows, CHUNK, MIX_W), lambda bi, n: (bi, n, 0)),
                   pl.BlockSpec((rows, CHUNK, MIX_W), lambda bi, n: (bi, nc - 1 - n, 0)),
                   st_spec],
        out_shape=[jax.ShapeDtypeStruct((b, l, MIX_W), BF16), jax.ShapeDtypeStruct((b, l, MIX_W), BF16),
                   jax.ShapeDtypeStruct(s0.shape, F32)],
        scratch_shapes=[pltpu.VMEM((rows, 2, N_HEADS, HEAD_DIM, HEAD_DIM), F32)],
        compiler_params=_cparams(("parallel", "arbitrary"), 40),
        name="gdn_scan",
    )(qkv, proj, qkv, proj, alog_row, dtb_row, s0)


def _merge_kernel(oaf, oab, obf, obb, ocf, ocb, ag, bo, cz, mg, x_ref, mod_ref, gain_ref, wbr_ref, wout_ref,
                  lng_ref, lnb_ref, out_ref, *, alpha):
    d_model = x_ref.shape[-1]
    branches = ((oaf, oab, ag, _silu), (obf, obb, bo, _sigmoid), (ocf, ocb, cz, _silu))
    mixed = None
    for nbr, (of_ref, ob_ref, gate_ref, act) in enumerate(branches):
        parts = []
        for h in range(N_HEADS):
            if len(of_ref.shape) == 4:
                oh = of_ref[h, 0] + ob_ref[h, 0]
            else:
                sl = slice(h * HEAD_DIM, (h + 1) * HEAD_DIM)
                oh = of_ref[0, :, sl].astype(F32) + ob_ref[0, :, sl].astype(F32)
            parts.append(oh * lax.rsqrt(jnp.mean(oh * oh, axis=-1, keepdims=True) + RMS_EPS))
        y = jnp.concatenate(parts, axis=1) * gain_ref[nbr:nbr + 1, :] * act(gate_ref[0].astype(F32))
        gate = _sigmoid(mg[0, :, nbr * d_model:(nbr + 1) * d_model].astype(F32))
        term = gate * _dot(y.astype(BF16), wbr_ref[nbr])
        mixed = term if mixed is None else mixed + term
    sub = _dot(mixed.astype(BF16), wout_ref[...])
    z = alpha * x_ref[0] + mod_ref[0, 2:3, :] * sub
    out_ref[0] = _ln(z) * lng_ref[...] + lnb_ref[...]


def _merge(outs, proj, x, mod, gain, w_branch, w_out, ln_g, ln_b, alpha, tl):
    b, l, d = x.shape
    u = MIX_W

    def tok(width, col):
        return pl.BlockSpec((1, tl, width), lambda bi, i: (bi, i, col))

    def const(shape):
        return pl.BlockSpec(shape, lambda bi, i: (0,) * len(shape))

    heads = pl.BlockSpec((N_HEADS, 1, tl, LANE), lambda bi, i: (0, bi, i, 0))
    return pl.pallas_call(
        functools.partial(_merge_kernel, alpha=alpha),
        grid=(b, l // tl),
        in_specs=[tok(u, 0), tok(u, 0), heads, heads, tok(u, 0), tok(u, 0),
            tok(u, LO_AG // u), tok(u, LO_BO // u), tok(u, LO_CZ // u),
            tok(3 * d, LO_MERGE // (3 * d)), tok(d, 0),
            pl.BlockSpec((1, 6, d), lambda bi, i: (bi, 0, 0)),
            const((3, u)), const((3, u, d)), const((d, d)), const((1, d)), const((1, d))],
        out_specs=tok(d, 0),
        out_shape=jax.ShapeDtypeStruct((b, l, d), F32),
        compiler_params=_cparams(("parallel", "parallel"), 48),
        name="merge",
    )(*outs, proj, proj, proj, proj, x, mod, gain, w_branch, w_out, ln_g, ln_b)


def _ffn_kernel(x_ref, mod_ref, wgu_ref, wd_ref, lng_ref, lnb_ref, out_ref, h_ref, acc_ref, *, alpha):
    f = pl.program_id(2)

    @pl.when(f == 0)
    def _():
        h_ref[...] = (_ln(x_ref[0]) * (1.0 + mod_ref[0, 4:5, :]) + mod_ref[0, 3:4, :]).astype(BF16)
        acc_ref[...] = jnp.zeros_like(acc_ref)

    gu = _dot(h_ref[...], wgu_ref[0, 0])
    half = gu.shape[1] // 2
    act = _silu(gu[:, :half]) * gu[:, half:]
    acc_ref[...] += _dot(act.astype(BF16), wd_ref[...])

    @pl.when(f == pl.num_programs(2) - 1)
    def _():
        z = alpha * x_ref[0] + mod_ref[0, 5:6, :] * acc_ref[...]
        out_ref[0] = _ln(z) * lng_ref[...] + lnb_ref[...]


def _ffn(x, mod, w_gate_up, w_down, ln_g, ln_b, alpha, tl):
    b, l, d = x.shape
    _, nf, _, tf2 = w_gate_up.shape
    return pl.pallas_call(
        functools.partial(_ffn_kernel, alpha=alpha),
        grid=(b, l // tl, nf),
        in_specs=[pl.BlockSpec((1, tl, d), lambda bi, i, j: (bi, i, 0)),
                  pl.BlockSpec((1, 6, d), lambda bi, i, j: (bi, 0, 0)),
                  pl.BlockSpec((1, 1, d, tf2), lambda bi, i, j: (0, j, 0, 0)),
                  pl.BlockSpec((tf2 // 2, d), lambda bi, i, j: (j, 0)),
                  pl.BlockSpec((1, d), lambda bi, i, j: (0, 0)),
                  pl.BlockSpec((1, d), lambda bi, i, j: (0, 0))],
        out_specs=pl.BlockSpec((1, tl, d), lambda bi, i, j: (bi, i, 0)),
        out_shape=jax.ShapeDtypeStruct((b, l, d), F32),
        scratch_shapes=[pltpu.VMEM((tl, d), BF16), pltpu.VMEM((tl, d), F32)],
        compiler_params=_cparams(("parallel", "parallel", "arbitrary"), 56),
        name="ffn",
    )(x, mod, w_gate_up, w_down, ln_g, ln_b)


def _top2_combine(logits):
    lane = lax.broadcasted_iota(jnp.int32, logits.shape, 1).astype(F32)
    neg = -jnp.inf
    lg = jnp.where(lane < N_EXPERTS, logits, neg)
    m1 = jnp.max(lg, axis=1, keepdims=True)
    i1 = jnp.min(jnp.where(lg == m1, lane, float(LANE)), axis=1, keepdims=True)
    lg2 = jnp.where(lane == i1, neg, lg)
    m2 = jnp.max(lg2, axis=1, keepdims=True)
    i2 = jnp.min(jnp.where(lg2 == m2, lane, float(LANE)), axis=1, keepdims=True)
    e2 = jnp.exp(m2 - m1)
    p1 = 1.0 / (1.0 + e2)
    chosen = (lane == i1) | (lane == i2)
    return jnp.where(lane == i1, p1, jnp.where(lane == i2, e2 * p1, 0.0)), chosen


def _moe_kernel(x_ref, mod_ref, wr_ref, wgu_ref, wd_ref, lng_ref, lnb_ref, out_ref,
                h_ref, acc_ref, comb_ref, pos_ref, post_ref, col_ref, xc_ref, yc_ref, cnt_ref, *, alpha, rb):
    e = pl.program_id(2)
    f = pl.program_id(3)
    tl, d = h_ref.shape

    @pl.when((e == 0) & (f == 0))
    def _():
        h = _ln(x_ref[0]) * (1.0 + mod_ref[0, 4:5, :]) + mod_ref[0, 3:4, :]
        h_ref[...] = h.astype(BF16)
        comb, chosen = _top2_combine(_dot(h, wr_ref[...], precision=HIGHEST))
        comb_ref[...] = comb
        blk = min(tl, LANE)
        before = jnp.where(_order_mask(blk, False, strict=True), 1.0, 0.0).astype(BF16)
        routed = jnp.where(chosen, 1.0, 0.0)
        total = jnp.zeros((1, LANE), F32)
        ranks = []
        for r0 in range(0, tl, blk):
            part = routed[r0:r0 + blk]
            ranks.append(_dot(before, part.astype(BF16)) + total)
            total = total + jnp.sum(part, axis=0, keepdims=True)
        rank = jnp.concatenate(ranks, axis=0)
        pos = jnp.where(chosen, rank, -1.0)
        pos_ref[...] = pos
        pos_t = pos.T[0:N_EXPERTS, :]
        post_ref[...] = pos_t
        for ex in range(N_EXPERTS):
            cnt_ref[ex] = jnp.sum(jnp.where(pos_t[ex:ex + 1, :] >= 0.0, 1.0, 0.0)).astype(jnp.int32)
        acc_ref[...] = jnp.zeros_like(acc_ref)

    nblk = (cnt_ref[e] + (rb - 1)) // rb

    def rows(r):
        return pl.ds(pl.multiple_of(r * rb, rb), rb)

    @pl.when(f == 0)
    def _():
        lane = lax.broadcasted_iota(jnp.int32, (tl, LANE), 1)
        col_ref[0] = jnp.sum(jnp.where(lane == e, pos_ref[...], 0.0), axis=1, keepdims=True)
        col_ref[1] = jnp.sum(jnp.where(lane == e, comb_ref[...], 0.0), axis=1, keepdims=True)
        pos_row = post_ref[pl.ds(e, 1), :]

        def gather(r, carry):
            slot = (lax.broadcasted_iota(jnp.int32, (rb, tl), 0) + r * rb).astype(F32)
            sel = jnp.where(pos_row == slot, 1.0, 0.0).astype(BF16)
            xc_ref[rows(r), :] = _dot(sel, h_ref[...]).astype(BF16)
            yc_ref[rows(r), :] = jnp.zeros((rb, d), F32)
            return carry

        lax.fori_loop(0, nblk, gather, 0)

    def expert(r, carry):
        xb = xc_ref[rows(r), :]
        gu = _dot(xb, wgu_ref[0, 0])
        half = gu.shape[1] // 2
        act = _silu(gu[:, :half]) * gu[:, half:]
        yc_ref[rows(r), :] += _dot(act.astype(BF16), wd_ref[0])
        return carry

    lax.fori_loop(0, nblk, expert, 0)

    @pl.when(f == pl.num_programs(3) - 1)
    def _():
        pos_col = col_ref[0]
        weight = col_ref[1]

        def scatter(r, carry):
            slot = (lax.broadcasted_iota(jnp.int32, (tl, rb), 1) + r * rb).astype(F32)
            sel_t = jnp.where(pos_col == slot, 1.0, 0.0).astype(BF16)
            acc_ref[...] += weight * _dot(sel_t, yc_ref[rows(r), :].astype(BF16))
            return carry

        lax.fori_loop(0, nblk, scatter, 0)

    @pl.when((e == pl.num_programs(2) - 1) & (f == pl.num_programs(3) - 1))
    def _():
        z = alpha * x_ref[0] + mod_ref[0, 5:6, :] * acc_ref[...]
        out_ref[0] = _ln(z) * lng_ref[...] + lnb_ref[...]


def _pack_gate_up(w_gate, w_up, widest):
    e, d, f = w_gate.shape
    tf = next(t for t in (1792, 896, 512, 256, 128) if t <= widest and f % t == 0)

    def tiles(w):
        return w.astype(BF16).reshape(e, d, f // tf, tf).transpose(0, 2, 1, 3)

    return jnp.concatenate([tiles(w_gate), tiles(w_up)], axis=-1)


def _moe(x, mod, w_router, w_gate_up, w_down, ln_g, ln_b, alpha, tl):
    b, l, d = x.shape
    ne, nf, _, tf2 = w_gate_up.shape
    rb = min(MOE_ROW_BLOCK, tl)
    compact_rows = -(-tl // rb) * rb
    return pl.pallas_call(
        functools.partial(_moe_kernel, alpha=alpha, rb=rb),
        grid=(b, l // tl, ne, nf),
        in_specs=[pl.BlockSpec((1, tl, d), lambda bi, i, e, j: (bi, i, 0)),
                  pl.BlockSpec((1, 6, d), lambda bi, i, e, j: (bi, 0, 0)),
                  pl.BlockSpec((d, LANE), lambda bi, i, e, j: (0, 0)),
                  pl.BlockSpec((1, 1, d, tf2), lambda bi, i, e, j: (e, j, 0, 0)),
                  pl.BlockSpec((1, tf2 // 2, d), lambda bi, i, e, j: (e, j, 0)),
                  pl.BlockSpec((1, d), lambda bi, i, e, j: (0, 0)),
                  pl.BlockSpec((1, d), lambda bi, i, e, j: (0, 0))],
        out_specs=pl.BlockSpec((1, tl, d), lambda bi, i, e, j: (bi, i, 0)),
        out_shape=jax.ShapeDtypeStruct((b, l, d), F32),
        scratch_shapes=[pltpu.VMEM((tl, d), BF16),
                        pltpu.VMEM((tl, d), F32),
                        pltpu.VMEM((tl, LANE), F32),
                        pltpu.VMEM((tl, LANE), F32),
                        pltpu.VMEM((N_EXPERTS, tl), F32),
                        pltpu.VMEM((2, tl, 1), F32),
                        pltpu.VMEM((compact_rows, d), BF16),
                        pltpu.VMEM((compact_rows, d), F32),
                        pltpu.SMEM((N_EXPERTS,), jnp.int32)],
        compiler_params=_cparams(("parallel", "parallel", "arbitrary", "arbitrary"), 60),
        name="moe",
    )(x, mod, w_router, w_gate_up, w_down, ln_g, ln_b)


def _reorder_w_in(w):
    d = w.shape[0]
    u = MIX_W
    o_a, o_b = 0, 5 * u
    o_bgt = o_b + 4 * u
    o_cqkv = o_bgt + 16
    o_cz = o_cqkv + 3 * u
    o_cgt = o_cz + u
    o_merge = o_cgt + 16
    lo = [w[:, o_merge:o_merge + 3 * d], w[:, o_cqkv:o_cz], w[:, o_a:o_a + u], w[:, o_a + 3 * u:o_a + 5 * u],
          w[:, o_b + 3 * u:o_bgt], w[:, o_cz:o_cgt]]
    hi = [w[:, o_a + u:o_a + 3 * u], w[:, o_b:o_b + 3 * u], w[:, o_bgt:o_bgt + 16], w[:, o_cgt:o_cgt + 16],
          jnp.zeros((d, N_HI - HI_GATES - 32), w.dtype)]
    return jnp.concatenate(lo, axis=1).astype(BF16), jnp.concatenate(hi, axis=1).astype(BF16)


def _pad_row(vals, offset):
    row = jnp.zeros((1, LANE), F32)
    return lax.dynamic_update_slice(row, vals.reshape(1, -1).astype(F32), (0, offset))


def _tile(l, pref):
    return pref if l % pref == 0 else l


def kernel(x, c, ctx, c_ctx, w_ada, b_ada, w_in, conv_w, lb_raw, m_gate_bias, gdn_a_log, gdn_dt_bias, mix_norm,
           w_branch, w_out, ln_g, ln_b, ffn_w_gate, ffn_w_up, ffn_w_down, moe_router, moe_w_gate, moe_w_up,
           moe_w_down):
    depth, d_model = w_in.shape[0], w_in.shape[1]
    b, l_lat, _ = x.shape
    l_ctx = ctx.shape[1]
    alpha = float((2 * depth) ** 0.25)
    assert d_model == 1024 and w_in.shape[2] == 9 * MIX_W + 3 * MIX_W + MIX_W + 32 + 3 * d_model

    lower = _lower_bounds(lb_raw.astype(F32))
    n_rows = -(-(b + 1) // 8) * 8
    cvec = jnp.concatenate([c, c_ctx[None, :], jnp.zeros((n_rows - b - 1, d_model), F32)], axis=0)
    mods = _ada(cvec, w_ada, b_ada)

    zeros_state = jnp.zeros((b, 2, N_HEADS, HEAD_DIM, HEAD_DIM), F32)
    zeros_hgrn = jnp.zeros((b, N_HEADS, 2, HEAD_DIM, HEAD_DIM), F32)
    zeros_nm = jnp.zeros((b, 2, N_HEADS, 8, HEAD_DIM), F32)
    zeros_cmn = jnp.zeros((b, 2, N_HEADS, HEAD_DIM, 2 * HEAD_DIM), F32)

    pack = 1024 // l_ctx if (1024 % l_ctx == 0 and b % max(1024 // l_ctx, 1) == 0) else 1
    ctx_packed = (b // pack, pack * l_ctx, d_model)

    for l in range(depth):
        mod_l = mods[l, :b].reshape(b, 6, d_model)
        mod_c = jnp.broadcast_to(mods[l, b].reshape(1, 6, d_model), (b, 6, d_model))
        w_lo, w_hi = _reorder_w_in(w_in[l])
        lb_row = lower[l].reshape(1, MIX_W)
        gbias = _pad_row(m_gate_bias[l], 0)
        alog_row = _pad_row(gdn_a_log[l], 16)
        dtb_row = _pad_row(gdn_dt_bias[l], 16)
        gain = mix_norm[l]
        wbr = w_branch[l].astype(BF16)
        wo = w_out[l].astype(BF16)
        g0, b0 = ln_g[l, 0].reshape(1, -1), ln_b[l, 0].reshape(1, -1)
        g1, b1 = ln_g[l, 1].reshape(1, -1), ln_b[l, 1].reshape(1, -1)

        lo_c, hi_c = _in_proj(ctx.reshape(ctx_packed), mod_c[:ctx_packed[0]], w_lo, w_hi, ctx_packed[1])
        lo_c, hi_c = lo_c.reshape(b, l_ctx, N_LO), hi_c.reshape(b, l_ctx, N_HI)
        lo_l, hi_l = _in_proj(x, mod_l, w_lo, w_hi, _tile(l_lat, 1024))
        qkv_c = _gdn_conv(lo_c, conv_w[l], _tile(l_ctx, 256))
        qkv_l = _gdn_conv(lo_l, conv_w[l], _tile(l_lat, 512))

        oa_c = _hgrn_scan(lo_c, hi_c, lb_row, zeros_hgrn)
        oa_l = _hgrn_scan(lo_l, hi_l, lb_row, oa_c[2])
        ob_c = _mlstm_scan(hi_c, gbias, zeros_cmn, zeros_nm, colmajor=False)
        ob_l = _mlstm_scan(hi_l, gbias, ob_c[2], ob_c[3], colmajor=True)
        oc_c = _gdn_scan(qkv_c, hi_c, alog_row, dtb_row, zeros_state)
        oc_l = _gdn_scan(qkv_l, hi_l, alog_row, dtb_row, oc_c[2])

        def channel_mixer(t, mod, tl):
            i = l // 2
            if l % 2 == 0:
                return _ffn(t, mod, _pack_gate_up(ffn_w_gate[i][None], ffn_w_up[i][None], 896),
                            ffn_w_down[i].astype(BF16), g1, b1, alpha, tl)
            w_r = jnp.concatenate([moe_router[i], jnp.zeros((d_model, LANE - N_EXPERTS), F32)], axis=1)
            return _moe(t, mod, w_r, _pack_gate_up(moe_w_gate[i], moe_w_up[i], 1792), moe_w_down[i].astype(BF16),
                        g1, b1, alpha, tl)

        outs_l = (oa_l[0], oa_l[1], ob_l[0], ob_l[1], oc_l[0], oc_l[1])
        x = _merge(outs_l, lo_l, x, mod_l, gain, wbr, wo, g0, b0, alpha, _tile(l_lat, 512))
        x = channel_mixer(x, mod_l, _tile(l_lat, 1024))
        if l < depth - 1:
            outs_c = (oa_c[0], oa_c[1], ob_c[0], ob_c[1], oc_c[0], oc_c[1])
            ctx = _merge(outs_c, lo_c, ctx, mod_c, gain, wbr, wo, g0, b0, alpha, _tile(l_ctx, 256))
            ctx = channel_mixer(ctx.reshape(ctx_packed), mod_c[:ctx_packed[0]], ctx_packed[1]).reshape(ctx.shape)
    return x
```

```python
import functools

import jax
import jax.numpy as jnp
from jax import lax
from jax.experimental import pallas as pl
from jax.experimental.pallas import tpu as pltpu

F32 = jnp.float32
BF16 = jnp.bfloat16
HIGHEST = lax.Precision.HIGHEST

HEAD_DIM = 128
N_HEADS = 4
MIX_W = N_HEADS * HEAD_DIM
CHUNK = 64
GRID_W = 64
SUB = 16
MLSTM_GROUP = 2
GDN_BATCH_ROWS = 4
HGRN_GROUP = 8
HGRN_BATCH_ROWS = 4
N_EXPERTS = 8
MOE_ROW_BLOCK = 288
LN_EPS = 1e-5
RMS_EPS = 1e-6
L2_EPS = 1e-6
MASK_NEG = -1e30
LOG2_E = 1.4426950408889634
LANE = 128

LO_MERGE = 0
LO_CQKV = 3 * 1024
LO_AQ = LO_CQKV + 3 * MIX_W
LO_AI = LO_AQ + MIX_W
LO_AG = LO_AI + MIX_W
LO_BO = LO_AG + MIX_W
LO_CZ = LO_BO + MIX_W
N_LO = LO_CZ + MIX_W
LO_TILE = N_LO // 4
HI_AF = 0
HI_B = 2 * MIX_W
HI_GATES = HI_B + 3 * MIX_W
HI_TILE = HI_GATES + 2 * LANE
N_HI = HI_TILE


def _cparams(sem, vmem_mb=None):
    kw = dict(dimension_semantics=sem)
    if vmem_mb is not None:
        kw["vmem_limit_bytes"] = vmem_mb << 20
    return pltpu.CompilerParams(**kw)


def _dot(a, b, precision=None):
    return jnp.dot(a, b, preferred_element_type=F32, precision=precision)


def _dot_nt(a, b, precision=None):
    return lax.dot_general(a, b, (((1,), (1,)), ((), ())), preferred_element_type=F32, precision=precision)


def _dot_tn(a, b):
    return lax.dot_general(a, b, (((0,), (0,)), ((), ())), preferred_element_type=F32)


def _ln(x):
    mu = jnp.mean(x, axis=-1, keepdims=True)
    xc = x - mu
    var = jnp.mean(xc * xc, axis=-1, keepdims=True)
    return xc * lax.rsqrt(var + LN_EPS)


def _sigmoid(x):
    return 1.0 / (1.0 + jnp.exp(-x))


def _silu(x):
    return x * _sigmoid(x)


def _softplus(x):
    return jnp.maximum(x, 0.0) + jnp.log(1.0 + jnp.exp(-jnp.abs(x)))


def _log_sigmoid(x):
    return -_softplus(-x)


def _order_mask(n, rev, strict=False):
    r = lax.broadcasted_iota(jnp.int32, (n, n), 0)
    c = lax.broadcasted_iota(jnp.int32, (n, n), 1)
    if rev:
        return (c > r) if strict else (c >= r)
    return (c < r) if strict else (c <= r)


def _cumsum_rows(x, rev):
    m = _order_mask(x.shape[0], rev).astype(F32)
    return _dot(m, x, precision=HIGHEST)


def _ada_kernel(c_ref, w_ref, b_ref, o_ref):
    s = _silu(c_ref[...])
    o_ref[0] = _dot(s, w_ref[0], precision=HIGHEST) + b_ref[0]


def _ada(cvec, w_ada, b_ada, tn=1536):
    depth, d, n6 = w_ada.shape
    rows = cvec.shape[0]
    return pl.pallas_call(
        _ada_kernel,
        grid=(depth, n6 // tn),
        in_specs=[pl.BlockSpec((rows, d), lambda l, n: (0, 0)),
                  pl.BlockSpec((1, d, tn), lambda l, n: (l, 0, n)),
                  pl.BlockSpec((1, 1, tn), lambda l, n: (l, 0, n))],
        out_specs=pl.BlockSpec((1, rows, tn), lambda l, n: (l, 0, n)),
        out_shape=jax.ShapeDtypeStruct((depth, rows, n6), F32),
        compiler_params=_cparams(("parallel", "parallel"), 40),
        name="ada",
    )(cvec, w_ada, b_ada.reshape(depth, 1, n6))


def _lower_bounds_kernel(lb_ref, o_ref):
    x = lb_ref[...]
    depth = x.shape[0]
    mx = jnp.max(x, axis=0, keepdims=True)
    e = jnp.exp(x - mx)
    p = e / jnp.sum(e, axis=0, keepdims=True)
    acc = jnp.zeros_like(p[0:1])
    for l in range(depth):
        o_ref[l:l + 1, :] = acc
        if l + 1 < depth:
            acc = acc + p[l + 1:l + 2]


def _lower_bounds(lb_raw):
    return pl.pallas_call(
        _lower_bounds_kernel,
        out_shape=jax.ShapeDtypeStruct(lb_raw.shape, F32),
        name="lower_bounds",
    )(lb_raw)


def _in_proj_kernel(x_ref, mod_ref, wlo_ref, whi_ref, lo_ref, hi_ref, h_ref, *, lo_tiles):
    n = pl.program_id(2)

    @pl.when(n == 0)
    def _():
        shift = mod_ref[0, 0:1, :]
        scale = mod_ref[0, 1:2, :]
        h_ref[...] = (_ln(x_ref[0]) * (1.0 + scale) + shift).astype(BF16)

    @pl.when(n < lo_tiles)
    def _():
        lo_ref[0] = _dot(h_ref[...], wlo_ref[...]).astype(BF16)

    @pl.when(n >= lo_tiles)
    def _():
        hi_ref[0] = _dot(h_ref[...], whi_ref[...])


def _in_proj(x, mod, w_lo, w_hi, tl):
    b, l, d = x.shape
    lo_tiles = N_LO // LO_TILE
    hi_tiles = N_HI // HI_TILE
    return pl.pallas_call(
        functools.partial(_in_proj_kernel, lo_tiles=lo_tiles),
        grid=(b, l // tl, lo_tiles + hi_tiles),
        in_specs=[pl.BlockSpec((1, tl, d), lambda bi, i, n: (bi, i, 0)),
                  pl.BlockSpec((1, 6, d), lambda bi, i, n: (bi, 0, 0)),
                  pl.BlockSpec((d, LO_TILE), lambda bi, i, n: (0, jnp.minimum(n, lo_tiles - 1))),
                  pl.BlockSpec((d, HI_TILE), lambda bi, i, n: (0, jnp.maximum(n - lo_tiles, 0)))],
        out_specs=[pl.BlockSpec((1, tl, LO_TILE), lambda bi, i, n: (bi, i, jnp.minimum(n, lo_tiles - 1))),
                   pl.BlockSpec((1, tl, HI_TILE), lambda bi, i, n: (bi, i, jnp.maximum(n - lo_tiles, 0)))],
        out_shape=[jax.ShapeDtypeStruct((b, l, N_LO), BF16), jax.ShapeDtypeStruct((b, l, N_HI), F32)],
        scratch_shapes=[pltpu.VMEM((tl, d), BF16)],
        compiler_params=_cparams(("parallel", "parallel", "arbitrary"), 60),
        name="in_proj",
    )(x, mod, w_lo, w_hi)


def _conv_kernel(x_ref, prev_ref, next_ref, w_ref, o_ref):
    i = pl.program_id(1)
    last = pl.num_programs(1) - 1
    x = x_ref[0].astype(F32)
    tl = x.shape[0]
    halo = prev_ref.shape[1]
    row = lax.broadcasted_iota(jnp.int32, x.shape, 0)
    before = jnp.where(i > 0, prev_ref[0].astype(F32)[halo - 1:halo, :], 0.0)
    after = jnp.where(i < last, next_ref[0].astype(F32)[0:1, :], 0.0)
    x_dn = jnp.where(row == 0, before, pltpu.roll(x, 1, axis=0))
    x_up = jnp.where(row == tl - 1, after, pltpu.roll(x, tl - 1, axis=0))
    y = _silu(x_dn * w_ref[0:1, :] + x * w_ref[1:2, :] + x_up * w_ref[2:3, :])
    for j in range(3 * N_HEADS):
        t = y[:, j * HEAD_DIM:(j + 1) * HEAD_DIM]
        if j < 2 * N_HEADS:
            t = t * lax.rsqrt(jnp.sum(t * t, axis=-1, keepdims=True) + L2_EPS)
            if j < N_HEADS:
                t = t * HEAD_DIM ** -0.5
        o_ref[0, :, j * HEAD_DIM:(j + 1) * HEAD_DIM] = t.astype(o_ref.dtype)


def _gdn_conv(proj, conv_w, tl):
    b, l, _ = proj.shape
    w3 = 3 * MIX_W
    cb = LO_CQKV // w3
    halo = 16
    nbh = l // halo
    rh = tl // halo
    return pl.pallas_call(
        _conv_kernel,
        grid=(b, l // tl),
        in_specs=[pl.BlockSpec((1, tl, w3), lambda bi, i: (bi, i, cb)),
                  pl.BlockSpec((1, halo, w3), lambda bi, i: (bi, jnp.maximum(i * rh - 1, 0), cb)),
                  pl.BlockSpec((1, halo, w3), lambda bi, i: (bi, jnp.minimum((i + 1) * rh, nbh - 1), cb)),
                  pl.BlockSpec((3, w3), lambda bi, i: (0, 0))],
        out_specs=pl.BlockSpec((1, tl, w3), lambda bi, i: (bi, i, 0)),
        out_shape=jax.ShapeDtypeStruct((b, l, w3), BF16),
        compiler_params=_cparams(("parallel", "parallel"), 40),
        name="gdn_conv",
    )(proj, proj, proj, conv_w)


def _hgrn_chunks(probs, states, emat):
    c = probs[0]["q"].shape[0]
    nb = c // SUB
    for p in probs:
        lb = p["lb"]
        sig = _sigmoid(p["fpre"])
        p["k"] = (1.0 - lb) * (1.0 - sig)
        p["g"] = _cumsum_rows(jnp.log(lb + (1.0 - lb) * sig), p["rev"])

    walls = []
    for p in probs:
        q = p["q"]
        g2 = p["g"] * LOG2_E
        lk2 = jnp.log(p["k"]) * LOG2_E
        rel = lk2 - g2
        rows = []
        half = SUB // 2
        dead = jnp.zeros((half, HEAD_DIM), F32)
        for j in range(nb):
            r0 = j * SUB
            pieces = []
            for s in range(SUB):
                skip = 0 if (s >= half and not p["rev"]) else 1 if (s < half and p["rev"]) else None
                parts = []
                for part in range(2):
                    if part == skip:
                        parts.append(dead)
                        continue
                    t0 = r0 + part * half
                    e2 = jnp.exp2(jnp.minimum(g2[t0:t0 + half] + rel[r0 + s:r0 + s + 1], lk2[r0 + s:r0 + s + 1]))
                    parts.append(e2 * q[t0:t0 + half])
                pieces.append(jnp.concatenate(parts, axis=0).astype(BF16))
            rows.append(jnp.concatenate(pieces, axis=1))
        walls.append(jnp.concatenate(rows, axis=0))
    a_diags = [_dot(w, emat) for w in walls]

    a_offs = []
    for p in probs:
        q, k, g, rev = p["q"], p["k"], p["g"], p["rev"]
        offs = []
        for i in range(nb):
            r0 = i * SUB
            if (not rev and i == 0) or (rev and i == nb - 1):
                offs.append(jnp.zeros((SUB, c), F32))
                continue
            bi = g[r0 + SUB:r0 + SUB + 1] if rev else g[r0 - 1:r0]
            lhs = (q[r0:r0 + SUB] * jnp.exp(g[r0:r0 + SUB] - bi)).astype(BF16)
            lo, hi = (r0 + SUB, c) if rev else (0, r0)
            live = k[lo:hi] * jnp.exp(bi - g[lo:hi])
            pad = jnp.zeros((c - (hi - lo), HEAD_DIM), F32)
            rhs = jnp.concatenate([pad, live] if rev else [live, pad], axis=0)
            offs.append(_dot_nt(lhs, rhs.astype(BF16)))
        a_offs.append(jnp.concatenate(offs, axis=0))

    r = lax.broadcasted_iota(jnp.int32, (c, c), 0)
    col = lax.broadcasted_iota(jnp.int32, (c, c), 1)
    shift = SUB.bit_length() - 1
    rb = lax.shift_right_logical(r, shift)
    cb = lax.shift_right_logical(col, shift)
    inters = [_dot_nt((p["q"] * jnp.exp(p["g"])).astype(BF16), s_t.astype(BF16)) for p, s_t in zip(probs, states)]
    outs, new_states = [], []
    for p, a_diag, a_off, inter, s_t in zip(probs, a_diags, a_offs, inters, states):
        rev, g = p["rev"], p["g"]
        order = (col >= r) if rev else (col <= r)
        off_side = (cb > rb) if rev else (cb < rb)
        a = jnp.where((rb == cb) & order, a_diag, jnp.where(off_side, a_off, 0.0))
        g_end = g[0:1] if rev else g[c - 1:c]
        vb = p["v"].astype(BF16)
        outs.append(_dot(a.astype(BF16), vb) + inter)
        kd = (p["k"] * jnp.exp(g_end - g)).astype(BF16)
        new_states.append(s_t * jnp.exp(g_end) + _dot_tn(vb, kd))
    return outs, new_states


def _hgrn_kernel(qf_ref, ff_ref, vf_ref, qb_ref, fb_ref, vb_ref, lb_ref, e_ref, s0_ref,
                 of_ref, ob_ref, sout_ref, s_ref):
    n = pl.program_id(1)

    @pl.when(n == 0)
    def _():
        s_ref[...] = s0_ref[...]

    probs, where = [], []
    for r in range(qf_ref.shape[0]):
        for d, (q_ref, f_ref, v_ref, o_ref) in enumerate(((qf_ref, ff_ref, vf_ref, of_ref),
                                                          (qb_ref, fb_ref, vb_ref, ob_ref))):
            for h in range(N_HEADS):
                sl = slice(h * HEAD_DIM, (h + 1) * HEAD_DIM)
                probs.append(dict(q=q_ref[r, :, sl].astype(F32), fpre=f_ref[r, :, sl],
                                  v=v_ref[r, :, sl].astype(F32), lb=lb_ref[:, sl], rev=d == 1))
                where.append((o_ref, sl, r, h, d))
    states = [s_ref[r, h, d] for _, _, r, h, d in where]
    outs, new_states = [], []
    for g0 in range(0, len(probs), HGRN_GROUP):
        o_g, s_g = _hgrn_chunks(probs[g0:g0 + HGRN_GROUP], states[g0:g0 + HGRN_GROUP], e_ref[...])
        outs += o_g
        new_states += s_g
    for (o_ref, sl, r, h, d), o, s_new in zip(where, outs, new_states):
        o_ref[r, :, sl] = o.astype(o_ref.dtype)
        s_ref[r, h, d] = s_new

    @pl.when(n == pl.num_programs(1) - 1)
    def _():
        sout_ref[...] = s_ref[...]


def _hgrn_emat():
    r = jnp.arange(SUB * HEAD_DIM)[:, None] // HEAD_DIM
    c = jnp.arange(CHUNK)[None, :] % SUB
    return (r == c).astype(BF16)


def _hgrn_scan(p_lo, p_hi, lb, s0):
    b, l, _ = p_lo.shape
    nc = l // CHUNK
    rows = HGRN_BATCH_ROWS if b % HGRN_BATCH_ROWS == 0 else 1
    q_col, i_col, f_col = LO_AQ // MIX_W, LO_AI // MIX_W, HI_AF // MIX_W

    def fwd(col):
        return pl.BlockSpec((rows, CHUNK, MIX_W), lambda bi, n: (bi, n, col))

    def bwd(col):
        return pl.BlockSpec((rows, CHUNK, MIX_W), lambda bi, n: (bi, nc - 1 - n, col))

    st_spec = pl.BlockSpec((rows, N_HEADS, 2, HEAD_DIM, HEAD_DIM), lambda bi, n: (bi, 0, 0, 0, 0))
    return pl.pallas_call(
        _hgrn_kernel,
        grid=(b // rows, nc),
        in_specs=[fwd(q_col), fwd(f_col), fwd(i_col), bwd(q_col), bwd(f_col + 1), bwd(i_col),
                  pl.BlockSpec((1, MIX_W), lambda bi, n: (0, 0)),
                  pl.BlockSpec((SUB * HEAD_DIM, CHUNK), lambda bi, n: (0, 0)),
                  st_spec],
        out_specs=[pl.BlockSpec((rows, CHUNK, MIX_W), lambda bi, n: (bi, n, 0)),
                   pl.BlockSpec((rows, CHUNK, MIX_W), lambda bi, n: (bi, nc - 1 - n, 0)),
                   st_spec],
        out_shape=[jax.ShapeDtypeStruct((b, l, MIX_W), BF16),
                   jax.ShapeDtypeStruct((b, l, MIX_W), BF16),
                   jax.ShapeDtypeStruct(s0.shape, F32)],
        scratch_shapes=[pltpu.VMEM((rows, N_HEADS, 2, HEAD_DIM, HEAD_DIM), F32)],
        compiler_params=_cparams(("parallel", "arbitrary"), 40),
        name="hgrn_scan",
    )(p_lo, p_hi, p_lo, p_lo, p_hi, p_lo, lb, _hgrn_emat(), s0)


def _mlstm_chunks(probs, states):
    c = probs[0]["q"].shape[0]
    dv = probs[0]["v"].shape[1]
    ones = jnp.ones((c, dv), BF16)
    for p, (cmn, m) in zip(probs, states):
        tri = _order_mask(c, p["rev"])
        log_d = jnp.where(tri, p["bb_c"] - p["bb_r"] + p["ic_r"], MASK_NEG)
        log_inter = p["bb_c"] + m
        m_t = jnp.maximum(jnp.max(log_d, axis=1, keepdims=True), log_inter)
        p["m_t"] = m_t
        p["dmat"] = jnp.where(tri, jnp.exp(log_d - m_t), 0.0)
        p["inter"] = jnp.exp(log_inter - m_t)
        p["qb"], p["kb"] = p["q"].astype(BF16), p["k"].astype(BF16)
        p["v1"] = jnp.concatenate([p["v"].astype(BF16), ones], axis=1)
    scores = [_dot_nt(p["qb"], p["kb"]) * p["dmat"] for p in probs]
    carried = [_dot(p["qb"], cmn.astype(BF16)) for p, (cmn, _) in zip(probs, states)]
    local = [_dot(s.astype(BF16), p["v1"]) for p, s in zip(probs, scores)]
    outs, new_states = [], []
    for p, car, loc, (cmn, m) in zip(probs, carried, local, states):
        m_t = p["m_t"]
        both = loc + p["inter"] * car
        outs.append(both[:, :dv] / jnp.maximum(jnp.abs(both[:, dv:]), jnp.exp(-m_t)))
        m_new = m_t[0:1] if p["rev"] else m_t[c - 1:c]
        bb_end = p["bb_c"][0:1] if p["rev"] else p["bb_c"][c - 1:c]
        decay = jnp.exp(bb_end + m - m_new)
        kw = p["k"] * jnp.exp(bb_end - p["bb_c"] + p["ic_c"] - m_new)
        new_states.append((decay * cmn + _dot_tn(kw.astype(BF16), p["v1"]), m_new))
    return outs, new_states


def _column_view(ref):
    rows, group, width = ref.shape[-3:]
    return ref.reshape(rows * group, width), group


def _chunk_rows(j):
    start = j * CHUNK
    return pl.ds(start if isinstance(start, int) else pl.multiple_of(start, CHUNK), CHUNK)


def _load_chunk(ref, j, colmajor):
    if colmajor:
        flat, group = _column_view(ref)
        return flat[pl.ds(j, CHUNK, stride=group), :]
    return ref[0, _chunk_rows(j), :]


def _store_chunk(ref, j, val, colmajor):
    if colmajor:
        flat, group = _column_view(ref)
        flat[pl.ds(j, CHUNK, stride=group), :] = val
    else:
        ref[0, _chunk_rows(j), :] = val


def _columns_as_rows(x, first):
    row = lax.broadcasted_iota(jnp.int32, (8, LANE), 0)
    lane = lax.broadcasted_iota(jnp.int32, (8, LANE), 1)
    sel = jnp.where(lane == row + first, 1.0, 0.0).astype(BF16)
    hi = x.astype(BF16)
    rest = x - hi.astype(F32)
    mid = rest.astype(BF16)
    lo = (rest - mid.astype(F32)).astype(BF16)
    return _dot_nt(sel, hi) + (_dot_nt(sel, mid) + _dot_nt(sel, lo))


def _mlstm_kernel(*refs, colmajor, chunks):
    per_dir = 3 * N_HEADS + 1
    fwd, bwd = refs[:per_dir], refs[per_dir:2 * per_dir]
    bias_ref, cm0_ref, nm0_ref, of_ref, ob_ref, cmout_ref, nmout_ref, cm_ref, nm_ref = refs[2 * per_dir:]
    n = pl.program_id(1)

    @pl.when(n == 0)
    def _():
        cm_ref[...] = cm0_ref[0]
        nm_ref[...] = nm0_ref[0]

    def step(i, carry):
        probs, where = [], []
        for d, (in_refs, o_ref) in enumerate(((fwd, of_ref), (bwd, ob_ref))):
            rev = d == 1
            j = chunks - 1 - i if rev else i
            gates = _load_chunk(in_refs[-1], j, colmajor) + bias_ref[...]
            cs = _cumsum_rows(_log_sigmoid(gates), rev)
            cs_rows = _columns_as_rows(cs, 8 + 4 * d)
            gate_rows = _columns_as_rows(gates, 4 * d)
            for h in range(N_HEADS):
                ci = 4 * d + h
                cf = 8 + 4 * d + h
                probs.append(dict(
                    q=_load_chunk(in_refs[h], j, colmajor),
                    k=_load_chunk(in_refs[N_HEADS + h], j, colmajor) * HEAD_DIM ** -0.5,
                    v=_load_chunk(in_refs[2 * N_HEADS + h], j, colmajor),
                    bb_c=cs[:, cf:cf + 1], bb_r=cs_rows[h:h + 1, :],
                    ic_c=gates[:, ci:ci + 1], ic_r=gate_rows[h:h + 1, :], rev=rev))
                where.append((o_ref, j, d, h))
        states = [(cm_ref[d, h], nm_ref[d, h, 0:1, 0:1]) for _, _, d, h in where]
        outs, new_states = [], []
        for g0 in range(0, len(probs), MLSTM_GROUP):
            o_g, s_g = _mlstm_chunks(probs[g0:g0 + MLSTM_GROUP], states[g0:g0 + MLSTM_GROUP])
            outs += o_g
            new_states += s_g
        for (o_ref, j, d, h), o, (cmn_new, m_new) in zip(where, outs, new_states):
            _store_chunk(o_ref.at[h], j, o, colmajor)
            cm_ref[d, h] = cmn_new
            nm_ref[d, h, 0:1, :] = jnp.broadcast_to(m_new, (1, HEAD_DIM))
        return carry

    lax.fori_loop(0, chunks, step, 0)

    @pl.when(n == pl.num_programs(1) - 1)
    def _():
        cmout_ref[0] = cm_ref[...]
        nmout_ref[0] = nm_ref[...]


def _mlstm_scan(proj, gbias, cm0, nm0, colmajor):
    b, l, n_proj = proj.shape
    cb = HI_B // LANE
    cg = HI_GATES // LANE
    if colmajor:
        chunks = 8
        assert l == CHUNK * GRID_W and GRID_W % chunks == 0
        nb = GRID_W // chunks
        src = proj.reshape(b, CHUNK, GRID_W, n_proj)
        in_blk = (1, CHUNK, chunks, LANE)
        out_blk = (N_HEADS, 1, CHUNK, chunks, LANE)
        out_view = (N_HEADS, b, CHUNK, GRID_W, LANE)

        def in_map(col, flip):
            return lambda bi, n: (bi, 0, nb - 1 - n if flip else n, col)

        def out_map(flip):
            return lambda bi, n: (0, bi, 0, nb - 1 - n if flip else n, 0)
    else:
        chunks = min(l // CHUNK, 4)
        nb = l // (chunks * CHUNK)
        src = proj
        in_blk = (1, chunks * CHUNK, LANE)
        out_blk = (N_HEADS, 1, chunks * CHUNK, LANE)
        out_view = (N_HEADS, b, l, LANE)

        def in_map(col, flip):
            return lambda bi, n: (bi, nb - 1 - n if flip else n, col)

        def out_map(flip):
            return lambda bi, n: (0, bi, nb - 1 - n if flip else n, 0)

    def specs(flip):
        cols = [cb + j for j in range(3 * N_HEADS)] + [cg]
        return [pl.BlockSpec(in_blk, in_map(col, flip)) for col in cols]

    cm_spec = pl.BlockSpec((1, 2, N_HEADS, HEAD_DIM, 2 * HEAD_DIM), lambda bi, n: (bi, 0, 0, 0, 0))
    nm_spec = pl.BlockSpec((1, 2, N_HEADS, 8, HEAD_DIM), lambda bi, n: (bi, 0, 0, 0, 0))
    n_in = 2 * (3 * N_HEADS + 1)
    o_f, o_b, cm, nm = pl.pallas_call(
        functools.partial(_mlstm_kernel, colmajor=colmajor, chunks=chunks),
        grid=(b, nb),
        in_specs=specs(False) + specs(True) + [pl.BlockSpec((1, LANE), lambda bi, n: (0, 0)), cm_spec, nm_spec],
        out_specs=[pl.BlockSpec(out_blk, out_map(False)), pl.BlockSpec(out_blk, out_map(True)), cm_spec, nm_spec],
        out_shape=[jax.ShapeDtypeStruct(out_view, F32), jax.ShapeDtypeStruct(out_view, F32),
                   jax.ShapeDtypeStruct(cm0.shape, F32), jax.ShapeDtypeStruct(nm0.shape, F32)],
        scratch_shapes=[pltpu.VMEM((2, N_HEADS, HEAD_DIM, 2 * HEAD_DIM), F32),
                        pltpu.VMEM((2, N_HEADS, 8, HEAD_DIM), F32)],
        compiler_params=_cparams(("parallel", "arbitrary"), 40),
        name="mlstm_scan",
    )(*([src] * n_in), gbias, cm0, nm0)
    head_major = (N_HEADS, b, l, LANE)
    return o_f.reshape(head_major), o_b.reshape(head_major), cm, nm


def _split_bf16(x):
    hi = x.astype(BF16)
    return hi, (x - hi.astype(F32)).astype(BF16)


def _dot3(a, b, nt=False):
    f = _dot_nt if nt else _dot
    return f(a[0], b[0]) + (f(a[0], b[1]) + f(a[1], b[0]))


def _inv_unit_triangular(a_list):
    n = a_list[0].shape[0]
    eye = (lax.broadcasted_iota(jnp.int32, (n, n), 0) == lax.broadcasted_iota(jnp.int32, (n, n), 1)).astype(F32)
    xs = [eye - a for a in a_list]
    splits = [_split_bf16(a) for a in a_list]
    ps = [_dot3(s, s) for s in splits]
    steps = (n - 1).bit_length() - 1
    for step in range(steps):
        last = step == steps - 1
        p_split = [_split_bf16(p) for p in ps]
        lhs = xs if last else [jnp.concatenate([x, p], axis=0) for x, p in zip(xs, ps)]
        prods = [_dot3(_split_bf16(l), s) for l, s in zip(lhs, p_split)]
        xs = [x + pr[:n] for x, pr in zip(xs, prods)]
        if not last:
            ps = [pr[n:] for pr in prods]
    return xs


def _gdn_chunks(probs, states):
    c = probs[0]["q"].shape[0]
    a_list, ldecs = [], []
    for p in probs:
        tri = _order_mask(c, p["rev"])
        strict = _order_mask(c, p["rev"], strict=True)
        ldec = jnp.where(tri, jnp.exp(jnp.minimum(p["gam_c"] - p["gam_r"], 0.0)), 0.0)
        p["kbeta"] = p["k"] * p["beta"]
        kk = _dot_nt(p["kbeta"].astype(BF16), p["k"].astype(BF16))
        a_list.append(jnp.where(strict, kk * ldec, 0.0))
        ldecs.append(ldec)
    t_invs = _inv_unit_triangular(a_list)
    sols = []
    for p, t_inv in zip(probs, t_invs):
        p["eg"] = jnp.exp(p["gam_c"])
        rhs = jnp.concatenate([p["v"] * p["beta"], p["kbeta"] * p["eg"]], axis=1)
        sols.append(_dot3(_split_bf16(t_inv), _split_bf16(rhs)))
    qks = [_dot_nt(p["q"].astype(BF16), p["k"].astype(BF16)) * ldec for p, ldec in zip(probs, ldecs)]
    inter = [_dot(jnp.concatenate([sol[:, HEAD_DIM:], p["q"] * p["eg"]], axis=0).astype(BF16), s.astype(BF16))
             for p, sol, s in zip(probs, sols, states)]
    outs, new_states = [], []
    for p, sol, qk, it, s in zip(probs, sols, qks, inter, states):
        g_end = p["gam_c"][0:1] if p["rev"] else p["gam_c"][c - 1:c]
        v_new = (sol[:, :HEAD_DIM] - it[:c]).astype(BF16)
        kg = p["k"] * jnp.exp(g_end - p["gam_c"])
        outs.append(it[c:] + _dot(qk.astype(BF16), v_new))
        new_states.append(jnp.exp(g_end) * s + _dot_tn(kg.astype(BF16), v_new))
    return outs, new_states


def _gdn_kernel(xf_ref, gf_ref, xb_ref, gb_ref, alog_ref, dtb_ref, s0_ref,
                of_ref, ob_ref, sout_ref, s_ref):
    n = pl.program_id(1)

    @pl.when(n == 0)
    def _():
        s_ref[...] = s0_ref[...]

    neg_a = -jnp.exp(alog_ref[...])
    probs, where = [], []
    for r in range(xf_ref.shape[0]):
        for d, (x_ref, g_ref, o_ref) in enumerate(((xf_ref, gf_ref, of_ref), (xb_ref, gb_ref, ob_ref))):
            rev = d == 1
            gates = g_ref[r]
            cs = _cumsum_rows(neg_a * _softplus(gates + dtb_ref[...]), rev)
            cs_t = cs.T
            betas = _sigmoid(gates)
            for h in range(N_HEADS):
                ca = 16 + 4 * d + h
                cbeta = 24 + 4 * d + h
                probs.append(dict(
                    q=x_ref[r, :, h * HEAD_DIM:(h + 1) * HEAD_DIM].astype(F32),
                    k=x_ref[r, :, MIX_W + h * HEAD_DIM:MIX_W + (h + 1) * HEAD_DIM].astype(F32),
                    v=x_ref[r, :, 2 * MIX_W + h * HEAD_DIM:2 * MIX_W + (h + 1) * HEAD_DIM].astype(F32),
                    gam_c=cs[:, ca:ca + 1], gam_r=cs_t[ca:ca + 1, :], beta=betas[:, cbeta:cbeta + 1], rev=rev))
                where.append((o_ref, r, d, h))
    outs, new_states = _gdn_chunks(probs, [s_ref[r, d, h] for _, r, d, h in where])
    for (o_ref, r, d, h), o, s_new in zip(where, outs, new_states):
        o_ref[r, :, h * HEAD_DIM:(h + 1) * HEAD_DIM] = o.astype(o_ref.dtype)
        s_ref[r, d, h] = s_new

    @pl.when(n == pl.num_programs(1) - 1)
    def _():
        sout_ref[...] = s_ref[...]


def _gdn_scan(qkv, proj, alog_row, dtb_row, s0):
    b, l, w3 = qkv.shape
    nc = l // CHUNK
    cg = HI_GATES // LANE
    rows = GDN_BATCH_ROWS if b % GDN_BATCH_ROWS == 0 else 1
    st_spec = pl.BlockSpec((rows, 2, N_HEADS, HEAD_DIM, HEAD_DIM), lambda bi, n: (bi, 0, 0, 0, 0))
    row_spec = pl.BlockSpec((1, LANE), lambda bi, n: (0, 0))
    return pl.pallas_call(
        _gdn_kernel,
        grid=(b // rows, nc),
        in_specs=[pl.BlockSpec((rows, CHUNK, w3), lambda bi, n: (bi, n, 0)),
                  pl.BlockSpec((rows, CHUNK, LANE), lambda bi, n: (bi, n, cg)),
                  pl.BlockSpec((rows, CHUNK, w3), lambda bi, n: (bi, nc - 1 - n, 0)),
                  pl.BlockSpec((rows, CHUNK, LANE), lambda bi, n: (bi, nc - 1 - n, cg)),
                  row_spec, row_spec, st_spec],
        out_specs=[pl.BlockSpec((rows, CHUNK, MIX_W), lambda bi, n: (bi, n, 0)),
                   pl.BlockSpec((rows, CHUNK, MIX_W), lambda bi, n: (bi, nc - 1 - n, 0)),
                   st_spec],
        out_shape=[jax.ShapeDtypeStruct((b, l, MIX_W), BF16), jax.ShapeDtypeStruct((b, l, MIX_W), BF16),
                   jax.ShapeDtypeStruct(s0.shape, F32)],
        scratch_shapes=[pltpu.VMEM((rows, 2, N_HEADS, HEAD_DIM, HEAD_DIM), F32)],
        compiler_params=_cparams(("parallel", "arbitrary"), 40),
        name="gdn_scan",
    )(qkv, proj, qkv, proj, alog_row, dtb_row, s0)


def _merge_kernel(oaf, oab, obf, obb, ocf, ocb, ag, bo, cz, mg, x_ref, mod_ref, gain_ref, wbr_ref, wout_ref,
                  lng_ref, lnb_ref, out_ref, *, alpha):
    d_model = x_ref.shape[-1]
    branches = ((oaf, oab, ag, _silu), (obf, obb, bo, _sigmoid), (ocf, ocb, cz, _silu))
    mixed = None
    for nbr, (of_ref, ob_ref, gate_ref, act) in enumerate(branches):
        parts = []
        for h in range(N_HEADS):
            if len(of_ref.shape) == 4:
                oh = of_ref[h, 0] + ob_ref[h, 0]
            else:
                sl = slice(h * HEAD_DIM, (h + 1) * HEAD_DIM)
                oh = of_ref[0, :, sl].astype(F32) + ob_ref[0, :, sl].astype(F32)
            parts.append(oh * lax.rsqrt(jnp.mean(oh * oh, axis=-1, keepdims=True) + RMS_EPS))
        y = jnp.concatenate(parts, axis=1) * gain_ref[nbr:nbr + 1, :] * act(gate_ref[0].astype(F32))
        gate = _sigmoid(mg[0, :, nbr * d_model:(nbr + 1) * d_model].astype(F32))
        term = gate * _dot(y.astype(BF16), wbr_ref[nbr])
        mixed = term if mixed is None else mixed + term
    sub = _dot(mixed.astype(BF16), wout_ref[...])
    z = alpha * x_ref[0] + mod_ref[0, 2:3, :] * sub
    out_ref[0] = _ln(z) * lng_ref[...] + lnb_ref[...]


def _merge(outs, proj, x, mod, gain, w_branch, w_out, ln_g, ln_b, alpha, tl):
    b, l, d = x.shape
    u = MIX_W

    def tok(width, col):
        return pl.BlockSpec((1, tl, width), lambda bi, i: (bi, i, col))

    def const(shape):
        return pl.BlockSpec(shape, lambda bi, i: (0,) * len(shape))

    heads = pl.BlockSpec((N_HEADS, 1, tl, LANE), lambda bi, i: (0, bi, i, 0))
    return pl.pallas_call(
        functools.partial(_merge_kernel, alpha=alpha),
        grid=(b, l // tl),
        in_specs=[tok(u, 0), tok(u, 0), heads, heads, tok(u, 0), tok(u, 0),
            tok(u, LO_AG // u), tok(u, LO_BO // u), tok(u, LO_CZ // u),
            tok(3 * d, LO_MERGE // (3 * d)), tok(d, 0),
            pl.BlockSpec((1, 6, d), lambda bi, i: (bi, 0, 0)),
            const((3, u)), const((3, u, d)), const((d, d)), const((1, d)), const((1, d))],
        out_specs=tok(d, 0),
        out_shape=jax.ShapeDtypeStruct((b, l, d), F32),
        compiler_params=_cparams(("parallel", "parallel"), 48),
        name="merge",
    )(*outs, proj, proj, proj, proj, x, mod, gain, w_branch, w_out, ln_g, ln_b)


def _ffn_kernel(x_ref, mod_ref, wgu_ref, wd_ref, lng_ref, lnb_ref, out_ref, h_ref, acc_ref, *, alpha):
    f = pl.program_id(2)

    @pl.when(f == 0)
    def _():
        h_ref[...] = (_ln(x_ref[0]) * (1.0 + mod_ref[0, 4:5, :]) + mod_ref[0, 3:4, :]).astype(BF16)
        acc_ref[...] = jnp.zeros_like(acc_ref)

    gu = _dot(h_ref[...], wgu_ref[0, 0])
    half = gu.shape[1] // 2
    act = _silu(gu[:, :half]) * gu[:, half:]
    acc_ref[...] += _dot(act.astype(BF16), wd_ref[...])

    @pl.when(f == pl.num_programs(2) - 1)
    def _():
        z = alpha * x_ref[0] + mod_ref[0, 5:6, :] * acc_ref[...]
        out_ref[0] = _ln(z) * lng_ref[...] + lnb_ref[...]


def _ffn(x, mod, w_gate_up, w_down, ln_g, ln_b, alpha, tl):
    b, l, d = x.shape
    _, nf, _, tf2 = w_gate_up.shape
    return pl.pallas_call(
        functools.partial(_ffn_kernel, alpha=alpha),
        grid=(b, l // tl, nf),
        in_specs=[pl.BlockSpec((1, tl, d), lambda bi, i, j: (bi, i, 0)),
                  pl.BlockSpec((1, 6, d), lambda bi, i, j: (bi, 0, 0)),
                  pl.BlockSpec((1, 1, d, tf2), lambda bi, i, j: (0, j, 0, 0)),
                  pl.BlockSpec((tf2 // 2, d), lambda bi, i, j: (j, 0)),
                  pl.BlockSpec((1, d), lambda bi, i, j: (0, 0)),
                  pl.BlockSpec((1, d), lambda bi, i, j: (0, 0))],
        out_specs=pl.BlockSpec((1, tl, d), lambda bi, i, j: (bi, i, 0)),
        out_shape=jax.ShapeDtypeStruct((b, l, d), F32),
        scratch_shapes=[pltpu.VMEM((tl, d), BF16), pltpu.VMEM((tl, d), F32)],
        compiler_params=_cparams(("parallel", "parallel", "arbitrary"), 56),
        name="ffn",
    )(x, mod, w_gate_up, w_down, ln_g, ln_b)


def _top2_combine(logits):
    lane = lax.broadcasted_iota(jnp.int32, logits.shape, 1).astype(F32)
    neg = -jnp.inf
    lg = jnp.where(lane < N_EXPERTS, logits, neg)
    m1 = jnp.max(lg, axis=1, keepdims=True)
    i1 = jnp.min(jnp.where(lg == m1, lane, float(LANE)), axis=1, keepdims=True)
    lg2 = jnp.where(lane == i1, neg, lg)
    m2 = jnp.max(lg2, axis=1, keepdims=True)
    i2 = jnp.min(jnp.where(lg2 == m2, lane, float(LANE)), axis=1, keepdims=True)
    e2 = jnp.exp(m2 - m1)
    p1 = 1.0 / (1.0 + e2)
    chosen = (lane == i1) | (lane == i2)
    return jnp.where(lane == i1, p1, jnp.where(lane == i2, e2 * p1, 0.0)), chosen


def _moe_kernel(x_ref, mod_ref, wr_ref, wgu_ref, wd_ref, lng_ref, lnb_ref, out_ref,
                h_ref, acc_ref, comb_ref, pos_ref, post_ref, col_ref, xc_ref, yc_ref, cnt_ref, *, alpha, rb):
    e = pl.program_id(2)
    f = pl.program_id(3)
    tl, d = h_ref.shape

    @pl.when((e == 0) & (f == 0))
    def _():
        h = _ln(x_ref[0]) * (1.0 + mod_ref[0, 4:5, :]) + mod_ref[0, 3:4, :]
        h_ref[...] = h.astype(BF16)
        comb, chosen = _top2_combine(_dot(h, wr_ref[...], precision=HIGHEST))
        comb_ref[...] = comb
        blk = min(tl, LANE)
        before = jnp.where(_order_mask(blk, False, strict=True), 1.0, 0.0).astype(BF16)
        routed = jnp.where(chosen, 1.0, 0.0)
        total = jnp.zeros((1, LANE), F32)
        ranks = []
        for r0 in range(0, tl, blk):
            part = routed[r0:r0 + blk]
            ranks.append(_dot(before, part.astype(BF16)) + total)
            total = total + jnp.sum(part, axis=0, keepdims=True)
        rank = jnp.concatenate(ranks, axis=0)
        pos = jnp.where(chosen, rank, -1.0)
        pos_ref[...] = pos
        pos_t = pos.T[0:N_EXPERTS, :]
        post_ref[...] = pos_t
        for ex in range(N_EXPERTS):
            cnt_ref[ex] = jnp.sum(jnp.where(pos_t[ex:ex + 1, :] >= 0.0, 1.0, 0.0)).astype(jnp.int32)
        acc_ref[...] = jnp.zeros_like(acc_ref)

    nblk = (cnt_ref[e] + (rb - 1)) // rb

    def rows(r):
        return pl.ds(pl.multiple_of(r * rb, rb), rb)

    @pl.when(f == 0)
    def _():
        lane = lax.broadcasted_iota(jnp.int32, (tl, LANE), 1)
        col_ref[0] = jnp.sum(jnp.where(lane == e, pos_ref[...], 0.0), axis=1, keepdims=True)
        col_ref[1] = jnp.sum(jnp.where(lane == e, comb_ref[...], 0.0), axis=1, keepdims=True)
        pos_row = post_ref[pl.ds(e, 1), :]

        def gather(r, carry):
            slot = (lax.broadcasted_iota(jnp.int32, (rb, tl), 0) + r * rb).astype(F32)
            sel = jnp.where(pos_row == slot, 1.0, 0.0).astype(BF16)
            xc_ref[rows(r), :] = _dot(sel, h_ref[...]).astype(BF16)
            yc_ref[rows(r), :] = jnp.zeros((rb, d), F32)
            return carry

        lax.fori_loop(0, nblk, gather, 0)

    def expert(r, carry):
        xb = xc_ref[rows(r), :]
        gu = _dot(xb, wgu_ref[0, 0])
        half = gu.shape[1] // 2
        act = _silu(gu[:, :half]) * gu[:, half:]
        yc_ref[rows(r), :] += _dot(act.astype(BF16), wd_ref[0])
        return carry

    lax.fori_loop(0, nblk, expert, 0)

    @pl.when(f == pl.num_programs(3) - 1)
    def _():
        pos_col = col_ref[0]
        weight = col_ref[1]

        def scatter(r, carry):
            slot = (lax.broadcasted_iota(jnp.int32, (tl, rb), 1) + r * rb).astype(F32)
            sel_t = jnp.where(pos_col == slot, 1.0, 0.0).astype(BF16)
            acc_ref[...] += weight * _dot(sel_t, yc_ref[rows(r), :].astype(BF16))
            return carry

        lax.fori_loop(0, nblk, scatter, 0)

    @pl.when((e == pl.num_programs(2) - 1) & (f == pl.num_programs(3) - 1))
    def _():
        z = alpha * x_ref[0] + mod_ref[0, 5:6, :] * acc_ref[...]
        out_ref[0] = _ln(z) * lng_ref[...] + lnb_ref[...]


def _pack_gate_up(w_gate, w_up, widest):
    e, d, f = w_gate.shape
    tf = next(t for t in (1792, 896, 512, 256, 128) if t <= widest and f % t == 0)

    def tiles(w):
        return w.astype(BF16).reshape(e, d, f // tf, tf).transpose(0, 2, 1, 3)

    return jnp.concatenate([tiles(w_gate), tiles(w_up)], axis=-1)


def _moe(x, mod, w_router, w_gate_up, w_down, ln_g, ln_b, alpha, tl):
    b, l, d = x.shape
    ne, nf, _, tf2 = w_gate_up.shape
    rb = min(MOE_ROW_BLOCK, tl)
    compact_rows = -(-tl // rb) * rb
    return pl.pallas_call(
        functools.partial(_moe_kernel, alpha=alpha, rb=rb),
        grid=(b, l // tl, ne, nf),
        in_specs=[pl.BlockSpec((1, tl, d), lambda bi, i, e, j: (bi, i, 0)),
                  pl.BlockSpec((1, 6, d), lambda bi, i, e, j: (bi, 0, 0)),
                  pl.BlockSpec((d, LANE), lambda bi, i, e, j: (0, 0)),
                  pl.BlockSpec((1, 1, d, tf2), lambda bi, i, e, j: (e, j, 0, 0)),
                  pl.BlockSpec((1, tf2 // 2, d), lambda bi, i, e, j: (e, j, 0)),
                  pl.BlockSpec((1, d), lambda bi, i, e, j: (0, 0)),
                  pl.BlockSpec((1, d), lambda bi, i, e, j: (0, 0))],
        out_specs=pl.BlockSpec((1, tl, d), lambda bi, i, e, j: (bi, i, 0)),
        out_shape=jax.ShapeDtypeStruct((b, l, d), F32),
        scratch_shapes=[pltpu.VMEM((tl, d), BF16),
                        pltpu.VMEM((tl, d), F32),
                        pltpu.VMEM((tl, LANE), F32),
                        pltpu.VMEM((tl, LANE), F32),
                        pltpu.VMEM((N_EXPERTS, tl), F32),
                        pltpu.VMEM((2, tl, 1), F32),
                        pltpu.VMEM((compact_rows, d), BF16),
                        pltpu.VMEM((compact_rows, d), F32),
                        pltpu.SMEM((N_EXPERTS,), jnp.int32)],
        compiler_params=_cparams(("parallel", "parallel", "arbitrary", "arbitrary"), 60),
        name="moe",
    )(x, mod, w_router, w_gate_up, w_down, ln_g, ln_b)


def _reorder_w_in(w):
    d = w.shape[0]
    u = MIX_W
    o_a, o_b = 0, 5 * u
    o_bgt = o_b + 4 * u
    o_cqkv = o_bgt + 16
    o_cz = o_cqkv + 3 * u
    o_cgt = o_cz + u
    o_merge = o_cgt + 16
    lo = [w[:, o_merge:o_merge + 3 * d], w[:, o_cqkv:o_cz], w[:, o_a:o_a + u], w[:, o_a + 3 * u:o_a + 5 * u],
          w[:, o_b + 3 * u:o_bgt], w[:, o_cz:o_cgt]]
    hi = [w[:, o_a + u:o_a + 3 * u], w[:, o_b:o_b + 3 * u], w[:, o_bgt:o_bgt + 16], w[:, o_cgt:o_cgt + 16],
          jnp.zeros((d, N_HI - HI_GATES - 32), w.dtype)]
    return jnp.concatenate(lo, axis=1).astype(BF16), jnp.concatenate(hi, axis=1).astype(BF16)


def _pad_row(vals, offset):
    row = jnp.zeros((1, LANE), F32)
    return lax.dynamic_update_slice(row, vals.reshape(1, -1).astype(F32), (0, offset))


def _tile(l, pref):
    return pref if l % pref == 0 else l


def kernel(x, c, ctx, c_ctx, w_ada, b_ada, w_in, conv_w, lb_raw, m_gate_bias, gdn_a_log, gdn_dt_bias, mix_norm,
           w_branch, w_out, ln_g, ln_b, ffn_w_gate, ffn_w_up, ffn_w_down, moe_router, moe_w_gate, moe_w_up,
           moe_w_down):
    depth, d_model = w_in.shape[0], w_in.shape[1]
    b, l_lat, _ = x.shape
    l_ctx = ctx.shape[1]
    alpha = float((2 * depth) ** 0.25)
    assert d_model == 1024 and w_in.shape[2] == 9 * MIX_W + 3 * MIX_W + MIX_W + 32 + 3 * d_model

    lower = _lower_bounds(lb_raw.astype(F32))
    n_rows = -(-(b + 1) // 8) * 8
    cvec = jnp.concatenate([c, c_ctx[None, :], jnp.zeros((n_rows - b - 1, d_model), F32)], axis=0)
    mods = _ada(cvec, w_ada, b_ada)

    zeros_state = jnp.zeros((b, 2, N_HEADS, HEAD_DIM, HEAD_DIM), F32)
    zeros_hgrn = jnp.zeros((b, N_HEADS, 2, HEAD_DIM, HEAD_DIM), F32)
    zeros_nm = jnp.zeros((b, 2, N_HEADS, 8, HEAD_DIM), F32)
    zeros_cmn = jnp.zeros((b, 2, N_HEADS, HEAD_DIM, 2 * HEAD_DIM), F32)

    pack = 1024 // l_ctx if (1024 % l_ctx == 0 and b % max(1024 // l_ctx, 1) == 0) else 1
    ctx_packed = (b // pack, pack * l_ctx, d_model)

    for l in range(depth):
        mod_l = mods[l, :b].reshape(b, 6, d_model)
        mod_c = jnp.broadcast_to(mods[l, b].reshape(1, 6, d_model), (b, 6, d_model))
        w_lo, w_hi = _reorder_w_in(w_in[l])
        lb_row = lower[l].reshape(1, MIX_W)
        gbias = _pad_row(m_gate_bias[l], 0)
        alog_row = _pad_row(gdn_a_log[l], 16)
        dtb_row = _pad_row(gdn_dt_bias[l], 16)
        gain = mix_norm[l]
        wbr = w_branch[l].astype(BF16)
        wo = w_out[l].astype(BF16)
        g0, b0 = ln_g[l, 0].reshape(1, -1), ln_b[l, 0].reshape(1, -1)
        g1, b1 = ln_g[l, 1].reshape(1, -1), ln_b[l, 1].reshape(1, -1)

        lo_c, hi_c = _in_proj(ctx.reshape(ctx_packed), mod_c[:ctx_packed[0]], w_lo, w_hi, ctx_packed[1])
        lo_c, hi_c = lo_c.reshape(b, l_ctx, N_LO), hi_c.reshape(b, l_ctx, N_HI)
        lo_l, hi_l = _in_proj(x, mod_l, w_lo, w_hi, _tile(l_lat, 1024))
        qkv_c = _gdn_conv(lo_c, conv_w[l], _tile(l_ctx, 256))
        qkv_l = _gdn_conv(lo_l, conv_w[l], _tile(l_lat, 512))

        oa_c = _hgrn_scan(lo_c, hi_c, lb_row, zeros_hgrn)
        oa_l = _hgrn_scan(lo_l, hi_l, lb_row, oa_c[2])
        ob_c = _mlstm_scan(hi_c, gbias, zeros_cmn, zeros_nm, colmajor=False)
        ob_l = _mlstm_scan(hi_l, gbias, ob_c[2], ob_c[3], colmajor=True)
        oc_c = _gdn_scan(qkv_c, hi_c, alog_row, dtb_row, zeros_state)
        oc_l = _gdn_scan(qkv_l, hi_l, alog_row, dtb_row, oc_c[2])

        def channel_mixer(t, mod, tl):
            i = l // 2
            if l % 2 == 0:
                return _ffn(t, mod, _pack_gate_up(ffn_w_gate[i][None], ffn_w_up[i][None], 896),
                            ffn_w_down[i].astype(BF16), g1, b1, alpha, tl)
            w_r = jnp.concatenate([moe_router[i], jnp.zeros((d_model, LANE - N_EXPERTS), F32)], axis=1)
            return _moe(t, mod, w_r, _pack_gate_up(moe_w_gate[i], moe_w_up[i], 1792), moe_w_down[i].astype(BF16),
                        g1, b1, alpha, tl)

        outs_l = (oa_l[0], oa_l[1], ob_l[0], ob_l[1], oc_l[0], oc_l[1])
        x = _merge(outs_l, lo_l, x, mod_l, gain, wbr, wo, g0, b0, alpha, _tile(l_lat, 512))
        x = channel_mixer(x, mod_l, _tile(l_lat, 1024))
        if l < depth - 1:
            outs_c = (oa_c[0], oa_c[1], ob_c[0], ob_c[1], oc_c[0], oc_c[1])
            ctx = _merge(outs_c, lo_c, ctx, mod_c, gain, wbr, wo, g0, b0, alpha, _tile(l_ctx, 256))
            ctx = channel_mixer(ctx.reshape(ctx_packed), mod_c[:ctx_packed[0]], ctx_packed[1]).reshape(ctx.shape)
    return x
```
